```python
import jax, jax.numpy as jnp
from jax import lax
import numpy as np

D_MODEL = 1024
BATCH = 32
SEQ = 256
DEPTH = 2
DEC_BATCH = 2
DEC_SEQ = 1024
PAST_LEN = 256

GRID_W = 64
EPS = 1e-6
POOL_GROUPS = 4
POOL_CH = 96
POOL_WIDTH = POOL_GROUPS * POOL_CH
POOL_WINDOWS = (2, 4, 8, 16)
FFT_GROUPS = 4
FFT_CH = 96
FFT_WIDTH = FFT_GROUPS * FFT_CH
N_HEADS = 8
N_KV_HEADS = 2
HEAD_DIM = 64
ATTN_WIDTH = N_HEADS * HEAD_DIM
KV_WIDTH = N_KV_HEADS * HEAD_DIM
Q_BLOCK = 128
ROPE_THETA = 10000.0
ROPE_PAIRS = HEAD_DIM // 4
SGU_GROUPS = 4
SGU_CH = 96
SGU_WIDTH = SGU_GROUPS * SGU_CH
SGU_CHUNK = 128
N_BRANCHES = 4
IN_COLS = POOL_WIDTH + FFT_WIDTH + ATTN_WIDTH + 2 * KV_WIDTH + 2 * SGU_WIDTH + N_BRANCHES * D_MODEL
N_EXPERTS = 32
TOP_K = 4
D_FF = 1024
SWIGLU_LIMIT = 7.0
SWIGLU_ALPHA = 1.702

kernel_name = "hybrid_prefix_diffusion_step"


def _normalize(x):
    xf = x.astype(jnp.float32)
    return (xf * lax.rsqrt(jnp.mean(xf * xf, axis=-1, keepdims=True) + EPS)).astype(x.dtype)


def rms_norm(x, g):
    return _normalize(x) * g


def _split_points():
    widths = (POOL_WIDTH, FFT_WIDTH, ATTN_WIDTH, KV_WIDTH, KV_WIDTH, 2 * SGU_WIDTH)
    pts, acc = [], 0
    for w in widths:
        acc += w
        pts.append(acc)
    return pts


def modulation(cond, w_mod, b_mod):
    m = jax.nn.silu(cond) @ w_mod + b_mod
    return [t[:, None, :] for t in jnp.split(m, 6, axis=-1)]


def axial_rope_tables(n_tokens):
    rows = n_tokens // GRID_W
    row = jnp.repeat(jnp.arange(rows, dtype=jnp.float32), GRID_W)
    col = jnp.tile(jnp.arange(GRID_W, dtype=jnp.float32), rows)
    freqs = ROPE_THETA ** (-jnp.arange(ROPE_PAIRS, dtype=jnp.float32) / ROPE_PAIRS)
    ang = jnp.stack([row[:, None] * freqs, col[:, None] * freqs], axis=1)
    return jnp.cos(ang), jnp.sin(ang)


def apply_rope(x, cos, sin):
    B, S, H, _ = x.shape
    xf = x.astype(jnp.float32).reshape(B, S, H, 2, ROPE_PAIRS, 2)
    x1, x2 = xf[..., 0], xf[..., 1]
    c = cos[None, :, None]
    s = sin[None, :, None]
    out = jnp.stack([x1 * c - x2 * s, x1 * s + x2 * c], axis=-1)
    return out.reshape(x.shape).astype(x.dtype)


def blocked_attention(q, k, v):
    B, S, H, Dh = q.shape
    G = H // N_KV_HEADS
    nb = S // Q_BLOCK
    qb = q.reshape(B, nb, Q_BLOCK, N_KV_HEADS, G, Dh).transpose(1, 0, 2, 3, 4, 5)
    scale = Dh ** -0.5

    def one_block(qi):
        s = jnp.einsum('bqkgd,btkd->bkgqt', qi, k).astype(jnp.float32) * scale
        p = jax.nn.softmax(s, axis=-1).astype(v.dtype)
        return jnp.einsum('bkgqt,btkd->bqkgd', p, v)

    out = lax.map(one_block, qb)
    return out.transpose(1, 0, 2, 3, 4, 5).reshape(B, S, H * Dh)


def pool_mixer(xp, pool_w, pool_scale):
    B, S, _ = xp.shape
    xf = xp.reshape(B, S, POOL_GROUPS, POOL_CH).astype(jnp.float32)
    cs = jnp.concatenate([jnp.zeros_like(xf[:, :1]), jnp.cumsum(xf, axis=1)], axis=1)
    win = jnp.array(POOL_WINDOWS, dtype=jnp.int32)
    t = jnp.arange(S, dtype=jnp.int32)[:, None]
    lo = jnp.clip(t - win // 2, 0, S)
    hi = jnp.clip(t - win // 2 + win, 0, S)
    full = (B, S, POOL_GROUPS, POOL_CH)
    total = (jnp.take_along_axis(cs, jnp.broadcast_to(hi[None, :, :, None], full), axis=1)
             - jnp.take_along_axis(cs, jnp.broadcast_to(lo[None, :, :, None], full), axis=1))
    mean = total / (hi - lo).astype(jnp.float32)[None, :, :, None]
    pooled = (mean - xf).astype(xp.dtype)
    mixed = jnp.einsum('bsgc,gcd->bsgd', pooled, pool_w).reshape(B, S, POOL_WIDTH)
    return mixed * pool_scale


def fourier_mixer(xq):
    B, S, _ = xq.shape
    xg = xq.reshape(B, S, FFT_GROUPS, FFT_CH).astype(jnp.float32)
    y = jnp.fft.fft2(xg, axes=(1, 3), norm='ortho').real
    return y.reshape(B, S, FFT_WIDTH).astype(xq.dtype)


def sgu_mixer(uv, sgu_w, sgu_b):
    B, S, _ = uv.shape
    u, v = jnp.split(jax.nn.gelu(uv), 2, axis=-1)
    n_chunks = S // SGU_CHUNK
    vg = _normalize(v.reshape(B, n_chunks, SGU_CHUNK, SGU_GROUPS, SGU_CH))
    spatial = jnp.einsum('gpq,bnqgc->bnpgc', sgu_w, vg) + sgu_b.T[None, None, :, :, None]
    return u * spatial.reshape(B, S, SGU_WIDTH)


def mixing_sublayer(h, rope, ctx_k, ctx_v, w_in, pool_w, pool_scale, q_norm_g, k_norm_g,
                    sgu_w, sgu_b, w_br_pool, w_br_fourier, w_br_attn, w_br_sgu, w_out):
    B, S, _ = h.shape
    proj = h @ w_in
    xp, xq, q, k, v, uv, gate_logits = jnp.split(proj, _split_points(), axis=-1)
    q = rms_norm(q.reshape(B, S, N_HEADS, HEAD_DIM), q_norm_g)
    k = rms_norm(k.reshape(B, S, N_KV_HEADS, HEAD_DIM), k_norm_g)
    v = v.reshape(B, S, N_KV_HEADS, HEAD_DIM)
    if rope is None:
        attn = blocked_attention(q, k, v)
    else:
        cos, sin = rope
        keys = jnp.concatenate([apply_rope(k, cos, sin), ctx_k], axis=1)
        vals = jnp.concatenate([v, ctx_v], axis=1)
        attn = blocked_attention(apply_rope(q, cos, sin), keys, vals)
    branches = jnp.stack([
        pool_mixer(xp, pool_w, pool_scale) @ w_br_pool,
        fourier_mixer(xq) @ w_br_fourier,
        attn @ w_br_attn,
        sgu_mixer(uv, sgu_w, sgu_b) @ w_br_sgu,
    ], axis=2)
    gates = jax.nn.sigmoid(gate_logits.reshape(B, S, N_BRANCHES, D_MODEL))
    merged = jnp.sum(gates * branches, axis=2)
    return merged @ w_out, k, v


def moe_ffn(h, router_w, router_b, w_gate, b_gate, w_up, b_up, w_down, b_down):
    B, S, D = h.shape
    t = h.reshape(B * S, D)
    logits = (t @ router_w + router_b).astype(jnp.float32)
    top_vals, top_idx = lax.top_k(logits, TOP_K)
    weights = jax.nn.softmax(top_vals, axis=-1)
    combine = jnp.sum(jax.nn.one_hot(top_idx, N_EXPERTS, dtype=jnp.float32) * weights[..., None], axis=1)
    combine = combine.astype(h.dtype)

    def expert(acc, e):
        wg, bg, wu, bu, wd, bd, ce = e
        gate = jnp.minimum(t @ wg + bg, SWIGLU_LIMIT)
        up = jnp.clip(t @ wu + bu, -SWIGLU_LIMIT, SWIGLU_LIMIT)
        glu = gate * jax.nn.sigmoid(SWIGLU_ALPHA * gate)
        y = ((up + 1) * glu) @ wd + bd
        return acc + ce[:, None] * y, None

    acc, _ = lax.scan(expert, jnp.zeros_like(t), (w_gate, b_gate, w_up, b_up, w_down, b_down, combine.T))
    return acc.reshape(B, S, D)


def trunk_layer(x, cond, rope, ctx_k, ctx_v, w_mod, b_mod, norm_mix_g, norm_ffn_g, w_in, pool_w,
                pool_scale, q_norm_g, k_norm_g, sgu_w, sgu_b, w_br_pool, w_br_fourier, w_br_attn,
                w_br_sgu, w_out, router_w, router_b, moe_w_gate, moe_b_gate, moe_w_up, moe_b_up,
                moe_w_down, moe_b_down):
    shift1, scale1, gate1, shift2, scale2, gate2 = modulation(cond, w_mod, b_mod)
    h = rms_norm(x, norm_mix_g) * (1 + scale1) + shift1
    mix, k, v = mixing_sublayer(h, rope, ctx_k, ctx_v, w_in, pool_w, pool_scale, q_norm_g, k_norm_g,
                                sgu_w, sgu_b, w_br_pool, w_br_fourier, w_br_attn, w_br_sgu, w_out)
    x = x + gate1 * mix
    h = rms_norm(x, norm_ffn_g) * (1 + scale2) + shift2
    x = x + gate2 * moe_ffn(h, router_w, router_b, moe_w_gate, moe_b_gate, moe_w_up, moe_b_up,
                            moe_w_down, moe_b_down)
    return x, k, v


def setup_inputs(seed: int = 0) -> dict:
    key = jax.random.key(seed)
    ks = jax.random.split(key, 40)

    def nrm(k, shape, scale):
        return jax.random.normal(k, shape, jnp.float32) * scale

    L, D, E, F = DEPTH, D_MODEL, N_EXPERTS, D_FF
    return {
        "x_prompt": nrm(ks[0], (BATCH, SEQ, D), 1.0),
        "x_sample": nrm(ks[1], (DEC_BATCH, DEC_SEQ, D), 1.0),
        "cache_k": nrm(ks[2], (DEC_BATCH, DEPTH, PAST_LEN, N_KV_HEADS, HEAD_DIM), 1.0),
        "cache_v": nrm(ks[3], (DEC_BATCH, DEPTH, PAST_LEN, N_KV_HEADS, HEAD_DIM), 1.0),
        "c": nrm(ks[4], (DEC_BATCH, D), 1.0),
        "c_ctx": nrm(ks[5], (D,), 1.0),
        "w_mod": nrm(ks[6], (L, D, 6 * D), 0.5 * D ** -0.5),
        "b_mod": nrm(ks[7], (L, 6 * D), 0.02),
        "norm_mix_g": 1.0 + nrm(ks[8], (L, D), 0.05),
        "norm_ffn_g": 1.0 + nrm(ks[9], (L, D), 0.05),
        "w_in": nrm(ks[10], (L, D, IN_COLS), D ** -0.5),
        "pool_w": nrm(ks[11], (L, POOL_GROUPS, POOL_CH, POOL_CH), POOL_CH ** -0.5),
        "pool_scale": 1.0 + nrm(ks[12], (L, POOL_WIDTH), 0.05),
        "q_norm_g": 1.0 + nrm(ks[13], (L, HEAD_DIM), 0.05),
        "k_norm_g": 1.0 + nrm(ks[14], (L, HEAD_DIM), 0.05),
        "sgu_w": nrm(ks[15], (L, SGU_GROUPS, SGU_CHUNK, SGU_CHUNK), SGU_CHUNK ** -0.5),
        "sgu_b": 1.0 + nrm(ks[16], (L, SGU_GROUPS, SGU_CHUNK), 0.02),
        "w_br_pool": nrm(ks[17], (L, POOL_WIDTH, D), POOL_WIDTH ** -0.5),
        "w_br_fourier": nrm(ks[18], (L, FFT_WIDTH, D), FFT_WIDTH ** -0.5),
        "w_br_attn": nrm(ks[19], (L, ATTN_WIDTH, D), ATTN_WIDTH ** -0.5),
        "w_br_sgu": nrm(ks[20], (L, SGU_WIDTH, D), SGU_WIDTH ** -0.5),
        "w_out": nrm(ks[21], (L, D, D), D ** -0.5),
        "router_w": nrm(ks[22], (L, D, E), D ** -0.5),
        "router_b": nrm(ks[23], (L, E), 0.01),
        "moe_w_gate": nrm(ks[24], (L, E, D, F), D ** -0.5),
        "moe_b_gate": nrm(ks[25], (L, E, F), 0.02),
        "moe_w_up": nrm(ks[26], (L, E, D, F), D ** -0.5),
        "moe_b_up": nrm(ks[27], (L, E, F), 0.02),
        "moe_w_down": nrm(ks[28], (L, E, F, D), F ** -0.5),
        "moe_b_down": nrm(ks[29], (L, E, D), 0.02),
        "final_norm_g": 1.0 + nrm(ks[30], (D,), 0.05),
    }


def reference(x_prompt, x_sample, cache_k, cache_v, c, c_ctx, w_mod, b_mod, norm_mix_g, norm_ffn_g,
              w_in, pool_w, pool_scale, q_norm_g, k_norm_g, sgu_w, sgu_b, w_br_pool, w_br_fourier,
              w_br_attn, w_br_sgu, w_out, router_w, router_b, moe_w_gate, moe_b_gate, moe_w_up,
              moe_b_up, moe_w_down, moe_b_down, final_norm_g):
    layer_params = (w_mod, b_mod, norm_mix_g, norm_ffn_g, w_in, pool_w, pool_scale, q_norm_g,
                    k_norm_g, sgu_w, sgu_b, w_br_pool, w_br_fourier, w_br_attn, w_br_sgu, w_out,
                    router_w, router_b, moe_w_gate, moe_b_gate, moe_w_up, moe_b_up, moe_w_down,
                    moe_b_down)

    cond_ctx = c_ctx[None, :]
    y = x_prompt
    new_k, new_v = [], []
    for l in range(DEPTH):
        y, k_l, v_l = trunk_layer(y, cond_ctx, None, None, None, *[p[l] for p in layer_params])
        new_k.append(k_l)
        new_v.append(v_l)
    y_prompt = rms_norm(y, final_norm_g)
    new_cache_k = jnp.stack(new_k, axis=1)
    new_cache_v = jnp.stack(new_v, axis=1)

    rope = axial_rope_tables(x_sample.shape[1])
    z = x_sample
    for l in range(DEPTH):
        z, _, _ = trunk_layer(z, c, rope, cache_k[:, l], cache_v[:, l], *[p[l] for p in layer_params])
    y_sample = rms_norm(z, final_norm_g)

    return (y_prompt, y_sample, new_cache_k, new_cache_v)
```

```python
import functools
import math

import numpy as np
import jax
import jax.numpy as jnp
from jax import lax
from jax.experimental import pallas as pl
from jax.experimental.pallas import tpu as pltpu

F32 = jnp.float32
BF16 = jnp.bfloat16

D_MODEL = 1024
BATCH = 32
SEQ = 256
DEPTH = 2
DEC_BATCH = 2
DEC_SEQ = 1024
PAST_LEN = 256
GRID_W = 64
EPS = 1e-6
POOL_GROUPS = 4
POOL_CH = 96
POOL_WIDTH = 384
POOL_WINDOWS = (2, 4, 8, 16)
POOL_PAD = 16
FFT_CH = 96
FFT_WIDTH = 384
N_HEADS = 8
N_KV_HEADS = 2
HEAD_DIM = 64
ATTN_WIDTH = 512
KV_WIDTH = 128
QKV_WIDTH = ATTN_WIDTH + 2 * KV_WIDTH
ROPE_THETA = 10000.0
ROPE_PAIRS = 16
SGU_GROUPS = 4
SGU_CH = 96
SGU_WIDTH = 384
SGU_CHUNK = 128
N_BRANCHES = 4
GATE_WIDTH = N_BRANCHES * D_MODEL
IN_COLS = POOL_WIDTH + FFT_WIDTH + QKV_WIDTH + 2 * SGU_WIDTH + GATE_WIDTH
N_EXPERTS = 32
TOP_K = 4
D_FF = 1024
SWIGLU_LIMIT = 7.0
SWIGLU_ALPHA = 1.702

T_CTX = BATCH * SEQ
T_LAT = DEC_BATCH * DEC_SEQ
T_ALL = T_CTX + T_LAT
N_COND = 8
VMEM_LIMIT = 56 * 1024 * 1024

COL_XP = 0
COL_XQ = COL_XP + POOL_WIDTH
COL_QKV = COL_XQ + FFT_WIDTH
COL_UV = COL_QKV + QKV_WIDTH
COL_GATE = COL_UV + 2 * SGU_WIDTH


def _params(*sem):
    return pltpu.CompilerParams(dimension_semantics=sem, vmem_limit_bytes=VMEM_LIMIT)


def _split_bf16(x):
    hi = x.astype(BF16)
    lo = (x - hi.astype(F32)).astype(BF16)
    return hi, lo


def _dot(a, b):
    return jnp.dot(a, b, preferred_element_type=F32)


def _dot_nt(a, b):
    return lax.dot_general(a, b, (((1,), (1,)), ((), ())), preferred_element_type=F32)


def _group_lane_select(lane, vals, width):
    out = vals[-1]
    for g in range(len(vals) - 2, -1, -1):
        out = jnp.where(lane < (g + 1) * width, vals[g], out)
    return out


def _cond_row(blk, blocks_ctx, blocks_per_latent):
    return jnp.where(blk < blocks_ctx, 0, 1 + (blk - blocks_ctx) // blocks_per_latent)


MOD_TN = 1536


def _mod_kernel(c_ref, w_ref, b_ref, o_ref):
    c = c_ref[...]
    s = c * jax.nn.sigmoid(c)
    sh, sl = _split_bf16(s)
    wh, wl = _split_bf16(w_ref[...])
    o_ref[...] = _dot(sh, wh) + _dot(sh, wl) + _dot(sl, wh) + b_ref[...]


def _modulation(cond, w_mod, b_mod):
    n_cols = 6 * D_MODEL
    return pl.pallas_call(
        _mod_kernel,
        grid=(DEPTH, n_cols // MOD_TN),
        in_specs=[
            pl.BlockSpec((N_COND, D_MODEL), lambda l, j: (0, 0)),
            pl.BlockSpec((None, D_MODEL, MOD_TN), lambda l, j: (l, 0, j)),
            pl.BlockSpec((None, 1, MOD_TN), lambda l, j: (l, 0, j)),
        ],
        out_specs=pl.BlockSpec((None, N_COND, MOD_TN), lambda l, j: (l, 0, j)),
        out_shape=jax.ShapeDtypeStruct((DEPTH, N_COND, n_cols), F32),
        compiler_params=_params("arbitrary", "arbitrary"),
        name="modulation",
    )(cond, w_mod, b_mod.reshape(DEPTH, 1, n_cols))


INPROJ_TM = 256
INPROJ_SEGMENTS = (
    (COL_XP, POOL_WIDTH), (COL_XQ, FFT_WIDTH), (COL_QKV, QKV_WIDTH),
    (COL_UV, 2 * SGU_WIDTH), (COL_GATE, GATE_WIDTH))
INPROJ_CHUNK = 1024


def _ada_norm(x, g, shift, scale):
    xn = x * lax.rsqrt(jnp.mean(x * x, axis=-1, keepdims=True) + EPS)
    return xn * g * (1.0 + scale) + shift


def _inproj_kernel(x_ref, mod_ref, g_ref, w_ref, *out_refs):
    h = _ada_norm(x_ref[...], g_ref[...], mod_ref[0:1, :], mod_ref[1:2, :])
    hb = h.astype(BF16)
    for (col, width), o_ref in zip(INPROJ_SEGMENTS, out_refs):
        for c0 in range(0, width, INPROJ_CHUNK):
            c1 = min(c0 + INPROJ_CHUNK, width)
            o_ref[:, c0:c1] = _dot(hb, w_ref[:, col + c0:col + c1])


def _in_projection(x, mod_l, g, w_in_bf16):
    tm = INPROJ_TM
    row = functools.partial(_cond_row, blocks_ctx=T_CTX // tm, blocks_per_latent=DEC_SEQ // tm)
    return pl.pallas_call(
        _inproj_kernel,
        grid=(T_ALL // tm,),
        in_specs=[
            pl.BlockSpec((tm, D_MODEL), lambda i: (i, 0)),
            pl.BlockSpec((None, 6, D_MODEL), lambda i: (row(i), 0, 0)),
            pl.BlockSpec((1, D_MODEL), lambda i: (0, 0)),
            pl.BlockSpec((D_MODEL, IN_COLS), lambda i: (0, 0)),
        ],
        out_specs=[pl.BlockSpec((tm, w), lambda i: (i, 0)) for _, w in INPROJ_SEGMENTS],
        out_shape=[jax.ShapeDtypeStruct((T_ALL, w), F32) for _, w in INPROJ_SEGMENTS],
        compiler_params=_params("arbitrary"),
        name="in_projection",
    )(x, mod_l, g.reshape(1, D_MODEL), w_in_bf16)


def _pool_mixer(xp, invcnt, w_bd, scale):
    s = xp.shape[0]
    n = s + 2 * POOL_PAD
    zeros = jnp.zeros((POOL_PAD, POOL_WIDTH), F32)
    xe = jnp.concatenate([zeros, xp, zeros], axis=0)

    def shift(a, k):
        return pltpu.roll(a, k % n, 0)

    s2 = xe + shift(xe, 1)
    s4 = shift(s2, 1) + shift(s2, -1)
    s8 = shift(s4, 2) + shift(s4, -2)
    s16 = shift(s8, 4) + shift(s8, -4)
    lane = lax.broadcasted_iota(jnp.int32, (1, POOL_WIDTH), 1)
    total = _group_lane_select(lane, [s2, s4, s8, s16], POOL_CH)[POOL_PAD:POOL_PAD + s]
    pooled = total * invcnt - xp
    return _dot(pooled.astype(BF16), w_bd) * scale


def _fourier_mixer(xq, f_cos_ch, f_sin_ch, f_pos):
    xb = xq.astype(BF16)
    a = _dot(xb, f_cos_ch).astype(BF16)
    b = _dot(xb, f_sin_ch).astype(BF16)
    return _dot(f_pos, jnp.concatenate([a, b], axis=0))


def _group_mean_sq(x, ones_bd, width):
    hi, lo = _split_bf16(x * x)
    return (_dot(hi, ones_bd) + _dot(lo, ones_bd)) * (1.0 / width)


def _mixers_kernel(xp_ref, xq_ref, uv_ref, invcnt_ref, pool_w_ref, pool_s_ref, fcc_ref, fsc_ref,
                   fpos_ref, ones_ref, sgu_w_ref, sgu_b_ref, pool_o, four_o, sgu_o):
    s = xp_ref.shape[0]
    pool_o[...] = _pool_mixer(xp_ref[...], invcnt_ref[...], pool_w_ref[...],
                              pool_s_ref[...]).astype(BF16)
    four_o[...] = _fourier_mixer(xq_ref[...], fcc_ref[...], fsc_ref[...], fpos_ref[...]).astype(BF16)

    act = jax.nn.gelu(uv_ref[...], approximate=True)
    u = act[:, :SGU_WIDTH]
    v = act[:, SGU_WIDTH:]
    vg = (v * lax.rsqrt(_group_mean_sq(v, ones_ref[...], SGU_CH) + EPS)).astype(BF16)
    lane = lax.broadcasted_iota(jnp.int32, (1, SGU_WIDTH), 1)
    w_stack = sgu_w_ref[...]
    bias = sgu_b_ref[...]
    for n in range(s // SGU_CHUNK):
        rows = slice(n * SGU_CHUNK, (n + 1) * SGU_CHUNK)
        r = _dot(w_stack, vg[rows])
        per_group = [r[g * SGU_CHUNK:(g + 1) * SGU_CHUNK] for g in range(SGU_GROUPS)]
        spatial = _group_lane_select(lane, per_group, SGU_CH) + bias
        sgu_o[rows, :] = (u[rows] * spatial).astype(BF16)


def _mixers(xp, xq, uv, consts, lw, seq, n_seq, block0):
    full = lambda shape: pl.BlockSpec(shape, lambda b: (0,) * len(shape))
    tok = lambda w: pl.BlockSpec((seq, w), lambda b: (block0 + b, 0))
    out = lambda: pl.BlockSpec((seq, POOL_WIDTH), lambda b: (b, 0))
    return pl.pallas_call(
        _mixers_kernel,
        grid=(n_seq,),
        in_specs=[
            tok(POOL_WIDTH), tok(FFT_WIDTH), tok(2 * SGU_WIDTH),
            full((seq, POOL_WIDTH)), full((POOL_WIDTH, POOL_WIDTH)), full((1, POOL_WIDTH)),
            full((FFT_WIDTH, FFT_WIDTH)), full((FFT_WIDTH, FFT_WIDTH)), full((seq, 2 * seq)),
            full((SGU_WIDTH, SGU_WIDTH)), full((SGU_GROUPS * SGU_CHUNK, SGU_CHUNK)),
            full((SGU_CHUNK, SGU_WIDTH)),
        ],
        out_specs=[out(), out(), out()],
        out_shape=[jax.ShapeDtypeStruct((n_seq * seq, POOL_WIDTH), BF16)] * 3,
        compiler_params=_params("arbitrary"),
        name=f"mixers_s{seq}",
    )(xp, xq, uv, consts["invcnt"], lw["pool_w_bd"], lw["pool_scale"], consts["f_cos_ch"],
      consts["f_sin_ch"], consts["f_pos"], consts["ones96"], lw["sgu_w_stack"], lw["sgu_bias"])


def _head_norm(x, ones_bd, g):
    return x * lax.rsqrt(_group_mean_sq(x, ones_bd, HEAD_DIM) + EPS) * g


def _rope(x, cos, sin_next, sin_prev):
    cols = []
    for c in range(x.shape[1] // 128):
        xc = x[:, c * 128:(c + 1) * 128]
        nxt = pltpu.roll(xc, 127, 1)
        prv = pltpu.roll(xc, 1, 1)
        cols.append(xc * cos + nxt * sin_next + prv * sin_prev)
    return cols[0] if len(cols) == 1 else jnp.concatenate(cols, axis=1)


def _attend(q, keys, vals, o_ref):
    qb = (q * (HEAD_DIM ** -0.5)).astype(BF16)
    group = N_HEADS // N_KV_HEADS
    for h in range(N_HEADS):
        j = h // group
        kh = keys[:, j * HEAD_DIM:(j + 1) * HEAD_DIM]
        vh = vals[:, j * HEAD_DIM:(j + 1) * HEAD_DIM]
        s = _dot_nt(qb[:, h * HEAD_DIM:(h + 1) * HEAD_DIM], kh)
        p = jnp.exp(s - jnp.max(s, axis=-1, keepdims=True))
        denom = jnp.sum(p, axis=-1, keepdims=True)
        o = _dot(p.astype(BF16), vh) / denom
        o_ref[:, h * HEAD_DIM:(h + 1) * HEAD_DIM] = o.astype(BF16)


def _attn_ctx_kernel(qkv_ref, gq_ref, gk_ref, ones_ref, o_ref, k_ref):
    qkv = qkv_ref[...]
    ones = ones_ref[...]
    q = _head_norm(qkv[:, :ATTN_WIDTH], ones, gq_ref[...])
    k = _head_norm(qkv[:, ATTN_WIDTH:ATTN_WIDTH + KV_WIDTH], ones[:KV_WIDTH, :KV_WIDTH], gk_ref[...])
    k_ref[...] = k
    _attend(q, k.astype(BF16), qkv[:, ATTN_WIDTH + KV_WIDTH:].astype(BF16), o_ref)


def _attention_ctx(qkv, consts, lw):
    full = lambda shape: pl.BlockSpec(shape, lambda b: (0,) * len(shape))
    return pl.pallas_call(
        _attn_ctx_kernel,
        grid=(BATCH,),
        in_specs=[pl.BlockSpec((SEQ, QKV_WIDTH), lambda b: (b, 0)),
                  full((1, ATTN_WIDTH)), full((1, KV_WIDTH)), full((ATTN_WIDTH, ATTN_WIDTH))],
        out_specs=[pl.BlockSpec((SEQ, ATTN_WIDTH), lambda b: (b, 0)),
                   pl.BlockSpec((SEQ, KV_WIDTH), lambda b: (b, 0))],
        out_shape=[jax.ShapeDtypeStruct((T_CTX, ATTN_WIDTH), BF16),
                   jax.ShapeDtypeStruct((T_CTX, KV_WIDTH), F32)],
        compiler_params=_params("arbitrary"),
        name="attention_ctx",
    )(qkv, lw["gq"], lw["gk"], consts["ones64"])


LAT_QBLK = 256


def _attn_lat_kernel(q_ref, kv_ref, ck_ref, cv_ref, gq_ref, gk_ref, ones_ref, cos_q, sn_q, sp_q,
                     cos_k, sn_k, sp_k, o_ref, keys, vals):
    ones = ones_ref[...]

    @pl.when(pl.program_id(1) == 0)
    def _():
        kv = kv_ref[...]
        k = _head_norm(kv[:, :KV_WIDTH], ones[:KV_WIDTH, :KV_WIDTH], gk_ref[...])
        keys[0:DEC_SEQ, :] = _rope(k, cos_k[...], sn_k[...], sp_k[...]).astype(BF16)
        keys[DEC_SEQ:, :] = ck_ref[...].astype(BF16)
        vals[0:DEC_SEQ, :] = kv[:, KV_WIDTH:].astype(BF16)
        vals[DEC_SEQ:, :] = cv_ref[...].astype(BF16)

    q = _head_norm(q_ref[...], ones, gq_ref[...])
    q = _rope(q, cos_q[...], sn_q[...], sp_q[...])
    _attend(q, keys[...], vals[...], o_ref)


def _attention_lat(qkv, cache_k_l, cache_v_l, consts, lw):
    nq = DEC_SEQ // LAT_QBLK
    q0 = T_CTX // LAT_QBLK
    s0 = T_CTX // DEC_SEQ
    full = lambda shape: pl.BlockSpec(shape, lambda b, j: (0,) * len(shape))
    rope_q = lambda: pl.BlockSpec((LAT_QBLK, 128), lambda b, j: (j, 0))
    rope_k = lambda: pl.BlockSpec((DEC_SEQ, 128), lambda b, j: (0, 0))
    return pl.pallas_call(
        _attn_lat_kernel,
        grid=(DEC_BATCH, nq),
        in_specs=[
            pl.BlockSpec((LAT_QBLK, ATTN_WIDTH), lambda b, j: (q0 + b * nq + j, 0)),
            pl.BlockSpec((DEC_SEQ, 2 * KV_WIDTH), lambda b, j: (s0 + b, ATTN_WIDTH // (2 * KV_WIDTH))),
            pl.BlockSpec((None, PAST_LEN, KV_WIDTH), lambda b, j: (b, 0, 0)),
            pl.BlockSpec((None, PAST_LEN, KV_WIDTH), lambda b, j: (b, 0, 0)),
            full((1, ATTN_WIDTH)), full((1, KV_WIDTH)), full((ATTN_WIDTH, ATTN_WIDTH)),
            rope_q(), rope_q(), rope_q(), rope_k(), rope_k(), rope_k(),
        ],
        out_specs=pl.BlockSpec((LAT_QBLK, ATTN_WIDTH), lambda b, j: (b * nq + j, 0)),
        out_shape=jax.ShapeDtypeStruct((T_LAT, ATTN_WIDTH), BF16),
        scratch_shapes=[pltpu.VMEM((DEC_SEQ + PAST_LEN, KV_WIDTH), BF16),
                        pltpu.VMEM((DEC_SEQ + PAST_LEN, KV_WIDTH), BF16)],
        compiler_params=_params("arbitrary", "arbitrary"),
        name="attention_lat",
    )(qkv, qkv, cache_k_l, cache_v_l, lw["gq"], lw["gk"], consts["ones64"],
      consts["rope_cos"], consts["rope_sin_next"], consts["rope_sin_prev"],
      consts["rope_cos"], consts["rope_sin_next"], consts["rope_sin_prev"])


MERGE_TM = 256


def _merge_kernel(pool_ref, four_ref, attn_ref, sgu_ref, gl_ref, x_ref, mod_ref, g_ref,
                  wp_ref, wf_ref, wa_ref, ws_ref, wo_ref, rw_ref, rb_ref,
                  x_o, h_o, comb_o):
    branches = (_dot(pool_ref[...], wp_ref[...]), _dot(four_ref[...], wf_ref[...]),
                _dot(attn_ref[...], wa_ref[...]), _dot(sgu_ref[...], ws_ref[...]))
    merged = None
    for i, br in enumerate(branches):
        term = jax.nn.sigmoid(gl_ref[:, i * D_MODEL:(i + 1) * D_MODEL]) * br
        merged = term if merged is None else merged + term
    mix = _dot(merged.astype(BF16), wo_ref[...])
    x = x_ref[...] + mod_ref[2:3, :] * mix
    x_o[...] = x
    h = _ada_norm(x, g_ref[...], mod_ref[3:4, :], mod_ref[4:5, :])
    h_o[...] = h.astype(h_o.dtype)

    hh, hl = _split_bf16(h)
    rh, rl = _split_bf16(rw_ref[...])
    logits = _dot(hh, rh) + _dot(hh, rl) + _dot(hl, rh) + rb_ref[...]
    lane = lax.broadcasted_iota(jnp.int32, logits.shape, 1).astype(F32)
    work = logits
    top = jnp.max(logits, axis=-1, keepdims=True)
    combine = jnp.zeros_like(logits)
    denom = jnp.zeros_like(top)
    for _ in range(TOP_K):
        m = jnp.max(work, axis=-1, keepdims=True)
        first = jnp.min(jnp.where(work == m, lane, float(N_EXPERTS)), axis=-1, keepdims=True)
        hit = lane == first
        e = jnp.exp(m - top)
        combine = jnp.where(hit, e, combine)
        denom = denom + e
        work = jnp.where(hit, -jnp.inf, work)
    comb_o[...] = combine / denom


def _merge(pool, four, attn, sgu, gl, x, mod_l, g_ffn, lw, h_dtype):
    tm = MERGE_TM
    row = functools.partial(_cond_row, blocks_ctx=T_CTX // tm, blocks_per_latent=DEC_SEQ // tm)
    full = lambda shape: pl.BlockSpec(shape, lambda i: (0,) * len(shape))
    tok = lambda w: pl.BlockSpec((tm, w), lambda i: (i, 0))
    return pl.pallas_call(
        _merge_kernel,
        grid=(T_ALL // tm,),
        in_specs=[
            tok(POOL_WIDTH), tok(FFT_WIDTH), tok(ATTN_WIDTH), tok(SGU_WIDTH), tok(GATE_WIDTH),
            tok(D_MODEL),
            pl.BlockSpec((None, 6, D_MODEL), lambda i: (row(i), 0, 0)),
            full((1, D_MODEL)),
            full((POOL_WIDTH, D_MODEL)), full((FFT_WIDTH, D_MODEL)), full((ATTN_WIDTH, D_MODEL)),
            full((SGU_WIDTH, D_MODEL)), full((D_MODEL, D_MODEL)),
            full((D_MODEL, N_EXPERTS)), full((1, N_EXPERTS)),
        ],
        out_specs=[tok(D_MODEL), tok(D_MODEL), tok(N_EXPERTS)],
        out_shape=[jax.ShapeDtypeStruct((T_ALL, D_MODEL), F32),
                   jax.ShapeDtypeStruct((T_ALL, D_MODEL), h_dtype),
                   jax.ShapeDtypeStruct((T_ALL, N_EXPERTS), F32)],
        compiler_params=_params("arbitrary"),
        name="merge_router",
    )(pool, four, attn, sgu, gl, x, mod_l, g_ffn.reshape(1, D_MODEL),
      lw["w_br_pool"], lw["w_br_fourier"], lw["w_br_attn"], lw["w_br_sgu"], lw["w_out"],
      lw["router_w"], lw["router_b"])


MOE_TM = 512


def _expert_ffn(xb, wg, bg, wu, bu, wd, bd):
    gate = jnp.minimum(_dot(xb, wg) + bg, SWIGLU_LIMIT)
    up = jnp.clip(_dot(xb, wu) + bu, -SWIGLU_LIMIT, SWIGLU_LIMIT)
    glu = gate * jax.nn.sigmoid(SWIGLU_ALPHA * gate)
    return _dot(((up + 1.0) * glu).astype(BF16), wd) + bd


def _moe_dense_kernel(h_ref, comb_ref, x_ref, mod_ref, fg_ref, wg_ref, bg_ref, wu_ref, bu_ref,
                      wd_ref, bd_ref, o_ref, acc, *, final_norm):
    e = pl.program_id(1)

    @pl.when(e == 0)
    def _():
        acc[...] = jnp.zeros_like(acc)

    y = _expert_ffn(h_ref[...], wg_ref[...].astype(BF16), bg_ref[...], wu_ref[...].astype(BF16),
                    bu_ref[...], wd_ref[...].astype(BF16), bd_ref[...])
    comb = comb_ref[...]
    lane = lax.broadcasted_iota(jnp.int32, comb.shape, 1)
    ce = jnp.sum(jnp.where(lane == e, comb, 0.0), axis=-1, keepdims=True)
    acc[...] += ce * y

    @pl.when(e == N_EXPERTS - 1)
    def _():
        x = x_ref[...] + mod_ref[5:6, :] * acc[...]
        if final_norm:
            x = x * lax.rsqrt(jnp.mean(x * x, axis=-1, keepdims=True) + EPS) * fg_ref[...]
        o_ref[...] = x


def _moe_dense(h, comb, x, mod_l, final_g, l, p, final_norm):
    tm = MOE_TM
    row = functools.partial(_cond_row, blocks_ctx=T_CTX // tm, blocks_per_latent=DEC_SEQ // tm)
    tok = lambda w: pl.BlockSpec((tm, w), lambda i, e: (i, 0))
    wspec = lambda a, b: pl.BlockSpec((None, None, a, b), lambda i, e: (l, e, 0, 0))
    bias = lambda a: a.reshape(DEPTH, N_EXPERTS, 1, a.shape[-1])
    return pl.pallas_call(
        functools.partial(_moe_dense_kernel, final_norm=final_norm),
        grid=(T_ALL // tm, N_EXPERTS),
        in_specs=[
            tok(D_MODEL), tok(N_EXPERTS), tok(D_MODEL),
            pl.BlockSpec((None, 6, D_MODEL), lambda i, e: (row(i), 0, 0)),
            pl.BlockSpec((1, D_MODEL), lambda i, e: (0, 0)),
            wspec(D_MODEL, D_FF), wspec(1, D_FF), wspec(D_MODEL, D_FF), wspec(1, D_FF),
            wspec(D_FF, D_MODEL), wspec(1, D_MODEL),
        ],
        out_specs=tok(D_MODEL),
        out_shape=jax.ShapeDtypeStruct((T_ALL, D_MODEL), F32),
        scratch_shapes=[pltpu.VMEM((tm, D_MODEL), F32)],
        compiler_params=_params("arbitrary", "arbitrary"),
        name="moe_dense",
    )(h, comb, x, mod_l, final_g.reshape(1, D_MODEL), p["moe_w_gate"], bias(p["moe_b_gate"]),
      p["moe_w_up"], bias(p["moe_b_up"]), p["moe_w_down"], bias(p["moe_b_down"]))


def _block_diag(blocks):
    g, a, b = blocks.shape
    eye = jnp.eye(g, dtype=blocks.dtype)
    return (eye[:, None, :, None] * blocks[:, :, None, :]).reshape(g * a, g * b)


def _dft_tables(n):
    k = np.arange(n, dtype=np.int64)
    ang = 2.0 * np.pi * ((k[:, None] * k[None, :]) % n).astype(np.float64) / n
    return np.cos(ang) / math.sqrt(n), np.sin(ang) / math.sqrt(n)


def _seq_constants(seq):
    t = np.arange(seq)[:, None]
    win = np.array(POOL_WINDOWS)[None, :]
    lo = np.clip(t - win // 2, 0, seq)
    hi = np.clip(t - win // 2 + win, 0, seq)
    invcnt = np.repeat(1.0 / (hi - lo).astype(np.float64), POOL_CH, axis=1)
    cos_p, sin_p = _dft_tables(seq)
    cos_c, sin_c = _dft_tables(FFT_CH)
    eye = np.eye(POOL_GROUPS)
    as_bf16 = lambda a: jnp.asarray(a, F32).astype(BF16)
    return {
        "invcnt": jnp.asarray(invcnt, F32),
        "f_pos": as_bf16(np.concatenate([cos_p, -sin_p], axis=1)),
        "f_cos_ch": as_bf16(np.kron(eye, cos_c)),
        "f_sin_ch": as_bf16(np.kron(eye, sin_c)),
        "ones96": as_bf16(np.kron(eye, np.ones((SGU_CH, SGU_CH)))),
    }


def _rope_constants():
    rows = DEC_SEQ // GRID_W
    row = jnp.repeat(jnp.arange(rows, dtype=F32), GRID_W)
    col = jnp.tile(jnp.arange(GRID_W, dtype=F32), rows)
    freqs = ROPE_THETA ** (-jnp.arange(ROPE_PAIRS, dtype=F32) / ROPE_PAIRS)
    ang = jnp.stack([row[:, None] * freqs, col[:, None] * freqs], axis=1)
    ang = jnp.repeat(ang.reshape(DEC_SEQ, 2 * ROPE_PAIRS), 2, axis=1)
    ang = jnp.tile(ang, (1, 128 // HEAD_DIM))
    even = (jnp.arange(128) % 2 == 0)[None, :]
    sin = jnp.sin(ang)
    return {
        "rope_cos": jnp.cos(ang),
        "rope_sin_next": jnp.where(even, -sin, 0.0),
        "rope_sin_prev": jnp.where(even, 0.0, sin),
    }


def _layer_weights(p, l):
    group_of_lane = np.arange(SGU_WIDTH) // SGU_CH
    return {
        "w_in": p["w_in"][l].astype(BF16),
        "pool_w_bd": _block_diag(p["pool_w"][l]).astype(BF16),
        "pool_scale": p["pool_scale"][l].reshape(1, POOL_WIDTH),
        "sgu_w_stack": p["sgu_w"][l].reshape(SGU_GROUPS * SGU_CHUNK, SGU_CHUNK).astype(BF16),
        "sgu_bias": p["sgu_b"][l].T[:, group_of_lane],
        "gq": jnp.tile(p["q_norm_g"][l], N_HEADS).reshape(1, ATTN_WIDTH),
        "gk": jnp.tile(p["k_norm_g"][l], N_KV_HEADS).reshape(1, KV_WIDTH),
        "w_br_pool": p["w_br_pool"][l].astype(BF16),
        "w_br_fourier": p["w_br_fourier"][l].astype(BF16),
        "w_br_attn": p["w_br_attn"][l].astype(BF16),
        "w_br_sgu": p["w_br_sgu"][l].astype(BF16),
        "w_out": p["w_out"][l].astype(BF16),
        "router_w": p["router_w"][l],
        "router_b": p["router_b"][l].reshape(1, N_EXPERTS),
    }


def kernel(x_prompt, x_sample, cache_k, cache_v, c, c_ctx, w_mod, b_mod, norm_mix_g, norm_ffn_g, w_in, pool_w, pool_scale, q_norm_g, k_norm_g, sgu_w, sgu_b, w_br_pool, w_br_fourier, w_br_attn, w_br_sgu, w_out, router_w, router_b, moe_w_gate, moe_b_gate, moe_w_up, moe_b_up, moe_w_down, moe_b_down, final_norm_g):
    p = dict(w_in=w_in, pool_w=pool_w, pool_scale=pool_scale, q_norm_g=q_norm_g, k_norm_g=k_norm_g,
             sgu_w=sgu_w, sgu_b=sgu_b, w_br_pool=w_br_pool, w_br_fourier=w_br_fourier,
             w_br_attn=w_br_attn, w_br_sgu=w_br_sgu, w_out=w_out, router_w=router_w,
             router_b=router_b, moe_w_gate=moe_w_gate, moe_b_gate=moe_b_gate, moe_w_up=moe_w_up,
             moe_b_up=moe_b_up, moe_w_down=moe_w_down, moe_b_down=moe_b_down)

    cond = jnp.concatenate([c_ctx[None, :], c, jnp.zeros((N_COND - 1 - DEC_BATCH, D_MODEL), F32)])
    mod = _modulation(cond, w_mod, b_mod).reshape(DEPTH, N_COND, 6, D_MODEL)

    ones64 = jnp.asarray(np.kron(np.eye(N_HEADS), np.ones((HEAD_DIM, HEAD_DIM))), BF16)
    consts_ctx = dict(_seq_constants(SEQ), ones64=ones64)
    consts_lat = dict(_seq_constants(DEC_SEQ), ones64=ones64, **_rope_constants())
    ck = cache_k.reshape(DEC_BATCH, DEPTH, PAST_LEN, KV_WIDTH)
    cv = cache_v.reshape(DEC_BATCH, DEPTH, PAST_LEN, KV_WIDTH)

    x = jnp.concatenate([x_prompt.reshape(T_CTX, D_MODEL), x_sample.reshape(T_LAT, D_MODEL)])
    new_k, new_v = [], []
    for l in range(DEPTH):
        lw = _layer_weights(p, l)
        xp, xq, qkv, uv, gl = _in_projection(x, mod[l], norm_mix_g[l], lw["w_in"])
        pool_c, four_c, sgu_c = _mixers(xp, xq, uv, consts_ctx, lw, SEQ, BATCH, 0)
        pool_l, four_l, sgu_l = _mixers(xp, xq, uv, consts_lat, lw, DEC_SEQ, DEC_BATCH,
                                        T_CTX // DEC_SEQ)
        attn_c, k_c = _attention_ctx(qkv, consts_ctx, lw)
        attn_l = _attention_lat(qkv, ck[:, l], cv[:, l], consts_lat, lw)
        new_k.append(k_c.reshape(BATCH, SEQ, N_KV_HEADS, HEAD_DIM))
        new_v.append(qkv[:T_CTX, ATTN_WIDTH + KV_WIDTH:].reshape(BATCH, SEQ, N_KV_HEADS, HEAD_DIM))
        cat = lambda a, b: jnp.concatenate([a, b], axis=0)
        x, h, comb = _merge(cat(pool_c, pool_l), cat(four_c, four_l), cat(attn_c, attn_l),
                            cat(sgu_c, sgu_l), gl, x, mod[l], norm_ffn_g[l], lw, BF16)
        x = _moe_dense(h, comb, x, mod[l], final_norm_g, l, p, final_norm=(l == DEPTH - 1))

    y_prompt = x[:T_CTX].reshape(BATCH, SEQ, D_MODEL)
    y_sample = x[T_CTX:].reshape(DEC_BATCH, DEC_SEQ, D_MODEL)
    return (y_prompt, y_sample, jnp.stack(new_k, axis=1), jnp.stack(new_v, axis=1))
```

```python
import functools
import math

import numpy as np
import jax
import jax.numpy as jnp
from jax import lax
from jax.experimental import pallas as pl
from jax.experimental.pallas import tpu as pltpu

F32 = jnp.float32
BF16 = jnp.bfloat16

D_MODEL = 1024
BATCH = 32
SEQ = 256
DEPTH = 2
DEC_BATCH = 2
DEC_SEQ = 1024
PAST_LEN = 256
GRID_W = 64
EPS = 1e-6
POOL_GROUPS = 4
POOL_CH = 96
POOL_WIDTH = 384
POOL_WINDOWS = (2, 4, 8, 16)
POOL_PAD = 16
FFT_CH = 96
FFT_WIDTH = 384
N_HEADS = 8
N_KV_HEADS = 2
HEAD_DIM = 64
ATTN_WIDTH = 512
KV_WIDTH = 128
QKV_WIDTH = ATTN_WIDTH + 2 * KV_WIDTH
ROPE_THETA = 10000.0
ROPE_PAIRS = 16
SGU_GROUPS = 4
SGU_CH = 96
SGU_WIDTH = 384
SGU_CHUNK = 128
N_BRANCHES = 4
GATE_WIDTH = N_BRANCHES * D_MODEL
IN_COLS = POOL_WIDTH + FFT_WIDTH + QKV_WIDTH + 2 * SGU_WIDTH + GATE_WIDTH
N_EXPERTS = 32
TOP_K = 4
D_FF = 1024
SWIGLU_LIMIT = 7.0
SWIGLU_ALPHA = 1.702

ROW_TILES = D_MODEL // 128

T_CTX = BATCH * SEQ
T_LAT = DEC_BATCH * DEC_SEQ
T_ALL = T_CTX + T_LAT
N_COND = 8
VMEM_LIMIT = 56 * 1024 * 1024

COL_XP = 0
COL_XQ = COL_XP + POOL_WIDTH
COL_QKV = COL_XQ + FFT_WIDTH
COL_UV = COL_QKV + QKV_WIDTH
COL_GATE = COL_UV + 2 * SGU_WIDTH


def _params(*sem):
    return pltpu.CompilerParams(dimension_semantics=sem, vmem_limit_bytes=VMEM_LIMIT)


def _split_bf16(x):
    hi = x.astype(BF16)
    lo = (x - hi.astype(F32)).astype(BF16)
    return hi, lo


def _dot(a, b):
    return jnp.dot(a, b, preferred_element_type=F32)


def _dot_nt(a, b):
    return lax.dot_general(a, b, (((1,), (1,)), ((), ())), preferred_element_type=F32)


def _group_lane_select(lane, vals, width):
    out = vals[-1]
    for g in range(len(vals) - 2, -1, -1):
        out = jnp.where(lane < (g + 1) * width, vals[g], out)
    return out


def _cond_row(blk, blocks_ctx, blocks_per_latent):
    return jnp.where(blk < blocks_ctx, 0, 1 + (blk - blocks_ctx) // blocks_per_latent)


MOD_TN = 1536


def _mod_kernel(c_ref, w_ref, b_ref, o_ref):
    c = c_ref[...]
    s = c * jax.nn.sigmoid(c)
    sh, sl = _split_bf16(s)
    wh, wl = _split_bf16(w_ref[...])
    o_ref[...] = _dot(sh, wh) + _dot(sh, wl) + _dot(sl, wh) + b_ref[...]


def _modulation(cond, w_mod, b_mod):
    n_cols = 6 * D_MODEL
    return pl.pallas_call(
        _mod_kernel,
        grid=(DEPTH, n_cols // MOD_TN),
        in_specs=[
            pl.BlockSpec((N_COND, D_MODEL), lambda l, j: (0, 0)),
            pl.BlockSpec((None, D_MODEL, MOD_TN), lambda l, j: (l, 0, j)),
            pl.BlockSpec((None, 1, MOD_TN), lambda l, j: (l, 0, j)),
        ],
        out_specs=pl.BlockSpec((None, N_COND, MOD_TN), lambda l, j: (l, 0, j)),
        out_shape=jax.ShapeDtypeStruct((DEPTH, N_COND, n_cols), F32),
        compiler_params=_params("arbitrary", "arbitrary"),
        name="modulation",
    )(cond, w_mod, b_mod.reshape(DEPTH, 1, n_cols))


INPROJ_TM = 256
INPROJ_SEGMENTS = (
    (COL_XP, POOL_WIDTH), (COL_XQ, FFT_WIDTH), (COL_QKV, QKV_WIDTH),
    (COL_UV, 2 * SGU_WIDTH), (COL_GATE, GATE_WIDTH))
INPROJ_CHUNK = 1024


def _ada_norm(x, g, shift, scale):
    xn = x * lax.rsqrt(jnp.mean(x * x, axis=-1, keepdims=True) + EPS)
    return xn * g * (1.0 + scale) + shift


def _inproj_kernel(x_ref, mod_ref, g_ref, w_ref, *out_refs):
    h = _ada_norm(x_ref[...], g_ref[...], mod_ref[0:1, :], mod_ref[1:2, :])
    hb = h.astype(BF16)
    for (col, width), o_ref in zip(INPROJ_SEGMENTS, out_refs):
        for c0 in range(0, width, INPROJ_CHUNK):
            c1 = min(c0 + INPROJ_CHUNK, width)
            o_ref[:, c0:c1] = _dot(hb, w_ref[:, col + c0:col + c1])


def _in_projection(x, mod_l, g, w_in_bf16):
    tm = INPROJ_TM
    row = functools.partial(_cond_row, blocks_ctx=T_CTX // tm, blocks_per_latent=DEC_SEQ // tm)
    return pl.pallas_call(
        _inproj_kernel,
        grid=(T_ALL // tm,),
        in_specs=[
            pl.BlockSpec((tm, D_MODEL), lambda i: (i, 0)),
            pl.BlockSpec((None, 6, D_MODEL), lambda i: (row(i), 0, 0)),
            pl.BlockSpec((1, D_MODEL), lambda i: (0, 0)),
            pl.BlockSpec((D_MODEL, IN_COLS), lambda i: (0, 0)),
        ],
        out_specs=[pl.BlockSpec((tm, w), lambda i: (i, 0)) for _, w in INPROJ_SEGMENTS],
        out_shape=[jax.ShapeDtypeStruct((T_ALL, w), F32) for _, w in INPROJ_SEGMENTS],
        compiler_params=_params("arbitrary"),
        name="in_projection",
    )(x, mod_l, g.reshape(1, D_MODEL), w_in_bf16)


def _pool_mixer(xp, invcnt, w_bd, scale):
    s = xp.shape[0]
    n = s + 2 * POOL_PAD
    zeros = jnp.zeros((POOL_PAD, POOL_WIDTH), F32)
    xe = jnp.concatenate([zeros, xp, zeros], axis=0)

    def shift(a, k):
        return pltpu.roll(a, k % n, 0)

    s2 = xe + shift(xe, 1)
    s4 = shift(s2, 1) + shift(s2, -1)
    s8 = shift(s4, 2) + shift(s4, -2)
    s16 = shift(s8, 4) + shift(s8, -4)
    lane = lax.broadcasted_iota(jnp.int32, (1, POOL_WIDTH), 1)
    total = _group_lane_select(lane, [s2, s4, s8, s16], POOL_CH)[POOL_PAD:POOL_PAD + s]
    pooled = total * invcnt - xp
    return _dot(pooled.astype(BF16), w_bd) * scale


def _fourier_mixer(xq, f_cos_ch, f_sin_ch, f_pos):
    xb = xq.astype(BF16)
    a = _dot(xb, f_cos_ch).astype(BF16)
    b = _dot(xb, f_sin_ch).astype(BF16)
    return _dot(f_pos, jnp.concatenate([a, b], axis=0))


def _group_mean_sq(x, ones_bd, width):
    hi, lo = _split_bf16(x * x)
    return (_dot(hi, ones_bd) + _dot(lo, ones_bd)) * (1.0 / width)


def _mixers_kernel(xp_ref, xq_ref, uv_ref, invcnt_ref, pool_w_ref, pool_s_ref, fcc_ref, fsc_ref,
                   fpos_ref, ones_ref, sgu_w_ref, sgu_b_ref, pool_o, four_o, sgu_o):
    s = xp_ref.shape[0]
    pool_o[...] = _pool_mixer(xp_ref[...], invcnt_ref[...], pool_w_ref[...],
                              pool_s_ref[...]).astype(BF16)
    four_o[...] = _fourier_mixer(xq_ref[...], fcc_ref[...], fsc_ref[...], fpos_ref[...]).astype(BF16)

    act = jax.nn.gelu(uv_ref[...], approximate=True)
    u = act[:, :SGU_WIDTH]
    v = act[:, SGU_WIDTH:]
    vg = (v * lax.rsqrt(_group_mean_sq(v, ones_ref[...], SGU_CH) + EPS)).astype(BF16)
    lane = lax.broadcasted_iota(jnp.int32, (1, SGU_WIDTH), 1)
    w_stack = sgu_w_ref[...]
    bias = sgu_b_ref[...]
    for n in range(s // SGU_CHUNK):
        rows = slice(n * SGU_CHUNK, (n + 1) * SGU_CHUNK)
        r = _dot(w_stack, vg[rows])
        per_group = [r[g * SGU_CHUNK:(g + 1) * SGU_CHUNK] for g in range(SGU_GROUPS)]
        spatial = _group_lane_select(lane, per_group, SGU_CH) + bias
        sgu_o[rows, :] = (u[rows] * spatial).astype(BF16)


def _mixers(xp, xq, uv, consts, lw, seq, n_seq, block0):
    full = lambda shape: pl.BlockSpec(shape, lambda b: (0,) * len(shape))
    tok = lambda w: pl.BlockSpec((seq, w), lambda b: (block0 + b, 0))
    out = lambda: pl.BlockSpec((seq, POOL_WIDTH), lambda b: (b, 0))
    return pl.pallas_call(
        _mixers_kernel,
        grid=(n_seq,),
        in_specs=[
            tok(POOL_WIDTH), tok(FFT_WIDTH), tok(2 * SGU_WIDTH),
            full((seq, POOL_WIDTH)), full((POOL_WIDTH, POOL_WIDTH)), full((1, POOL_WIDTH)),
            full((FFT_WIDTH, FFT_WIDTH)), full((FFT_WIDTH, FFT_WIDTH)), full((seq, 2 * seq)),
            full((SGU_WIDTH, SGU_WIDTH)), full((SGU_GROUPS * SGU_CHUNK, SGU_CHUNK)),
            full((SGU_CHUNK, SGU_WIDTH)),
        ],
        out_specs=[out(), out(), out()],
        out_shape=[jax.ShapeDtypeStruct((n_seq * seq, POOL_WIDTH), BF16)] * 3,
        compiler_params=_params("arbitrary"),
        name=f"mixers_s{seq}",
    )(xp, xq, uv, consts["invcnt"], lw["pool_w_bd"], lw["pool_scale"], consts["f_cos_ch"],
      consts["f_sin_ch"], consts["f_pos"], consts["ones96"], lw["sgu_w_stack"], lw["sgu_bias"])


def _head_norm(x, ones_bd, g):
    return x * lax.rsqrt(_group_mean_sq(x, ones_bd, HEAD_DIM) + EPS) * g


def _rope(x, cos, sin_next, sin_prev):
    cols = []
    for c in range(x.shape[1] // 128):
        xc = x[:, c * 128:(c + 1) * 128]
        nxt = pltpu.roll(xc, 127, 1)
        prv = pltpu.roll(xc, 1, 1)
        cols.append(xc * cos + nxt * sin_next + prv * sin_prev)
    return cols[0] if len(cols) == 1 else jnp.concatenate(cols, axis=1)


def _attend(q, keys, vals, o_ref):
    qb = (q * (HEAD_DIM ** -0.5)).astype(BF16)
    group = N_HEADS // N_KV_HEADS
    for h in range(N_HEADS):
        j = h // group
        kh = keys[:, j * HEAD_DIM:(j + 1) * HEAD_DIM]
        vh = vals[:, j * HEAD_DIM:(j + 1) * HEAD_DIM]
        s = _dot_nt(qb[:, h * HEAD_DIM:(h + 1) * HEAD_DIM], kh)
        p = jnp.exp(s - jnp.max(s, axis=-1, keepdims=True))
        denom = jnp.sum(p, axis=-1, keepdims=True)
        o = _dot(p.astype(BF16), vh) / denom
        o_ref[:, h * HEAD_DIM:(h + 1) * HEAD_DIM] = o.astype(BF16)


def _attn_ctx_kernel(qkv_ref, gq_ref, gk_ref, ones_ref, o_ref, k_ref):
    qkv = qkv_ref[...]
    ones = ones_ref[...]
    q = _head_norm(qkv[:, :ATTN_WIDTH], ones, gq_ref[...])
    k = _head_norm(qkv[:, ATTN_WIDTH:ATTN_WIDTH + KV_WIDTH], ones[:KV_WIDTH, :KV_WIDTH], gk_ref[...])
    k_ref[...] = k
    _attend(q, k.astype(BF16), qkv[:, ATTN_WIDTH + KV_WIDTH:].astype(BF16), o_ref)


def _attention_ctx(qkv, consts, lw):
    full = lambda shape: pl.BlockSpec(shape, lambda b: (0,) * len(shape))
    return pl.pallas_call(
        _attn_ctx_kernel,
        grid=(BATCH,),
        in_specs=[pl.BlockSpec((SEQ, QKV_WIDTH), lambda b: (b, 0)),
                  full((1, ATTN_WIDTH)), full((1, KV_WIDTH)), full((ATTN_WIDTH, ATTN_WIDTH))],
        out_specs=[pl.BlockSpec((SEQ, ATTN_WIDTH), lambda b: (b, 0)),
                   pl.BlockSpec((SEQ, KV_WIDTH), lambda b: (b, 0))],
        out_shape=[jax.ShapeDtypeStruct((T_CTX, ATTN_WIDTH), BF16),
                   jax.ShapeDtypeStruct((T_CTX, KV_WIDTH), F32)],
        compiler_params=_params("arbitrary"),
        name="attention_ctx",
    )(qkv, lw["gq"], lw["gk"], consts["ones64"])


LAT_QBLK = 256


def _attn_lat_kernel(q_ref, kv_ref, ck_ref, cv_ref, gq_ref, gk_ref, ones_ref, cos_q, sn_q, sp_q,
                     cos_k, sn_k, sp_k, o_ref, keys, vals):
    ones = ones_ref[...]

    @pl.when(pl.program_id(1) == 0)
    def _():
        kv = kv_ref[...]
        k = _head_norm(kv[:, :KV_WIDTH], ones[:KV_WIDTH, :KV_WIDTH], gk_ref[...])
        keys[0:DEC_SEQ, :] = _rope(k, cos_k[...], sn_k[...], sp_k[...]).astype(BF16)
        keys[DEC_SEQ:, :] = ck_ref[...].astype(BF16)
        vals[0:DEC_SEQ, :] = kv[:, KV_WIDTH:].astype(BF16)
        vals[DEC_SEQ:, :] = cv_ref[...].astype(BF16)

    q = _head_norm(q_ref[...], ones, gq_ref[...])
    q = _rope(q, cos_q[...], sn_q[...], sp_q[...])
    _attend(q, keys[...], vals[...], o_ref)


def _attention_lat(qkv, cache_k_l, cache_v_l, consts, lw):
    nq = DEC_SEQ // LAT_QBLK
    q0 = T_CTX // LAT_QBLK
    s0 = T_CTX // DEC_SEQ
    full = lambda shape: pl.BlockSpec(shape, lambda b, j: (0,) * len(shape))
    rope_q = lambda: pl.BlockSpec((LAT_QBLK, 128), lambda b, j: (j, 0))
    rope_k = lambda: pl.BlockSpec((DEC_SEQ, 128), lambda b, j: (0, 0))
    return pl.pallas_call(
        _attn_lat_kernel,
        grid=(DEC_BATCH, nq),
        in_specs=[
            pl.BlockSpec((LAT_QBLK, ATTN_WIDTH), lambda b, j: (q0 + b * nq + j, 0)),
            pl.BlockSpec((DEC_SEQ, 2 * KV_WIDTH), lambda b, j: (s0 + b, ATTN_WIDTH // (2 * KV_WIDTH))),
            pl.BlockSpec((None, PAST_LEN, KV_WIDTH), lambda b, j: (b, 0, 0)),
            pl.BlockSpec((None, PAST_LEN, KV_WIDTH), lambda b, j: (b, 0, 0)),
            full((1, ATTN_WIDTH)), full((1, KV_WIDTH)), full((ATTN_WIDTH, ATTN_WIDTH)),
            rope_q(), rope_q(), rope_q(), rope_k(), rope_k(), rope_k(),
        ],
        out_specs=pl.BlockSpec((LAT_QBLK, ATTN_WIDTH), lambda b, j: (b * nq + j, 0)),
        out_shape=jax.ShapeDtypeStruct((T_LAT, ATTN_WIDTH), BF16),
        scratch_shapes=[pltpu.VMEM((DEC_SEQ + PAST_LEN, KV_WIDTH), BF16),
                        pltpu.VMEM((DEC_SEQ + PAST_LEN, KV_WIDTH), BF16)],
        compiler_params=_params("arbitrary", "arbitrary"),
        name="attention_lat",
    )(qkv, qkv, cache_k_l, cache_v_l, lw["gq"], lw["gk"], consts["ones64"],
      consts["rope_cos"], consts["rope_sin_next"], consts["rope_sin_prev"],
      consts["rope_cos"], consts["rope_sin_next"], consts["rope_sin_prev"])


MERGE_TM = 256


def _merge_kernel(pool_ref, four_ref, attn_ref, sgu_ref, gl_ref, x_ref, mod_ref, g_ref,
                  wp_ref, wf_ref, wa_ref, ws_ref, wo_ref, rw_ref, rb_ref,
                  x_o, h_o, idx_o, wgt_o):
    branches = (_dot(pool_ref[...], wp_ref[...]), _dot(four_ref[...], wf_ref[...]),
                _dot(attn_ref[...], wa_ref[...]), _dot(sgu_ref[...], ws_ref[...]))
    merged = None
    for i, br in enumerate(branches):
        term = jax.nn.sigmoid(gl_ref[:, i * D_MODEL:(i + 1) * D_MODEL]) * br
        merged = term if merged is None else merged + term
    mix = _dot(merged.astype(BF16), wo_ref[...])
    x = x_ref[...] + mod_ref[2:3, :] * mix
    x_o[...] = x
    h = _ada_norm(x, g_ref[...], mod_ref[3:4, :], mod_ref[4:5, :])
    for s in range(ROW_TILES):
        h_o[:, s, :] = h[:, s * 128:(s + 1) * 128]

    hh, hl = _split_bf16(h)
    rh, rl = _split_bf16(rw_ref[...])
    logits = _dot(hh, rh) + _dot(hh, rl) + _dot(hl, rh) + rb_ref[...]
    lane = lax.broadcasted_iota(jnp.int32, logits.shape, 1).astype(F32)
    slot = lax.broadcasted_iota(jnp.int32, (logits.shape[0], TOP_K), 1)
    work = logits
    top = jnp.max(logits, axis=-1, keepdims=True)
    idx = jnp.zeros(slot.shape, F32)
    wgt = jnp.zeros(slot.shape, F32)
    denom = jnp.zeros_like(top)
    for k in range(TOP_K):
        m = jnp.max(work, axis=-1, keepdims=True)
        first = jnp.min(jnp.where(work == m, lane, float(N_EXPERTS)), axis=-1, keepdims=True)
        e = jnp.exp(m - top)
        idx = jnp.where(slot == k, first, idx)
        wgt = jnp.where(slot == k, e, wgt)
        denom = denom + e
        work = jnp.where(lane == first, -jnp.inf, work)
    idx_o[...] = idx.astype(jnp.int32)
    wgt_o[...] = wgt / denom


def _merge(pool, four, attn, sgu, gl, x, mod_l, g_ffn, lw):
    tm = MERGE_TM
    row = functools.partial(_cond_row, blocks_ctx=T_CTX // tm, blocks_per_latent=DEC_SEQ // tm)
    full = lambda shape: pl.BlockSpec(shape, lambda i: (0,) * len(shape))
    tok = lambda w: pl.BlockSpec((tm, w), lambda i: (i, 0))
    return pl.pallas_call(
        _merge_kernel,
        grid=(T_ALL // tm,),
        in_specs=[
            tok(POOL_WIDTH), tok(FFT_WIDTH), tok(ATTN_WIDTH), tok(SGU_WIDTH), tok(GATE_WIDTH),
            tok(D_MODEL),
            pl.BlockSpec((None, 6, D_MODEL), lambda i: (row(i), 0, 0)),
            full((1, D_MODEL)),
            full((POOL_WIDTH, D_MODEL)), full((FFT_WIDTH, D_MODEL)), full((ATTN_WIDTH, D_MODEL)),
            full((SGU_WIDTH, D_MODEL)), full((D_MODEL, D_MODEL)),
            full((D_MODEL, N_EXPERTS)), full((1, N_EXPERTS)),
        ],
        out_specs=[tok(D_MODEL), pl.BlockSpec((tm, ROW_TILES, 128), lambda i: (i, 0, 0)),
                   tok(TOP_K), tok(TOP_K)],
        out_shape=[jax.ShapeDtypeStruct((T_ALL, D_MODEL), F32),
                   jax.ShapeDtypeStruct((T_ALL, ROW_TILES, 128), F32),
                   jax.ShapeDtypeStruct((T_ALL, TOP_K), jnp.int32),
                   jax.ShapeDtypeStruct((T_ALL, TOP_K), F32)],
        compiler_params=_params("arbitrary"),
        name="merge_router",
    )(pool, four, attn, sgu, gl, x, mod_l, g_ffn.reshape(1, D_MODEL),
      lw["w_br_pool"], lw["w_br_fourier"], lw["w_br_attn"], lw["w_br_sgu"], lw["w_out"],
      lw["router_w"], lw["router_b"])


N_PAIRS = T_ALL * TOP_K
FFN_TM = 256
N_ROW_TILES = N_PAIRS // FFN_TM + N_EXPERTS
N_SLOTS = N_ROW_TILES * FFN_TM
ROUTE_TB = 256


def _route(top_idx):
    e_flat = top_idx.reshape(N_PAIRS)
    onehot = (e_flat[:, None] == jnp.arange(N_EXPERTS, dtype=jnp.int32)[None, :]).astype(jnp.int32)
    csum = jnp.cumsum(onehot, axis=0)
    tiles = (csum[-1] + FFN_TM - 1) // FFN_TM
    tile_end = jnp.cumsum(tiles)
    row_start = (tile_end - tiles) * FFN_TM
    slot = jnp.sum(onehot * (csum - 1 + row_start[None, :]), axis=1)
    tile_ids = jnp.arange(N_ROW_TILES, dtype=jnp.int32)
    tile_expert = jnp.sum((tile_end[None, :] <= tile_ids[:, None]).astype(jnp.int32), axis=1)
    tile_expert = jnp.minimum(tile_expert, N_EXPERTS - 1)
    slot = slot.astype(jnp.int32).reshape(T_ALL // ROUTE_TB, 1, ROUTE_TB * TOP_K)
    return slot, tile_expert.astype(jnp.int32), tile_end[-1:].astype(jnp.int32)


def _row_copies(make, n_rows):
    def start(r, carry):
        for k in range(TOP_K):
            make(r, k).start()
        return carry

    def wait(r, carry):
        for k in range(TOP_K):
            make(r, k).wait()
        return carry

    lax.fori_loop(0, n_rows, start, 0)
    lax.fori_loop(0, n_rows, wait, 0)


def _dispatch_kernel(slot_ref, h_ref, xs_in, xs_out, sem):
    del xs_in
    _row_copies(lambda r, k: pltpu.make_async_copy(
        h_ref.at[r], xs_out.at[slot_ref[0, r * TOP_K + k]], sem), ROUTE_TB)


def _dispatch(slot, h3):
    zeros = jnp.zeros((N_SLOTS, ROW_TILES, 128), F32)
    return pl.pallas_call(
        _dispatch_kernel,
        grid=(T_ALL // ROUTE_TB,),
        in_specs=[
            pl.BlockSpec((None, 1, ROUTE_TB * TOP_K), lambda i: (i, 0, 0), memory_space=pltpu.SMEM),
            pl.BlockSpec((ROUTE_TB, ROW_TILES, 128), lambda i: (i, 0, 0)),
            pl.BlockSpec(memory_space=pl.ANY),
        ],
        out_specs=pl.BlockSpec(memory_space=pl.ANY),
        out_shape=jax.ShapeDtypeStruct((N_SLOTS, ROW_TILES, 128), F32),
        scratch_shapes=[pltpu.SemaphoreType.DMA(())],
        input_output_aliases={2: 0},
        compiler_params=_params("arbitrary"),
        name="moe_dispatch",
    )(slot, h3, zeros)


def _expert_ffn(xb, wg, bg, wu, bu, wd, bd):
    gate = jnp.minimum(_dot(xb, wg) + bg, SWIGLU_LIMIT)
    up = jnp.clip(_dot(xb, wu) + bu, -SWIGLU_LIMIT, SWIGLU_LIMIT)
    glu = gate * jax.nn.sigmoid(SWIGLU_ALPHA * gate)
    return _dot(((up + 1.0) * glu).astype(BF16), wd) + bd


def _ffn_kernel(te_ref, nu_ref, xs_ref, wg_ref, bg_ref, wu_ref, bu_ref, wd_ref, bd_ref, ys_ref,
                w_bf16, xb):
    i = pl.program_id(0)

    @pl.when(i >= nu_ref[0])
    def _():
        ys_ref[...] = jnp.zeros_like(ys_ref)

    @pl.when(i < nu_ref[0])
    def _():
        @pl.when((i == 0) | (te_ref[i] != te_ref[jnp.maximum(i - 1, 0)]))
        def _():
            w_bf16[0] = wg_ref[...].astype(BF16)
            w_bf16[1] = wu_ref[...].astype(BF16)
            w_bf16[2] = wd_ref[...].astype(BF16)

        for s in range(ROW_TILES):
            xb[:, s * 128:(s + 1) * 128] = xs_ref[:, s, :].astype(BF16)
        y = _expert_ffn(xb[...], w_bf16[0], bg_ref[...], w_bf16[1], bu_ref[...], w_bf16[2],
                        bd_ref[...])
        for s in range(ROW_TILES):
            ys_ref[:, s, :] = y[:, s * 128:(s + 1) * 128]


def _routed_ffn(tile_expert, n_used, xs, l, p):
    rows = pl.BlockSpec((FFN_TM, ROW_TILES, 128),
                        lambda i, te, nu: (jnp.minimum(i, nu[0] - 1), 0, 0))
    wspec = lambda a, b: pl.BlockSpec((None, None, a, b), lambda i, te, nu: (l, te[i], 0, 0))
    bias = lambda a: a.reshape(DEPTH, N_EXPERTS, 1, a.shape[-1])
    return pl.pallas_call(
        _ffn_kernel,
        grid_spec=pltpu.PrefetchScalarGridSpec(
            num_scalar_prefetch=2,
            grid=(N_ROW_TILES,),
            in_specs=[rows, wspec(D_MODEL, D_FF), wspec(1, D_FF), wspec(D_MODEL, D_FF),
                      wspec(1, D_FF), wspec(D_FF, D_MODEL), wspec(1, D_MODEL)],
            out_specs=pl.BlockSpec((FFN_TM, ROW_TILES, 128), lambda i, te, nu: (i, 0, 0)),
            scratch_shapes=[pltpu.VMEM((3, D_MODEL, D_FF), BF16), pltpu.VMEM((FFN_TM, D_MODEL), BF16)],
        ),
        out_shape=jax.ShapeDtypeStruct((N_SLOTS, ROW_TILES, 128), F32),
        compiler_params=_params("arbitrary"),
        name="moe_ffn",
    )(tile_expert, n_used, xs, p["moe_w_gate"], bias(p["moe_b_gate"]), p["moe_w_up"],
      bias(p["moe_b_up"]), p["moe_w_down"], bias(p["moe_b_down"]))


def _combine_kernel(slot_ref, w_ref, x_ref, mod_ref, fg_ref, ys_ref, o_ref, buf, sem, *, final_norm):
    _row_copies(lambda r, k: pltpu.make_async_copy(
        ys_ref.at[slot_ref[0, r * TOP_K + k]], buf.at[k, r], sem), ROUTE_TB)
    w = w_ref[...]
    sum_sq = jnp.zeros((ROUTE_TB, 1), F32)
    for s in range(ROW_TILES):
        cols = slice(s * 128, (s + 1) * 128)
        acc = w[:, 0:1] * buf[0, :, s, :]
        for k in range(1, TOP_K):
            acc = acc + w[:, k:k + 1] * buf[k, :, s, :]
        piece = x_ref[:, cols] + mod_ref[5:6, cols] * acc
        sum_sq = sum_sq + jnp.sum(piece * piece, axis=-1, keepdims=True)
        o_ref[:, cols] = piece
    if final_norm:
        o_ref[...] = o_ref[...] * lax.rsqrt(sum_sq * (1.0 / D_MODEL) + EPS) * fg_ref[...]


def _combine(slot, top_w, x, mod_l, final_g, ys, final_norm):
    tb = ROUTE_TB
    row = functools.partial(_cond_row, blocks_ctx=T_CTX // tb, blocks_per_latent=DEC_SEQ // tb)
    tok = lambda w: pl.BlockSpec((tb, w), lambda i: (i, 0))
    return pl.pallas_call(
        functools.partial(_combine_kernel, final_norm=final_norm),
        grid=(T_ALL // tb,),
        in_specs=[
            pl.BlockSpec((None, 1, tb * TOP_K), lambda i: (i, 0, 0), memory_space=pltpu.SMEM),
            tok(TOP_K), tok(D_MODEL),
            pl.BlockSpec((None, 6, D_MODEL), lambda i: (row(i), 0, 0)),
            pl.BlockSpec((1, D_MODEL), lambda i: (0, 0)),
            pl.BlockSpec(memory_space=pl.ANY),
        ],
        out_specs=tok(D_MODEL),
        out_shape=jax.ShapeDtypeStruct((T_ALL, D_MODEL), F32),
        scratch_shapes=[pltpu.VMEM((TOP_K, tb, ROW_TILES, 128), F32), pltpu.SemaphoreType.DMA(())],
        compiler_params=_params("arbitrary"),
        name="moe_combine",
    )(slot, top_w, x, mod_l, final_g.reshape(1, D_MODEL), ys)


def _block_diag(blocks):
    g, a, b = blocks.shape
    eye = jnp.eye(g, dtype=blocks.dtype)
    return (eye[:, None, :, None] * blocks[:, :, None, :]).reshape(g * a, g * b)


def _dft_tables(n):
    k = np.arange(n, dtype=np.int64)
    ang = 2.0 * np.pi * ((k[:, None] * k[None, :]) % n).astype(np.float64) / n
    return np.cos(ang) / math.sqrt(n), np.sin(ang) / math.sqrt(n)


def _seq_constants(seq):
    t = np.arange(seq)[:, None]
    win = np.array(POOL_WINDOWS)[None, :]
    lo = np.clip(t - win // 2, 0, seq)
    hi = np.clip(t - win // 2 + win, 0, seq)
    invcnt = np.repeat(1.0 / (hi - lo).astype(np.float64), POOL_CH, axis=1)
    cos_p, sin_p = _dft_tables(seq)
    cos_c, sin_c = _dft_tables(FFT_CH)
    eye = np.eye(POOL_GROUPS)
    as_bf16 = lambda a: jnp.asarray(a, F32).astype(BF16)
    return {
        "invcnt": jnp.asarray(invcnt, F32),
        "f_pos": as_bf16(np.concatenate([cos_p, -sin_p], axis=1)),
        "f_cos_ch": as_bf16(np.kron(eye, cos_c)),
        "f_sin_ch": as_bf16(np.kron(eye, sin_c)),
        "ones96": as_bf16(np.kron(eye, np.ones((SGU_CH, SGU_CH)))),
    }


def _rope_constants():
    rows = DEC_SEQ // GRID_W
    row = jnp.repeat(jnp.arange(rows, dtype=F32), GRID_W)
    col = jnp.tile(jnp.arange(GRID_W, dtype=F32), rows)
    freqs = ROPE_THETA ** (-jnp.arange(ROPE_PAIRS, dtype=F32) / ROPE_PAIRS)
    ang = jnp.stack([row[:, None] * freqs, col[:, None] * freqs], axis=1)
    ang = jnp.repeat(ang.reshape(DEC_SEQ, 2 * ROPE_PAIRS), 2, axis=1)
    ang = jnp.tile(ang, (1, 128 // HEAD_DIM))
    even = (jnp.arange(128) % 2 == 0)[None, :]
    sin = jnp.sin(ang)
    return {
        "rope_cos": jnp.cos(ang),
        "rope_sin_next": jnp.where(even, -sin, 0.0),
        "rope_sin_prev": jnp.where(even, 0.0, sin),
    }


def _layer_weights(p, l):
    group_of_lane = np.arange(SGU_WIDTH) // SGU_CH
    return {
        "w_in": p["w_in"][l].astype(BF16),
        "pool_w_bd": _block_diag(p["pool_w"][l]).astype(BF16),
        "pool_scale": p["pool_scale"][l].reshape(1, POOL_WIDTH),
        "sgu_w_stack": p["sgu_w"][l].reshape(SGU_GROUPS * SGU_CHUNK, SGU_CHUNK).astype(BF16),
        "sgu_bias": p["sgu_b"][l].T[:, group_of_lane],
        "gq": jnp.tile(p["q_norm_g"][l], N_HEADS).reshape(1, ATTN_WIDTH),
        "gk": jnp.tile(p["k_norm_g"][l], N_KV_HEADS).reshape(1, KV_WIDTH),
        "w_br_pool": p["w_br_pool"][l].astype(BF16),
        "w_br_fourier": p["w_br_fourier"][l].astype(BF16),
        "w_br_attn": p["w_br_attn"][l].astype(BF16),
        "w_br_sgu": p["w_br_sgu"][l].astype(BF16),
        "w_out": p["w_out"][l].astype(BF16),
        "router_w": p["router_w"][l],
        "router_b": p["router_b"][l].reshape(1, N_EXPERTS),
    }


def kernel(x_prompt, x_sample, cache_k, cache_v, c, c_ctx, w_mod, b_mod, norm_mix_g, norm_ffn_g, w_in, pool_w, pool_scale, q_norm_g, k_norm_g, sgu_w, sgu_b, w_br_pool, w_br_fourier, w_br_attn, w_br_sgu, w_out, router_w, router_b, moe_w_gate, moe_b_gate, moe_w_up, moe_b_up, moe_w_down, moe_b_down, final_norm_g):
    p = dict(w_in=w_in, pool_w=pool_w, pool_scale=pool_scale, q_norm_g=q_norm_g, k_norm_g=k_norm_g,
             sgu_w=sgu_w, sgu_b=sgu_b, w_br_pool=w_br_pool, w_br_fourier=w_br_fourier,
             w_br_attn=w_br_attn, w_br_sgu=w_br_sgu, w_out=w_out, router_w=router_w,
             router_b=router_b, moe_w_gate=moe_w_gate, moe_b_gate=moe_b_gate, moe_w_up=moe_w_up,
             moe_b_up=moe_b_up, moe_w_down=moe_w_down, moe_b_down=moe_b_down)

    cond = jnp.concatenate([c_ctx[None, :], c, jnp.zeros((N_COND - 1 - DEC_BATCH, D_MODEL), F32)])
    mod = _modulation(cond, w_mod, b_mod).reshape(DEPTH, N_COND, 6, D_MODEL)

    ones64 = jnp.asarray(np.kron(np.eye(N_HEADS), np.ones((HEAD_DIM, HEAD_DIM))), BF16)
    consts_ctx = dict(_seq_constants(SEQ), ones64=ones64)
    consts_lat = dict(_seq_constants(DEC_SEQ), ones64=ones64, **_rope_constants())
    ck = cache_k.reshape(DEC_BATCH, DEPTH, PAST_LEN, KV_WIDTH)
    cv = cache_v.reshape(DEC_BATCH, DEPTH, PAST_LEN, KV_WIDTH)

    x = jnp.concatenate([x_prompt.reshape(T_CTX, D_MODEL), x_sample.reshape(T_LAT, D_MODEL)])
    new_k, new_v = [], []
    for l in range(DEPTH):
        lw = _layer_weights(p, l)
        xp, xq, qkv, uv, gl = _in_projection(x, mod[l], norm_mix_g[l], lw["w_in"])
        pool_c, four_c, sgu_c = _mixers(xp, xq, uv, consts_ctx, lw, SEQ, BATCH, 0)
        pool_l, four_l, sgu_l = _mixers(xp, xq, uv, consts_lat, lw, DEC_SEQ, DEC_BATCH,
                                        T_CTX // DEC_SEQ)
        attn_c, k_c = _attention_ctx(qkv, consts_ctx, lw)
        attn_l = _attention_lat(qkv, ck[:, l], cv[:, l], consts_lat, lw)
        new_k.append(k_c.reshape(BATCH, SEQ, N_KV_HEADS, HEAD_DIM))
        new_v.append(qkv[:T_CTX, ATTN_WIDTH + KV_WIDTH:].reshape(BATCH, SEQ, N_KV_HEADS, HEAD_DIM))
        cat = lambda a, b: jnp.concatenate([a, b], axis=0)
        x, h3, top_idx, top_w = _merge(cat(pool_c, pool_l), cat(four_c, four_l), cat(attn_c, attn_l),
                                       cat(sgu_c, sgu_l), gl, x, mod[l], norm_ffn_g[l], lw)
        slot, tile_expert, n_used = _route(top_idx)
        ys = _routed_ffn(tile_expert, n_used, _dispatch(slot, h3), l, p)
        x = _combine(slot, top_w, x, mod[l], final_norm_g, ys, final_norm=(l == DEPTH - 1))

    y_prompt = x[:T_CTX].reshape(BATCH, SEQ, D_MODEL)
    y_sample = x[T_CTX:].reshape(DEC_BATCH, DEC_SEQ, D_MODEL)
    return (y_prompt, y_sample, jnp.stack(new_k, axis=1), jnp.stack(new_v, axis=1))
```

```python
import functools
import math

import numpy as np
import jax
import jax.numpy as jnp
from jax import lax
from jax.experimental import pallas as pl
from jax.experimental.pallas import tpu as pltpu

F32 = jnp.float32
BF16 = jnp.bfloat16

D_MODEL = 1024
BATCH = 32
SEQ = 256
DEPTH = 2
DEC_BATCH = 2
DEC_SEQ = 1024
PAST_LEN = 256
GRID_W = 64
EPS = 1e-6
POOL_GROUPS = 4
POOL_CH = 96
POOL_WIDTH = 384
POOL_WINDOWS = (2, 4, 8, 16)
POOL_PAD = 16
FFT_CH = 96
FFT_WIDTH = 384
N_HEADS = 8
N_KV_HEADS = 2
HEAD_DIM = 64
ATTN_WIDTH = 512
KV_WIDTH = 128
QKV_WIDTH = ATTN_WIDTH + 2 * KV_WIDTH
ROPE_THETA = 10000.0
ROPE_PAIRS = 16
SGU_GROUPS = 4
SGU_CH = 96
SGU_WIDTH = 384
SGU_CHUNK = 128
N_BRANCHES = 4
GATE_WIDTH = N_BRANCHES * D_MODEL
IN_COLS = POOL_WIDTH + FFT_WIDTH + QKV_WIDTH + 2 * SGU_WIDTH + GATE_WIDTH
N_EXPERTS = 32
TOP_K = 4
D_FF = 1024
SWIGLU_LIMIT = 7.0
SWIGLU_ALPHA = 1.702

T_CTX = BATCH * SEQ
T_LAT = DEC_BATCH * DEC_SEQ
T_ALL = T_CTX + T_LAT
N_COND = 8
VMEM_LIMIT = 56 * 1024 * 1024

COL_XP = 0
COL_XQ = COL_XP + POOL_WIDTH
COL_QKV = COL_XQ + FFT_WIDTH
COL_UV = COL_QKV + QKV_WIDTH
COL_GATE = COL_UV + 2 * SGU_WIDTH


def _params(*sem):
    return pltpu.CompilerParams(dimension_semantics=sem, vmem_limit_bytes=VMEM_LIMIT)


def _split_bf16(x):
    hi = x.astype(BF16)
    lo = (x - hi.astype(F32)).astype(BF16)
    return hi, lo


def _dot(a, b):
    return jnp.dot(a, b, preferred_element_type=F32)


def _dot_nt(a, b):
    return lax.dot_general(a, b, (((1,), (1,)), ((), ())), preferred_element_type=F32)


def _group_lane_select(lane, vals, width):
    out = vals[-1]
    for g in range(len(vals) - 2, -1, -1):
        out = jnp.where(lane < (g + 1) * width, vals[g], out)
    return out


def _cond_row(blk, blocks_ctx, blocks_per_latent):
    return jnp.where(blk < blocks_ctx, 0, 1 + (blk - blocks_ctx) // blocks_per_latent)


MOD_TN = 1536


def _mod_kernel(c_ref, w_ref, b_ref, o_ref):
    c = c_ref[...]
    s = c * jax.nn.sigmoid(c)
    sh, sl = _split_bf16(s)
    wh, wl = _split_bf16(w_ref[...])
    o_ref[...] = _dot(sh, wh) + _dot(sh, wl) + _dot(sl, wh) + b_ref[...]


def _modulation(cond, w_mod, b_mod):
    n_cols = 6 * D_MODEL
    return pl.pallas_call(
        _mod_kernel,
        grid=(DEPTH, n_cols // MOD_TN),
        in_specs=[
            pl.BlockSpec((N_COND, D_MODEL), lambda l, j: (0, 0)),
            pl.BlockSpec((None, D_MODEL, MOD_TN), lambda l, j: (l, 0, j)),
            pl.BlockSpec((None, 1, MOD_TN), lambda l, j: (l, 0, j)),
        ],
        out_specs=pl.BlockSpec((None, N_COND, MOD_TN), lambda l, j: (l, 0, j)),
        out_shape=jax.ShapeDtypeStruct((DEPTH, N_COND, n_cols), F32),
        compiler_params=_params("arbitrary", "arbitrary"),
        name="modulation",
    )(cond, w_mod, b_mod.reshape(DEPTH, 1, n_cols))


INPROJ_TM = 256
INPROJ_SEGMENTS = (
    (COL_XP, POOL_WIDTH), (COL_XQ, FFT_WIDTH), (COL_QKV, QKV_WIDTH),
    (COL_UV, 2 * SGU_WIDTH), (COL_GATE, GATE_WIDTH))
INPROJ_CHUNK = 1024


def _ada_norm(x, g, shift, scale):
    xn = x * lax.rsqrt(jnp.mean(x * x, axis=-1, keepdims=True) + EPS)
    return xn * g * (1.0 + scale) + shift


def _token_specs(parts, tm, width):
    if len(parts) == 1:
        return [pl.BlockSpec((tm, width), lambda i: (i, 0))]
    nc = T_CTX // tm
    return [pl.BlockSpec((tm, width), lambda i: (jnp.minimum(i, nc - 1), 0)),
            pl.BlockSpec((tm, width), lambda i: (jnp.maximum(i - nc, 0), 0))]


def _token_load(refs, tm):
    if len(refs) == 1:
        return refs[0][...]
    return jnp.where(pl.program_id(0) < T_CTX // tm, refs[0][...], refs[1][...])


def _inproj_kernel(*refs, n_x):
    x_refs, (mod_ref, g_ref, w_ref), out_refs = refs[:n_x], refs[n_x:n_x + 3], refs[n_x + 3:]
    h = _ada_norm(_token_load(x_refs, INPROJ_TM), g_ref[...], mod_ref[0:1, :], mod_ref[1:2, :])
    hb = h.astype(BF16)
    for (col, width), o_ref in zip(INPROJ_SEGMENTS, out_refs):
        for c0 in range(0, width, INPROJ_CHUNK):
            c1 = min(c0 + INPROJ_CHUNK, width)
            o_ref[:, c0:c1] = _dot(hb, w_ref[:, col + c0:col + c1])


def _in_projection(x_parts, mod_l, g, w_in_bf16):
    tm = INPROJ_TM
    row = functools.partial(_cond_row, blocks_ctx=T_CTX // tm, blocks_per_latent=DEC_SEQ // tm)
    return pl.pallas_call(
        functools.partial(_inproj_kernel, n_x=len(x_parts)),
        grid=(T_ALL // tm,),
        in_specs=_token_specs(x_parts, tm, D_MODEL) + [
            pl.BlockSpec((None, 6, D_MODEL), lambda i: (row(i), 0, 0)),
            pl.BlockSpec((1, D_MODEL), lambda i: (0, 0)),
            pl.BlockSpec((D_MODEL, IN_COLS), lambda i: (0, 0)),
        ],
        out_specs=[pl.BlockSpec((tm, w), lambda i: (i, 0)) for _, w in INPROJ_SEGMENTS],
        out_shape=[jax.ShapeDtypeStruct((T_ALL, w), F32) for _, w in INPROJ_SEGMENTS],
        compiler_params=_params("arbitrary"),
        name="in_projection",
    )(*x_parts, mod_l, g.reshape(1, D_MODEL), w_in_bf16)


def _pool_mixer(xp, invcnt, w_bd, scale):
    s = xp.shape[0]
    n = s + 2 * POOL_PAD
    zeros = jnp.zeros((POOL_PAD, POOL_WIDTH), F32)
    xe = jnp.concatenate([zeros, xp, zeros], axis=0)

    def shift(a, k):
        return pltpu.roll(a, k % n, 0)

    s2 = xe + shift(xe, 1)
    s4 = shift(s2, 1) + shift(s2, -1)
    s8 = shift(s4, 2) + shift(s4, -2)
    s16 = shift(s8, 4) + shift(s8, -4)
    lane = lax.broadcasted_iota(jnp.int32, (1, POOL_WIDTH), 1)
    total = _group_lane_select(lane, [s2, s4, s8, s16], POOL_CH)[POOL_PAD:POOL_PAD + s]
    pooled = total * invcnt - xp
    return _dot(pooled.astype(BF16), w_bd) * scale


def _fourier_mixer(xq, f_cos_ch, f_sin_ch, f_pos):
    xb = xq.astype(BF16)
    a = _dot(xb, f_cos_ch).astype(BF16)
    b = _dot(xb, f_sin_ch).astype(BF16)
    return _dot(f_pos, jnp.concatenate([a, b], axis=0))


def _group_mean_sq(x, ones_bd, width):
    hi, lo = _split_bf16(x * x)
    return (_dot(hi, ones_bd) + _dot(lo, ones_bd)) * (1.0 / width)


def _mixers_kernel(xp_ref, xq_ref, uv_ref, invcnt_ref, pool_w_ref, pool_s_ref, fcc_ref, fsc_ref,
                   fpos_ref, ones_ref, sgu_w_ref, sgu_b_ref, pool_o, four_o, sgu_o):
    s = xp_ref.shape[0]
    pool_o[...] = _pool_mixer(xp_ref[...], invcnt_ref[...], pool_w_ref[...],
                              pool_s_ref[...]).astype(BF16)
    four_o[...] = _fourier_mixer(xq_ref[...], fcc_ref[...], fsc_ref[...], fpos_ref[...]).astype(BF16)

    act = jax.nn.gelu(uv_ref[...], approximate=True)
    u = act[:, :SGU_WIDTH]
    v = act[:, SGU_WIDTH:]
    vg = (v * lax.rsqrt(_group_mean_sq(v, ones_ref[...], SGU_CH) + EPS)).astype(BF16)
    lane = lax.broadcasted_iota(jnp.int32, (1, SGU_WIDTH), 1)
    w_stack = sgu_w_ref[...]
    bias = sgu_b_ref[...]
    for n in range(s // SGU_CHUNK):
        rows = slice(n * SGU_CHUNK, (n + 1) * SGU_CHUNK)
        r = _dot(w_stack, vg[rows])
        per_group = [r[g * SGU_CHUNK:(g + 1) * SGU_CHUNK] for g in range(SGU_GROUPS)]
        spatial = _group_lane_select(lane, per_group, SGU_CH) + bias
        sgu_o[rows, :] = (u[rows] * spatial).astype(BF16)


def _mixers(xp, xq, uv, consts, lw, seq, n_seq, block0):
    full = lambda shape: pl.BlockSpec(shape, lambda b: (0,) * len(shape))
    tok = lambda w: pl.BlockSpec((seq, w), lambda b: (block0 + b, 0))
    out = lambda: pl.BlockSpec((seq, POOL_WIDTH), lambda b: (b, 0))
    return pl.pallas_call(
        _mixers_kernel,
        grid=(n_seq,),
        in_specs=[
            tok(POOL_WIDTH), tok(FFT_WIDTH), tok(2 * SGU_WIDTH),
            full((seq, POOL_WIDTH)), full((POOL_WIDTH, POOL_WIDTH)), full((1, POOL_WIDTH)),
            full((FFT_WIDTH, FFT_WIDTH)), full((FFT_WIDTH, FFT_WIDTH)), full((seq, 2 * seq)),
            full((SGU_WIDTH, SGU_WIDTH)), full((SGU_GROUPS * SGU_CHUNK, SGU_CHUNK)),
            full((SGU_CHUNK, SGU_WIDTH)),
        ],
        out_specs=[out(), out(), out()],
        out_shape=[jax.ShapeDtypeStruct((n_seq * seq, POOL_WIDTH), BF16)] * 3,
        compiler_params=_params("arbitrary"),
        name=f"mixers_s{seq}",
    )(xp, xq, uv, consts["invcnt"], lw["pool_w_bd"], lw["pool_scale"], consts["f_cos_ch"],
      consts["f_sin_ch"], consts["f_pos"], consts["ones96"], lw["sgu_w_stack"], lw["sgu_bias"])


def _head_norm(x, ones_bd, g):
    return x * lax.rsqrt(_group_mean_sq(x, ones_bd, HEAD_DIM) + EPS) * g


def _rope(x, cos, sin_next, sin_prev):
    cols = []
    for c in range(x.shape[1] // 128):
        xc = x[:, c * 128:(c + 1) * 128]
        nxt = pltpu.roll(xc, 127, 1)
        prv = pltpu.roll(xc, 1, 1)
        cols.append(xc * cos + nxt * sin_next + prv * sin_prev)
    return cols[0] if len(cols) == 1 else jnp.concatenate(cols, axis=1)


def _attend(q, keys, vals, o_ref):
    qb = (q * (HEAD_DIM ** -0.5)).astype(BF16)
    group = N_HEADS // N_KV_HEADS
    for h in range(N_HEADS):
        j = h // group
        kh = keys[:, j * HEAD_DIM:(j + 1) * HEAD_DIM]
        vh = vals[:, j * HEAD_DIM:(j + 1) * HEAD_DIM]
        s = _dot_nt(qb[:, h * HEAD_DIM:(h + 1) * HEAD_DIM], kh)
        p = jnp.exp(s - jnp.max(s, axis=-1, keepdims=True))
        denom = jnp.sum(p, axis=-1, keepdims=True)
        o = _dot(p.astype(BF16), vh) / denom
        o_ref[:, h * HEAD_DIM:(h + 1) * HEAD_DIM] = o.astype(BF16)


def _attn_ctx_kernel(qkv_ref, gq_ref, gk_ref, ones_ref, o_ref, k_ref, v_ref):
    qkv = qkv_ref[...]
    ones = ones_ref[...]
    q = _head_norm(qkv[:, :ATTN_WIDTH], ones, gq_ref[...])
    k = _head_norm(qkv[:, ATTN_WIDTH:ATTN_WIDTH + KV_WIDTH], ones[:KV_WIDTH, :KV_WIDTH], gk_ref[...])
    v = qkv[:, ATTN_WIDTH + KV_WIDTH:]
    k_ref[...] = k
    v_ref[...] = v
    _attend(q, k.astype(BF16), v.astype(BF16), o_ref)


def _attention_ctx(qkv, consts, lw):
    full = lambda shape: pl.BlockSpec(shape, lambda b: (0,) * len(shape))
    return pl.pallas_call(
        _attn_ctx_kernel,
        grid=(BATCH,),
        in_specs=[pl.BlockSpec((SEQ, QKV_WIDTH), lambda b: (b, 0)),
                  full((1, ATTN_WIDTH)), full((1, KV_WIDTH)), full((ATTN_WIDTH, ATTN_WIDTH))],
        out_specs=[pl.BlockSpec((SEQ, ATTN_WIDTH), lambda b: (b, 0)),
                   pl.BlockSpec((SEQ, KV_WIDTH), lambda b: (b, 0)),
                   pl.BlockSpec((SEQ, KV_WIDTH), lambda b: (b, 0))],
        out_shape=[jax.ShapeDtypeStruct((T_CTX, ATTN_WIDTH), BF16),
                   jax.ShapeDtypeStruct((T_CTX, KV_WIDTH), F32),
                   jax.ShapeDtypeStruct((T_CTX, KV_WIDTH), F32)],
        compiler_params=_params("arbitrary"),
        name="attention_ctx",
    )(qkv, lw["gq"], lw["gk"], consts["ones64"])


LAT_QBLK = 256


def _attn_lat_kernel(q_ref, kv_ref, ck_ref, cv_ref, gq_ref, gk_ref, ones_ref, cos_q, sn_q, sp_q,
                     cos_k, sn_k, sp_k, o_ref, keys, vals):
    ones = ones_ref[...]

    @pl.when(pl.program_id(1) == 0)
    def _():
        kv = kv_ref[...]
        k = _head_norm(kv[:, :KV_WIDTH], ones[:KV_WIDTH, :KV_WIDTH], gk_ref[...])
        keys[0:DEC_SEQ, :] = _rope(k, cos_k[...], sn_k[...], sp_k[...]).astype(BF16)
        keys[DEC_SEQ:, :] = ck_ref[...].astype(BF16)
        vals[0:DEC_SEQ, :] = kv[:, KV_WIDTH:].astype(BF16)
        vals[DEC_SEQ:, :] = cv_ref[...].astype(BF16)

    q = _head_norm(q_ref[...], ones, gq_ref[...])
    q = _rope(q, cos_q[...], sn_q[...], sp_q[...])
    _attend(q, keys[...], vals[...], o_ref)


def _attention_lat(qkv, cache_k_l, cache_v_l, consts, lw):
    nq = DEC_SEQ // LAT_QBLK
    q0 = T_CTX // LAT_QBLK
    s0 = T_CTX // DEC_SEQ
    full = lambda shape: pl.BlockSpec(shape, lambda b, j: (0,) * len(shape))
    rope_q = lambda: pl.BlockSpec((LAT_QBLK, 128), lambda b, j: (j, 0))
    rope_k = lambda: pl.BlockSpec((DEC_SEQ, 128), lambda b, j: (0, 0))
    return pl.pallas_call(
        _attn_lat_kernel,
        grid=(DEC_BATCH, nq),
        in_specs=[
            pl.BlockSpec((LAT_QBLK, ATTN_WIDTH), lambda b, j: (q0 + b * nq + j, 0)),
            pl.BlockSpec((DEC_SEQ, 2 * KV_WIDTH), lambda b, j: (s0 + b, ATTN_WIDTH // (2 * KV_WIDTH))),
            pl.BlockSpec((None, PAST_LEN, KV_WIDTH), lambda b, j: (b, 0, 0)),
            pl.BlockSpec((None, PAST_LEN, KV_WIDTH), lambda b, j: (b, 0, 0)),
            full((1, ATTN_WIDTH)), full((1, KV_WIDTH)), full((ATTN_WIDTH, ATTN_WIDTH)),
            rope_q(), rope_q(), rope_q(), rope_k(), rope_k(), rope_k(),
        ],
        out_specs=pl.BlockSpec((LAT_QBLK, ATTN_WIDTH), lambda b, j: (b * nq + j, 0)),
        out_shape=jax.ShapeDtypeStruct((T_LAT, ATTN_WIDTH), BF16),
        scratch_shapes=[pltpu.VMEM((DEC_SEQ + PAST_LEN, KV_WIDTH), BF16),
                        pltpu.VMEM((DEC_SEQ + PAST_LEN, KV_WIDTH), BF16)],
        compiler_params=_params("arbitrary", "arbitrary"),
        name="attention_lat",
    )(qkv, qkv, cache_k_l, cache_v_l, lw["gq"], lw["gk"], consts["ones64"],
      consts["rope_cos"], consts["rope_sin_next"], consts["rope_sin_prev"],
      consts["rope_cos"], consts["rope_sin_next"], consts["rope_sin_prev"])


MERGE_TM = 256


def _merge_kernel(*refs, n_x):
    branch_refs, gl_ref, x_refs = refs[:8], refs[8], refs[9:9 + n_x]
    (mod_ref, g_ref, wp_ref, wf_ref, wa_ref, ws_ref, wo_ref, rw_ref, rb_ref,
     x_o, h_o, idx_o, wgt_o) = refs[9 + n_x:]
    merged = None
    for i, w_ref in enumerate((wp_ref, wf_ref, wa_ref, ws_ref)):
        br = _dot(_token_load(branch_refs[2 * i:2 * i + 2], MERGE_TM), w_ref[...])
        term = jax.nn.sigmoid(gl_ref[:, i * D_MODEL:(i + 1) * D_MODEL]) * br
        merged = term if merged is None else merged + term
    mix = _dot(merged.astype(BF16), wo_ref[...])
    x = _token_load(x_refs, MERGE_TM) + mod_ref[2:3, :] * mix
    x_o[...] = x
    h = _ada_norm(x, g_ref[...], mod_ref[3:4, :], mod_ref[4:5, :])
    h_o[...] = h

    hh, hl = _split_bf16(h)
    rh, rl = _split_bf16(rw_ref[...])
    logits = _dot(hh, rh) + _dot(hh, rl) + _dot(hl, rh) + rb_ref[...]
    lane = lax.broadcasted_iota(jnp.int32, logits.shape, 1).astype(F32)
    slot = lax.broadcasted_iota(jnp.int32, (logits.shape[0], TOP_K), 1)
    work = logits
    top = jnp.max(logits, axis=-1, keepdims=True)
    idx = jnp.zeros(slot.shape, F32)
    wgt = jnp.zeros(slot.shape, F32)
    denom = jnp.zeros_like(top)
    for k in range(TOP_K):
        m = jnp.max(work, axis=-1, keepdims=True)
        first = jnp.min(jnp.where(work == m, lane, float(N_EXPERTS)), axis=-1, keepdims=True)
        e = jnp.exp(m - top)
        idx = jnp.where(slot == k, first, idx)
        wgt = jnp.where(slot == k, e, wgt)
        denom = denom + e
        work = jnp.where(lane == first, -jnp.inf, work)
    idx_o[...] = idx.astype(jnp.int32)
    wgt_o[...] = wgt / denom


def _merge(pool, four, attn, sgu, gl, x_parts, mod_l, g_ffn, lw):
    tm = MERGE_TM
    row = functools.partial(_cond_row, blocks_ctx=T_CTX // tm, blocks_per_latent=DEC_SEQ // tm)
    full = lambda shape: pl.BlockSpec(shape, lambda i: (0,) * len(shape))
    tok = lambda w: pl.BlockSpec((tm, w), lambda i: (i, 0))
    branch_specs = []
    for pair, width in ((pool, POOL_WIDTH), (four, FFT_WIDTH), (attn, ATTN_WIDTH), (sgu, SGU_WIDTH)):
        branch_specs += _token_specs(pair, tm, width)
    return pl.pallas_call(
        functools.partial(_merge_kernel, n_x=len(x_parts)),
        grid=(T_ALL // tm,),
        in_specs=branch_specs + [tok(GATE_WIDTH)] + _token_specs(x_parts, tm, D_MODEL) + [
            pl.BlockSpec((None, 6, D_MODEL), lambda i: (row(i), 0, 0)),
            full((1, D_MODEL)),
            full((POOL_WIDTH, D_MODEL)), full((FFT_WIDTH, D_MODEL)), full((ATTN_WIDTH, D_MODEL)),
            full((SGU_WIDTH, D_MODEL)), full((D_MODEL, D_MODEL)),
            full((D_MODEL, N_EXPERTS)), full((1, N_EXPERTS)),
        ],
        out_specs=[tok(D_MODEL), tok(D_MODEL), tok(TOP_K), tok(TOP_K)],
        out_shape=[jax.ShapeDtypeStruct((T_ALL, D_MODEL), F32),
                   jax.ShapeDtypeStruct((T_ALL, D_MODEL), F32),
                   jax.ShapeDtypeStruct((T_ALL, TOP_K), jnp.int32),
                   jax.ShapeDtypeStruct((T_ALL, TOP_K), F32)],
        compiler_params=_params("arbitrary"),
        name="merge_router",
    )(*pool, *four, *attn, *sgu, gl, *x_parts, mod_l, g_ffn.reshape(1, D_MODEL),
      lw["w_br_pool"], lw["w_br_fourier"], lw["w_br_attn"], lw["w_br_sgu"], lw["w_out"],
      lw["router_w"], lw["router_b"])


N_PAIRS = T_ALL * TOP_K
FFN_TM = 256
N_ROW_TILES = N_PAIRS // FFN_TM + N_EXPERTS
N_SLOTS = N_ROW_TILES * FFN_TM
ROUTE_TB = 256


def _route(top_idx):
    e_flat = top_idx.reshape(N_PAIRS)
    onehot = (e_flat[:, None] == jnp.arange(N_EXPERTS, dtype=jnp.int32)[None, :]).astype(jnp.int32)
    csum = jnp.cumsum(onehot, axis=0)
    tiles = (csum[-1] + FFN_TM - 1) // FFN_TM
    tile_end = jnp.cumsum(tiles)
    row_start = (tile_end - tiles) * FFN_TM
    slot = jnp.sum(onehot * (csum - 1 + row_start[None, :]), axis=1)
    tile_ids = jnp.arange(N_ROW_TILES, dtype=jnp.int32)
    tile_expert = jnp.sum((tile_end[None, :] <= tile_ids[:, None]).astype(jnp.int32), axis=1)
    tile_expert = jnp.minimum(tile_expert, N_EXPERTS - 1)
    slot = slot.astype(jnp.int32).reshape(T_ALL // ROUTE_TB, 1, ROUTE_TB * TOP_K)
    return slot, tile_expert.astype(jnp.int32), tile_end[-1:].astype(jnp.int32)


def _for_each_pair_row(fn):
    def body(r, carry):
        for k in range(TOP_K):
            fn(r, k)
        return carry

    lax.fori_loop(0, ROUTE_TB, body, 0)


def _dispatch_kernel(slot_ref, h_ref, xs_in, xs_out, sem):
    del xs_in

    def copy(r, k):
        return pltpu.make_async_copy(h_ref.at[pl.ds(r, 1), :],
                                     xs_out.at[pl.ds(slot_ref[0, r * TOP_K + k], 1), :], sem)

    _for_each_pair_row(lambda r, k: copy(r, k).start())
    _for_each_pair_row(lambda r, k: copy(r, k).wait())


def _dispatch(slot, h):
    zeros = jnp.zeros((N_SLOTS, D_MODEL), F32)
    return pl.pallas_call(
        _dispatch_kernel,
        grid=(T_ALL // ROUTE_TB,),
        in_specs=[
            pl.BlockSpec((None, 1, ROUTE_TB * TOP_K), lambda i: (i, 0, 0), memory_space=pltpu.SMEM),
            pl.BlockSpec((ROUTE_TB, D_MODEL), lambda i: (i, 0)),
            pl.BlockSpec(memory_space=pl.ANY),
        ],
        out_specs=pl.BlockSpec(memory_space=pl.ANY),
        out_shape=jax.ShapeDtypeStruct((N_SLOTS, D_MODEL), F32),
        scratch_shapes=[pltpu.SemaphoreType.DMA(())],
        input_output_aliases={2: 0},
        compiler_params=_params("arbitrary"),
        name="moe_dispatch",
    )(slot, h, zeros)


def _expert_ffn(xb, wg, bg, wu, bu, wd, bd):
    gate = jnp.minimum(_dot(xb, wg) + bg, SWIGLU_LIMIT)
    up = jnp.clip(_dot(xb, wu) + bu, -SWIGLU_LIMIT, SWIGLU_LIMIT)
    glu = gate * jax.nn.sigmoid(SWIGLU_ALPHA * gate)
    return _dot(((up + 1.0) * glu).astype(BF16), wd) + bd


def _ffn_kernel(te_ref, nu_ref, xs_ref, wg_ref, bg_ref, wu_ref, bu_ref, wd_ref, bd_ref, ys_ref,
                w_bf16):
    i = pl.program_id(0)

    @pl.when(i >= nu_ref[0])
    def _():
        ys_ref[...] = jnp.zeros_like(ys_ref)

    @pl.when(i < nu_ref[0])
    def _():
        @pl.when((i == 0) | (te_ref[i] != te_ref[jnp.maximum(i - 1, 0)]))
        def _():
            w_bf16[0] = wg_ref[...].astype(BF16)
            w_bf16[1] = wu_ref[...].astype(BF16)
            w_bf16[2] = wd_ref[...].astype(BF16)

        ys_ref[...] = _expert_ffn(xs_ref[...].astype(BF16), w_bf16[0], bg_ref[...], w_bf16[1],
                                  bu_ref[...], w_bf16[2], bd_ref[...])


def _routed_ffn(tile_expert, n_used, xs, l, p):
    rows = pl.BlockSpec((FFN_TM, D_MODEL), lambda i, te, nu: (jnp.minimum(i, nu[0] - 1), 0))
    wspec = lambda a, b: pl.BlockSpec((None, None, a, b), lambda i, te, nu: (l, te[i], 0, 0))
    bias = lambda a: a.reshape(DEPTH, N_EXPERTS, 1, a.shape[-1])
    return pl.pallas_call(
        _ffn_kernel,
        grid_spec=pltpu.PrefetchScalarGridSpec(
            num_scalar_prefetch=2,
            grid=(N_ROW_TILES,),
            in_specs=[rows, wspec(D_MODEL, D_FF), wspec(1, D_FF), wspec(D_MODEL, D_FF),
                      wspec(1, D_FF), wspec(D_FF, D_MODEL), wspec(1, D_MODEL)],
            out_specs=pl.BlockSpec((FFN_TM, D_MODEL), lambda i, te, nu: (i, 0)),
            scratch_shapes=[pltpu.VMEM((3, D_MODEL, D_FF), BF16)],
        ),
        out_shape=jax.ShapeDtypeStruct((N_SLOTS, D_MODEL), F32),
        compiler_params=_params("arbitrary"),
        name="moe_ffn",
    )(tile_expert, n_used, xs, p["moe_w_gate"], bias(p["moe_b_gate"]), p["moe_w_up"],
      bias(p["moe_b_up"]), p["moe_w_down"], bias(p["moe_b_down"]))


COMBINE_CHUNK = 256


def _combine_kernel(slot_ref, slot_next_ref, w_ref, x_ref, mod_ref, fg_ref, ys_ref, o_ref, buf, sems,
                    *, final_norm):
    i = pl.program_id(0)
    cur = i % 2

    def copy(slots, b, r, k):
        return pltpu.make_async_copy(ys_ref.at[pl.ds(slots[0, r * TOP_K + k], 1), :],
                                     buf.at[b, k, pl.ds(r, 1), :], sems.at[b])

    @pl.when(i == 0)
    def _():
        _for_each_pair_row(lambda r, k: copy(slot_ref, 0, r, k).start())

    @pl.when(i + 1 < pl.num_programs(0))
    def _():
        _for_each_pair_row(lambda r, k: copy(slot_next_ref, 1 - cur, r, k).start())

    _for_each_pair_row(lambda r, k: copy(slot_ref, cur, r, k).wait())

    w = w_ref[...]
    sum_sq = jnp.zeros((ROUTE_TB, 1), F32)
    for c0 in range(0, D_MODEL, COMBINE_CHUNK):
        cols = slice(c0, c0 + COMBINE_CHUNK)
        acc = w[:, 0:1] * buf[cur, 0, :, cols]
        for k in range(1, TOP_K):
            acc = acc + w[:, k:k + 1] * buf[cur, k, :, cols]
        piece = x_ref[:, cols] + mod_ref[5:6, cols] * acc
        sum_sq = sum_sq + jnp.sum(piece * piece, axis=-1, keepdims=True)
        o_ref[:, cols] = piece
    if final_norm:
        o_ref[...] = o_ref[...] * lax.rsqrt(sum_sq * (1.0 / D_MODEL) + EPS) * fg_ref[...]


def _combine(slot, top_w, x, mod_l, final_g, ys, final_norm):
    tb = ROUTE_TB
    nb = T_ALL // tb
    row = functools.partial(_cond_row, blocks_ctx=T_CTX // tb, blocks_per_latent=DEC_SEQ // tb)
    tok = lambda w: pl.BlockSpec((tb, w), lambda i: (i, 0))
    slots = lambda ahead: pl.BlockSpec((None, 1, tb * TOP_K),
                                       lambda i: (jnp.minimum(i + ahead, nb - 1), 0, 0),
                                       memory_space=pltpu.SMEM)
    return pl.pallas_call(
        functools.partial(_combine_kernel, final_norm=final_norm),
        grid=(nb,),
        in_specs=[
            slots(0), slots(1), tok(TOP_K), tok(D_MODEL),
            pl.BlockSpec((None, 6, D_MODEL), lambda i: (row(i), 0, 0)),
            pl.BlockSpec((1, D_MODEL), lambda i: (0, 0)),
            pl.BlockSpec(memory_space=pl.ANY),
        ],
        out_specs=tok(D_MODEL),
        out_shape=jax.ShapeDtypeStruct((T_ALL, D_MODEL), F32),
        scratch_shapes=[pltpu.VMEM((2, TOP_K, tb, D_MODEL), F32), pltpu.SemaphoreType.DMA((2,))],
        compiler_params=_params("arbitrary"),
        name="moe_combine",
    )(slot, slot, top_w, x, mod_l, final_g.reshape(1, D_MODEL), ys)


def _block_diag(blocks):
    g, a, b = blocks.shape
    eye = jnp.eye(g, dtype=blocks.dtype)
    return (eye[:, None, :, None] * blocks[:, :, None, :]).reshape(g * a, g * b)


def _dft_tables(n):
    k = np.arange(n, dtype=np.int64)
    ang = 2.0 * np.pi * ((k[:, None] * k[None, :]) % n).astype(np.float64) / n
    return np.cos(ang) / math.sqrt(n), np.sin(ang) / math.sqrt(n)


def _seq_constants(seq):
    t = np.arange(seq)[:, None]
    win = np.array(POOL_WINDOWS)[None, :]
    lo = np.clip(t - win // 2, 0, seq)
    hi = np.clip(t - win // 2 + win, 0, seq)
    invcnt = np.repeat(1.0 / (hi - lo).astype(np.float64), POOL_CH, axis=1)
    cos_p, sin_p = _dft_tables(seq)
    cos_c, sin_c = _dft_tables(FFT_CH)
    eye = np.eye(POOL_GROUPS)
    as_bf16 = lambda a: jnp.asarray(a, F32).astype(BF16)
    return {
        "invcnt": jnp.asarray(invcnt, F32),
        "f_pos": as_bf16(np.concatenate([cos_p, -sin_p], axis=1)),
        "f_cos_ch": as_bf16(np.kron(eye, cos_c)),
        "f_sin_ch": as_bf16(np.kron(eye, sin_c)),
        "ones96": as_bf16(np.kron(eye, np.ones((SGU_CH, SGU_CH)))),
    }


def _rope_constants():
    rows = DEC_SEQ // GRID_W
    row = jnp.repeat(jnp.arange(rows, dtype=F32), GRID_W)
    col = jnp.tile(jnp.arange(GRID_W, dtype=F32), rows)
    freqs = ROPE_THETA ** (-jnp.arange(ROPE_PAIRS, dtype=F32) / ROPE_PAIRS)
    ang = jnp.stack([row[:, None] * freqs, col[:, None] * freqs], axis=1)
    ang = jnp.repeat(ang.reshape(DEC_SEQ, 2 * ROPE_PAIRS), 2, axis=1)
    ang = jnp.tile(ang, (1, 128 // HEAD_DIM))
    even = (jnp.arange(128) % 2 == 0)[None, :]
    sin = jnp.sin(ang)
    return {
        "rope_cos": jnp.cos(ang),
        "rope_sin_next": jnp.where(even, -sin, 0.0),
        "rope_sin_prev": jnp.where(even, 0.0, sin),
    }


def _layer_weights(p, l):
    group_of_lane = np.arange(SGU_WIDTH) // SGU_CH
    return {
        "w_in": p["w_in"][l].astype(BF16),
        "pool_w_bd": _block_diag(p["pool_w"][l]).astype(BF16),
        "pool_scale": p["pool_scale"][l].reshape(1, POOL_WIDTH),
        "sgu_w_stack": p["sgu_w"][l].reshape(SGU_GROUPS * SGU_CHUNK, SGU_CHUNK).astype(BF16),
        "sgu_bias": p["sgu_b"][l].T[:, group_of_lane],
        "gq": jnp.tile(p["q_norm_g"][l], N_HEADS).reshape(1, ATTN_WIDTH),
        "gk": jnp.tile(p["k_norm_g"][l], N_KV_HEADS).reshape(1, KV_WIDTH),
        "w_br_pool": p["w_br_pool"][l].astype(BF16),
        "w_br_fourier": p["w_br_fourier"][l].astype(BF16),
        "w_br_attn": p["w_br_attn"][l].astype(BF16),
        "w_br_sgu": p["w_br_sgu"][l].astype(BF16),
        "w_out": p["w_out"][l].astype(BF16),
        "router_w": p["router_w"][l],
        "router_b": p["router_b"][l].reshape(1, N_EXPERTS),
    }


def kernel(x_prompt, x_sample, cache_k, cache_v, c, c_ctx, w_mod, b_mod, norm_mix_g, norm_ffn_g, w_in, pool_w, pool_scale, q_norm_g, k_norm_g, sgu_w, sgu_b, w_br_pool, w_br_fourier, w_br_attn, w_br_sgu, w_out, router_w, router_b, moe_w_gate, moe_b_gate, moe_w_up, moe_b_up, moe_w_down, moe_b_down, final_norm_g):
    p = dict(w_in=w_in, pool_w=pool_w, pool_scale=pool_scale, q_norm_g=q_norm_g, k_norm_g=k_norm_g,
             sgu_w=sgu_w, sgu_b=sgu_b, w_br_pool=w_br_pool, w_br_fourier=w_br_fourier,
             w_br_attn=w_br_attn, w_br_sgu=w_br_sgu, w_out=w_out, router_w=router_w,
             router_b=router_b, moe_w_gate=moe_w_gate, moe_b_gate=moe_b_gate, moe_w_up=moe_w_up,
             moe_b_up=moe_b_up, moe_w_down=moe_w_down, moe_b_down=moe_b_down)

    cond = jnp.concatenate([c_ctx[None, :], c, jnp.zeros((N_COND - 1 - DEC_BATCH, D_MODEL), F32)])
    mod = _modulation(cond, w_mod, b_mod).reshape(DEPTH, N_COND, 6, D_MODEL)

    ones64 = jnp.asarray(np.kron(np.eye(N_HEADS), np.ones((HEAD_DIM, HEAD_DIM))), BF16)
    consts_ctx = dict(_seq_constants(SEQ), ones64=ones64)
    consts_lat = dict(_seq_constants(DEC_SEQ), ones64=ones64, **_rope_constants())
    ck = cache_k.reshape(DEC_BATCH, DEPTH, PAST_LEN, KV_WIDTH)
    cv = cache_v.reshape(DEC_BATCH, DEPTH, PAST_LEN, KV_WIDTH)

    x_parts = (x_prompt.reshape(T_CTX, D_MODEL), x_sample.reshape(T_LAT, D_MODEL))
    new_k, new_v = [], []
    for l in range(DEPTH):
        lw = _layer_weights(p, l)
        xp, xq, qkv, uv, gl = _in_projection(x_parts, mod[l], norm_mix_g[l], lw["w_in"])
        pool_c, four_c, sgu_c = _mixers(xp, xq, uv, consts_ctx, lw, SEQ, BATCH, 0)
        pool_l, four_l, sgu_l = _mixers(xp, xq, uv, consts_lat, lw, DEC_SEQ, DEC_BATCH,
                                        T_CTX // DEC_SEQ)
        attn_c, k_c, v_c = _attention_ctx(qkv, consts_ctx, lw)
        attn_l = _attention_lat(qkv, ck[:, l], cv[:, l], consts_lat, lw)
        new_k.append(k_c.reshape(BATCH, SEQ, N_KV_HEADS, HEAD_DIM))
        new_v.append(v_c.reshape(BATCH, SEQ, N_KV_HEADS, HEAD_DIM))
        x, h, top_idx, top_w = _merge((pool_c, pool_l), (four_c, four_l), (attn_c, attn_l),
                                      (sgu_c, sgu_l), gl, x_parts, mod[l], norm_ffn_g[l], lw)
        slot, tile_expert, n_used = _route(top_idx)
        ys = _routed_ffn(tile_expert, n_used, _dispatch(slot, h), l, p)
        x = _combine(slot, top_w, x, mod[l], final_norm_g, ys, final_norm=(l == DEPTH - 1))
        x_parts = (x,)

    y_prompt = x[:T_CTX].reshape(BATCH, SEQ, D_MODEL)
    y_sample = x[T_CTX:].reshape(DEC_BATCH, DEC_SEQ, D_MODEL)
    return (y_prompt, y_sample, jnp.stack(new_k, axis=1), jnp.stack(new_v, axis=1))
```

```python
import functools
import math

import numpy as np
import jax
import jax.numpy as jnp
from jax import lax
from jax.experimental import pallas as pl
from jax.experimental.pallas import tpu as pltpu

F32 = jnp.float32
BF16 = jnp.bfloat16

D_MODEL = 1024
BATCH = 32
SEQ = 256
DEPTH = 2
DEC_BATCH = 2
DEC_SEQ = 1024
PAST_LEN = 256
GRID_W = 64
EPS = 1e-6
POOL_GROUPS = 4
POOL_CH = 96
POOL_WIDTH = 384
POOL_WINDOWS = (2, 4, 8, 16)
POOL_PAD = 16
FFT_CH = 96
FFT_WIDTH = 384
N_HEADS = 8
N_KV_HEADS = 2
HEAD_DIM = 64
ATTN_WIDTH = 512
KV_WIDTH = 128
QKV_WIDTH = ATTN_WIDTH + 2 * KV_WIDTH
ROPE_THETA = 10000.0
ROPE_PAIRS = 16
SGU_GROUPS = 4
SGU_CH = 96
SGU_WIDTH = 384
SGU_CHUNK = 128
N_BRANCHES = 4
GATE_WIDTH = N_BRANCHES * D_MODEL
IN_COLS = POOL_WIDTH + FFT_WIDTH + QKV_WIDTH + 2 * SGU_WIDTH + GATE_WIDTH
N_EXPERTS = 32
TOP_K = 4
D_FF = 1024
SWIGLU_LIMIT = 7.0
SWIGLU_ALPHA = 1.702

T_CTX = BATCH * SEQ
T_LAT = DEC_BATCH * DEC_SEQ
T_ALL = T_CTX + T_LAT
N_COND = 8
VMEM_LIMIT = 56 * 1024 * 1024

COL_XP = 0
COL_XQ = COL_XP + POOL_WIDTH
COL_QKV = COL_XQ + FFT_WIDTH
COL_UV = COL_QKV + QKV_WIDTH
COL_GATE = COL_UV + 2 * SGU_WIDTH


def _params(*sem):
    return pltpu.CompilerParams(dimension_semantics=sem, vmem_limit_bytes=VMEM_LIMIT)


def _split_bf16(x):
    hi = x.astype(BF16)
    lo = (x - hi.astype(F32)).astype(BF16)
    return hi, lo


def _dot(a, b):
    return jnp.dot(a, b, preferred_element_type=F32)


def _dot_nt(a, b):
    return lax.dot_general(a, b, (((1,), (1,)), ((), ())), preferred_element_type=F32)


def _group_lane_select(lane, vals, width):
    out = vals[-1]
    for g in range(len(vals) - 2, -1, -1):
        out = jnp.where(lane < (g + 1) * width, vals[g], out)
    return out


def _cond_row(blk, blocks_ctx, blocks_per_latent):
    return jnp.where(blk < blocks_ctx, 0, 1 + (blk - blocks_ctx) // blocks_per_latent)


MOD_TN = 1536


def _mod_kernel(c_ref, w_ref, b_ref, o_ref):
    c = c_ref[...]
    s = c * jax.nn.sigmoid(c)
    sh, sl = _split_bf16(s)
    wh, wl = _split_bf16(w_ref[...])
    o_ref[...] = _dot(sh, wh) + _dot(sh, wl) + _dot(sl, wh) + b_ref[...]


def _modulation(cond, w_mod, b_mod):
    n_cols = 6 * D_MODEL
    return pl.pallas_call(
        _mod_kernel,
        grid=(DEPTH, n_cols // MOD_TN),
        in_specs=[
            pl.BlockSpec((N_COND, D_MODEL), lambda l, j: (0, 0)),
            pl.BlockSpec((None, D_MODEL, MOD_TN), lambda l, j: (l, 0, j)),
            pl.BlockSpec((None, 1, MOD_TN), lambda l, j: (l, 0, j)),
        ],
        out_specs=pl.BlockSpec((None, N_COND, MOD_TN), lambda l, j: (l, 0, j)),
        out_shape=jax.ShapeDtypeStruct((DEPTH, N_COND, n_cols), F32),
        compiler_params=_params("arbitrary", "arbitrary"),
        name="modulation",
    )(cond, w_mod, b_mod.reshape(DEPTH, 1, n_cols))


INPROJ_TM = 256
INPROJ_SEGMENTS = (
    (COL_XP, POOL_WIDTH), (COL_XQ, FFT_WIDTH), (COL_QKV, QKV_WIDTH),
    (COL_UV, 2 * SGU_WIDTH), (COL_GATE, GATE_WIDTH))
INPROJ_CHUNK = 1024


def _ada_norm(x, g, shift, scale):
    xn = x * lax.rsqrt(jnp.mean(x * x, axis=-1, keepdims=True) + EPS)
    return xn * g * (1.0 + scale) + shift


def _token_specs(parts, tm, width):
    if len(parts) == 1:
        return [pl.BlockSpec((tm, width), lambda i: (i, 0))]
    nc = T_CTX // tm
    return [pl.BlockSpec((tm, width), lambda i: (jnp.minimum(i, nc - 1), 0)),
            pl.BlockSpec((tm, width), lambda i: (jnp.maximum(i - nc, 0), 0))]


def _token_load(refs, tm):
    if len(refs) == 1:
        return refs[0][...]
    return jnp.where(pl.program_id(0) < T_CTX // tm, refs[0][...], refs[1][...])


def _inproj_kernel(*refs, n_x):
    x_refs, (mod_ref, g_ref, w_ref), out_refs = refs[:n_x], refs[n_x:n_x + 3], refs[n_x + 3:]
    h = _ada_norm(_token_load(x_refs, INPROJ_TM), g_ref[...], mod_ref[0:1, :], mod_ref[1:2, :])
    hb = h.astype(BF16)
    for (col, width), o_ref in zip(INPROJ_SEGMENTS, out_refs):
        for c0 in range(0, width, INPROJ_CHUNK):
            c1 = min(c0 + INPROJ_CHUNK, width)
            o_ref[:, c0:c1] = _dot(hb, w_ref[:, col + c0:col + c1])


def _in_projection(x_parts, mod_l, g, w_in_bf16):
    tm = INPROJ_TM
    row = functools.partial(_cond_row, blocks_ctx=T_CTX // tm, blocks_per_latent=DEC_SEQ // tm)
    return pl.pallas_call(
        functools.partial(_inproj_kernel, n_x=len(x_parts)),
        grid=(T_ALL // tm,),
        in_specs=_token_specs(x_parts, tm, D_MODEL) + [
            pl.BlockSpec((None, 6, D_MODEL), lambda i: (row(i), 0, 0)),
            pl.BlockSpec((1, D_MODEL), lambda i: (0, 0)),
            pl.BlockSpec((D_MODEL, IN_COLS), lambda i: (0, 0)),
        ],
        out_specs=[pl.BlockSpec((tm, w), lambda i: (i, 0)) for _, w in INPROJ_SEGMENTS],
        out_shape=[jax.ShapeDtypeStruct((T_ALL, w), F32) for _, w in INPROJ_SEGMENTS],
        compiler_params=_params("arbitrary"),
        name="in_projection",
    )(*x_parts, mod_l, g.reshape(1, D_MODEL), w_in_bf16)


def _pool_mixer(xp, invcnt, w_bd, scale):
    s = xp.shape[0]
    n = s + 2 * POOL_PAD
    zeros = jnp.zeros((POOL_PAD, POOL_WIDTH), F32)
    xe = jnp.concatenate([zeros, xp, zeros], axis=0)

    def shift(a, k):
        return pltpu.roll(a, k % n, 0)

    s2 = xe + shift(xe, 1)
    s4 = shift(s2, 1) + shift(s2, -1)
    s8 = shift(s4, 2) + shift(s4, -2)
    s16 = shift(s8, 4) + shift(s8, -4)
    lane = lax.broadcasted_iota(jnp.int32, (1, POOL_WIDTH), 1)
    total = _group_lane_select(lane, [s2, s4, s8, s16], POOL_CH)[POOL_PAD:POOL_PAD + s]
    pooled = total * invcnt - xp
    return _dot(pooled.astype(BF16), w_bd) * scale


def _fourier_mixer(xq, f_cos_ch, f_sin_ch, f_pos):
    xb = xq.astype(BF16)
    a = _dot(xb, f_cos_ch).astype(BF16)
    b = _dot(xb, f_sin_ch).astype(BF16)
    return _dot(f_pos, jnp.concatenate([a, b], axis=0))


def _group_mean_sq(x, ones_bd, width):
    hi, lo = _split_bf16(x * x)
    return (_dot(hi, ones_bd) + _dot(lo, ones_bd)) * (1.0 / width)


def _mixers_kernel(xp_ref, xq_ref, uv_ref, invcnt_ref, pool_w_ref, pool_s_ref, fcc_ref, fsc_ref,
                   fpos_ref, ones_ref, sgu_w_ref, sgu_b_ref, pool_o, four_o, sgu_o):
    s = xp_ref.shape[0]
    pool_o[...] = _pool_mixer(xp_ref[...], invcnt_ref[...], pool_w_ref[...],
                              pool_s_ref[...]).astype(BF16)
    four_o[...] = _fourier_mixer(xq_ref[...], fcc_ref[...], fsc_ref[...], fpos_ref[...]).astype(BF16)

    act = jax.nn.gelu(uv_ref[...], approximate=True)
    u = act[:, :SGU_WIDTH]
    v = act[:, SGU_WIDTH:]
    vg = (v * lax.rsqrt(_group_mean_sq(v, ones_ref[...], SGU_CH) + EPS)).astype(BF16)
    lane = lax.broadcasted_iota(jnp.int32, (1, SGU_WIDTH), 1)
    w_stack = sgu_w_ref[...]
    bias = sgu_b_ref[...]
    for n in range(s // SGU_CHUNK):
        rows = slice(n * SGU_CHUNK, (n + 1) * SGU_CHUNK)
        r = _dot(w_stack, vg[rows])
        per_group = [r[g * SGU_CHUNK:(g + 1) * SGU_CHUNK] for g in range(SGU_GROUPS)]
        spatial = _group_lane_select(lane, per_group, SGU_CH) + bias
        sgu_o[rows, :] = (u[rows] * spatial).astype(BF16)


def _mixers(xp, xq, uv, consts, lw, seq, n_seq, block0):
    full = lambda shape: pl.BlockSpec(shape, lambda b: (0,) * len(shape))
    tok = lambda w: pl.BlockSpec((seq, w), lambda b: (block0 + b, 0))
    out = lambda: pl.BlockSpec((seq, POOL_WIDTH), lambda b: (b, 0))
    return pl.pallas_call(
        _mixers_kernel,
        grid=(n_seq,),
        in_specs=[
            tok(POOL_WIDTH), tok(FFT_WIDTH), tok(2 * SGU_WIDTH),
            full((seq, POOL_WIDTH)), full((POOL_WIDTH, POOL_WIDTH)), full((1, POOL_WIDTH)),
            full((FFT_WIDTH, FFT_WIDTH)), full((FFT_WIDTH, FFT_WIDTH)), full((seq, 2 * seq)),
            full((SGU_WIDTH, SGU_WIDTH)), full((SGU_GROUPS * SGU_CHUNK, SGU_CHUNK)),
            full((SGU_CHUNK, SGU_WIDTH)),
        ],
        out_specs=[out(), out(), out()],
        out_shape=[jax.ShapeDtypeStruct((n_seq * seq, POOL_WIDTH), BF16)] * 3,
        compiler_params=_params("arbitrary"),
        name=f"mixers_s{seq}",
    )(xp, xq, uv, consts["invcnt"], lw["pool_w_bd"], lw["pool_scale"], consts["f_cos_ch"],
      consts["f_sin_ch"], consts["f_pos"], consts["ones96"], lw["sgu_w_stack"], lw["sgu_bias"])


def _head_norm(x, ones_bd, g):
    return x * lax.rsqrt(_group_mean_sq(x, ones_bd, HEAD_DIM) + EPS) * g


def _rope(x, cos, sin_next, sin_prev):
    cols = []
    for c in range(x.shape[1] // 128):
        xc = x[:, c * 128:(c + 1) * 128]
        nxt = pltpu.roll(xc, 127, 1)
        prv = pltpu.roll(xc, 1, 1)
        cols.append(xc * cos + nxt * sin_next + prv * sin_prev)
    return cols[0] if len(cols) == 1 else jnp.concatenate(cols, axis=1)


def _attend(q, keys, vals, o_ref):
    qb = (q * (HEAD_DIM ** -0.5)).astype(BF16)
    group = N_HEADS // N_KV_HEADS
    for h in range(N_HEADS):
        j = h // group
        kh = keys[:, j * HEAD_DIM:(j + 1) * HEAD_DIM]
        vh = vals[:, j * HEAD_DIM:(j + 1) * HEAD_DIM]
        s = _dot_nt(qb[:, h * HEAD_DIM:(h + 1) * HEAD_DIM], kh)
        p = jnp.exp(s - jnp.max(s, axis=-1, keepdims=True))
        denom = jnp.sum(p, axis=-1, keepdims=True)
        o = _dot(p.astype(BF16), vh) / denom
        o_ref[:, h * HEAD_DIM:(h + 1) * HEAD_DIM] = o.astype(BF16)


def _attn_ctx_kernel(qkv_ref, gq_ref, gk_ref, ones_ref, o_ref, k_ref, v_ref):
    qkv = qkv_ref[...]
    ones = ones_ref[...]
    q = _head_norm(qkv[:, :ATTN_WIDTH], ones, gq_ref[...])
    k = _head_norm(qkv[:, ATTN_WIDTH:ATTN_WIDTH + KV_WIDTH], ones[:KV_WIDTH, :KV_WIDTH], gk_ref[...])
    v = qkv[:, ATTN_WIDTH + KV_WIDTH:]
    k_ref[...] = k
    v_ref[...] = v
    _attend(q, k.astype(BF16), v.astype(BF16), o_ref)


def _attention_ctx(qkv, consts, lw):
    full = lambda shape: pl.BlockSpec(shape, lambda b: (0,) * len(shape))
    return pl.pallas_call(
        _attn_ctx_kernel,
        grid=(BATCH,),
        in_specs=[pl.BlockSpec((SEQ, QKV_WIDTH), lambda b: (b, 0)),
                  full((1, ATTN_WIDTH)), full((1, KV_WIDTH)), full((ATTN_WIDTH, ATTN_WIDTH))],
        out_specs=[pl.BlockSpec((SEQ, ATTN_WIDTH), lambda b: (b, 0)),
                   pl.BlockSpec((SEQ, KV_WIDTH), lambda b: (b, 0)),
                   pl.BlockSpec((SEQ, KV_WIDTH), lambda b: (b, 0))],
        out_shape=[jax.ShapeDtypeStruct((T_CTX, ATTN_WIDTH), BF16),
                   jax.ShapeDtypeStruct((T_CTX, KV_WIDTH), F32),
                   jax.ShapeDtypeStruct((T_CTX, KV_WIDTH), F32)],
        compiler_params=_params("arbitrary"),
        name="attention_ctx",
    )(qkv, lw["gq"], lw["gk"], consts["ones64"])


LAT_QBLK = 256


def _attn_lat_kernel(q_ref, kv_ref, ck_ref, cv_ref, gq_ref, gk_ref, ones_ref, cos_q, sn_q, sp_q,
                     cos_k, sn_k, sp_k, o_ref, keys, vals):
    ones = ones_ref[...]

    @pl.when(pl.program_id(1) == 0)
    def _():
        kv = kv_ref[...]
        k = _head_norm(kv[:, :KV_WIDTH], ones[:KV_WIDTH, :KV_WIDTH], gk_ref[...])
        keys[0:DEC_SEQ, :] = _rope(k, cos_k[...], sn_k[...], sp_k[...]).astype(BF16)
        keys[DEC_SEQ:, :] = ck_ref[...].astype(BF16)
        vals[0:DEC_SEQ, :] = kv[:, KV_WIDTH:].astype(BF16)
        vals[DEC_SEQ:, :] = cv_ref[...].astype(BF16)

    q = _head_norm(q_ref[...], ones, gq_ref[...])
    q = _rope(q, cos_q[...], sn_q[...], sp_q[...])
    _attend(q, keys[...], vals[...], o_ref)


def _attention_lat(qkv, cache_k_l, cache_v_l, consts, lw):
    nq = DEC_SEQ // LAT_QBLK
    q0 = T_CTX // LAT_QBLK
    s0 = T_CTX // DEC_SEQ
    full = lambda shape: pl.BlockSpec(shape, lambda b, j: (0,) * len(shape))
    rope_q = lambda: pl.BlockSpec((LAT_QBLK, 128), lambda b, j: (j, 0))
    rope_k = lambda: pl.BlockSpec((DEC_SEQ, 128), lambda b, j: (0, 0))
    return pl.pallas_call(
        _attn_lat_kernel,
        grid=(DEC_BATCH, nq),
        in_specs=[
            pl.BlockSpec((LAT_QBLK, ATTN_WIDTH), lambda b, j: (q0 + b * nq + j, 0)),
            pl.BlockSpec((DEC_SEQ, 2 * KV_WIDTH), lambda b, j: (s0 + b, ATTN_WIDTH // (2 * KV_WIDTH))),
            pl.BlockSpec((None, PAST_LEN, KV_WIDTH), lambda b, j: (b, 0, 0)),
            pl.BlockSpec((None, PAST_LEN, KV_WIDTH), lambda b, j: (b, 0, 0)),
            full((1, ATTN_WIDTH)), full((1, KV_WIDTH)), full((ATTN_WIDTH, ATTN_WIDTH)),
            rope_q(), rope_q(), rope_q(), rope_k(), rope_k(), rope_k(),
        ],
        out_specs=pl.BlockSpec((LAT_QBLK, ATTN_WIDTH), lambda b, j: (b * nq + j, 0)),
        out_shape=jax.ShapeDtypeStruct((T_LAT, ATTN_WIDTH), BF16),
        scratch_shapes=[pltpu.VMEM((DEC_SEQ + PAST_LEN, KV_WIDTH), BF16),
                        pltpu.VMEM((DEC_SEQ + PAST_LEN, KV_WIDTH), BF16)],
        compiler_params=_params("arbitrary", "arbitrary"),
        name="attention_lat",
    )(qkv, qkv, cache_k_l, cache_v_l, lw["gq"], lw["gk"], consts["ones64"],
      consts["rope_cos"], consts["rope_sin_next"], consts["rope_sin_prev"],
      consts["rope_cos"], consts["rope_sin_next"], consts["rope_sin_prev"])


MERGE_TM = 256


def _merge_kernel(*refs, n_x):
    branch_refs, gl_ref, x_refs = refs[:8], refs[8], refs[9:9 + n_x]
    (mod_ref, g_ref, wp_ref, wf_ref, wa_ref, ws_ref, wo_ref, rw_ref, rb_ref,
     x_o, h_o, idx_o, wgt_o) = refs[9 + n_x:]
    merged = None
    for i, w_ref in enumerate((wp_ref, wf_ref, wa_ref, ws_ref)):
        br = _dot(_token_load(branch_refs[2 * i:2 * i + 2], MERGE_TM), w_ref[...])
        term = jax.nn.sigmoid(gl_ref[:, i * D_MODEL:(i + 1) * D_MODEL]) * br
        merged = term if merged is None else merged + term
    mix = _dot(merged.astype(BF16), wo_ref[...])
    x = _token_load(x_refs, MERGE_TM) + mod_ref[2:3, :] * mix
    x_o[...] = x
    h = _ada_norm(x, g_ref[...], mod_ref[3:4, :], mod_ref[4:5, :])
    h_o[...] = h

    hh, hl = _split_bf16(h)
    rh, rl = _split_bf16(rw_ref[...])
    logits = _dot(hh, rh) + _dot(hh, rl) + _dot(hl, rh) + rb_ref[...]
    lane = lax.broadcasted_iota(jnp.int32, logits.shape, 1).astype(F32)
    slot = lax.broadcasted_iota(jnp.int32, (logits.shape[0], TOP_K), 1)
    work = logits
    top = jnp.max(logits, axis=-1, keepdims=True)
    idx = jnp.zeros(slot.shape, F32)
    wgt = jnp.zeros(slot.shape, F32)
    denom = jnp.zeros_like(top)
    for k in range(TOP_K):
        m = jnp.max(work, axis=-1, keepdims=True)
        first = jnp.min(jnp.where(work == m, lane, float(N_EXPERTS)), axis=-1, keepdims=True)
        e = jnp.exp(m - top)
        idx = jnp.where(slot == k, first, idx)
        wgt = jnp.where(slot == k, e, wgt)
        denom = denom + e
        work = jnp.where(lane == first, -jnp.inf, work)
    idx_o[...] = idx.astype(jnp.int32)
    wgt_o[...] = wgt / denom


def _merge(pool, four, attn, sgu, gl, x_parts, mod_l, g_ffn, lw):
    tm = MERGE_TM
    row = functools.partial(_cond_row, blocks_ctx=T_CTX // tm, blocks_per_latent=DEC_SEQ // tm)
    full = lambda shape: pl.BlockSpec(shape, lambda i: (0,) * len(shape))
    tok = lambda w: pl.BlockSpec((tm, w), lambda i: (i, 0))
    branch_specs = []
    for pair, width in ((pool, POOL_WIDTH), (four, FFT_WIDTH), (attn, ATTN_WIDTH), (sgu, SGU_WIDTH)):
        branch_specs += _token_specs(pair, tm, width)
    return pl.pallas_call(
        functools.partial(_merge_kernel, n_x=len(x_parts)),
        grid=(T_ALL // tm,),
        in_specs=branch_specs + [tok(GATE_WIDTH)] + _token_specs(x_parts, tm, D_MODEL) + [
            pl.BlockSpec((None, 6, D_MODEL), lambda i: (row(i), 0, 0)),
            full((1, D_MODEL)),
            full((POOL_WIDTH, D_MODEL)), full((FFT_WIDTH, D_MODEL)), full((ATTN_WIDTH, D_MODEL)),
            full((SGU_WIDTH, D_MODEL)), full((D_MODEL, D_MODEL)),
            full((D_MODEL, N_EXPERTS)), full((1, N_EXPERTS)),
        ],
        out_specs=[tok(D_MODEL), tok(D_MODEL), tok(TOP_K), tok(TOP_K)],
        out_shape=[jax.ShapeDtypeStruct((T_ALL, D_MODEL), F32),
                   jax.ShapeDtypeStruct((T_ALL, D_MODEL), F32),
                   jax.ShapeDtypeStruct((T_ALL, TOP_K), jnp.int32),
                   jax.ShapeDtypeStruct((T_ALL, TOP_K), F32)],
        compiler_params=_params("arbitrary"),
        name="merge_router",
    )(*pool, *four, *attn, *sgu, gl, *x_parts, mod_l, g_ffn.reshape(1, D_MODEL),
      lw["w_br_pool"], lw["w_br_fourier"], lw["w_br_attn"], lw["w_br_sgu"], lw["w_out"],
      lw["router_w"], lw["router_b"])


N_PAIRS = T_ALL * TOP_K
FFN_TM = 512
N_ROW_TILES = N_PAIRS // FFN_TM + N_EXPERTS
N_SLOTS = N_ROW_TILES * FFN_TM
ROUTE_TB = 256


def _route(top_idx):
    e_flat = top_idx.reshape(N_PAIRS)
    onehot = (e_flat[:, None] == jnp.arange(N_EXPERTS, dtype=jnp.int32)[None, :]).astype(jnp.int32)
    csum = jnp.cumsum(onehot, axis=0)
    tiles = (csum[-1] + FFN_TM - 1) // FFN_TM
    tile_end = jnp.cumsum(tiles)
    row_start = (tile_end - tiles) * FFN_TM
    slot = jnp.sum(onehot * (csum - 1 + row_start[None, :]), axis=1)
    tile_ids = jnp.arange(N_ROW_TILES, dtype=jnp.int32)
    tile_expert = jnp.sum((tile_end[None, :] <= tile_ids[:, None]).astype(jnp.int32), axis=1)
    tile_expert = jnp.minimum(tile_expert, N_EXPERTS - 1)
    rows_before = (tile_ids - (tile_end - tiles)[tile_expert]) * FFN_TM
    tile_rows = jnp.where(tile_ids < tile_end[-1],
                          jnp.clip(csum[-1][tile_expert] - rows_before, 0, FFN_TM), 0)
    slot = slot.astype(jnp.int32).reshape(T_ALL // ROUTE_TB, 1, ROUTE_TB * TOP_K)
    return (slot, tile_expert.astype(jnp.int32), tile_rows.astype(jnp.int32),
            tile_end[-1:].astype(jnp.int32))


def _start_pair_rows(copy):
    def body(g, carry):
        r0 = pl.multiple_of(g * 8, 8)
        for u in range(8):
            for k in range(TOP_K):
                copy(r0 + u, k).start(priority=k % 2)
        return carry

    lax.fori_loop(0, ROUTE_TB // 8, body, 0)


def _dispatch_kernel(slot_ref, h_ref, xs_in, xs_out, sem):
    del xs_in

    def copy(r, k):
        return pltpu.make_async_copy(h_ref.at[pl.ds(r, 1), :],
                                     xs_out.at[pl.ds(slot_ref[0, r * TOP_K + k], 1), :], sem)

    _start_pair_rows(copy)
    for _ in range(TOP_K):
        pltpu.make_async_copy(h_ref, xs_out.at[pl.ds(0, ROUTE_TB), :], sem).wait()


def _dispatch(slot, h):
    zeros = jnp.zeros((N_SLOTS, D_MODEL), F32)
    return pl.pallas_call(
        _dispatch_kernel,
        grid=(T_ALL // ROUTE_TB,),
        in_specs=[
            pl.BlockSpec((None, 1, ROUTE_TB * TOP_K), lambda i: (i, 0, 0), memory_space=pltpu.SMEM),
            pl.BlockSpec((ROUTE_TB, D_MODEL), lambda i: (i, 0)),
            pl.BlockSpec(memory_space=pl.ANY),
        ],
        out_specs=pl.BlockSpec(memory_space=pl.ANY),
        out_shape=jax.ShapeDtypeStruct((N_SLOTS, D_MODEL), F32),
        scratch_shapes=[pltpu.SemaphoreType.DMA(())],
        input_output_aliases={2: 0},
        compiler_params=_params("arbitrary"),
        name="moe_dispatch",
    )(slot, h, zeros)


def _expert_ffn(xb, wg, bg, wu, bu, wd, bd):
    gate = jnp.minimum(_dot(xb, wg) + bg, SWIGLU_LIMIT)
    up = jnp.clip(_dot(xb, wu) + bu, -SWIGLU_LIMIT, SWIGLU_LIMIT)
    glu = gate * jax.nn.sigmoid(SWIGLU_ALPHA * gate)
    return _dot(((up + 1.0) * glu).astype(BF16), wd) + bd


def _ffn_kernel(te_ref, tr_ref, nu_ref, xs_ref, wg_ref, bg_ref, wu_ref, bu_ref, wd_ref, bd_ref,
                ys_ref, w_bf16):
    i = pl.program_id(0)
    n_rows = tr_ref[i]
    half = FFN_TM // 2

    def ffn(rows):
        return _expert_ffn(xs_ref[rows, :].astype(BF16), w_bf16[0], bg_ref[...], w_bf16[1],
                           bu_ref[...], w_bf16[2], bd_ref[...])

    @pl.when((n_rows > 0) & ((i == 0) | (te_ref[i] != te_ref[jnp.maximum(i - 1, 0)])))
    def _():
        w_bf16[0] = wg_ref[...].astype(BF16)
        w_bf16[1] = wu_ref[...].astype(BF16)
        w_bf16[2] = wd_ref[...].astype(BF16)

    @pl.when(n_rows > half)
    def _():
        ys_ref[...] = ffn(slice(None))

    @pl.when(n_rows <= half)
    def _():
        ys_ref[half:, :] = jnp.zeros((half, D_MODEL), F32)

        @pl.when(n_rows > 0)
        def _():
            ys_ref[:half, :] = ffn(slice(0, half))

        @pl.when(n_rows == 0)
        def _():
            ys_ref[:half, :] = jnp.zeros((half, D_MODEL), F32)


def _routed_ffn(tile_expert, tile_rows, n_used, xs, l, p):
    rows = pl.BlockSpec((FFN_TM, D_MODEL), lambda i, te, tr, nu: (jnp.minimum(i, nu[0] - 1), 0))
    wspec = lambda a, b: pl.BlockSpec((None, None, a, b), lambda i, te, tr, nu: (l, te[i], 0, 0))
    bias = lambda a: a.reshape(DEPTH, N_EXPERTS, 1, a.shape[-1])
    return pl.pallas_call(
        _ffn_kernel,
        grid_spec=pltpu.PrefetchScalarGridSpec(
            num_scalar_prefetch=3,
            grid=(N_ROW_TILES,),
            in_specs=[rows, wspec(D_MODEL, D_FF), wspec(1, D_FF), wspec(D_MODEL, D_FF),
                      wspec(1, D_FF), wspec(D_FF, D_MODEL), wspec(1, D_MODEL)],
            out_specs=pl.BlockSpec((FFN_TM, D_MODEL), lambda i, te, tr, nu: (i, 0)),
            scratch_shapes=[pltpu.VMEM((3, D_MODEL, D_FF), BF16)],
        ),
        out_shape=jax.ShapeDtypeStruct((N_SLOTS, D_MODEL), F32),
        compiler_params=_params("arbitrary"),
        name="moe_ffn",
    )(tile_expert, tile_rows, n_used, xs, p["moe_w_gate"], bias(p["moe_b_gate"]), p["moe_w_up"],
      bias(p["moe_b_up"]), p["moe_w_down"], bias(p["moe_b_down"]))


COMBINE_CHUNK = 256


def _combine_kernel(slot_ref, slot_next_ref, w_ref, x_ref, mod_ref, fg_ref, ys_ref, o_ref, buf, sems,
                    *, final_norm):
    i = pl.program_id(0)
    cur = i % 2

    def copy(slots, b, r, k):
        return pltpu.make_async_copy(ys_ref.at[pl.ds(slots[0, r * TOP_K + k], 1), :],
                                     buf.at[b, k, pl.ds(r, 1), :], sems.at[b])

    @pl.when(i == 0)
    def _():
        _start_pair_rows(functools.partial(copy, slot_ref, 0))

    @pl.when(i + 1 < pl.num_programs(0))
    def _():
        _start_pair_rows(functools.partial(copy, slot_next_ref, 1 - cur))

    for k in range(TOP_K):
        pltpu.make_async_copy(ys_ref.at[pl.ds(0, ROUTE_TB), :], buf.at[cur, k], sems.at[cur]).wait()

    w = w_ref[...]
    sum_sq = jnp.zeros((ROUTE_TB, 1), F32)
    for c0 in range(0, D_MODEL, COMBINE_CHUNK):
        cols = slice(c0, c0 + COMBINE_CHUNK)
        acc = w[:, 0:1] * buf[cur, 0, :, cols]
        for k in range(1, TOP_K):
            acc = acc + w[:, k:k + 1] * buf[cur, k, :, cols]
        piece = x_ref[:, cols] + mod_ref[5:6, cols] * acc
        sum_sq = sum_sq + jnp.sum(piece * piece, axis=-1, keepdims=True)
        o_ref[:, cols] = piece
    if final_norm:
        o_ref[...] = o_ref[...] * lax.rsqrt(sum_sq * (1.0 / D_MODEL) + EPS) * fg_ref[...]


def _combine(slot, top_w, x, mod_l, final_g, ys, final_norm):
    tb = ROUTE_TB
    nb = T_ALL // tb
    row = functools.partial(_cond_row, blocks_ctx=T_CTX // tb, blocks_per_latent=DEC_SEQ // tb)
    tok = lambda w: pl.BlockSpec((tb, w), lambda i: (i, 0))
    slots = lambda ahead: pl.BlockSpec((None, 1, tb * TOP_K),
                                       lambda i: (jnp.minimum(i + ahead, nb - 1), 0, 0),
                                       memory_space=pltpu.SMEM)
    return pl.pallas_call(
        functools.partial(_combine_kernel, final_norm=final_norm),
        grid=(nb,),
        in_specs=[
            slots(0), slots(1), tok(TOP_K), tok(D_MODEL),
            pl.BlockSpec((None, 6, D_MODEL), lambda i: (row(i), 0, 0)),
            pl.BlockSpec((1, D_MODEL), lambda i: (0, 0)),
            pl.BlockSpec(memory_space=pl.ANY),
        ],
        out_specs=tok(D_MODEL),
        out_shape=jax.ShapeDtypeStruct((T_ALL, D_MODEL), F32),
        scratch_shapes=[pltpu.VMEM((2, TOP_K, tb, D_MODEL), F32), pltpu.SemaphoreType.DMA((2,))],
        compiler_params=_params("arbitrary"),
        name="moe_combine",
    )(slot, slot, top_w, x, mod_l, final_g.reshape(1, D_MODEL), ys)


def _block_diag(blocks):
    g, a, b = blocks.shape
    eye = jnp.eye(g, dtype=blocks.dtype)
    return (eye[:, None, :, None] * blocks[:, :, None, :]).reshape(g * a, g * b)


def _dft_tables(n):
    k = np.arange(n, dtype=np.int64)
    ang = 2.0 * np.pi * ((k[:, None] * k[None, :]) % n).astype(np.float64) / n
    return np.cos(ang) / math.sqrt(n), np.sin(ang) / math.sqrt(n)


def _seq_constants(seq):
    t = np.arange(seq)[:, None]
    win = np.array(POOL_WINDOWS)[None, :]
    lo = np.clip(t - win // 2, 0, seq)
    hi = np.clip(t - win // 2 + win, 0, seq)
    invcnt = np.repeat(1.0 / (hi - lo).astype(np.float64), POOL_CH, axis=1)
    cos_p, sin_p = _dft_tables(seq)
    cos_c, sin_c = _dft_tables(FFT_CH)
    eye = np.eye(POOL_GROUPS)
    as_bf16 = lambda a: jnp.asarray(a, F32).astype(BF16)
    return {
        "invcnt": jnp.asarray(invcnt, F32),
        "f_pos": as_bf16(np.concatenate([cos_p, -sin_p], axis=1)),
        "f_cos_ch": as_bf16(np.kron(eye, cos_c)),
        "f_sin_ch": as_bf16(np.kron(eye, sin_c)),
        "ones96": as_bf16(np.kron(eye, np.ones((SGU_CH, SGU_CH)))),
    }


def _rope_constants():
    rows = DEC_SEQ // GRID_W
    row = jnp.repeat(jnp.arange(rows, dtype=F32), GRID_W)
    col = jnp.tile(jnp.arange(GRID_W, dtype=F32), rows)
    freqs = ROPE_THETA ** (-jnp.arange(ROPE_PAIRS, dtype=F32) / ROPE_PAIRS)
    ang = jnp.stack([row[:, None] * freqs, col[:, None] * freqs], axis=1)
    ang = jnp.repeat(ang.reshape(DEC_SEQ, 2 * ROPE_PAIRS), 2, axis=1)
    ang = jnp.tile(ang, (1, 128 // HEAD_DIM))
    even = (jnp.arange(128) % 2 == 0)[None, :]
    sin = jnp.sin(ang)
    return {
        "rope_cos": jnp.cos(ang),
        "rope_sin_next": jnp.where(even, -sin, 0.0),
        "rope_sin_prev": jnp.where(even, 0.0, sin),
    }


def _layer_weights(p, l):
    group_of_lane = np.arange(SGU_WIDTH) // SGU_CH
    return {
        "w_in": p["w_in"][l].astype(BF16),
        "pool_w_bd": _block_diag(p["pool_w"][l]).astype(BF16),
        "pool_scale": p["pool_scale"][l].reshape(1, POOL_WIDTH),
        "sgu_w_stack": p["sgu_w"][l].reshape(SGU_GROUPS * SGU_CHUNK, SGU_CHUNK).astype(BF16),
        "sgu_bias": p["sgu_b"][l].T[:, group_of_lane],
        "gq": jnp.tile(p["q_norm_g"][l], N_HEADS).reshape(1, ATTN_WIDTH),
        "gk": jnp.tile(p["k_norm_g"][l], N_KV_HEADS).reshape(1, KV_WIDTH),
        "w_br_pool": p["w_br_pool"][l].astype(BF16),
        "w_br_fourier": p["w_br_fourier"][l].astype(BF16),
        "w_br_attn": p["w_br_attn"][l].astype(BF16),
        "w_br_sgu": p["w_br_sgu"][l].astype(BF16),
        "w_out": p["w_out"][l].astype(BF16),
        "router_w": p["router_w"][l],
        "router_b": p["router_b"][l].reshape(1, N_EXPERTS),
    }


def kernel(x_prompt, x_sample, cache_k, cache_v, c, c_ctx, w_mod, b_mod, norm_mix_g, norm_ffn_g, w_in, pool_w, pool_scale, q_norm_g, k_norm_g, sgu_w, sgu_b, w_br_pool, w_br_fourier, w_br_attn, w_br_sgu, w_out, router_w, router_b, moe_w_gate, moe_b_gate, moe_w_up, moe_b_up, moe_w_down, moe_b_down, final_norm_g):
    p = dict(w_in=w_in, pool_w=pool_w, pool_scale=pool_scale, q_norm_g=q_norm_g, k_norm_g=k_norm_g,
             sgu_w=sgu_w, sgu_b=sgu_b, w_br_pool=w_br_pool, w_br_fourier=w_br_fourier,
             w_br_attn=w_br_attn, w_br_sgu=w_br_sgu, w_out=w_out, router_w=router_w,
             router_b=router_b, moe_w_gate=moe_w_gate, moe_b_gate=moe_b_gate, moe_w_up=moe_w_up,
             moe_b_up=moe_b_up, moe_w_down=moe_w_down, moe_b_down=moe_b_down)

    cond = jnp.concatenate([c_ctx[None, :], c, jnp.zeros((N_COND - 1 - DEC_BATCH, D_MODEL), F32)])
    mod = _modulation(cond, w_mod, b_mod).reshape(DEPTH, N_COND, 6, D_MODEL)

    ones64 = jnp.asarray(np.kron(np.eye(N_HEADS), np.ones((HEAD_DIM, HEAD_DIM))), BF16)
    consts_ctx = dict(_seq_constants(SEQ), ones64=ones64)
    consts_lat = dict(_seq_constants(DEC_SEQ), ones64=ones64, **_rope_constants())
    ck = cache_k.reshape(DEC_BATCH, DEPTH, PAST_LEN, KV_WIDTH)
    cv = cache_v.reshape(DEC_BATCH, DEPTH, PAST_LEN, KV_WIDTH)

    x_parts = (x_prompt.reshape(T_CTX, D_MODEL), x_sample.reshape(T_LAT, D_MODEL))
    new_k, new_v = [], []
    for l in range(DEPTH):
        lw = _layer_weights(p, l)
        xp, xq, qkv, uv, gl = _in_projection(x_parts, mod[l], norm_mix_g[l], lw["w_in"])
        pool_c, four_c, sgu_c = _mixers(xp, xq, uv, consts_ctx, lw, SEQ, BATCH, 0)
        pool_l, four_l, sgu_l = _mixers(xp, xq, uv, consts_lat, lw, DEC_SEQ, DEC_BATCH,
                                        T_CTX // DEC_SEQ)
        attn_c, k_c, v_c = _attention_ctx(qkv, consts_ctx, lw)
        attn_l = _attention_lat(qkv, ck[:, l], cv[:, l], consts_lat, lw)
        new_k.append(k_c.reshape(BATCH, SEQ, N_KV_HEADS, HEAD_DIM))
        new_v.append(v_c.reshape(BATCH, SEQ, N_KV_HEADS, HEAD_DIM))
        x, h, top_idx, top_w = _merge((pool_c, pool_l), (four_c, four_l), (attn_c, attn_l),
                                      (sgu_c, sgu_l), gl, x_parts, mod[l], norm_ffn_g[l], lw)
        slot, tile_expert, tile_rows, n_used = _route(top_idx)
        ys = _routed_ffn(tile_expert, tile_rows, n_used, _dispatch(slot, h), l, p)
        x = _combine(slot, top_w, x, mod[l], final_norm_g, ys, final_norm=(l == DEPTH - 1))
        x_parts = (x,)

    y_prompt = x[:T_CTX].reshape(BATCH, SEQ, D_MODEL)
    y_sample = x[T_CTX:].reshape(DEC_BATCH, DEC_SEQ, D_MODEL)
    return (y_prompt, y_sample, jnp.stack(new_k, axis=1), jnp.stack(new_v, axis=1))
```

```python
import functools
import math

import numpy as np
import jax
import jax.numpy as jnp
from jax import lax
from jax.experimental import pallas as pl
from jax.experimental.pallas import tpu as pltpu

F32 = jnp.float32
BF16 = jnp.bfloat16

D_MODEL = 1024
BATCH = 32
SEQ = 256
DEPTH = 2
DEC_BATCH = 2
DEC_SEQ = 1024
PAST_LEN = 256
GRID_W = 64
EPS = 1e-6
POOL_GROUPS = 4
POOL_CH = 96
POOL_WIDTH = 384
POOL_WINDOWS = (2, 4, 8, 16)
POOL_PAD = 16
FFT_CH = 96
FFT_WIDTH = 384
N_HEADS = 8
N_KV_HEADS = 2
HEAD_DIM = 64
ATTN_WIDTH = 512
KV_WIDTH = 128
QKV_WIDTH = ATTN_WIDTH + 2 * KV_WIDTH
ROPE_THETA = 10000.0
ROPE_PAIRS = 16
SGU_GROUPS = 4
SGU_CH = 96
SGU_WIDTH = 384
SGU_CHUNK = 128
N_BRANCHES = 4
GATE_WIDTH = N_BRANCHES * D_MODEL
IN_COLS = POOL_WIDTH + FFT_WIDTH + QKV_WIDTH + 2 * SGU_WIDTH + GATE_WIDTH
N_EXPERTS = 32
TOP_K = 4
D_FF = 1024
SWIGLU_LIMIT = 7.0
SWIGLU_ALPHA = 1.702

ROW_GROUP = 8
LANE_TILES = D_MODEL // 128

T_CTX = BATCH * SEQ
T_LAT = DEC_BATCH * DEC_SEQ
T_ALL = T_CTX + T_LAT
N_COND = 8
VMEM_LIMIT = 56 * 1024 * 1024

COL_XP = 0
COL_XQ = COL_XP + POOL_WIDTH
COL_QKV = COL_XQ + FFT_WIDTH
COL_UV = COL_QKV + QKV_WIDTH
COL_GATE = COL_UV + 2 * SGU_WIDTH


def _params(*sem):
    return pltpu.CompilerParams(dimension_semantics=sem, vmem_limit_bytes=VMEM_LIMIT)


def _split_bf16(x):
    hi = x.astype(BF16)
    lo = (x - hi.astype(F32)).astype(BF16)
    return hi, lo


def _dot(a, b):
    return jnp.dot(a, b, preferred_element_type=F32)


def _dot_nt(a, b):
    return lax.dot_general(a, b, (((1,), (1,)), ((), ())), preferred_element_type=F32)


def _group_lane_select(lane, vals, width):
    out = vals[-1]
    for g in range(len(vals) - 2, -1, -1):
        out = jnp.where(lane < (g + 1) * width, vals[g], out)
    return out


def _cond_row(blk, blocks_ctx, blocks_per_latent):
    return jnp.where(blk < blocks_ctx, 0, 1 + (blk - blocks_ctx) // blocks_per_latent)


MOD_TN = 1536


def _mod_kernel(c_ref, w_ref, b_ref, o_ref):
    c = c_ref[...]
    s = c * jax.nn.sigmoid(c)
    sh, sl = _split_bf16(s)
    wh, wl = _split_bf16(w_ref[...])
    o_ref[...] = _dot(sh, wh) + _dot(sh, wl) + _dot(sl, wh) + b_ref[...]


def _modulation(cond, w_mod, b_mod):
    n_cols = 6 * D_MODEL
    return pl.pallas_call(
        _mod_kernel,
        grid=(DEPTH, n_cols // MOD_TN),
        in_specs=[
            pl.BlockSpec((N_COND, D_MODEL), lambda l, j: (0, 0)),
            pl.BlockSpec((None, D_MODEL, MOD_TN), lambda l, j: (l, 0, j)),
            pl.BlockSpec((None, 1, MOD_TN), lambda l, j: (l, 0, j)),
        ],
        out_specs=pl.BlockSpec((None, N_COND, MOD_TN), lambda l, j: (l, 0, j)),
        out_shape=jax.ShapeDtypeStruct((DEPTH, N_COND, n_cols), F32),
        compiler_params=_params("arbitrary", "arbitrary"),
        name="modulation",
    )(cond, w_mod, b_mod.reshape(DEPTH, 1, n_cols))


INPROJ_TM = 256
INPROJ_SEGMENTS = (
    (COL_XP, POOL_WIDTH), (COL_XQ, FFT_WIDTH), (COL_QKV, QKV_WIDTH),
    (COL_UV, 2 * SGU_WIDTH), (COL_GATE, GATE_WIDTH))
INPROJ_CHUNK = 1024


def _ada_norm(x, g, shift, scale):
    xn = x * lax.rsqrt(jnp.mean(x * x, axis=-1, keepdims=True) + EPS)
    return xn * g * (1.0 + scale) + shift


def _token_specs(parts, tm, width):
    if len(parts) == 1:
        return [pl.BlockSpec((tm, width), lambda i: (i, 0))]
    nc = T_CTX // tm
    return [pl.BlockSpec((tm, width), lambda i: (jnp.minimum(i, nc - 1), 0)),
            pl.BlockSpec((tm, width), lambda i: (jnp.maximum(i - nc, 0), 0))]


def _token_load(refs, tm):
    if len(refs) == 1:
        return refs[0][...]
    return jnp.where(pl.program_id(0) < T_CTX // tm, refs[0][...], refs[1][...])


def _inproj_kernel(*refs, n_x):
    x_refs, (mod_ref, g_ref, w_ref), out_refs = refs[:n_x], refs[n_x:n_x + 3], refs[n_x + 3:]
    h = _ada_norm(_token_load(x_refs, INPROJ_TM), g_ref[...], mod_ref[0:1, :], mod_ref[1:2, :])
    hb = h.astype(BF16)
    for (col, width), o_ref in zip(INPROJ_SEGMENTS, out_refs):
        for c0 in range(0, width, INPROJ_CHUNK):
            c1 = min(c0 + INPROJ_CHUNK, width)
            o_ref[:, c0:c1] = _dot(hb, w_ref[:, col + c0:col + c1])


def _in_projection(x_parts, mod_l, g, w_in_bf16):
    tm = INPROJ_TM
    row = functools.partial(_cond_row, blocks_ctx=T_CTX // tm, blocks_per_latent=DEC_SEQ // tm)
    return pl.pallas_call(
        functools.partial(_inproj_kernel, n_x=len(x_parts)),
        grid=(T_ALL // tm,),
        in_specs=_token_specs(x_parts, tm, D_MODEL) + [
            pl.BlockSpec((None, 6, D_MODEL), lambda i: (row(i), 0, 0)),
            pl.BlockSpec((1, D_MODEL), lambda i: (0, 0)),
            pl.BlockSpec((D_MODEL, IN_COLS), lambda i: (0, 0)),
        ],
        out_specs=[pl.BlockSpec((tm, w), lambda i: (i, 0)) for _, w in INPROJ_SEGMENTS],
        out_shape=[jax.ShapeDtypeStruct((T_ALL, w), F32) for _, w in INPROJ_SEGMENTS],
        compiler_params=_params("arbitrary"),
        name="in_projection",
    )(*x_parts, mod_l, g.reshape(1, D_MODEL), w_in_bf16)


def _pool_mixer(xp, invcnt, w_bd, scale):
    s = xp.shape[0]
    n = s + 2 * POOL_PAD
    zeros = jnp.zeros((POOL_PAD, POOL_WIDTH), F32)
    xe = jnp.concatenate([zeros, xp, zeros], axis=0)

    def shift(a, k):
        return pltpu.roll(a, k % n, 0)

    s2 = xe + shift(xe, 1)
    s4 = shift(s2, 1) + shift(s2, -1)
    s8 = shift(s4, 2) + shift(s4, -2)
    s16 = shift(s8, 4) + shift(s8, -4)
    lane = lax.broadcasted_iota(jnp.int32, (1, POOL_WIDTH), 1)
    total = _group_lane_select(lane, [s2, s4, s8, s16], POOL_CH)[POOL_PAD:POOL_PAD + s]
    pooled = total * invcnt - xp
    return _dot(pooled.astype(BF16), w_bd) * scale


def _fourier_mixer(xq, f_cos_ch, f_sin_ch, f_pos):
    xb = xq.astype(BF16)
    a = _dot(xb, f_cos_ch).astype(BF16)
    b = _dot(xb, f_sin_ch).astype(BF16)
    return _dot(f_pos, jnp.concatenate([a, b], axis=0))


def _group_mean_sq(x, ones_bd, width):
    hi, lo = _split_bf16(x * x)
    return (_dot(hi, ones_bd) + _dot(lo, ones_bd)) * (1.0 / width)


def _mixers_kernel(xp_ref, xq_ref, uv_ref, invcnt_ref, pool_w_ref, pool_s_ref, fcc_ref, fsc_ref,
                   fpos_ref, ones_ref, sgu_w_ref, sgu_b_ref, pool_o, four_o, sgu_o):
    s = xp_ref.shape[0]
    pool_o[...] = _pool_mixer(xp_ref[...], invcnt_ref[...], pool_w_ref[...],
                              pool_s_ref[...]).astype(BF16)
    four_o[...] = _fourier_mixer(xq_ref[...], fcc_ref[...], fsc_ref[...], fpos_ref[...]).astype(BF16)

    act = jax.nn.gelu(uv_ref[...], approximate=True)
    u = act[:, :SGU_WIDTH]
    v = act[:, SGU_WIDTH:]
    vg = (v * lax.rsqrt(_group_mean_sq(v, ones_ref[...], SGU_CH) + EPS)).astype(BF16)
    lane = lax.broadcasted_iota(jnp.int32, (1, SGU_WIDTH), 1)
    w_stack = sgu_w_ref[...]
    bias = sgu_b_ref[...]
    for n in range(s // SGU_CHUNK):
        rows = slice(n * SGU_CHUNK, (n + 1) * SGU_CHUNK)
        r = _dot(w_stack, vg[rows])
        per_group = [r[g * SGU_CHUNK:(g + 1) * SGU_CHUNK] for g in range(SGU_GROUPS)]
        spatial = _group_lane_select(lane, per_group, SGU_CH) + bias
        sgu_o[rows, :] = (u[rows] * spatial).astype(BF16)


def _mixers(xp, xq, uv, consts, lw, seq, n_seq, block0):
    full = lambda shape: pl.BlockSpec(shape, lambda b: (0,) * len(shape))
    tok = lambda w: pl.BlockSpec((seq, w), lambda b: (block0 + b, 0))
    out = lambda: pl.BlockSpec((seq, POOL_WIDTH), lambda b: (b, 0))
    return pl.pallas_call(
        _mixers_kernel,
        grid=(n_seq,),
        in_specs=[
            tok(POOL_WIDTH), tok(FFT_WIDTH), tok(2 * SGU_WIDTH),
            full((seq, POOL_WIDTH)), full((POOL_WIDTH, POOL_WIDTH)), full((1, POOL_WIDTH)),
            full((FFT_WIDTH, FFT_WIDTH)), full((FFT_WIDTH, FFT_WIDTH)), full((seq, 2 * seq)),
            full((SGU_WIDTH, SGU_WIDTH)), full((SGU_GROUPS * SGU_CHUNK, SGU_CHUNK)),
            full((SGU_CHUNK, SGU_WIDTH)),
        ],
        out_specs=[out(), out(), out()],
        out_shape=[jax.ShapeDtypeStruct((n_seq * seq, POOL_WIDTH), BF16)] * 3,
        compiler_params=_params("arbitrary"),
        name=f"mixers_s{seq}",
    )(xp, xq, uv, consts["invcnt"], lw["pool_w_bd"], lw["pool_scale"], consts["f_cos_ch"],
      consts["f_sin_ch"], consts["f_pos"], consts["ones96"], lw["sgu_w_stack"], lw["sgu_bias"])


def _head_norm(x, ones_bd, g):
    return x * lax.rsqrt(_group_mean_sq(x, ones_bd, HEAD_DIM) + EPS) * g


def _rope(x, cos, sin_next, sin_prev):
    cols = []
    for c in range(x.shape[1] // 128):
        xc = x[:, c * 128:(c + 1) * 128]
        nxt = pltpu.roll(xc, 127, 1)
        prv = pltpu.roll(xc, 1, 1)
        cols.append(xc * cos + nxt * sin_next + prv * sin_prev)
    return cols[0] if len(cols) == 1 else jnp.concatenate(cols, axis=1)


def _attend(q, keys, vals, o_ref):
    qb = (q * (HEAD_DIM ** -0.5)).astype(BF16)
    group = N_HEADS // N_KV_HEADS
    for h in range(N_HEADS):
        j = h // group
        kh = keys[:, j * HEAD_DIM:(j + 1) * HEAD_DIM]
        vh = vals[:, j * HEAD_DIM:(j + 1) * HEAD_DIM]
        s = _dot_nt(qb[:, h * HEAD_DIM:(h + 1) * HEAD_DIM], kh)
        p = jnp.exp(s - jnp.max(s, axis=-1, keepdims=True))
        denom = jnp.sum(p, axis=-1, keepdims=True)
        o = _dot(p.astype(BF16), vh) / denom
        o_ref[:, h * HEAD_DIM:(h + 1) * HEAD_DIM] = o.astype(BF16)


def _attn_ctx_kernel(qkv_ref, gq_ref, gk_ref, ones_ref, o_ref, k_ref, v_ref):
    qkv = qkv_ref[...]
    ones = ones_ref[...]
    q = _head_norm(qkv[:, :ATTN_WIDTH], ones, gq_ref[...])
    k = _head_norm(qkv[:, ATTN_WIDTH:ATTN_WIDTH + KV_WIDTH], ones[:KV_WIDTH, :KV_WIDTH], gk_ref[...])
    v = qkv[:, ATTN_WIDTH + KV_WIDTH:]
    k_ref[...] = k
    v_ref[...] = v
    _attend(q, k.astype(BF16), v.astype(BF16), o_ref)


def _attention_ctx(qkv, consts, lw):
    full = lambda shape: pl.BlockSpec(shape, lambda b: (0,) * len(shape))
    return pl.pallas_call(
        _attn_ctx_kernel,
        grid=(BATCH,),
        in_specs=[pl.BlockSpec((SEQ, QKV_WIDTH), lambda b: (b, 0)),
                  full((1, ATTN_WIDTH)), full((1, KV_WIDTH)), full((ATTN_WIDTH, ATTN_WIDTH))],
        out_specs=[pl.BlockSpec((SEQ, ATTN_WIDTH), lambda b: (b, 0)),
                   pl.BlockSpec((SEQ, KV_WIDTH), lambda b: (b, 0)),
                   pl.BlockSpec((SEQ, KV_WIDTH), lambda b: (b, 0))],
        out_shape=[jax.ShapeDtypeStruct((T_CTX, ATTN_WIDTH), BF16),
                   jax.ShapeDtypeStruct((T_CTX, KV_WIDTH), F32),
                   jax.ShapeDtypeStruct((T_CTX, KV_WIDTH), F32)],
        compiler_params=_params("arbitrary"),
        name="attention_ctx",
    )(qkv, lw["gq"], lw["gk"], consts["ones64"])


LAT_QBLK = 256


def _attn_lat_kernel(q_ref, kv_ref, ck_ref, cv_ref, gq_ref, gk_ref, ones_ref, cos_q, sn_q, sp_q,
                     cos_k, sn_k, sp_k, o_ref, keys, vals):
    ones = ones_ref[...]

    @pl.when(pl.program_id(1) == 0)
    def _():
        kv = kv_ref[...]
        k = _head_norm(kv[:, :KV_WIDTH], ones[:KV_WIDTH, :KV_WIDTH], gk_ref[...])
        keys[0:DEC_SEQ, :] = _rope(k, cos_k[...], sn_k[...], sp_k[...]).astype(BF16)
        keys[DEC_SEQ:, :] = ck_ref[...].astype(BF16)
        vals[0:DEC_SEQ, :] = kv[:, KV_WIDTH:].astype(BF16)
        vals[DEC_SEQ:, :] = cv_ref[...].astype(BF16)

    q = _head_norm(q_ref[...], ones, gq_ref[...])
    q = _rope(q, cos_q[...], sn_q[...], sp_q[...])
    _attend(q, keys[...], vals[...], o_ref)


def _attention_lat(qkv, cache_k_l, cache_v_l, consts, lw):
    nq = DEC_SEQ // LAT_QBLK
    q0 = T_CTX // LAT_QBLK
    s0 = T_CTX // DEC_SEQ
    full = lambda shape: pl.BlockSpec(shape, lambda b, j: (0,) * len(shape))
    rope_q = lambda: pl.BlockSpec((LAT_QBLK, 128), lambda b, j: (j, 0))
    rope_k = lambda: pl.BlockSpec((DEC_SEQ, 128), lambda b, j: (0, 0))
    return pl.pallas_call(
        _attn_lat_kernel,
        grid=(DEC_BATCH, nq),
        in_specs=[
            pl.BlockSpec((LAT_QBLK, ATTN_WIDTH), lambda b, j: (q0 + b * nq + j, 0)),
            pl.BlockSpec((DEC_SEQ, 2 * KV_WIDTH), lambda b, j: (s0 + b, ATTN_WIDTH // (2 * KV_WIDTH))),
            pl.BlockSpec((None, PAST_LEN, KV_WIDTH), lambda b, j: (b, 0, 0)),
            pl.BlockSpec((None, PAST_LEN, KV_WIDTH), lambda b, j: (b, 0, 0)),
            full((1, ATTN_WIDTH)), full((1, KV_WIDTH)), full((ATTN_WIDTH, ATTN_WIDTH)),
            rope_q(), rope_q(), rope_q(), rope_k(), rope_k(), rope_k(),
        ],
        out_specs=pl.BlockSpec((LAT_QBLK, ATTN_WIDTH), lambda b, j: (b * nq + j, 0)),
        out_shape=jax.ShapeDtypeStruct((T_LAT, ATTN_WIDTH), BF16),
        scratch_shapes=[pltpu.VMEM((DEC_SEQ + PAST_LEN, KV_WIDTH), BF16),
                        pltpu.VMEM((DEC_SEQ + PAST_LEN, KV_WIDTH), BF16)],
        compiler_params=_params("arbitrary", "arbitrary"),
        name="attention_lat",
    )(qkv, qkv, cache_k_l, cache_v_l, lw["gq"], lw["gk"], consts["ones64"],
      consts["rope_cos"], consts["rope_sin_next"], consts["rope_sin_prev"],
      consts["rope_cos"], consts["rope_sin_next"], consts["rope_sin_prev"])


MERGE_TM = 256


def _merge_kernel(*refs, n_x):
    branch_refs, gl_ref, x_refs = refs[:8], refs[8], refs[9:9 + n_x]
    (mod_ref, g_ref, wp_ref, wf_ref, wa_ref, ws_ref, wo_ref, rw_ref, rb_ref,
     x_o, h_o, idx_o, wgt_o) = refs[9 + n_x:]
    merged = None
    for i, w_ref in enumerate((wp_ref, wf_ref, wa_ref, ws_ref)):
        br = _dot(_token_load(branch_refs[2 * i:2 * i + 2], MERGE_TM), w_ref[...])
        term = jax.nn.sigmoid(gl_ref[:, i * D_MODEL:(i + 1) * D_MODEL]) * br
        merged = term if merged is None else merged + term
    mix = _dot(merged.astype(BF16), wo_ref[...])
    x = _token_load(x_refs, MERGE_TM) + mod_ref[2:3, :] * mix
    x_o[...] = x
    h = _ada_norm(x, g_ref[...], mod_ref[3:4, :], mod_ref[4:5, :])
    for c in range(LANE_TILES):
        h_o[:, c, :, :] = h[:, c * 128:(c + 1) * 128].reshape(MERGE_TM // ROW_GROUP, ROW_GROUP, 128)

    hh, hl = _split_bf16(h)
    rh, rl = _split_bf16(rw_ref[...])
    logits = _dot(hh, rh) + _dot(hh, rl) + _dot(hl, rh) + rb_ref[...]
    lane = lax.broadcasted_iota(jnp.int32, logits.shape, 1).astype(F32)
    slot = lax.broadcasted_iota(jnp.int32, (logits.shape[0], TOP_K), 1)
    work = logits
    top = jnp.max(logits, axis=-1, keepdims=True)
    idx = jnp.zeros(slot.shape, F32)
    wgt = jnp.zeros(slot.shape, F32)
    denom = jnp.zeros_like(top)
    for k in range(TOP_K):
        m = jnp.max(work, axis=-1, keepdims=True)
        first = jnp.min(jnp.where(work == m, lane, float(N_EXPERTS)), axis=-1, keepdims=True)
        e = jnp.exp(m - top)
        idx = jnp.where(slot == k, first, idx)
        wgt = jnp.where(slot == k, e, wgt)
        denom = denom + e
        work = jnp.where(lane == first, -jnp.inf, work)
    idx_o[...] = idx.astype(jnp.int32)
    wgt_o[...] = wgt / denom


def _merge(pool, four, attn, sgu, gl, x_parts, mod_l, g_ffn, lw):
    tm = MERGE_TM
    row = functools.partial(_cond_row, blocks_ctx=T_CTX // tm, blocks_per_latent=DEC_SEQ // tm)
    full = lambda shape: pl.BlockSpec(shape, lambda i: (0,) * len(shape))
    tok = lambda w: pl.BlockSpec((tm, w), lambda i: (i, 0))
    branch_specs = []
    for pair, width in ((pool, POOL_WIDTH), (four, FFT_WIDTH), (attn, ATTN_WIDTH), (sgu, SGU_WIDTH)):
        branch_specs += _token_specs(pair, tm, width)
    return pl.pallas_call(
        functools.partial(_merge_kernel, n_x=len(x_parts)),
        grid=(T_ALL // tm,),
        in_specs=branch_specs + [tok(GATE_WIDTH)] + _token_specs(x_parts, tm, D_MODEL) + [
            pl.BlockSpec((None, 6, D_MODEL), lambda i: (row(i), 0, 0)),
            full((1, D_MODEL)),
            full((POOL_WIDTH, D_MODEL)), full((FFT_WIDTH, D_MODEL)), full((ATTN_WIDTH, D_MODEL)),
            full((SGU_WIDTH, D_MODEL)), full((D_MODEL, D_MODEL)),
            full((D_MODEL, N_EXPERTS)), full((1, N_EXPERTS)),
        ],
        out_specs=[tok(D_MODEL),
                   pl.BlockSpec((tm // ROW_GROUP, LANE_TILES, ROW_GROUP, 128), lambda i: (i, 0, 0, 0)),
                   tok(TOP_K), tok(TOP_K)],
        out_shape=[jax.ShapeDtypeStruct((T_ALL, D_MODEL), F32),
                   jax.ShapeDtypeStruct((T_ALL // ROW_GROUP, LANE_TILES, ROW_GROUP, 128), F32),
                   jax.ShapeDtypeStruct((T_ALL, TOP_K), jnp.int32),
                   jax.ShapeDtypeStruct((T_ALL, TOP_K), F32)],
        compiler_params=_params("arbitrary"),
        name="merge_router",
    )(*pool, *four, *attn, *sgu, gl, *x_parts, mod_l, g_ffn.reshape(1, D_MODEL),
      lw["w_br_pool"], lw["w_br_fourier"], lw["w_br_attn"], lw["w_br_sgu"], lw["w_out"],
      lw["router_w"], lw["router_b"])


N_PAIRS = T_ALL * TOP_K
FFN_TM = 512
N_ROW_TILES = N_PAIRS // FFN_TM + N_EXPERTS
N_SLOTS = N_ROW_TILES * FFN_TM
ROUTE_TB = 256


def _route(top_idx):
    e_flat = top_idx.reshape(N_PAIRS)
    onehot = (e_flat[:, None] == jnp.arange(N_EXPERTS, dtype=jnp.int32)[None, :]).astype(jnp.int32)
    csum = jnp.cumsum(onehot, axis=0)
    tiles = (csum[-1] + FFN_TM - 1) // FFN_TM
    tile_end = jnp.cumsum(tiles)
    row_start = (tile_end - tiles) * FFN_TM
    slot = jnp.sum(onehot * (csum - 1 + row_start[None, :]), axis=1)
    tile_ids = jnp.arange(N_ROW_TILES, dtype=jnp.int32)
    tile_expert = jnp.sum((tile_end[None, :] <= tile_ids[:, None]).astype(jnp.int32), axis=1)
    tile_expert = jnp.minimum(tile_expert, N_EXPERTS - 1)
    rows_before = (tile_ids - (tile_end - tiles)[tile_expert]) * FFN_TM
    tile_rows = jnp.where(tile_ids < tile_end[-1],
                          jnp.clip(csum[-1][tile_expert] - rows_before, 0, FFN_TM), 0)
    slot = slot.astype(jnp.int32).reshape(T_ALL // ROUTE_TB, 1, ROUTE_TB * TOP_K)
    return (slot, tile_expert.astype(jnp.int32), tile_rows.astype(jnp.int32),
            tile_end[-1:].astype(jnp.int32))


def _start_pair_rows(copy):
    def body(g, carry):
        for u in range(ROW_GROUP):
            for k in range(TOP_K):
                copy(g, u, k).start(priority=k % 2)
        return carry

    lax.fori_loop(0, ROUTE_TB // ROW_GROUP, body, 0)


def _pair_slot(slot_ref, g, u, k):
    return slot_ref[0, (g * ROW_GROUP + u) * TOP_K + k]


def _dispatch_kernel(tr_ref, slot_ref, h_ref, xs_out, zeros, sem, zero_sem):
    @pl.when(pl.program_id(0) == 0)
    def _():
        zeros[...] = jnp.zeros_like(zeros)

        def for_each_unfilled_tile(fn):
            def body(i, carry):
                @pl.when(tr_ref[i] < FFN_TM)
                def _():
                    fn(pltpu.make_async_copy(zeros, xs_out.at[pl.ds(i * FFN_TM, FFN_TM)], zero_sem))
                return carry

            lax.fori_loop(0, N_ROW_TILES, body, 0)

        for_each_unfilled_tile(lambda c: c.start())
        for_each_unfilled_tile(lambda c: c.wait())

    def copy(g, u, k):
        return pltpu.make_async_copy(h_ref.at[g, :, u, :], xs_out.at[_pair_slot(slot_ref, g, u, k)], sem)

    _start_pair_rows(copy)
    for _ in range(TOP_K):
        pltpu.make_async_copy(xs_out.at[pl.ds(0, ROUTE_TB)], xs_out.at[pl.ds(0, ROUTE_TB)], sem).wait()


def _dispatch(tile_rows, slot, h):
    return pl.pallas_call(
        _dispatch_kernel,
        grid_spec=pltpu.PrefetchScalarGridSpec(
            num_scalar_prefetch=1,
            grid=(T_ALL // ROUTE_TB,),
            in_specs=[
                pl.BlockSpec((None, 1, ROUTE_TB * TOP_K), lambda i, tr: (i, 0, 0),
                             memory_space=pltpu.SMEM),
                pl.BlockSpec((ROUTE_TB // ROW_GROUP, LANE_TILES, ROW_GROUP, 128),
                             lambda i, tr: (i, 0, 0, 0)),
            ],
            out_specs=pl.BlockSpec(memory_space=pl.ANY),
            scratch_shapes=[pltpu.VMEM((FFN_TM, LANE_TILES, 128), F32), pltpu.SemaphoreType.DMA(()),
                            pltpu.SemaphoreType.DMA(())],
        ),
        out_shape=jax.ShapeDtypeStruct((N_SLOTS, LANE_TILES, 128), F32),
        compiler_params=_params("arbitrary"),
        name="moe_dispatch",
    )(tile_rows, slot, h)


def _expert_ffn(xb, wg, bg, wu, bu, wd, bd):
    gate = jnp.minimum(_dot(xb, wg) + bg, SWIGLU_LIMIT)
    up = jnp.clip(_dot(xb, wu) + bu, -SWIGLU_LIMIT, SWIGLU_LIMIT)
    glu = gate * jax.nn.sigmoid(SWIGLU_ALPHA * gate)
    return _dot(((up + 1.0) * glu).astype(BF16), wd) + bd


def _ffn_kernel(te_ref, tr_ref, nu_ref, xs_ref, wg_ref, bg_ref, wu_ref, bu_ref, wd_ref, bd_ref,
                ys_ref, w_bf16, xb):
    i = pl.program_id(0)
    n_rows = tr_ref[i]
    half = FFN_TM // 2

    def lane_tile(ref, c, rows):
        return ref.at[pl.ds(c, rows, stride=LANE_TILES), :]

    def ffn(rows):
        for c in range(LANE_TILES):
            xb[:rows, c * 128:(c + 1) * 128] = lane_tile(xs_ref, c, rows)[...].astype(BF16)
        y = _expert_ffn(xb[:rows, :], w_bf16[0], bg_ref[...], w_bf16[1], bu_ref[...], w_bf16[2],
                        bd_ref[...])
        for c in range(LANE_TILES):
            lane_tile(ys_ref, c, rows)[...] = y[:, c * 128:(c + 1) * 128]

    @pl.when((n_rows > 0) & ((i == 0) | (te_ref[i] != te_ref[jnp.maximum(i - 1, 0)])))
    def _():
        w_bf16[0] = wg_ref[...].astype(BF16)
        w_bf16[1] = wu_ref[...].astype(BF16)
        w_bf16[2] = wd_ref[...].astype(BF16)

    @pl.when(n_rows > half)
    def _():
        ffn(FFN_TM)

    @pl.when(n_rows <= half)
    def _():
        ys_ref[half * LANE_TILES:, :] = jnp.zeros((half * LANE_TILES, 128), F32)

        @pl.when(n_rows > 0)
        def _():
            ffn(half)

        @pl.when(n_rows == 0)
        def _():
            ys_ref[:half * LANE_TILES, :] = jnp.zeros((half * LANE_TILES, 128), F32)


def _routed_ffn(tile_expert, tile_rows, n_used, xs, l, p):
    rows = pl.BlockSpec((FFN_TM * LANE_TILES, 128), lambda i, te, tr, nu: (jnp.minimum(i, nu[0] - 1), 0))
    wspec = lambda a, b: pl.BlockSpec((None, None, a, b), lambda i, te, tr, nu: (l, te[i], 0, 0))
    bias = lambda a: a.reshape(DEPTH, N_EXPERTS, 1, a.shape[-1])
    return pl.pallas_call(
        _ffn_kernel,
        grid_spec=pltpu.PrefetchScalarGridSpec(
            num_scalar_prefetch=3,
            grid=(N_ROW_TILES,),
            in_specs=[rows, wspec(D_MODEL, D_FF), wspec(1, D_FF), wspec(D_MODEL, D_FF),
                      wspec(1, D_FF), wspec(D_FF, D_MODEL), wspec(1, D_MODEL)],
            out_specs=pl.BlockSpec((FFN_TM * LANE_TILES, 128), lambda i, te, tr, nu: (i, 0)),
            scratch_shapes=[pltpu.VMEM((3, D_MODEL, D_FF), BF16), pltpu.VMEM((FFN_TM, D_MODEL), BF16)],
        ),
        out_shape=jax.ShapeDtypeStruct((N_SLOTS * LANE_TILES, 128), F32),
        compiler_params=_params("arbitrary"),
        name="moe_ffn",
    )(tile_expert, tile_rows, n_used, xs.reshape(N_SLOTS * LANE_TILES, 128), p["moe_w_gate"],
      bias(p["moe_b_gate"]), p["moe_w_up"], bias(p["moe_b_up"]), p["moe_w_down"],
      bias(p["moe_b_down"]))


def _combine_kernel(slot_ref, slot_next_ref, w_ref, x_ref, mod_ref, fg_ref, ys_ref, o_ref, buf, sems,
                    *, final_norm):
    i = pl.program_id(0)
    cur = i % 2

    def copy(slots, b, g, u, k):
        return pltpu.make_async_copy(ys_ref.at[_pair_slot(slots, g, u, k)], buf.at[b, k, g, :, u, :],
                                     sems.at[b])

    @pl.when(i == 0)
    def _():
        _start_pair_rows(functools.partial(copy, slot_ref, 0))

    @pl.when(i + 1 < pl.num_programs(0))
    def _():
        _start_pair_rows(functools.partial(copy, slot_next_ref, 1 - cur))

    for k in range(TOP_K):
        pltpu.make_async_copy(ys_ref.at[pl.ds(0, ROUTE_TB)], ys_ref.at[pl.ds(0, ROUTE_TB)],
                              sems.at[cur]).wait()

    def rows_of(k, c):
        return buf[cur, k, :, c, :, :].reshape(ROUTE_TB, 128)

    w = w_ref[...]
    sum_sq = jnp.zeros((ROUTE_TB, 1), F32)
    for c in range(LANE_TILES):
        cols = slice(c * 128, (c + 1) * 128)
        acc = w[:, 0:1] * rows_of(0, c)
        for k in range(1, TOP_K):
            acc = acc + w[:, k:k + 1] * rows_of(k, c)
        piece = x_ref[:, cols] + mod_ref[5:6, cols] * acc
        sum_sq = sum_sq + jnp.sum(piece * piece, axis=-1, keepdims=True)
        o_ref[:, cols] = piece
    if final_norm:
        o_ref[...] = o_ref[...] * lax.rsqrt(sum_sq * (1.0 / D_MODEL) + EPS) * fg_ref[...]


def _combine(slot, top_w, x, mod_l, final_g, ys, final_norm):
    tb = ROUTE_TB
    nb = T_ALL // tb
    row = functools.partial(_cond_row, blocks_ctx=T_CTX // tb, blocks_per_latent=DEC_SEQ // tb)
    tok = lambda w: pl.BlockSpec((tb, w), lambda i: (i, 0))
    slots = lambda ahead: pl.BlockSpec((None, 1, tb * TOP_K),
                                       lambda i: (jnp.minimum(i + ahead, nb - 1), 0, 0),
                                       memory_space=pltpu.SMEM)
    return pl.pallas_call(
        functools.partial(_combine_kernel, final_norm=final_norm),
        grid=(nb,),
        in_specs=[
            slots(0), slots(1), tok(TOP_K), tok(D_MODEL),
            pl.BlockSpec((None, 6, D_MODEL), lambda i: (row(i), 0, 0)),
            pl.BlockSpec((1, D_MODEL), lambda i: (0, 0)),
            pl.BlockSpec(memory_space=pl.ANY),
        ],
        out_specs=tok(D_MODEL),
        out_shape=jax.ShapeDtypeStruct((T_ALL, D_MODEL), F32),
        scratch_shapes=[pltpu.VMEM((2, TOP_K, tb // ROW_GROUP, LANE_TILES, ROW_GROUP, 128), F32),
                        pltpu.SemaphoreType.DMA((2,))],
        compiler_params=_params("arbitrary"),
        name="moe_combine",
    )(slot, slot, top_w, x, mod_l, final_g.reshape(1, D_MODEL), ys.reshape(N_SLOTS, LANE_TILES, 128))


def _block_diag(blocks):
    g, a, b = blocks.shape
    eye = jnp.eye(g, dtype=blocks.dtype)
    return (eye[:, None, :, None] * blocks[:, :, None, :]).reshape(g * a, g * b)


def _dft_tables(n):
    k = np.arange(n, dtype=np.int64)
    ang = 2.0 * np.pi * ((k[:, None] * k[None, :]) % n).astype(np.float64) / n
    return np.cos(ang) / math.sqrt(n), np.sin(ang) / math.sqrt(n)


def _seq_constants(seq):
    t = np.arange(seq)[:, None]
    win = np.array(POOL_WINDOWS)[None, :]
    lo = np.clip(t - win // 2, 0, seq)
    hi = np.clip(t - win // 2 + win, 0, seq)
    invcnt = np.repeat(1.0 / (hi - lo).astype(np.float64), POOL_CH, axis=1)
    cos_p, sin_p = _dft_tables(seq)
    cos_c, sin_c = _dft_tables(FFT_CH)
    eye = np.eye(POOL_GROUPS)
    as_bf16 = lambda a: jnp.asarray(a, F32).astype(BF16)
    return {
        "invcnt": jnp.asarray(invcnt, F32),
        "f_pos": as_bf16(np.concatenate([cos_p, -sin_p], axis=1)),
        "f_cos_ch": as_bf16(np.kron(eye, cos_c)),
        "f_sin_ch": as_bf16(np.kron(eye, sin_c)),
        "ones96": as_bf16(np.kron(eye, np.ones((SGU_CH, SGU_CH)))),
    }


def _rope_constants():
    rows = DEC_SEQ // GRID_W
    row = jnp.repeat(jnp.arange(rows, dtype=F32), GRID_W)
    col = jnp.tile(jnp.arange(GRID_W, dtype=F32), rows)
    freqs = ROPE_THETA ** (-jnp.arange(ROPE_PAIRS, dtype=F32) / ROPE_PAIRS)
    ang = jnp.stack([row[:, None] * freqs, col[:, None] * freqs], axis=1)
    ang = jnp.repeat(ang.reshape(DEC_SEQ, 2 * ROPE_PAIRS), 2, axis=1)
    ang = jnp.tile(ang, (1, 128 // HEAD_DIM))
    even = (jnp.arange(128) % 2 == 0)[None, :]
    sin = jnp.sin(ang)
    return {
        "rope_cos": jnp.cos(ang),
        "rope_sin_next": jnp.where(even, -sin, 0.0),
        "rope_sin_prev": jnp.where(even, 0.0, sin),
    }


def _layer_weights(p, l):
    group_of_lane = np.arange(SGU_WIDTH) // SGU_CH
    return {
        "w_in": p["w_in"][l].astype(BF16),
        "pool_w_bd": _block_diag(p["pool_w"][l]).astype(BF16),
        "pool_scale": p["pool_scale"][l].reshape(1, POOL_WIDTH),
        "sgu_w_stack": p["sgu_w"][l].reshape(SGU_GROUPS * SGU_CHUNK, SGU_CHUNK).astype(BF16),
        "sgu_bias": p["sgu_b"][l].T[:, group_of_lane],
        "gq": jnp.tile(p["q_norm_g"][l], N_HEADS).reshape(1, ATTN_WIDTH),
        "gk": jnp.tile(p["k_norm_g"][l], N_KV_HEADS).reshape(1, KV_WIDTH),
        "w_br_pool": p["w_br_pool"][l].astype(BF16),
        "w_br_fourier": p["w_br_fourier"][l].astype(BF16),
        "w_br_attn": p["w_br_attn"][l].astype(BF16),
        "w_br_sgu": p["w_br_sgu"][l].astype(BF16),
        "w_out": p["w_out"][l].astype(BF16),
        "router_w": p["router_w"][l],
        "router_b": p["router_b"][l].reshape(1, N_EXPERTS),
    }


def kernel(x_prompt, x_sample, cache_k, cache_v, c, c_ctx, w_mod, b_mod, norm_mix_g, norm_ffn_g, w_in, pool_w, pool_scale, q_norm_g, k_norm_g, sgu_w, sgu_b, w_br_pool, w_br_fourier, w_br_attn, w_br_sgu, w_out, router_w, router_b, moe_w_gate, moe_b_gate, moe_w_up, moe_b_up, moe_w_down, moe_b_down, final_norm_g):
    p = dict(w_in=w_in, pool_w=pool_w, pool_scale=pool_scale, q_norm_g=q_norm_g, k_norm_g=k_norm_g,
             sgu_w=sgu_w, sgu_b=sgu_b, w_br_pool=w_br_pool, w_br_fourier=w_br_fourier,
             w_br_attn=w_br_attn, w_br_sgu=w_br_sgu, w_out=w_out, router_w=router_w,
             router_b=router_b, moe_w_gate=moe_w_gate, moe_b_gate=moe_b_gate, moe_w_up=moe_w_up,
             moe_b_up=moe_b_up, moe_w_down=moe_w_down, moe_b_down=moe_b_down)

    cond = jnp.concatenate([c_ctx[None, :], c, jnp.zeros((N_COND - 1 - DEC_BATCH, D_MODEL), F32)])
    mod = _modulation(cond, w_mod, b_mod).reshape(DEPTH, N_COND, 6, D_MODEL)

    ones64 = jnp.asarray(np.kron(np.eye(N_HEADS), np.ones((HEAD_DIM, HEAD_DIM))), BF16)
    consts_ctx = dict(_seq_constants(SEQ), ones64=ones64)
    consts_lat = dict(_seq_constants(DEC_SEQ), ones64=ones64, **_rope_constants())
    ck = cache_k.reshape(DEC_BATCH, DEPTH, PAST_LEN, KV_WIDTH)
    cv = cache_v.reshape(DEC_BATCH, DEPTH, PAST_LEN, KV_WIDTH)

    x_parts = (x_prompt.reshape(T_CTX, D_MODEL), x_sample.reshape(T_LAT, D_MODEL))
    new_k, new_v = [], []
    for l in range(DEPTH):
        lw = _layer_weights(p, l)
        xp, xq, qkv, uv, gl = _in_projection(x_parts, mod[l], norm_mix_g[l], lw["w_in"])
        pool_c, four_c, sgu_c = _mixers(xp, xq, uv, consts_ctx, lw, SEQ, BATCH, 0)
        pool_l, four_l, sgu_l = _mixers(xp, xq, uv, consts_lat, lw, DEC_SEQ, DEC_BATCH,
                                        T_CTX // DEC_SEQ)
        attn_c, k_c, v_c = _attention_ctx(qkv, consts_ctx, lw)
        attn_l = _attention_lat(qkv, ck[:, l], cv[:, l], consts_lat, lw)
        new_k.append(k_c.reshape(BATCH, SEQ, N_KV_HEADS, HEAD_DIM))
        new_v.append(v_c.reshape(BATCH, SEQ, N_KV_HEADS, HEAD_DIM))
        x, h, top_idx, top_w = _merge((pool_c, pool_l), (four_c, four_l), (attn_c, attn_l),
                                      (sgu_c, sgu_l), gl, x_parts, mod[l], norm_ffn_g[l], lw)
        slot, tile_expert, tile_rows, n_used = _route(top_idx)
        ys = _routed_ffn(tile_expert, tile_rows, n_used, _dispatch(tile_rows, slot, h), l, p)
        x = _combine(slot, top_w, x, mod[l], final_norm_g, ys, final_norm=(l == DEPTH - 1))
        x_parts = (x,)

    y_prompt = x[:T_CTX].reshape(BATCH, SEQ, D_MODEL)
    y_sample = x[T_CTX:].reshape(DEC_BATCH, DEC_SEQ, D_MODEL)
    return (y_prompt, y_sample, jnp.stack(new_k, axis=1), jnp.stack(new_v, axis=1))
```

```python
import functools
import math

import numpy as np
import jax
import jax.numpy as jnp
from jax import lax
from jax.experimental import pallas as pl
from jax.experimental.pallas import tpu as pltpu

F32 = jnp.float32
BF16 = jnp.bfloat16

D_MODEL = 1024
BATCH = 32
SEQ = 256
DEPTH = 2
DEC_BATCH = 2
DEC_SEQ = 1024
PAST_LEN = 256
GRID_W = 64
EPS = 1e-6
POOL_GROUPS = 4
POOL_CH = 96
POOL_WIDTH = 384
POOL_WINDOWS = (2, 4, 8, 16)
POOL_PAD = 16
FFT_CH = 96
FFT_WIDTH = 384
N_HEADS = 8
N_KV_HEADS = 2
HEAD_DIM = 64
ATTN_WIDTH = 512
KV_WIDTH = 128
QKV_WIDTH = ATTN_WIDTH + 2 * KV_WIDTH
ROPE_THETA = 10000.0
ROPE_PAIRS = 16
SGU_GROUPS = 4
SGU_CH = 96
SGU_WIDTH = 384
SGU_CHUNK = 128
N_BRANCHES = 4
GATE_WIDTH = N_BRANCHES * D_MODEL
IN_COLS = POOL_WIDTH + FFT_WIDTH + QKV_WIDTH + 2 * SGU_WIDTH + GATE_WIDTH
N_EXPERTS = 32
TOP_K = 4
D_FF = 1024
SWIGLU_LIMIT = 7.0
SWIGLU_ALPHA = 1.702

ROW_GROUP = 8
LANE_TILES = D_MODEL // 128

T_CTX = BATCH * SEQ
T_LAT = DEC_BATCH * DEC_SEQ
T_ALL = T_CTX + T_LAT
N_COND = 8
VMEM_LIMIT = 56 * 1024 * 1024

COL_XP = 0
COL_XQ = COL_XP + POOL_WIDTH
COL_QKV = COL_XQ + FFT_WIDTH
COL_UV = COL_QKV + QKV_WIDTH
COL_GATE = COL_UV + 2 * SGU_WIDTH


def _params(*sem):
    return pltpu.CompilerParams(dimension_semantics=sem, vmem_limit_bytes=VMEM_LIMIT)


def _split_bf16(x):
    hi = x.astype(BF16)
    lo = (x - hi.astype(F32)).astype(BF16)
    return hi, lo


def _dot(a, b):
    return jnp.dot(a, b, preferred_element_type=F32)


def _dot_nt(a, b):
    return lax.dot_general(a, b, (((1,), (1,)), ((), ())), preferred_element_type=F32)


def _group_lane_select(lane, vals, width):
    out = vals[-1]
    for g in range(len(vals) - 2, -1, -1):
        out = jnp.where(lane < (g + 1) * width, vals[g], out)
    return out


def _cond_row(blk, blocks_ctx, blocks_per_latent):
    return jnp.where(blk < blocks_ctx, 0, 1 + (blk - blocks_ctx) // blocks_per_latent)


MOD_TN = 1536


def _mod_kernel(c_ref, w_ref, b_ref, o_ref):
    c = c_ref[...]
    s = c * jax.nn.sigmoid(c)
    sh, sl = _split_bf16(s)
    wh, wl = _split_bf16(w_ref[...])
    o_ref[...] = _dot(sh, wh) + _dot(sh, wl) + _dot(sl, wh) + b_ref[...]


def _modulation(cond, w_mod, b_mod):
    n_cols = 6 * D_MODEL
    return pl.pallas_call(
        _mod_kernel,
        grid=(DEPTH, n_cols // MOD_TN),
        in_specs=[
            pl.BlockSpec((N_COND, D_MODEL), lambda l, j: (0, 0)),
            pl.BlockSpec((None, D_MODEL, MOD_TN), lambda l, j: (l, 0, j)),
            pl.BlockSpec((None, 1, MOD_TN), lambda l, j: (l, 0, j)),
        ],
        out_specs=pl.BlockSpec((None, N_COND, MOD_TN), lambda l, j: (l, 0, j)),
        out_shape=jax.ShapeDtypeStruct((DEPTH, N_COND, n_cols), F32),
        compiler_params=_params("arbitrary", "arbitrary"),
        name="modulation",
    )(cond, w_mod, b_mod.reshape(DEPTH, 1, n_cols))


INPROJ_TM = 256
INPROJ_SEGMENTS = (
    (COL_XP, POOL_WIDTH), (COL_XQ, FFT_WIDTH), (COL_QKV, QKV_WIDTH),
    (COL_UV, 2 * SGU_WIDTH), (COL_GATE, GATE_WIDTH))
INPROJ_CHUNK = 1024


def _ada_norm(x, g, shift, scale):
    xn = x * lax.rsqrt(jnp.mean(x * x, axis=-1, keepdims=True) + EPS)
    return xn * g * (1.0 + scale) + shift


def _token_specs(parts, tm, width):
    if len(parts) == 1:
        return [pl.BlockSpec((tm, width), lambda i: (i, 0))]
    nc = T_CTX // tm
    return [pl.BlockSpec((tm, width), lambda i: (jnp.minimum(i, nc - 1), 0)),
            pl.BlockSpec((tm, width), lambda i: (jnp.maximum(i - nc, 0), 0))]


def _token_load(refs, tm):
    if len(refs) == 1:
        return refs[0][...]
    return jnp.where(pl.program_id(0) < T_CTX // tm, refs[0][...], refs[1][...])


def _inproj_kernel(*refs, n_x):
    x_refs, (mod_ref, g_ref, w_ref), out_refs = refs[:n_x], refs[n_x:n_x + 3], refs[n_x + 3:]
    h = _ada_norm(_token_load(x_refs, INPROJ_TM), g_ref[...], mod_ref[0:1, :], mod_ref[1:2, :])
    hb = h.astype(BF16)
    for (col, width), o_ref in zip(INPROJ_SEGMENTS, out_refs):
        for c0 in range(0, width, INPROJ_CHUNK):
            c1 = min(c0 + INPROJ_CHUNK, width)
            o_ref[:, c0:c1] = _dot(hb, w_ref[:, col + c0:col + c1])


def _in_projection(x_parts, mod_l, g, w_in_bf16):
    tm = INPROJ_TM
    row = functools.partial(_cond_row, blocks_ctx=T_CTX // tm, blocks_per_latent=DEC_SEQ // tm)
    return pl.pallas_call(
        functools.partial(_inproj_kernel, n_x=len(x_parts)),
        grid=(T_ALL // tm,),
        in_specs=_token_specs(x_parts, tm, D_MODEL) + [
            pl.BlockSpec((None, 6, D_MODEL), lambda i: (row(i), 0, 0)),
            pl.BlockSpec((1, D_MODEL), lambda i: (0, 0)),
            pl.BlockSpec((D_MODEL, IN_COLS), lambda i: (0, 0)),
        ],
        out_specs=[pl.BlockSpec((tm, w), lambda i: (i, 0)) for _, w in INPROJ_SEGMENTS],
        out_shape=[jax.ShapeDtypeStruct((T_ALL, w), F32) for _, w in INPROJ_SEGMENTS],
        compiler_params=_params("arbitrary"),
        name="in_projection",
    )(*x_parts, mod_l, g.reshape(1, D_MODEL), w_in_bf16)


def _pool_mixer(xp, invcnt, w_bd, scale):
    s = xp.shape[0]
    n = s + 2 * POOL_PAD
    zeros = jnp.zeros((POOL_PAD, POOL_WIDTH), F32)
    xe = jnp.concatenate([zeros, xp, zeros], axis=0)

    def shift(a, k):
        return pltpu.roll(a, k % n, 0)

    s2 = xe + shift(xe, 1)
    s4 = shift(s2, 1) + shift(s2, -1)
    s8 = shift(s4, 2) + shift(s4, -2)
    s16 = shift(s8, 4) + shift(s8, -4)
    lane = lax.broadcasted_iota(jnp.int32, (1, POOL_WIDTH), 1)
    total = _group_lane_select(lane, [s2, s4, s8, s16], POOL_CH)[POOL_PAD:POOL_PAD + s]
    pooled = total * invcnt - xp
    return _dot(pooled.astype(BF16), w_bd) * scale


def _fourier_mixer(xq, f_cos_ch, f_sin_ch, f_pos):
    xb = xq.astype(BF16)
    a = _dot(xb, f_cos_ch).astype(BF16)
    b = _dot(xb, f_sin_ch).astype(BF16)
    return _dot(f_pos, jnp.concatenate([a, b], axis=0))


def _group_mean_sq(x, ones_bd, width):
    hi, lo = _split_bf16(x * x)
    return (_dot(hi, ones_bd) + _dot(lo, ones_bd)) * (1.0 / width)


def _mixers_kernel(xp_ref, xq_ref, uv_ref, invcnt_ref, pool_w_ref, pool_s_ref, fcc_ref, fsc_ref,
                   fpos_ref, ones_ref, sgu_w_ref, sgu_b_ref, pool_o, four_o, sgu_o):
    s = xp_ref.shape[0]
    pool_o[...] = _pool_mixer(xp_ref[...], invcnt_ref[...], pool_w_ref[...],
                              pool_s_ref[...]).astype(BF16)
    four_o[...] = _fourier_mixer(xq_ref[...], fcc_ref[...], fsc_ref[...], fpos_ref[...]).astype(BF16)

    act = jax.nn.gelu(uv_ref[...], approximate=True)
    u = act[:, :SGU_WIDTH]
    v = act[:, SGU_WIDTH:]
    vg = (v * lax.rsqrt(_group_mean_sq(v, ones_ref[...], SGU_CH) + EPS)).astype(BF16)
    lane = lax.broadcasted_iota(jnp.int32, (1, SGU_WIDTH), 1)
    w_stack = sgu_w_ref[...]
    bias = sgu_b_ref[...]
    for n in range(s // SGU_CHUNK):
        rows = slice(n * SGU_CHUNK, (n + 1) * SGU_CHUNK)
        r = _dot(w_stack, vg[rows])
        per_group = [r[g * SGU_CHUNK:(g + 1) * SGU_CHUNK] for g in range(SGU_GROUPS)]
        spatial = _group_lane_select(lane, per_group, SGU_CH) + bias
        sgu_o[rows, :] = (u[rows] * spatial).astype(BF16)


def _mixers(xp, xq, uv, consts, lw, seq, n_seq, block0):
    full = lambda shape: pl.BlockSpec(shape, lambda b: (0,) * len(shape))
    tok = lambda w: pl.BlockSpec((seq, w), lambda b: (block0 + b, 0))
    out = lambda: pl.BlockSpec((seq, POOL_WIDTH), lambda b: (b, 0))
    return pl.pallas_call(
        _mixers_kernel,
        grid=(n_seq,),
        in_specs=[
            tok(POOL_WIDTH), tok(FFT_WIDTH), tok(2 * SGU_WIDTH),
            full((seq, POOL_WIDTH)), full((POOL_WIDTH, POOL_WIDTH)), full((1, POOL_WIDTH)),
            full((FFT_WIDTH, FFT_WIDTH)), full((FFT_WIDTH, FFT_WIDTH)), full((seq, 2 * seq)),
            full((SGU_WIDTH, SGU_WIDTH)), full((SGU_GROUPS * SGU_CHUNK, SGU_CHUNK)),
            full((SGU_CHUNK, SGU_WIDTH)),
        ],
        out_specs=[out(), out(), out()],
        out_shape=[jax.ShapeDtypeStruct((n_seq * seq, POOL_WIDTH), BF16)] * 3,
        compiler_params=_params("arbitrary"),
        name=f"mixers_s{seq}",
    )(xp, xq, uv, consts["invcnt"], lw["pool_w_bd"], lw["pool_scale"], consts["f_cos_ch"],
      consts["f_sin_ch"], consts["f_pos"], consts["ones96"], lw["sgu_w_stack"], lw["sgu_bias"])


def _head_norm(x, ones_bd, g):
    return x * lax.rsqrt(_group_mean_sq(x, ones_bd, HEAD_DIM) + EPS) * g


def _rope(x, cos, sin_next, sin_prev):
    cols = []
    for c in range(x.shape[1] // 128):
        xc = x[:, c * 128:(c + 1) * 128]
        nxt = pltpu.roll(xc, 127, 1)
        prv = pltpu.roll(xc, 1, 1)
        cols.append(xc * cos + nxt * sin_next + prv * sin_prev)
    return cols[0] if len(cols) == 1 else jnp.concatenate(cols, axis=1)


def _attend(q, keys, vals, o_ref):
    qb = (q * (HEAD_DIM ** -0.5)).astype(BF16)
    group = N_HEADS // N_KV_HEADS
    for h in range(N_HEADS):
        j = h // group
        kh = keys[:, j * HEAD_DIM:(j + 1) * HEAD_DIM]
        vh = vals[:, j * HEAD_DIM:(j + 1) * HEAD_DIM]
        s = _dot_nt(qb[:, h * HEAD_DIM:(h + 1) * HEAD_DIM], kh)
        p = jnp.exp(s - jnp.max(s, axis=-1, keepdims=True))
        denom = jnp.sum(p, axis=-1, keepdims=True)
        o = _dot(p.astype(BF16), vh) / denom
        o_ref[:, h * HEAD_DIM:(h + 1) * HEAD_DIM] = o.astype(BF16)


def _attn_ctx_kernel(qkv_ref, gq_ref, gk_ref, ones_ref, o_ref, k_ref, v_ref):
    qkv = qkv_ref[...]
    ones = ones_ref[...]
    q = _head_norm(qkv[:, :ATTN_WIDTH], ones, gq_ref[...])
    k = _head_norm(qkv[:, ATTN_WIDTH:ATTN_WIDTH + KV_WIDTH], ones[:KV_WIDTH, :KV_WIDTH], gk_ref[...])
    v = qkv[:, ATTN_WIDTH + KV_WIDTH:]
    k_ref[...] = k
    v_ref[...] = v
    _attend(q, k.astype(BF16), v.astype(BF16), o_ref)


def _attention_ctx(qkv, consts, lw):
    full = lambda shape: pl.BlockSpec(shape, lambda b: (0,) * len(shape))
    return pl.pallas_call(
        _attn_ctx_kernel,
        grid=(BATCH,),
        in_specs=[pl.BlockSpec((SEQ, QKV_WIDTH), lambda b: (b, 0)),
                  full((1, ATTN_WIDTH)), full((1, KV_WIDTH)), full((ATTN_WIDTH, ATTN_WIDTH))],
        out_specs=[pl.BlockSpec((SEQ, ATTN_WIDTH), lambda b: (b, 0)),
                   pl.BlockSpec((SEQ, KV_WIDTH), lambda b: (b, 0)),
                   pl.BlockSpec((SEQ, KV_WIDTH), lambda b: (b, 0))],
        out_shape=[jax.ShapeDtypeStruct((T_CTX, ATTN_WIDTH), BF16),
                   jax.ShapeDtypeStruct((T_CTX, KV_WIDTH), F32),
                   jax.ShapeDtypeStruct((T_CTX, KV_WIDTH), F32)],
        compiler_params=_params("arbitrary"),
        name="attention_ctx",
    )(qkv, lw["gq"], lw["gk"], consts["ones64"])


LAT_QBLK = 256


def _attn_lat_kernel(q_ref, kv_ref, ck_ref, cv_ref, gq_ref, gk_ref, ones_ref, cos_q, sn_q, sp_q,
                     cos_k, sn_k, sp_k, o_ref, keys, vals):
    ones = ones_ref[...]

    @pl.when(pl.program_id(1) == 0)
    def _():
        kv = kv_ref[...]
        k = _head_norm(kv[:, :KV_WIDTH], ones[:KV_WIDTH, :KV_WIDTH], gk_ref[...])
        keys[0:DEC_SEQ, :] = _rope(k, cos_k[...], sn_k[...], sp_k[...]).astype(BF16)
        keys[DEC_SEQ:, :] = ck_ref[...].astype(BF16)
        vals[0:DEC_SEQ, :] = kv[:, KV_WIDTH:].astype(BF16)
        vals[DEC_SEQ:, :] = cv_ref[...].astype(BF16)

    q = _head_norm(q_ref[...], ones, gq_ref[...])
    q = _rope(q, cos_q[...], sn_q[...], sp_q[...])
    _attend(q, keys[...], vals[...], o_ref)


def _attention_lat(qkv, cache_k_l, cache_v_l, consts, lw):
    nq = DEC_SEQ // LAT_QBLK
    q0 = T_CTX // LAT_QBLK
    s0 = T_CTX // DEC_SEQ
    full = lambda shape: pl.BlockSpec(shape, lambda b, j: (0,) * len(shape))
    rope_q = lambda: pl.BlockSpec((LAT_QBLK, 128), lambda b, j: (j, 0))
    rope_k = lambda: pl.BlockSpec((DEC_SEQ, 128), lambda b, j: (0, 0))
    return pl.pallas_call(
        _attn_lat_kernel,
        grid=(DEC_BATCH, nq),
        in_specs=[
            pl.BlockSpec((LAT_QBLK, ATTN_WIDTH), lambda b, j: (q0 + b * nq + j, 0)),
            pl.BlockSpec((DEC_SEQ, 2 * KV_WIDTH), lambda b, j: (s0 + b, ATTN_WIDTH // (2 * KV_WIDTH))),
            pl.BlockSpec((None, PAST_LEN, KV_WIDTH), lambda b, j: (b, 0, 0)),
            pl.BlockSpec((None, PAST_LEN, KV_WIDTH), lambda b, j: (b, 0, 0)),
            full((1, ATTN_WIDTH)), full((1, KV_WIDTH)), full((ATTN_WIDTH, ATTN_WIDTH)),
            rope_q(), rope_q(), rope_q(), rope_k(), rope_k(), rope_k(),
        ],
        out_specs=pl.BlockSpec((LAT_QBLK, ATTN_WIDTH), lambda b, j: (b * nq + j, 0)),
        out_shape=jax.ShapeDtypeStruct((T_LAT, ATTN_WIDTH), BF16),
        scratch_shapes=[pltpu.VMEM((DEC_SEQ + PAST_LEN, KV_WIDTH), BF16),
                        pltpu.VMEM((DEC_SEQ + PAST_LEN, KV_WIDTH), BF16)],
        compiler_params=_params("arbitrary", "arbitrary"),
        name="attention_lat",
    )(qkv, qkv, cache_k_l, cache_v_l, lw["gq"], lw["gk"], consts["ones64"],
      consts["rope_cos"], consts["rope_sin_next"], consts["rope_sin_prev"],
      consts["rope_cos"], consts["rope_sin_next"], consts["rope_sin_prev"])


MERGE_TM = 256


def _merge_kernel(*refs, n_x):
    branch_refs, gl_ref, x_refs = refs[:8], refs[8], refs[9:9 + n_x]
    (mod_ref, g_ref, wp_ref, wf_ref, wa_ref, ws_ref, wo_ref, rw_ref, rb_ref,
     x_o, h_o, idx_o, wgt_o) = refs[9 + n_x:]
    merged = None
    for i, w_ref in enumerate((wp_ref, wf_ref, wa_ref, ws_ref)):
        br = _dot(_token_load(branch_refs[2 * i:2 * i + 2], MERGE_TM), w_ref[...])
        term = jax.nn.sigmoid(gl_ref[:, i * D_MODEL:(i + 1) * D_MODEL]) * br
        merged = term if merged is None else merged + term
    mix = _dot(merged.astype(BF16), wo_ref[...])
    x = _token_load(x_refs, MERGE_TM) + mod_ref[2:3, :] * mix
    x_o[...] = x
    h = _ada_norm(x, g_ref[...], mod_ref[3:4, :], mod_ref[4:5, :])
    for c in range(LANE_TILES):
        h_o[:, c, :, :] = h[:, c * 128:(c + 1) * 128].reshape(MERGE_TM // ROW_GROUP, ROW_GROUP, 128)

    hh, hl = _split_bf16(h)
    rh, rl = _split_bf16(rw_ref[...])
    logits = _dot(hh, rh) + _dot(hh, rl) + _dot(hl, rh) + rb_ref[...]
    lane = lax.broadcasted_iota(jnp.int32, logits.shape, 1).astype(F32)
    slot = lax.broadcasted_iota(jnp.int32, (logits.shape[0], TOP_K), 1)
    work = logits
    top = jnp.max(logits, axis=-1, keepdims=True)
    idx = jnp.zeros(slot.shape, F32)
    wgt = jnp.zeros(slot.shape, F32)
    denom = jnp.zeros_like(top)
    for k in range(TOP_K):
        m = jnp.max(work, axis=-1, keepdims=True)
        first = jnp.min(jnp.where(work == m, lane, float(N_EXPERTS)), axis=-1, keepdims=True)
        e = jnp.exp(m - top)
        idx = jnp.where(slot == k, first, idx)
        wgt = jnp.where(slot == k, e, wgt)
        denom = denom + e
        work = jnp.where(lane == first, -jnp.inf, work)
    idx_o[...] = idx.astype(jnp.int32)
    wgt_o[...] = wgt / denom


def _merge(pool, four, attn, sgu, gl, x_parts, mod_l, g_ffn, lw):
    tm = MERGE_TM
    row = functools.partial(_cond_row, blocks_ctx=T_CTX // tm, blocks_per_latent=DEC_SEQ // tm)
    full = lambda shape: pl.BlockSpec(shape, lambda i: (0,) * len(shape))
    tok = lambda w: pl.BlockSpec((tm, w), lambda i: (i, 0))
    branch_specs = []
    for pair, width in ((pool, POOL_WIDTH), (four, FFT_WIDTH), (attn, ATTN_WIDTH), (sgu, SGU_WIDTH)):
        branch_specs += _token_specs(pair, tm, width)
    return pl.pallas_call(
        functools.partial(_merge_kernel, n_x=len(x_parts)),
        grid=(T_ALL // tm,),
        in_specs=branch_specs + [tok(GATE_WIDTH)] + _token_specs(x_parts, tm, D_MODEL) + [
            pl.BlockSpec((None, 6, D_MODEL), lambda i: (row(i), 0, 0)),
            full((1, D_MODEL)),
            full((POOL_WIDTH, D_MODEL)), full((FFT_WIDTH, D_MODEL)), full((ATTN_WIDTH, D_MODEL)),
            full((SGU_WIDTH, D_MODEL)), full((D_MODEL, D_MODEL)),
            full((D_MODEL, N_EXPERTS)), full((1, N_EXPERTS)),
        ],
        out_specs=[tok(D_MODEL),
                   pl.BlockSpec((tm // ROW_GROUP, LANE_TILES, ROW_GROUP, 128), lambda i: (i, 0, 0, 0)),
                   tok(TOP_K), tok(TOP_K)],
        out_shape=[jax.ShapeDtypeStruct((T_ALL, D_MODEL), F32),
                   jax.ShapeDtypeStruct((T_ALL // ROW_GROUP, LANE_TILES, ROW_GROUP, 128), F32),
                   jax.ShapeDtypeStruct((T_ALL, TOP_K), jnp.int32),
                   jax.ShapeDtypeStruct((T_ALL, TOP_K), F32)],
        compiler_params=_params("arbitrary"),
        name="merge_router",
    )(*pool, *four, *attn, *sgu, gl, *x_parts, mod_l, g_ffn.reshape(1, D_MODEL),
      lw["w_br_pool"], lw["w_br_fourier"], lw["w_br_attn"], lw["w_br_sgu"], lw["w_out"],
      lw["router_w"], lw["router_b"])


N_PAIRS = T_ALL * TOP_K
FFN_TM = 512
N_ROW_TILES = N_PAIRS // FFN_TM + N_EXPERTS
N_SLOTS = N_ROW_TILES * FFN_TM
ROUTE_TB = 256


def _route(top_idx):
    e_flat = top_idx.reshape(N_PAIRS)
    onehot = (e_flat[:, None] == jnp.arange(N_EXPERTS, dtype=jnp.int32)[None, :]).astype(jnp.int32)
    csum = jnp.cumsum(onehot, axis=0)
    tiles = (csum[-1] + FFN_TM - 1) // FFN_TM
    tile_end = jnp.cumsum(tiles)
    row_start = (tile_end - tiles) * FFN_TM
    slot = jnp.sum(onehot * (csum - 1 + row_start[None, :]), axis=1)
    tile_ids = jnp.arange(N_ROW_TILES, dtype=jnp.int32)
    tile_expert = jnp.sum((tile_end[None, :] <= tile_ids[:, None]).astype(jnp.int32), axis=1)
    tile_expert = jnp.minimum(tile_expert, N_EXPERTS - 1)
    rows_before = (tile_ids - (tile_end - tiles)[tile_expert]) * FFN_TM
    tile_rows = jnp.where(tile_ids < tile_end[-1],
                          jnp.clip(csum[-1][tile_expert] - rows_before, 0, FFN_TM), 0)
    slot = slot.astype(jnp.int32).reshape(T_ALL // ROUTE_TB, 1, ROUTE_TB * TOP_K)
    used = (tiles > 0).astype(jnp.int32)
    rank = jnp.cumsum(used) - 1
    experts = jnp.arange(N_EXPERTS, dtype=jnp.int32)
    group_expert = jnp.sum(jnp.where((rank[None, :] == experts[:, None]) & (used[None, :] > 0),
                                     experts[None, :], 0), axis=1)
    tiles_info = dict(expert=tile_expert.astype(jnp.int32), rows=tile_rows.astype(jnp.int32),
                      group=rank[tile_expert].astype(jnp.int32),
                      group_expert=group_expert.astype(jnp.int32),
                      n_groups=jnp.sum(used).reshape(1).astype(jnp.int32),
                      n_used=tile_end[-1:].astype(jnp.int32))
    return slot, tiles_info


def _start_pair_rows(copy):
    def body(g, carry):
        for u in range(ROW_GROUP):
            for k in range(TOP_K):
                copy(g, u, k).start(priority=k % 2)
        return carry

    lax.fori_loop(0, ROUTE_TB // ROW_GROUP, body, 0)


def _pair_slot(slot_ref, g, u, k):
    return slot_ref[0, (g * ROW_GROUP + u) * TOP_K + k]


def _dispatch_kernel(tr_ref, slot_ref, h_ref, xs_out, zeros, sem, zero_sem):
    @pl.when(pl.program_id(0) == 0)
    def _():
        zeros[...] = jnp.zeros_like(zeros)

        def for_each_unfilled_tile(fn):
            def body(i, carry):
                @pl.when(tr_ref[i] < FFN_TM)
                def _():
                    fn(pltpu.make_async_copy(zeros, xs_out.at[pl.ds(i * FFN_TM, FFN_TM)], zero_sem))
                return carry

            lax.fori_loop(0, N_ROW_TILES, body, 0)

        for_each_unfilled_tile(lambda c: c.start())
        for_each_unfilled_tile(lambda c: c.wait())

    def copy(g, u, k):
        return pltpu.make_async_copy(h_ref.at[g, :, u, :], xs_out.at[_pair_slot(slot_ref, g, u, k)], sem)

    _start_pair_rows(copy)
    for _ in range(TOP_K):
        pltpu.make_async_copy(xs_out.at[pl.ds(0, ROUTE_TB)], xs_out.at[pl.ds(0, ROUTE_TB)], sem).wait()


def _dispatch(tile_rows, slot, h):
    return pl.pallas_call(
        _dispatch_kernel,
        grid_spec=pltpu.PrefetchScalarGridSpec(
            num_scalar_prefetch=1,
            grid=(T_ALL // ROUTE_TB,),
            in_specs=[
                pl.BlockSpec((None, 1, ROUTE_TB * TOP_K), lambda i, tr: (i, 0, 0),
                             memory_space=pltpu.SMEM),
                pl.BlockSpec((ROUTE_TB // ROW_GROUP, LANE_TILES, ROW_GROUP, 128),
                             lambda i, tr: (i, 0, 0, 0)),
            ],
            out_specs=pl.BlockSpec(memory_space=pl.ANY),
            scratch_shapes=[pltpu.VMEM((FFN_TM, LANE_TILES, 128), F32), pltpu.SemaphoreType.DMA(()),
                            pltpu.SemaphoreType.DMA(())],
        ),
        out_shape=jax.ShapeDtypeStruct((N_SLOTS, LANE_TILES, 128), F32),
        compiler_params=_params("arbitrary"),
        name="moe_dispatch",
    )(tile_rows, slot, h)


def _expert_ffn(xb, wg, bg, wu, bu, wd, bd):
    gate = jnp.minimum(_dot(xb, wg) + bg, SWIGLU_LIMIT)
    up = jnp.clip(_dot(xb, wu) + bu, -SWIGLU_LIMIT, SWIGLU_LIMIT)
    glu = gate * jax.nn.sigmoid(SWIGLU_ALPHA * gate)
    return _dot(((up + 1.0) * glu).astype(BF16), wd) + bd


def _ffn_kernel(te_ref, tr_ref, tg_ref, ge_ref, ng_ref, nu_ref, xs_ref, wg_hbm, bg_ref, wu_hbm,
                bu_ref, wd_hbm, bd_ref, ys_ref, stage, w_bf16, xb, sems, *, layer):
    i = pl.program_id(0)
    n_rows = tr_ref[i]
    group = tg_ref[i]
    slot = group % 2
    half = FFN_TM // 2

    def weight_copies(g, s):
        e = ge_ref[g]
        return [pltpu.make_async_copy(w.at[layer, e], stage.at[s, j], sems.at[s])
                for j, w in enumerate((wg_hbm, wu_hbm, wd_hbm))]

    def lane_tile(ref, c, rows):
        return ref.at[pl.ds(c, rows, stride=LANE_TILES), :]

    def ffn(rows):
        for c in range(LANE_TILES):
            xb[:rows, c * 128:(c + 1) * 128] = lane_tile(xs_ref, c, rows)[...].astype(BF16)
        y = _expert_ffn(xb[:rows, :], w_bf16[0], bg_ref[...], w_bf16[1], bu_ref[...], w_bf16[2],
                        bd_ref[...])
        for c in range(LANE_TILES):
            lane_tile(ys_ref, c, rows)[...] = y[:, c * 128:(c + 1) * 128]

    @pl.when(i == 0)
    def _():
        for c in weight_copies(0, 0):
            c.start()

    @pl.when((n_rows > 0) & ((i == 0) | (te_ref[i] != te_ref[jnp.maximum(i - 1, 0)])))
    def _():
        for c in weight_copies(group, slot):
            c.wait()
        for j in range(3):
            w_bf16[j] = stage[slot, j].astype(BF16)

        @pl.when(group + 1 < ng_ref[0])
        def _():
            for c in weight_copies(group + 1, 1 - slot):
                c.start()

    @pl.when(n_rows > half)
    def _():
        ffn(FFN_TM)

    @pl.when(n_rows <= half)
    def _():
        ys_ref[half * LANE_TILES:, :] = jnp.zeros((half * LANE_TILES, 128), F32)

        @pl.when(n_rows > 0)
        def _():
            ffn(half)

        @pl.when(n_rows == 0)
        def _():
            ys_ref[:half * LANE_TILES, :] = jnp.zeros((half * LANE_TILES, 128), F32)


def _routed_ffn(tiles, xs, l, p):
    n_prefetch = 6
    rows = pl.BlockSpec((FFN_TM * LANE_TILES, 128),
                        lambda i, te, tr, tg, ge, ng, nu: (jnp.minimum(i, nu[0] - 1), 0))
    bias_spec = lambda w: pl.BlockSpec((None, None, 1, w),
                                       lambda i, te, tr, tg, ge, ng, nu: (l, te[i], 0, 0))
    bias = lambda a: a.reshape(DEPTH, N_EXPERTS, 1, a.shape[-1])
    hbm = pl.BlockSpec(memory_space=pl.ANY)
    return pl.pallas_call(
        functools.partial(_ffn_kernel, layer=l),
        grid_spec=pltpu.PrefetchScalarGridSpec(
            num_scalar_prefetch=n_prefetch,
            grid=(N_ROW_TILES,),
            in_specs=[rows, hbm, bias_spec(D_FF), hbm, bias_spec(D_FF), hbm, bias_spec(D_MODEL)],
            out_specs=pl.BlockSpec((FFN_TM * LANE_TILES, 128),
                                   lambda i, te, tr, tg, ge, ng, nu: (i, 0)),
            scratch_shapes=[pltpu.VMEM((2, 3, D_MODEL, D_FF), F32), pltpu.VMEM((3, D_MODEL, D_FF), BF16),
                            pltpu.VMEM((FFN_TM, D_MODEL), BF16), pltpu.SemaphoreType.DMA((2,))],
        ),
        out_shape=jax.ShapeDtypeStruct((N_SLOTS * LANE_TILES, 128), F32),
        compiler_params=_params("arbitrary"),
        name="moe_ffn",
    )(tiles["expert"], tiles["rows"], tiles["group"], tiles["group_expert"], tiles["n_groups"],
      tiles["n_used"], xs.reshape(N_SLOTS * LANE_TILES, 128), p["moe_w_gate"], bias(p["moe_b_gate"]),
      p["moe_w_up"], bias(p["moe_b_up"]), p["moe_w_down"], bias(p["moe_b_down"]))


def _combine_kernel(slot_ref, slot_next_ref, w_ref, x_ref, mod_ref, fg_ref, ys_ref, o_ref, buf, sems,
                    *, final_norm):
    i = pl.program_id(0)
    cur = i % 2

    def copy(slots, b, g, u, k):
        return pltpu.make_async_copy(ys_ref.at[_pair_slot(slots, g, u, k)], buf.at[b, k, g, :, u, :],
                                     sems.at[b])

    @pl.when(i == 0)
    def _():
        _start_pair_rows(functools.partial(copy, slot_ref, 0))

    @pl.when(i + 1 < pl.num_programs(0))
    def _():
        _start_pair_rows(functools.partial(copy, slot_next_ref, 1 - cur))

    for k in range(TOP_K):
        pltpu.make_async_copy(ys_ref.at[pl.ds(0, ROUTE_TB)], ys_ref.at[pl.ds(0, ROUTE_TB)],
                              sems.at[cur]).wait()

    def rows_of(k, c):
        return buf[cur, k, :, c, :, :].reshape(ROUTE_TB, 128)

    w = w_ref[...]
    sum_sq = jnp.zeros((ROUTE_TB, 1), F32)
    for c in range(LANE_TILES):
        cols = slice(c * 128, (c + 1) * 128)
        acc = w[:, 0:1] * rows_of(0, c)
        for k in range(1, TOP_K):
            acc = acc + w[:, k:k + 1] * rows_of(k, c)
        piece = x_ref[:, cols] + mod_ref[5:6, cols] * acc
        sum_sq = sum_sq + jnp.sum(piece * piece, axis=-1, keepdims=True)
        o_ref[:, cols] = piece
    if final_norm:
        o_ref[...] = o_ref[...] * lax.rsqrt(sum_sq * (1.0 / D_MODEL) + EPS) * fg_ref[...]


def _combine(slot, top_w, x, mod_l, final_g, ys, final_norm):
    tb = ROUTE_TB
    nb = T_ALL // tb
    row = functools.partial(_cond_row, blocks_ctx=T_CTX // tb, blocks_per_latent=DEC_SEQ // tb)
    tok = lambda w: pl.BlockSpec((tb, w), lambda i: (i, 0))
    slots = lambda ahead: pl.BlockSpec((None, 1, tb * TOP_K),
                                       lambda i: (jnp.minimum(i + ahead, nb - 1), 0, 0),
                                       memory_space=pltpu.SMEM)
    return pl.pallas_call(
        functools.partial(_combine_kernel, final_norm=final_norm),
        grid=(nb,),
        in_specs=[
            slots(0), slots(1), tok(TOP_K), tok(D_MODEL),
            pl.BlockSpec((None, 6, D_MODEL), lambda i: (row(i), 0, 0)),
            pl.BlockSpec((1, D_MODEL), lambda i: (0, 0)),
            pl.BlockSpec(memory_space=pl.ANY),
        ],
        out_specs=tok(D_MODEL),
        out_shape=jax.ShapeDtypeStruct((T_ALL, D_MODEL), F32),
        scratch_shapes=[pltpu.VMEM((2, TOP_K, tb // ROW_GROUP, LANE_TILES, ROW_GROUP, 128), F32),
                        pltpu.SemaphoreType.DMA((2,))],
        compiler_params=_params("arbitrary"),
        name="moe_combine",
    )(slot, slot, top_w, x, mod_l, final_g.reshape(1, D_MODEL), ys.reshape(N_SLOTS, LANE_TILES, 128))


def _block_diag(blocks):
    g, a, b = blocks.shape
    eye = jnp.eye(g, dtype=blocks.dtype)
    return (eye[:, None, :, None] * blocks[:, :, None, :]).reshape(g * a, g * b)


def _dft_tables(n):
    k = np.arange(n, dtype=np.int64)
    ang = 2.0 * np.pi * ((k[:, None] * k[None, :]) % n).astype(np.float64) / n
    return np.cos(ang) / math.sqrt(n), np.sin(ang) / math.sqrt(n)


def _seq_constants(seq):
    t = np.arange(seq)[:, None]
    win = np.array(POOL_WINDOWS)[None, :]
    lo = np.clip(t - win // 2, 0, seq)
    hi = np.clip(t - win // 2 + win, 0, seq)
    invcnt = np.repeat(1.0 / (hi - lo).astype(np.float64), POOL_CH, axis=1)
    cos_p, sin_p = _dft_tables(seq)
    cos_c, sin_c = _dft_tables(FFT_CH)
    eye = np.eye(POOL_GROUPS)
    as_bf16 = lambda a: jnp.asarray(a, F32).astype(BF16)
    return {
        "invcnt": jnp.asarray(invcnt, F32),
        "f_pos": as_bf16(np.concatenate([cos_p, -sin_p], axis=1)),
        "f_cos_ch": as_bf16(np.kron(eye, cos_c)),
        "f_sin_ch": as_bf16(np.kron(eye, sin_c)),
        "ones96": as_bf16(np.kron(eye, np.ones((SGU_CH, SGU_CH)))),
    }


def _rope_constants():
    rows = DEC_SEQ // GRID_W
    row = jnp.repeat(jnp.arange(rows, dtype=F32), GRID_W)
    col = jnp.tile(jnp.arange(GRID_W, dtype=F32), rows)
    freqs = ROPE_THETA ** (-jnp.arange(ROPE_PAIRS, dtype=F32) / ROPE_PAIRS)
    ang = jnp.stack([row[:, None] * freqs, col[:, None] * freqs], axis=1)
    ang = jnp.repeat(ang.reshape(DEC_SEQ, 2 * ROPE_PAIRS), 2, axis=1)
    ang = jnp.tile(ang, (1, 128 // HEAD_DIM))
    even = (jnp.arange(128) % 2 == 0)[None, :]
    sin = jnp.sin(ang)
    return {
        "rope_cos": jnp.cos(ang),
        "rope_sin_next": jnp.where(even, -sin, 0.0),
        "rope_sin_prev": jnp.where(even, 0.0, sin),
    }


def _layer_weights(p, l):
    group_of_lane = np.arange(SGU_WIDTH) // SGU_CH
    return {
        "w_in": p["w_in"][l].astype(BF16),
        "pool_w_bd": _block_diag(p["pool_w"][l]).astype(BF16),
        "pool_scale": p["pool_scale"][l].reshape(1, POOL_WIDTH),
        "sgu_w_stack": p["sgu_w"][l].reshape(SGU_GROUPS * SGU_CHUNK, SGU_CHUNK).astype(BF16),
        "sgu_bias": p["sgu_b"][l].T[:, group_of_lane],
        "gq": jnp.tile(p["q_norm_g"][l], N_HEADS).reshape(1, ATTN_WIDTH),
        "gk": jnp.tile(p["k_norm_g"][l], N_KV_HEADS).reshape(1, KV_WIDTH),
        "w_br_pool": p["w_br_pool"][l].astype(BF16),
        "w_br_fourier": p["w_br_fourier"][l].astype(BF16),
        "w_br_attn": p["w_br_attn"][l].astype(BF16),
        "w_br_sgu": p["w_br_sgu"][l].astype(BF16),
        "w_out": p["w_out"][l].astype(BF16),
        "router_w": p["router_w"][l],
        "router_b": p["router_b"][l].reshape(1, N_EXPERTS),
    }


def kernel(x_prompt, x_sample, cache_k, cache_v, c, c_ctx, w_mod, b_mod, norm_mix_g, norm_ffn_g, w_in, pool_w, pool_scale, q_norm_g, k_norm_g, sgu_w, sgu_b, w_br_pool, w_br_fourier, w_br_attn, w_br_sgu, w_out, router_w, router_b, moe_w_gate, moe_b_gate, moe_w_up, moe_b_up, moe_w_down, moe_b_down, final_norm_g):
    p = dict(w_in=w_in, pool_w=pool_w, pool_scale=pool_scale, q_norm_g=q_norm_g, k_norm_g=k_norm_g,
             sgu_w=sgu_w, sgu_b=sgu_b, w_br_pool=w_br_pool, w_br_fourier=w_br_fourier,
             w_br_attn=w_br_attn, w_br_sgu=w_br_sgu, w_out=w_out, router_w=router_w,
             router_b=router_b, moe_w_gate=moe_w_gate, moe_b_gate=moe_b_gate, moe_w_up=moe_w_up,
             moe_b_up=moe_b_up, moe_w_down=moe_w_down, moe_b_down=moe_b_down)

    cond = jnp.concatenate([c_ctx[None, :], c, jnp.zeros((N_COND - 1 - DEC_BATCH, D_MODEL), F32)])
    mod = _modulation(cond, w_mod, b_mod).reshape(DEPTH, N_COND, 6, D_MODEL)

    ones64 = jnp.asarray(np.kron(np.eye(N_HEADS), np.ones((HEAD_DIM, HEAD_DIM))), BF16)
    consts_ctx = dict(_seq_constants(SEQ), ones64=ones64)
    consts_lat = dict(_seq_constants(DEC_SEQ), ones64=ones64, **_rope_constants())
    ck = cache_k.reshape(DEC_BATCH, DEPTH, PAST_LEN, KV_WIDTH)
    cv = cache_v.reshape(DEC_BATCH, DEPTH, PAST_LEN, KV_WIDTH)

    x_parts = (x_prompt.reshape(T_CTX, D_MODEL), x_sample.reshape(T_LAT, D_MODEL))
    new_k, new_v = [], []
    for l in range(DEPTH):
        lw = _layer_weights(p, l)
        xp, xq, qkv, uv, gl = _in_projection(x_parts, mod[l], norm_mix_g[l], lw["w_in"])
        pool_c, four_c, sgu_c = _mixers(xp, xq, uv, consts_ctx, lw, SEQ, BATCH, 0)
        pool_l, four_l, sgu_l = _mixers(xp, xq, uv, consts_lat, lw, DEC_SEQ, DEC_BATCH,
                                        T_CTX // DEC_SEQ)
        attn_c, k_c, v_c = _attention_ctx(qkv, consts_ctx, lw)
        attn_l = _attention_lat(qkv, ck[:, l], cv[:, l], consts_lat, lw)
        new_k.append(k_c.reshape(BATCH, SEQ, N_KV_HEADS, HEAD_DIM))
        new_v.append(v_c.reshape(BATCH, SEQ, N_KV_HEADS, HEAD_DIM))
        x, h, top_idx, top_w = _merge((pool_c, pool_l), (four_c, four_l), (attn_c, attn_l),
                                      (sgu_c, sgu_l), gl, x_parts, mod[l], norm_ffn_g[l], lw)
        slot, tiles = _route(top_idx)
        ys = _routed_ffn(tiles, _dispatch(tiles["rows"], slot, h), l, p)
        x = _combine(slot, top_w, x, mod[l], final_norm_g, ys, final_norm=(l == DEPTH - 1))
        x_parts = (x,)

    y_prompt = x[:T_CTX].reshape(BATCH, SEQ, D_MODEL)
    y_sample = x[T_CTX:].reshape(DEC_BATCH, DEC_SEQ, D_MODEL)
    return (y_prompt, y_sample, jnp.stack(new_k, axis=1), jnp.stack(new_v, axis=1))
```

```python
import functools
import math

import numpy as np
import jax
import jax.numpy as jnp
from jax import lax
from jax.experimental import pallas as pl
from jax.experimental.pallas import tpu as pltpu

F32 = jnp.float32
BF16 = jnp.bfloat16

D_MODEL = 1024
BATCH = 32
SEQ = 256
DEPTH = 2
DEC_BATCH = 2
DEC_SEQ = 1024
PAST_LEN = 256
GRID_W = 64
EPS = 1e-6
POOL_GROUPS = 4
POOL_CH = 96
POOL_WIDTH = 384
POOL_WINDOWS = (2, 4, 8, 16)
POOL_PAD = 16
FFT_CH = 96
FFT_WIDTH = 384
N_HEADS = 8
N_KV_HEADS = 2
HEAD_DIM = 64
ATTN_WIDTH = 512
KV_WIDTH = 128
QKV_WIDTH = ATTN_WIDTH + 2 * KV_WIDTH
ROPE_THETA = 10000.0
ROPE_PAIRS = 16
SGU_GROUPS = 4
SGU_CH = 96
SGU_WIDTH = 384
SGU_CHUNK = 128
N_BRANCHES = 4
GATE_WIDTH = N_BRANCHES * D_MODEL
IN_COLS = POOL_WIDTH + FFT_WIDTH + QKV_WIDTH + 2 * SGU_WIDTH + GATE_WIDTH
N_EXPERTS = 32
TOP_K = 4
D_FF = 1024
SWIGLU_LIMIT = 7.0
SWIGLU_ALPHA = 1.702

ROW_GROUP = 8
LANE_TILES = D_MODEL // 128

T_CTX = BATCH * SEQ
T_LAT = DEC_BATCH * DEC_SEQ
T_ALL = T_CTX + T_LAT
N_COND = 8
VMEM_LIMIT = 56 * 1024 * 1024

COL_XP = 0
COL_XQ = COL_XP + POOL_WIDTH
COL_QKV = COL_XQ + FFT_WIDTH
COL_UV = COL_QKV + QKV_WIDTH
COL_GATE = COL_UV + 2 * SGU_WIDTH


def _params(*sem):
    return pltpu.CompilerParams(dimension_semantics=sem, vmem_limit_bytes=VMEM_LIMIT)


def _split_bf16(x):
    hi = x.astype(BF16)
    lo = (x - hi.astype(F32)).astype(BF16)
    return hi, lo


def _dot(a, b):
    return jnp.dot(a, b, preferred_element_type=F32)


def _dot_nt(a, b):
    return lax.dot_general(a, b, (((1,), (1,)), ((), ())), preferred_element_type=F32)


def _group_lane_select(lane, vals, width):
    out = vals[-1]
    for g in range(len(vals) - 2, -1, -1):
        out = jnp.where(lane < (g + 1) * width, vals[g], out)
    return out


def _cond_row(blk, blocks_ctx, blocks_per_latent):
    return jnp.where(blk < blocks_ctx, 0, 1 + (blk - blocks_ctx) // blocks_per_latent)


MOD_TN = 1536


def _mod_kernel(c_ref, w_ref, b_ref, o_ref):
    c = c_ref[...]
    s = c * jax.nn.sigmoid(c)
    sh, sl = _split_bf16(s)
    wh, wl = _split_bf16(w_ref[...])
    o_ref[...] = _dot(sh, wh) + _dot(sh, wl) + _dot(sl, wh) + b_ref[...]


def _modulation(cond, w_mod, b_mod):
    n_cols = 6 * D_MODEL
    return pl.pallas_call(
        _mod_kernel,
        grid=(DEPTH, n_cols // MOD_TN),
        in_specs=[
            pl.BlockSpec((N_COND, D_MODEL), lambda l, j: (0, 0)),
            pl.BlockSpec((None, D_MODEL, MOD_TN), lambda l, j: (l, 0, j)),
            pl.BlockSpec((None, 1, MOD_TN), lambda l, j: (l, 0, j)),
        ],
        out_specs=pl.BlockSpec((None, N_COND, MOD_TN), lambda l, j: (l, 0, j)),
        out_shape=jax.ShapeDtypeStruct((DEPTH, N_COND, n_cols), F32),
        compiler_params=_params("arbitrary", "arbitrary"),
        name="modulation",
    )(cond, w_mod, b_mod.reshape(DEPTH, 1, n_cols))


INPROJ_TM = 256
INPROJ_SEGMENTS = (
    (COL_XP, POOL_WIDTH), (COL_XQ, FFT_WIDTH), (COL_QKV, QKV_WIDTH),
    (COL_UV, 2 * SGU_WIDTH), (COL_GATE, GATE_WIDTH))
INPROJ_CHUNK = 1024


def _ada_norm(x, g, shift, scale):
    xn = x * lax.rsqrt(jnp.mean(x * x, axis=-1, keepdims=True) + EPS)
    return xn * g * (1.0 + scale) + shift


def _token_specs(parts, tm, width):
    if len(parts) == 1:
        return [pl.BlockSpec((tm, width), lambda i: (i, 0))]
    nc = T_CTX // tm
    return [pl.BlockSpec((tm, width), lambda i: (jnp.minimum(i, nc - 1), 0)),
            pl.BlockSpec((tm, width), lambda i: (jnp.maximum(i - nc, 0), 0))]


def _token_load(refs, tm):
    if len(refs) == 1:
        return refs[0][...]
    return jnp.where(pl.program_id(0) < T_CTX // tm, refs[0][...], refs[1][...])


def _inproj_kernel(*refs, n_x):
    x_refs, (mod_ref, g_ref, w_ref), out_refs = refs[:n_x], refs[n_x:n_x + 3], refs[n_x + 3:]
    h = _ada_norm(_token_load(x_refs, INPROJ_TM), g_ref[...], mod_ref[0:1, :], mod_ref[1:2, :])
    hb = h.astype(BF16)
    for (col, width), o_ref in zip(INPROJ_SEGMENTS, out_refs):
        for c0 in range(0, width, INPROJ_CHUNK):
            c1 = min(c0 + INPROJ_CHUNK, width)
            proj = _dot(hb, w_ref[:, col + c0:col + c1])
            o_ref[:, c0:c1] = jax.nn.sigmoid(proj) if col == COL_GATE else proj


def _in_projection(x_parts, mod_l, g, w_in_bf16):
    tm = INPROJ_TM
    row = functools.partial(_cond_row, blocks_ctx=T_CTX // tm, blocks_per_latent=DEC_SEQ // tm)
    return pl.pallas_call(
        functools.partial(_inproj_kernel, n_x=len(x_parts)),
        grid=(T_ALL // tm,),
        in_specs=_token_specs(x_parts, tm, D_MODEL) + [
            pl.BlockSpec((None, 6, D_MODEL), lambda i: (row(i), 0, 0)),
            pl.BlockSpec((1, D_MODEL), lambda i: (0, 0)),
            pl.BlockSpec((D_MODEL, IN_COLS), lambda i: (0, 0)),
        ],
        out_specs=[pl.BlockSpec((tm, w), lambda i: (i, 0)) for _, w in INPROJ_SEGMENTS],
        out_shape=[jax.ShapeDtypeStruct((T_ALL, w), F32) for _, w in INPROJ_SEGMENTS],
        compiler_params=_params("arbitrary"),
        name="in_projection",
    )(*x_parts, mod_l, g.reshape(1, D_MODEL), w_in_bf16)


def _pool_mixer(xp, invcnt, w_bd, scale):
    s = xp.shape[0]
    n = s + 2 * POOL_PAD
    zeros = jnp.zeros((POOL_PAD, POOL_WIDTH), F32)
    xe = jnp.concatenate([zeros, xp, zeros], axis=0)

    def shift(a, k):
        return pltpu.roll(a, k % n, 0)

    s2 = xe + shift(xe, 1)
    s4 = shift(s2, 1) + shift(s2, -1)
    s8 = shift(s4, 2) + shift(s4, -2)
    s16 = shift(s8, 4) + shift(s8, -4)
    lane = lax.broadcasted_iota(jnp.int32, (1, POOL_WIDTH), 1)
    total = _group_lane_select(lane, [s2, s4, s8, s16], POOL_CH)[POOL_PAD:POOL_PAD + s]
    pooled = total * invcnt - xp
    return _dot(pooled.astype(BF16), w_bd) * scale


def _fourier_mixer(xq, f_cos_ch, f_sin_ch, f_pos):
    xb = xq.astype(BF16)
    a = _dot(xb, f_cos_ch).astype(BF16)
    b = _dot(xb, f_sin_ch).astype(BF16)
    return _dot(f_pos, jnp.concatenate([a, b], axis=0))


def _group_mean_sq(x, ones_bd, width):
    hi, lo = _split_bf16(x * x)
    return (_dot(hi, ones_bd) + _dot(lo, ones_bd)) * (1.0 / width)


def _mixers_kernel(xp_ref, xq_ref, uv_ref, invcnt_ref, pool_w_ref, pool_s_ref, fcc_ref, fsc_ref,
                   fpos_ref, ones_ref, sgu_w_ref, sgu_b_ref, pool_o, four_o, sgu_o):
    s = xp_ref.shape[0]
    pool_o[...] = _pool_mixer(xp_ref[...], invcnt_ref[...], pool_w_ref[...],
                              pool_s_ref[...]).astype(BF16)
    four_o[...] = _fourier_mixer(xq_ref[...], fcc_ref[...], fsc_ref[...], fpos_ref[...]).astype(BF16)

    act = jax.nn.gelu(uv_ref[...], approximate=True)
    u = act[:, :SGU_WIDTH]
    v = act[:, SGU_WIDTH:]
    vg = (v * lax.rsqrt(_group_mean_sq(v, ones_ref[...], SGU_CH) + EPS)).astype(BF16)
    lane = lax.broadcasted_iota(jnp.int32, (1, SGU_WIDTH), 1)
    w_stack = sgu_w_ref[...]
    bias = sgu_b_ref[...]
    for n in range(s // SGU_CHUNK):
        rows = slice(n * SGU_CHUNK, (n + 1) * SGU_CHUNK)
        r = _dot(w_stack, vg[rows])
        per_group = [r[g * SGU_CHUNK:(g + 1) * SGU_CHUNK] for g in range(SGU_GROUPS)]
        spatial = _group_lane_select(lane, per_group, SGU_CH) + bias
        sgu_o[rows, :] = (u[rows] * spatial).astype(BF16)


def _mixers(xp, xq, uv, consts, lw, seq, n_seq, block0):
    full = lambda shape: pl.BlockSpec(shape, lambda b: (0,) * len(shape))
    tok = lambda w: pl.BlockSpec((seq, w), lambda b: (block0 + b, 0))
    out = lambda: pl.BlockSpec((seq, POOL_WIDTH), lambda b: (b, 0))
    return pl.pallas_call(
        _mixers_kernel,
        grid=(n_seq,),
        in_specs=[
            tok(POOL_WIDTH), tok(FFT_WIDTH), tok(2 * SGU_WIDTH),
            full((seq, POOL_WIDTH)), full((POOL_WIDTH, POOL_WIDTH)), full((1, POOL_WIDTH)),
            full((FFT_WIDTH, FFT_WIDTH)), full((FFT_WIDTH, FFT_WIDTH)), full((seq, 2 * seq)),
            full((SGU_WIDTH, SGU_WIDTH)), full((SGU_GROUPS * SGU_CHUNK, SGU_CHUNK)),
            full((SGU_CHUNK, SGU_WIDTH)),
        ],
        out_specs=[out(), out(), out()],
        out_shape=[jax.ShapeDtypeStruct((n_seq * seq, POOL_WIDTH), BF16)] * 3,
        compiler_params=_params("arbitrary"),
        name=f"mixers_s{seq}",
    )(xp, xq, uv, consts["invcnt"], lw["pool_w_bd"], lw["pool_scale"], consts["f_cos_ch"],
      consts["f_sin_ch"], consts["f_pos"], consts["ones96"], lw["sgu_w_stack"], lw["sgu_bias"])


def _head_norm(x, ones_bd, g):
    return x * lax.rsqrt(_group_mean_sq(x, ones_bd, HEAD_DIM) + EPS) * g


def _rope(x, cos, sin_next, sin_prev):
    cols = []
    for c in range(x.shape[1] // 128):
        xc = x[:, c * 128:(c + 1) * 128]
        nxt = pltpu.roll(xc, 127, 1)
        prv = pltpu.roll(xc, 1, 1)
        cols.append(xc * cos + nxt * sin_next + prv * sin_prev)
    return cols[0] if len(cols) == 1 else jnp.concatenate(cols, axis=1)


def _attend(q, keys, vals, o_ref):
    qb = (q * (HEAD_DIM ** -0.5)).astype(BF16)
    group = N_HEADS // N_KV_HEADS
    for h in range(N_HEADS):
        j = h // group
        kh = keys[:, j * HEAD_DIM:(j + 1) * HEAD_DIM]
        vh = vals[:, j * HEAD_DIM:(j + 1) * HEAD_DIM]
        s = _dot_nt(qb[:, h * HEAD_DIM:(h + 1) * HEAD_DIM], kh)
        p = jnp.exp(s - jnp.max(s, axis=-1, keepdims=True))
        denom = jnp.sum(p, axis=-1, keepdims=True)
        o = _dot(p.astype(BF16), vh) / denom
        o_ref[:, h * HEAD_DIM:(h + 1) * HEAD_DIM] = o.astype(BF16)


def _attn_ctx_kernel(qkv_ref, gq_ref, gk_ref, ones_ref, o_ref, k_ref, v_ref):
    qkv = qkv_ref[...]
    ones = ones_ref[...]
    q = _head_norm(qkv[:, :ATTN_WIDTH], ones, gq_ref[...])
    k = _head_norm(qkv[:, ATTN_WIDTH:ATTN_WIDTH + KV_WIDTH], ones[:KV_WIDTH, :KV_WIDTH], gk_ref[...])
    v = qkv[:, ATTN_WIDTH + KV_WIDTH:]
    k_ref[...] = k
    v_ref[...] = v
    _attend(q, k.astype(BF16), v.astype(BF16), o_ref)


def _attention_ctx(qkv, consts, lw):
    full = lambda shape: pl.BlockSpec(shape, lambda b: (0,) * len(shape))
    return pl.pallas_call(
        _attn_ctx_kernel,
        grid=(BATCH,),
        in_specs=[pl.BlockSpec((SEQ, QKV_WIDTH), lambda b: (b, 0)),
                  full((1, ATTN_WIDTH)), full((1, KV_WIDTH)), full((ATTN_WIDTH, ATTN_WIDTH))],
        out_specs=[pl.BlockSpec((SEQ, ATTN_WIDTH), lambda b: (b, 0)),
                   pl.BlockSpec((SEQ, KV_WIDTH), lambda b: (b, 0)),
                   pl.BlockSpec((SEQ, KV_WIDTH), lambda b: (b, 0))],
        out_shape=[jax.ShapeDtypeStruct((T_CTX, ATTN_WIDTH), BF16),
                   jax.ShapeDtypeStruct((T_CTX, KV_WIDTH), F32),
                   jax.ShapeDtypeStruct((T_CTX, KV_WIDTH), F32)],
        compiler_params=_params("arbitrary"),
        name="attention_ctx",
    )(qkv, lw["gq"], lw["gk"], consts["ones64"])


LAT_QBLK = 256


def _attn_lat_kernel(q_ref, kv_ref, ck_ref, cv_ref, gq_ref, gk_ref, ones_ref, cos_q, sn_q, sp_q,
                     cos_k, sn_k, sp_k, o_ref, keys, vals):
    ones = ones_ref[...]

    @pl.when(pl.program_id(1) == 0)
    def _():
        kv = kv_ref[...]
        k = _head_norm(kv[:, :KV_WIDTH], ones[:KV_WIDTH, :KV_WIDTH], gk_ref[...])
        keys[0:DEC_SEQ, :] = _rope(k, cos_k[...], sn_k[...], sp_k[...]).astype(BF16)
        keys[DEC_SEQ:, :] = ck_ref[...].astype(BF16)
        vals[0:DEC_SEQ, :] = kv[:, KV_WIDTH:].astype(BF16)
        vals[DEC_SEQ:, :] = cv_ref[...].astype(BF16)

    q = _head_norm(q_ref[...], ones, gq_ref[...])
    q = _rope(q, cos_q[...], sn_q[...], sp_q[...])
    _attend(q, keys[...], vals[...], o_ref)


def _attention_lat(qkv, cache_k_l, cache_v_l, consts, lw):
    nq = DEC_SEQ // LAT_QBLK
    q0 = T_CTX // LAT_QBLK
    s0 = T_CTX // DEC_SEQ
    full = lambda shape: pl.BlockSpec(shape, lambda b, j: (0,) * len(shape))
    rope_q = lambda: pl.BlockSpec((LAT_QBLK, 128), lambda b, j: (j, 0))
    rope_k = lambda: pl.BlockSpec((DEC_SEQ, 128), lambda b, j: (0, 0))
    return pl.pallas_call(
        _attn_lat_kernel,
        grid=(DEC_BATCH, nq),
        in_specs=[
            pl.BlockSpec((LAT_QBLK, ATTN_WIDTH), lambda b, j: (q0 + b * nq + j, 0)),
            pl.BlockSpec((DEC_SEQ, 2 * KV_WIDTH), lambda b, j: (s0 + b, ATTN_WIDTH // (2 * KV_WIDTH))),
            pl.BlockSpec((None, PAST_LEN, KV_WIDTH), lambda b, j: (b, 0, 0)),
            pl.BlockSpec((None, PAST_LEN, KV_WIDTH), lambda b, j: (b, 0, 0)),
            full((1, ATTN_WIDTH)), full((1, KV_WIDTH)), full((ATTN_WIDTH, ATTN_WIDTH)),
            rope_q(), rope_q(), rope_q(), rope_k(), rope_k(), rope_k(),
        ],
        out_specs=pl.BlockSpec((LAT_QBLK, ATTN_WIDTH), lambda b, j: (b * nq + j, 0)),
        out_shape=jax.ShapeDtypeStruct((T_LAT, ATTN_WIDTH), BF16),
        scratch_shapes=[pltpu.VMEM((DEC_SEQ + PAST_LEN, KV_WIDTH), BF16),
                        pltpu.VMEM((DEC_SEQ + PAST_LEN, KV_WIDTH), BF16)],
        compiler_params=_params("arbitrary", "arbitrary"),
        name="attention_lat",
    )(qkv, qkv, cache_k_l, cache_v_l, lw["gq"], lw["gk"], consts["ones64"],
      consts["rope_cos"], consts["rope_sin_next"], consts["rope_sin_prev"],
      consts["rope_cos"], consts["rope_sin_next"], consts["rope_sin_prev"])


MERGE_TM = 512


def _merge_kernel(*refs, n_x):
    branch_refs, gate_ref, x_refs = refs[:8], refs[8], refs[9:9 + n_x]
    (mod_ref, g_ref, wp_ref, wf_ref, wa_ref, ws_ref, wo_ref, rw_ref, rb_ref,
     x_o, h_o, idx_o, wgt_o) = refs[9 + n_x:]
    merged = None
    for i, w_ref in enumerate((wp_ref, wf_ref, wa_ref, ws_ref)):
        br = _dot(_token_load(branch_refs[2 * i:2 * i + 2], MERGE_TM), w_ref[...])
        term = gate_ref[:, i * D_MODEL:(i + 1) * D_MODEL] * br
        merged = term if merged is None else merged + term
    mix = _dot(merged.astype(BF16), wo_ref[...])
    x = _token_load(x_refs, MERGE_TM) + mod_ref[2:3, :] * mix
    x_o[...] = x
    h = _ada_norm(x, g_ref[...], mod_ref[3:4, :], mod_ref[4:5, :])
    for c in range(LANE_TILES):
        h_o[:, c, :, :] = h[:, c * 128:(c + 1) * 128].reshape(MERGE_TM // ROW_GROUP, ROW_GROUP, 128)

    hh, hl = _split_bf16(h)
    rh, rl = _split_bf16(rw_ref[...])
    logits = _dot_nt(rh, hh) + _dot_nt(rl, hh) + _dot_nt(rh, hl) + rb_ref[...]
    expert = lax.broadcasted_iota(jnp.int32, logits.shape, 0).astype(F32)
    work = logits
    top = jnp.max(logits, axis=0, keepdims=True)
    idx, wgt = [], []
    denom = jnp.zeros_like(top)
    for _ in range(TOP_K):
        m = jnp.max(work, axis=0, keepdims=True)
        first = jnp.min(jnp.where(work == m, expert, float(N_EXPERTS)), axis=0, keepdims=True)
        e = jnp.exp(m - top)
        idx.append(first)
        wgt.append(e)
        denom = denom + e
        work = jnp.where(expert == first, -jnp.inf, work)
    idx_o[...] = jnp.concatenate(idx, axis=0).astype(jnp.int32)
    wgt_o[...] = jnp.concatenate(wgt, axis=0) / denom


def _merge(pool, four, attn, sgu, gates, x_parts, mod_l, g_ffn, lw):
    tm = MERGE_TM
    row = functools.partial(_cond_row, blocks_ctx=T_CTX // tm, blocks_per_latent=DEC_SEQ // tm)
    full = lambda shape: pl.BlockSpec(shape, lambda i: (0,) * len(shape))
    tok = lambda w: pl.BlockSpec((tm, w), lambda i: (i, 0))
    branch_specs = []
    for pair, width in ((pool, POOL_WIDTH), (four, FFT_WIDTH), (attn, ATTN_WIDTH), (sgu, SGU_WIDTH)):
        branch_specs += _token_specs(pair, tm, width)
    return pl.pallas_call(
        functools.partial(_merge_kernel, n_x=len(x_parts)),
        grid=(T_ALL // tm,),
        in_specs=branch_specs + [tok(GATE_WIDTH)] + _token_specs(x_parts, tm, D_MODEL) + [
            pl.BlockSpec((None, 6, D_MODEL), lambda i: (row(i), 0, 0)),
            full((1, D_MODEL)),
            full((POOL_WIDTH, D_MODEL)), full((FFT_WIDTH, D_MODEL)), full((ATTN_WIDTH, D_MODEL)),
            full((SGU_WIDTH, D_MODEL)), full((D_MODEL, D_MODEL)),
            full((N_EXPERTS, D_MODEL)), full((N_EXPERTS, 1)),
        ],
        out_specs=[tok(D_MODEL),
                   pl.BlockSpec((tm // ROW_GROUP, LANE_TILES, ROW_GROUP, 128), lambda i: (i, 0, 0, 0)),
                   pl.BlockSpec((TOP_K, tm), lambda i: (0, i)), pl.BlockSpec((TOP_K, tm), lambda i: (0, i))],
        out_shape=[jax.ShapeDtypeStruct((T_ALL, D_MODEL), F32),
                   jax.ShapeDtypeStruct((T_ALL // ROW_GROUP, LANE_TILES, ROW_GROUP, 128), F32),
                   jax.ShapeDtypeStruct((TOP_K, T_ALL), jnp.int32),
                   jax.ShapeDtypeStruct((TOP_K, T_ALL), F32)],
        compiler_params=_params("arbitrary"),
        name="merge_router",
    )(*pool, *four, *attn, *sgu, gates, *x_parts, mod_l, g_ffn.reshape(1, D_MODEL),
      lw["w_br_pool"], lw["w_br_fourier"], lw["w_br_attn"], lw["w_br_sgu"], lw["w_out"],
      lw["router_w"], lw["router_b"])


N_PAIRS = T_ALL * TOP_K
FFN_TM = 512
N_ROW_TILES = N_PAIRS // FFN_TM + N_EXPERTS
N_SLOTS = N_ROW_TILES * FFN_TM
ROUTE_TB = 256


def _route(top_idx):
    experts = jnp.arange(N_EXPERTS, dtype=jnp.int32)
    onehot = (top_idx.T[:, :, None] == experts[None, None, :]).astype(jnp.int32)
    per_token = jnp.sum(onehot, axis=1)
    csum = jnp.cumsum(per_token, axis=0)
    counts = csum[-1]
    tiles = (counts + FFN_TM - 1) // FFN_TM
    tile_end = jnp.cumsum(tiles)
    row_start = (tile_end - tiles) * FFN_TM
    first_row = csum - per_token + row_start[None, :]
    slot = jnp.sum(onehot * first_row[:, None, :], axis=2)
    tile_ids = jnp.arange(N_ROW_TILES, dtype=jnp.int32)
    tile_expert = jnp.sum((tile_end[None, :] <= tile_ids[:, None]).astype(jnp.int32), axis=1)
    tile_expert = jnp.minimum(tile_expert, N_EXPERTS - 1)
    rows_before = (tile_ids - (tile_end - tiles)[tile_expert]) * FFN_TM
    tile_rows = jnp.where(tile_ids < tile_end[-1],
                          jnp.clip(counts[tile_expert] - rows_before, 0, FFN_TM), 0)
    slot = slot.astype(jnp.int32).reshape(T_ALL // ROUTE_TB, 1, ROUTE_TB * TOP_K)
    used = (tiles > 0).astype(jnp.int32)
    rank = jnp.cumsum(used) - 1
    group_expert = jnp.sum(jnp.where((rank[None, :] == experts[:, None]) & (used[None, :] > 0),
                                     experts[None, :], 0), axis=1)
    tiles_info = dict(expert=tile_expert.astype(jnp.int32), rows=tile_rows.astype(jnp.int32),
                      group=rank[tile_expert].astype(jnp.int32),
                      group_expert=group_expert.astype(jnp.int32),
                      n_groups=jnp.sum(used).reshape(1).astype(jnp.int32),
                      n_used=tile_end[-1:].astype(jnp.int32))
    return slot, tiles_info


def _start_pair_rows(copy):
    def body(g, carry):
        for u in range(ROW_GROUP):
            for k in range(TOP_K):
                copy(g, u, k).start(priority=k % 2)
        return carry

    lax.fori_loop(0, ROUTE_TB // ROW_GROUP, body, 0)


def _pair_slot(slot_ref, g, u, k):
    return slot_ref[0, (g * ROW_GROUP + u) * TOP_K + k]


def _dispatch_kernel(tr_ref, slot_ref, h_ref, xs_out, zeros, sem, zero_sem):
    @pl.when(pl.program_id(0) == 0)
    def _():
        zeros[...] = jnp.zeros_like(zeros)

        def for_each_unfilled_tile(fn):
            def body(i, carry):
                @pl.when(tr_ref[i] < FFN_TM)
                def _():
                    fn(pltpu.make_async_copy(zeros, xs_out.at[pl.ds(i * FFN_TM, FFN_TM)], zero_sem))
                return carry

            lax.fori_loop(0, N_ROW_TILES, body, 0)

        for_each_unfilled_tile(lambda c: c.start())
        for_each_unfilled_tile(lambda c: c.wait())

    def copy(g, u, k):
        return pltpu.make_async_copy(h_ref.at[g, :, u, :], xs_out.at[_pair_slot(slot_ref, g, u, k)], sem)

    _start_pair_rows(copy)
    for _ in range(TOP_K):
        pltpu.make_async_copy(xs_out.at[pl.ds(0, ROUTE_TB)], xs_out.at[pl.ds(0, ROUTE_TB)], sem).wait()


def _dispatch(tile_rows, slot, h):
    return pl.pallas_call(
        _dispatch_kernel,
        grid_spec=pltpu.PrefetchScalarGridSpec(
            num_scalar_prefetch=1,
            grid=(T_ALL // ROUTE_TB,),
            in_specs=[
                pl.BlockSpec((None, 1, ROUTE_TB * TOP_K), lambda i, tr: (i, 0, 0),
                             memory_space=pltpu.SMEM),
                pl.BlockSpec((ROUTE_TB // ROW_GROUP, LANE_TILES, ROW_GROUP, 128),
                             lambda i, tr: (i, 0, 0, 0)),
            ],
            out_specs=pl.BlockSpec(memory_space=pl.ANY),
            scratch_shapes=[pltpu.VMEM((FFN_TM, LANE_TILES, 128), F32), pltpu.SemaphoreType.DMA(()),
                            pltpu.SemaphoreType.DMA(())],
        ),
        out_shape=jax.ShapeDtypeStruct((N_SLOTS, LANE_TILES, 128), F32),
        compiler_params=_params("arbitrary"),
        name="moe_dispatch",
    )(tile_rows, slot, h)


def _expert_ffn(xb, wg, bg, wu, bu, wd, bd):
    gate = jnp.minimum(_dot(xb, wg) + bg, SWIGLU_LIMIT)
    up = jnp.clip(_dot(xb, wu) + bu, -SWIGLU_LIMIT, SWIGLU_LIMIT)
    glu = gate * jax.nn.sigmoid(SWIGLU_ALPHA * gate)
    return _dot(((up + 1.0) * glu).astype(BF16), wd) + bd


def _ffn_kernel(te_ref, tr_ref, tg_ref, ge_ref, ng_ref, nu_ref, xs_ref, wg_hbm, bg_ref, wu_hbm,
                bu_ref, wd_hbm, bd_ref, ys_ref, stage, w_bf16, xb, sems, *, layer):
    i = pl.program_id(0)
    n_rows = tr_ref[i]
    group = tg_ref[i]
    slot = group % 2
    half = FFN_TM // 2

    def weight_copies(g, s):
        e = ge_ref[g]
        return [pltpu.make_async_copy(w.at[layer, e], stage.at[s, j], sems.at[s])
                for j, w in enumerate((wg_hbm, wu_hbm, wd_hbm))]

    def lane_tile(ref, c, rows):
        return ref.at[pl.ds(c, rows, stride=LANE_TILES), :]

    def ffn(rows):
        for c in range(LANE_TILES):
            xb[:rows, c * 128:(c + 1) * 128] = lane_tile(xs_ref, c, rows)[...].astype(BF16)
        y = _expert_ffn(xb[:rows, :], w_bf16[0], bg_ref[...], w_bf16[1], bu_ref[...], w_bf16[2],
                        bd_ref[...])
        for c in range(LANE_TILES):
            lane_tile(ys_ref, c, rows)[...] = y[:, c * 128:(c + 1) * 128]

    @pl.when(i == 0)
    def _():
        for c in weight_copies(0, 0):
            c.start()

    @pl.when((n_rows > 0) & ((i == 0) | (te_ref[i] != te_ref[jnp.maximum(i - 1, 0)])))
    def _():
        for c in weight_copies(group, slot):
            c.wait()
        for j in range(3):
            w_bf16[j] = stage[slot, j].astype(BF16)

        @pl.when(group + 1 < ng_ref[0])
        def _():
            for c in weight_copies(group + 1, 1 - slot):
                c.start()

    @pl.when(n_rows > half)
    def _():
        ffn(FFN_TM)

    @pl.when(n_rows <= half)
    def _():
        ys_ref[half * LANE_TILES:, :] = jnp.zeros((half * LANE_TILES, 128), F32)

        @pl.when(n_rows > 0)
        def _():
            ffn(half)

        @pl.when(n_rows == 0)
        def _():
            ys_ref[:half * LANE_TILES, :] = jnp.zeros((half * LANE_TILES, 128), F32)


def _routed_ffn(tiles, xs, l, p):
    n_prefetch = 6
    rows = pl.BlockSpec((FFN_TM * LANE_TILES, 128),
                        lambda i, te, tr, tg, ge, ng, nu: (jnp.minimum(i, nu[0] - 1), 0))
    bias_spec = lambda w: pl.BlockSpec((None, None, 1, w),
                                       lambda i, te, tr, tg, ge, ng, nu: (l, te[i], 0, 0))
    bias = lambda a: a.reshape(DEPTH, N_EXPERTS, 1, a.shape[-1])
    hbm = pl.BlockSpec(memory_space=pl.ANY)
    return pl.pallas_call(
        functools.partial(_ffn_kernel, layer=l),
        grid_spec=pltpu.PrefetchScalarGridSpec(
            num_scalar_prefetch=n_prefetch,
            grid=(N_ROW_TILES,),
            in_specs=[rows, hbm, bias_spec(D_FF), hbm, bias_spec(D_FF), hbm, bias_spec(D_MODEL)],
            out_specs=pl.BlockSpec((FFN_TM * LANE_TILES, 128),
                                   lambda i, te, tr, tg, ge, ng, nu: (i, 0)),
            scratch_shapes=[pltpu.VMEM((2, 3, D_MODEL, D_FF), F32), pltpu.VMEM((3, D_MODEL, D_FF), BF16),
                            pltpu.VMEM((FFN_TM, D_MODEL), BF16), pltpu.SemaphoreType.DMA((2,))],
        ),
        out_shape=jax.ShapeDtypeStruct((N_SLOTS * LANE_TILES, 128), F32),
        compiler_params=_params("arbitrary"),
        name="moe_ffn",
    )(tiles["expert"], tiles["rows"], tiles["group"], tiles["group_expert"], tiles["n_groups"],
      tiles["n_used"], xs.reshape(N_SLOTS * LANE_TILES, 128), p["moe_w_gate"], bias(p["moe_b_gate"]),
      p["moe_w_up"], bias(p["moe_b_up"]), p["moe_w_down"], bias(p["moe_b_down"]))


def _combine_kernel(slot_ref, slot_next_ref, w_ref, x_ref, mod_ref, fg_ref, ys_ref, o_ref, buf, sems,
                    *, final_norm):
    i = pl.program_id(0)
    cur = i % 2

    def copy(slots, b, g, u, k):
        return pltpu.make_async_copy(ys_ref.at[_pair_slot(slots, g, u, k)], buf.at[b, k, g, :, u, :],
                                     sems.at[b])

    @pl.when(i == 0)
    def _():
        _start_pair_rows(functools.partial(copy, slot_ref, 0))

    @pl.when(i + 1 < pl.num_programs(0))
    def _():
        _start_pair_rows(functools.partial(copy, slot_next_ref, 1 - cur))

    for k in range(TOP_K):
        pltpu.make_async_copy(ys_ref.at[pl.ds(0, ROUTE_TB)], ys_ref.at[pl.ds(0, ROUTE_TB)],
                              sems.at[cur]).wait()

    def rows_of(k, c):
        return buf[cur, k, :, c, :, :].reshape(ROUTE_TB, 128)

    w = w_ref[...]
    sum_sq = jnp.zeros((ROUTE_TB, 1), F32)
    for c in range(LANE_TILES):
        cols = slice(c * 128, (c + 1) * 128)
        acc = w[:, 0:1] * rows_of(0, c)
        for k in range(1, TOP_K):
            acc = acc + w[:, k:k + 1] * rows_of(k, c)
        piece = x_ref[:, cols] + mod_ref[5:6, cols] * acc
        sum_sq = sum_sq + jnp.sum(piece * piece, axis=-1, keepdims=True)
        o_ref[:, cols] = piece
    if final_norm:
        o_ref[...] = o_ref[...] * lax.rsqrt(sum_sq * (1.0 / D_MODEL) + EPS) * fg_ref[...]


def _combine(slot, top_w, x, mod_l, final_g, ys, final_norm):
    tb = ROUTE_TB
    nb = T_ALL // tb
    row = functools.partial(_cond_row, blocks_ctx=T_CTX // tb, blocks_per_latent=DEC_SEQ // tb)
    tok = lambda w: pl.BlockSpec((tb, w), lambda i: (i, 0))
    slots = lambda ahead: pl.BlockSpec((None, 1, tb * TOP_K),
                                       lambda i: (jnp.minimum(i + ahead, nb - 1), 0, 0),
                                       memory_space=pltpu.SMEM)
    return pl.pallas_call(
        functools.partial(_combine_kernel, final_norm=final_norm),
        grid=(nb,),
        in_specs=[
            slots(0), slots(1), tok(TOP_K), tok(D_MODEL),
            pl.BlockSpec((None, 6, D_MODEL), lambda i: (row(i), 0, 0)),
            pl.BlockSpec((1, D_MODEL), lambda i: (0, 0)),
            pl.BlockSpec(memory_space=pl.ANY),
        ],
        out_specs=tok(D_MODEL),
        out_shape=jax.ShapeDtypeStruct((T_ALL, D_MODEL), F32),
        scratch_shapes=[pltpu.VMEM((2, TOP_K, tb // ROW_GROUP, LANE_TILES, ROW_GROUP, 128), F32),
                        pltpu.SemaphoreType.DMA((2,))],
        compiler_params=_params("arbitrary"),
        name="moe_combine",
    )(slot, slot, top_w, x, mod_l, final_g.reshape(1, D_MODEL), ys.reshape(N_SLOTS, LANE_TILES, 128))


def _block_diag(blocks):
    g, a, b = blocks.shape
    eye = jnp.eye(g, dtype=blocks.dtype)
    return (eye[:, None, :, None] * blocks[:, :, None, :]).reshape(g * a, g * b)


def _dft_tables(n):
    k = np.arange(n, dtype=np.int64)
    ang = 2.0 * np.pi * ((k[:, None] * k[None, :]) % n).astype(np.float64) / n
    return np.cos(ang) / math.sqrt(n), np.sin(ang) / math.sqrt(n)


def _seq_constants(seq):
    t = np.arange(seq)[:, None]
    win = np.array(POOL_WINDOWS)[None, :]
    lo = np.clip(t - win // 2, 0, seq)
    hi = np.clip(t - win // 2 + win, 0, seq)
    invcnt = np.repeat(1.0 / (hi - lo).astype(np.float64), POOL_CH, axis=1)
    cos_p, sin_p = _dft_tables(seq)
    cos_c, sin_c = _dft_tables(FFT_CH)
    eye = np.eye(POOL_GROUPS)
    as_bf16 = lambda a: jnp.asarray(a, F32).astype(BF16)
    return {
        "invcnt": jnp.asarray(invcnt, F32),
        "f_pos": as_bf16(np.concatenate([cos_p, -sin_p], axis=1)),
        "f_cos_ch": as_bf16(np.kron(eye, cos_c)),
        "f_sin_ch": as_bf16(np.kron(eye, sin_c)),
        "ones96": as_bf16(np.kron(eye, np.ones((SGU_CH, SGU_CH)))),
    }


def _rope_constants():
    rows = DEC_SEQ // GRID_W
    row = jnp.repeat(jnp.arange(rows, dtype=F32), GRID_W)
    col = jnp.tile(jnp.arange(GRID_W, dtype=F32), rows)
    freqs = ROPE_THETA ** (-jnp.arange(ROPE_PAIRS, dtype=F32) / ROPE_PAIRS)
    ang = jnp.stack([row[:, None] * freqs, col[:, None] * freqs], axis=1)
    ang = jnp.repeat(ang.reshape(DEC_SEQ, 2 * ROPE_PAIRS), 2, axis=1)
    ang = jnp.tile(ang, (1, 128 // HEAD_DIM))
    even = (jnp.arange(128) % 2 == 0)[None, :]
    sin = jnp.sin(ang)
    return {
        "rope_cos": jnp.cos(ang),
        "rope_sin_next": jnp.where(even, -sin, 0.0),
        "rope_sin_prev": jnp.where(even, 0.0, sin),
    }


def _layer_weights(p, l):
    group_of_lane = np.arange(SGU_WIDTH) // SGU_CH
    return {
        "w_in": p["w_in"][l].astype(BF16),
        "pool_w_bd": _block_diag(p["pool_w"][l]).astype(BF16),
        "pool_scale": p["pool_scale"][l].reshape(1, POOL_WIDTH),
        "sgu_w_stack": p["sgu_w"][l].reshape(SGU_GROUPS * SGU_CHUNK, SGU_CHUNK).astype(BF16),
        "sgu_bias": p["sgu_b"][l].T[:, group_of_lane],
        "gq": jnp.tile(p["q_norm_g"][l], N_HEADS).reshape(1, ATTN_WIDTH),
        "gk": jnp.tile(p["k_norm_g"][l], N_KV_HEADS).reshape(1, KV_WIDTH),
        "w_br_pool": p["w_br_pool"][l].astype(BF16),
        "w_br_fourier": p["w_br_fourier"][l].astype(BF16),
        "w_br_attn": p["w_br_attn"][l].astype(BF16),
        "w_br_sgu": p["w_br_sgu"][l].astype(BF16),
        "w_out": p["w_out"][l].astype(BF16),
        "router_w": p["router_w"][l].T,
        "router_b": p["router_b"][l].reshape(N_EXPERTS, 1),
    }


def kernel(x_prompt, x_sample, cache_k, cache_v, c, c_ctx, w_mod, b_mod, norm_mix_g, norm_ffn_g, w_in, pool_w, pool_scale, q_norm_g, k_norm_g, sgu_w, sgu_b, w_br_pool, w_br_fourier, w_br_attn, w_br_sgu, w_out, router_w, router_b, moe_w_gate, moe_b_gate, moe_w_up, moe_b_up, moe_w_down, moe_b_down, final_norm_g):
    p = dict(w_in=w_in, pool_w=pool_w, pool_scale=pool_scale, q_norm_g=q_norm_g, k_norm_g=k_norm_g,
             sgu_w=sgu_w, sgu_b=sgu_b, w_br_pool=w_br_pool, w_br_fourier=w_br_fourier,
             w_br_attn=w_br_attn, w_br_sgu=w_br_sgu, w_out=w_out, router_w=router_w,
             router_b=router_b, moe_w_gate=moe_w_gate, moe_b_gate=moe_b_gate, moe_w_up=moe_w_up,
             moe_b_up=moe_b_up, moe_w_down=moe_w_down, moe_b_down=moe_b_down)

    cond = jnp.concatenate([c_ctx[None, :], c, jnp.zeros((N_COND - 1 - DEC_BATCH, D_MODEL), F32)])
    mod = _modulation(cond, w_mod, b_mod).reshape(DEPTH, N_COND, 6, D_MODEL)

    ones64 = jnp.asarray(np.kron(np.eye(N_HEADS), np.ones((HEAD_DIM, HEAD_DIM))), BF16)
    consts_ctx = dict(_seq_constants(SEQ), ones64=ones64)
    consts_lat = dict(_seq_constants(DEC_SEQ), ones64=ones64, **_rope_constants())
    ck = cache_k.reshape(DEC_BATCH, DEPTH, PAST_LEN, KV_WIDTH)
    cv = cache_v.reshape(DEC_BATCH, DEPTH, PAST_LEN, KV_WIDTH)

    x_parts = (x_prompt.reshape(T_CTX, D_MODEL), x_sample.reshape(T_LAT, D_MODEL))
    new_k, new_v = [], []
    for l in range(DEPTH):
        lw = _layer_weights(p, l)
        xp, xq, qkv, uv, gates = _in_projection(x_parts, mod[l], norm_mix_g[l], lw["w_in"])
        pool_c, four_c, sgu_c = _mixers(xp, xq, uv, consts_ctx, lw, SEQ, BATCH, 0)
        pool_l, four_l, sgu_l = _mixers(xp, xq, uv, consts_lat, lw, DEC_SEQ, DEC_BATCH,
                                        T_CTX // DEC_SEQ)
        attn_c, k_c, v_c = _attention_ctx(qkv, consts_ctx, lw)
        attn_l = _attention_lat(qkv, ck[:, l], cv[:, l], consts_lat, lw)
        new_k.append(k_c.reshape(BATCH, SEQ, N_KV_HEADS, HEAD_DIM))
        new_v.append(v_c.reshape(BATCH, SEQ, N_KV_HEADS, HEAD_DIM))
        x, h, top_idx, top_w = _merge((pool_c, pool_l), (four_c, four_l), (attn_c, attn_l),
                                      (sgu_c, sgu_l), gates, x_parts, mod[l], norm_ffn_g[l], lw)
        slot, tiles = _route(top_idx)
        ys = _routed_ffn(tiles, _dispatch(tiles["rows"], slot, h), l, p)
        x = _combine(slot, top_w.T, x, mod[l], final_norm_g, ys, final_norm=(l == DEPTH - 1))
        x_parts = (x,)

    y_prompt = x[:T_CTX].reshape(BATCH, SEQ, D_MODEL)
    y_sample = x[T_CTX:].reshape(DEC_BATCH, DEC_SEQ, D_MODEL)
    return (y_prompt, y_sample, jnp.stack(new_k, axis=1), jnp.stack(new_v, axis=1))
```

```python
import functools
import math

import numpy as np
import jax
import jax.numpy as jnp
from jax import lax
from jax.experimental import pallas as pl
from jax.experimental.pallas import tpu as pltpu

F32 = jnp.float32
BF16 = jnp.bfloat16

D_MODEL = 1024
BATCH = 32
SEQ = 256
DEPTH = 2
DEC_BATCH = 2
DEC_SEQ = 1024
PAST_LEN = 256
GRID_W = 64
EPS = 1e-6
POOL_GROUPS = 4
POOL_CH = 96
POOL_WIDTH = 384
POOL_WINDOWS = (2, 4, 8, 16)
POOL_PAD = 16
FFT_CH = 96
FFT_WIDTH = 384
N_HEADS = 8
N_KV_HEADS = 2
HEAD_DIM = 64
ATTN_WIDTH = 512
KV_WIDTH = 128
QKV_WIDTH = ATTN_WIDTH + 2 * KV_WIDTH
ROPE_THETA = 10000.0
ROPE_PAIRS = 16
SGU_GROUPS = 4
SGU_CH = 96
SGU_WIDTH = 384
SGU_CHUNK = 128
N_BRANCHES = 4
GATE_WIDTH = N_BRANCHES * D_MODEL
IN_COLS = POOL_WIDTH + FFT_WIDTH + QKV_WIDTH + 2 * SGU_WIDTH + GATE_WIDTH
N_EXPERTS = 32
TOP_K = 4
D_FF = 1024
SWIGLU_LIMIT = 7.0
SWIGLU_ALPHA = 1.702

ROW_GROUP = 8
LANE_TILES = D_MODEL // 128

T_CTX = BATCH * SEQ
T_LAT = DEC_BATCH * DEC_SEQ
T_ALL = T_CTX + T_LAT
N_COND = 8
VMEM_LIMIT = 56 * 1024 * 1024

COL_XP = 0
COL_XQ = COL_XP + POOL_WIDTH
COL_QKV = COL_XQ + FFT_WIDTH
COL_UV = COL_QKV + QKV_WIDTH
COL_GATE = COL_UV + 2 * SGU_WIDTH


def _params(*sem):
    return pltpu.CompilerParams(dimension_semantics=sem, vmem_limit_bytes=VMEM_LIMIT)


def _split_bf16(x):
    hi = x.astype(BF16)
    lo = (x - hi.astype(F32)).astype(BF16)
    return hi, lo


def _dot(a, b):
    return jnp.dot(a, b, preferred_element_type=F32)


def _dot_nt(a, b):
    return lax.dot_general(a, b, (((1,), (1,)), ((), ())), preferred_element_type=F32)


def _group_lane_select(lane, vals, width):
    out = vals[-1]
    for g in range(len(vals) - 2, -1, -1):
        out = jnp.where(lane < (g + 1) * width, vals[g], out)
    return out


def _cond_row(blk, blocks_ctx, blocks_per_latent):
    return jnp.where(blk < blocks_ctx, 0, 1 + (blk - blocks_ctx) // blocks_per_latent)


MOD_TN = 1536


def _mod_kernel(c_ref, w_ref, b_ref, o_ref):
    c = c_ref[...]
    s = c * jax.nn.sigmoid(c)
    sh, sl = _split_bf16(s)
    wh, wl = _split_bf16(w_ref[...])
    o_ref[...] = _dot(sh, wh) + _dot(sh, wl) + _dot(sl, wh) + b_ref[...]


def _modulation(cond, w_mod, b_mod):
    n_cols = 6 * D_MODEL
    return pl.pallas_call(
        _mod_kernel,
        grid=(DEPTH, n_cols // MOD_TN),
        in_specs=[
            pl.BlockSpec((N_COND, D_MODEL), lambda l, j: (0, 0)),
            pl.BlockSpec((None, D_MODEL, MOD_TN), lambda l, j: (l, 0, j)),
            pl.BlockSpec((None, 1, MOD_TN), lambda l, j: (l, 0, j)),
        ],
        out_specs=pl.BlockSpec((None, N_COND, MOD_TN), lambda l, j: (l, 0, j)),
        out_shape=jax.ShapeDtypeStruct((DEPTH, N_COND, n_cols), F32),
        compiler_params=_params("arbitrary", "arbitrary"),
        name="modulation",
    )(cond, w_mod, b_mod.reshape(DEPTH, 1, n_cols))


INPROJ_TM = 256
INPROJ_SEGMENTS = (
    (COL_XP, POOL_WIDTH), (COL_XQ, FFT_WIDTH), (COL_QKV, QKV_WIDTH),
    (COL_UV, 2 * SGU_WIDTH), (COL_GATE, GATE_WIDTH))
INPROJ_CHUNK = 1024


def _ada_norm(x, g, shift, scale):
    xn = x * lax.rsqrt(jnp.mean(x * x, axis=-1, keepdims=True) + EPS)
    return xn * g * (1.0 + scale) + shift


def _token_specs(parts, tm, width):
    if len(parts) == 1:
        return [pl.BlockSpec((tm, width), lambda i: (i, 0))]
    nc = T_CTX // tm
    return [pl.BlockSpec((tm, width), lambda i: (jnp.minimum(i, nc - 1), 0)),
            pl.BlockSpec((tm, width), lambda i: (jnp.maximum(i - nc, 0), 0))]


def _token_load(refs, tm):
    if len(refs) == 1:
        return refs[0][...]
    return jnp.where(pl.program_id(0) < T_CTX // tm, refs[0][...], refs[1][...])


def _inproj_kernel(*refs, n_x):
    x_refs, (mod_ref, g_ref, w_ref), out_refs = refs[:n_x], refs[n_x:n_x + 3], refs[n_x + 3:]
    h = _ada_norm(_token_load(x_refs, INPROJ_TM), g_ref[...], mod_ref[0:1, :], mod_ref[1:2, :])
    hb = h.astype(BF16)
    for (col, width), o_ref in zip(INPROJ_SEGMENTS, out_refs):
        for c0 in range(0, width, INPROJ_CHUNK):
            c1 = min(c0 + INPROJ_CHUNK, width)
            proj = _dot(hb, w_ref[:, col + c0:col + c1])
            o_ref[:, c0:c1] = jax.nn.sigmoid(proj) if col == COL_GATE else proj


def _in_projection(x_parts, mod_l, g, w_in_bf16):
    tm = INPROJ_TM
    row = functools.partial(_cond_row, blocks_ctx=T_CTX // tm, blocks_per_latent=DEC_SEQ // tm)
    return pl.pallas_call(
        functools.partial(_inproj_kernel, n_x=len(x_parts)),
        grid=(T_ALL // tm,),
        in_specs=_token_specs(x_parts, tm, D_MODEL) + [
            pl.BlockSpec((None, 6, D_MODEL), lambda i: (row(i), 0, 0)),
            pl.BlockSpec((1, D_MODEL), lambda i: (0, 0)),
            pl.BlockSpec((D_MODEL, IN_COLS), lambda i: (0, 0)),
        ],
        out_specs=[pl.BlockSpec((tm, w), lambda i: (i, 0)) for _, w in INPROJ_SEGMENTS],
        out_shape=[jax.ShapeDtypeStruct((T_ALL, w), F32) for _, w in INPROJ_SEGMENTS],
        compiler_params=_params("arbitrary"),
        name="in_projection",
    )(*x_parts, mod_l, g.reshape(1, D_MODEL), w_in_bf16)


def _pool_mixer(xp, invcnt, w_bd, scale):
    s = xp.shape[0]
    n = s + 2 * POOL_PAD
    zeros = jnp.zeros((POOL_PAD, POOL_WIDTH), F32)
    xe = jnp.concatenate([zeros, xp, zeros], axis=0)

    def shift(a, k):
        return pltpu.roll(a, k % n, 0)

    s2 = xe + shift(xe, 1)
    s4 = shift(s2, 1) + shift(s2, -1)
    s8 = shift(s4, 2) + shift(s4, -2)
    s16 = shift(s8, 4) + shift(s8, -4)
    lane = lax.broadcasted_iota(jnp.int32, (1, POOL_WIDTH), 1)
    total = _group_lane_select(lane, [s2, s4, s8, s16], POOL_CH)[POOL_PAD:POOL_PAD + s]
    pooled = total * invcnt - xp
    return _dot(pooled.astype(BF16), w_bd) * scale


def _fourier_mixer(xq, f_cos_ch, f_sin_ch, f_pos):
    xb = xq.astype(BF16)
    a = _dot(xb, f_cos_ch).astype(BF16)
    b = _dot(xb, f_sin_ch).astype(BF16)
    return _dot(f_pos, jnp.concatenate([a, b], axis=0))


def _group_mean_sq(x, ones_bd, width):
    hi, lo = _split_bf16(x * x)
    return (_dot(hi, ones_bd) + _dot(lo, ones_bd)) * (1.0 / width)


def _mixers_kernel(xp_ref, xq_ref, uv_ref, invcnt_ref, pool_w_ref, pool_s_ref, fcc_ref, fsc_ref,
                   fpos_ref, ones_ref, sgu_w_ref, sgu_b_ref, pool_o, four_o, sgu_o):
    s = xp_ref.shape[0]
    pool_o[...] = _pool_mixer(xp_ref[...], invcnt_ref[...], pool_w_ref[...],
                              pool_s_ref[...]).astype(BF16)
    four_o[...] = _fourier_mixer(xq_ref[...], fcc_ref[...], fsc_ref[...], fpos_ref[...]).astype(BF16)

    act = jax.nn.gelu(uv_ref[...], approximate=True)
    u = act[:, :SGU_WIDTH]
    v = act[:, SGU_WIDTH:]
    vg = (v * lax.rsqrt(_group_mean_sq(v, ones_ref[...], SGU_CH) + EPS)).astype(BF16)
    lane = lax.broadcasted_iota(jnp.int32, (1, SGU_WIDTH), 1)
    w_stack = sgu_w_ref[...]
    bias = sgu_b_ref[...]
    for n in range(s // SGU_CHUNK):
        rows = slice(n * SGU_CHUNK, (n + 1) * SGU_CHUNK)
        r = _dot(w_stack, vg[rows])
        per_group = [r[g * SGU_CHUNK:(g + 1) * SGU_CHUNK] for g in range(SGU_GROUPS)]
        spatial = _group_lane_select(lane, per_group, SGU_CH) + bias
        sgu_o[rows, :] = (u[rows] * spatial).astype(BF16)


def _mixers(xp, xq, uv, consts, lw, seq, n_seq, block0):
    full = lambda shape: pl.BlockSpec(shape, lambda b: (0,) * len(shape))
    tok = lambda w: pl.BlockSpec((seq, w), lambda b: (block0 + b, 0))
    out = lambda: pl.BlockSpec((seq, POOL_WIDTH), lambda b: (b, 0))
    return pl.pallas_call(
        _mixers_kernel,
        grid=(n_seq,),
        in_specs=[
            tok(POOL_WIDTH), tok(FFT_WIDTH), tok(2 * SGU_WIDTH),
            full((seq, POOL_WIDTH)), full((POOL_WIDTH, POOL_WIDTH)), full((1, POOL_WIDTH)),
            full((FFT_WIDTH, FFT_WIDTH)), full((FFT_WIDTH, FFT_WIDTH)), full((seq, 2 * seq)),
            full((SGU_WIDTH, SGU_WIDTH)), full((SGU_GROUPS * SGU_CHUNK, SGU_CHUNK)),
            full((SGU_CHUNK, SGU_WIDTH)),
        ],
        out_specs=[out(), out(), out()],
        out_shape=[jax.ShapeDtypeStruct((n_seq * seq, POOL_WIDTH), BF16)] * 3,
        compiler_params=_params("arbitrary"),
        name=f"mixers_s{seq}",
    )(xp, xq, uv, consts["invcnt"], lw["pool_w_bd"], lw["pool_scale"], consts["f_cos_ch"],
      consts["f_sin_ch"], consts["f_pos"], consts["ones96"], lw["sgu_w_stack"], lw["sgu_bias"])


def _head_norm(x, ones_bd, g):
    return x * lax.rsqrt(_group_mean_sq(x, ones_bd, HEAD_DIM) + EPS) * g


def _rope(x, cos, sin_next, sin_prev):
    cols = []
    for c in range(x.shape[1] // 128):
        xc = x[:, c * 128:(c + 1) * 128]
        nxt = pltpu.roll(xc, 127, 1)
        prv = pltpu.roll(xc, 1, 1)
        cols.append(xc * cos + nxt * sin_next + prv * sin_prev)
    return cols[0] if len(cols) == 1 else jnp.concatenate(cols, axis=1)


def _attend(q, keys, vals, o_ref):
    qb = (q * (HEAD_DIM ** -0.5)).astype(BF16)
    group = N_HEADS // N_KV_HEADS
    for h in range(N_HEADS):
        j = h // group
        kh = keys[:, j * HEAD_DIM:(j + 1) * HEAD_DIM]
        vh = vals[:, j * HEAD_DIM:(j + 1) * HEAD_DIM]
        s = _dot_nt(qb[:, h * HEAD_DIM:(h + 1) * HEAD_DIM], kh)
        p = jnp.exp(s - jnp.max(s, axis=-1, keepdims=True))
        denom = jnp.sum(p, axis=-1, keepdims=True)
        o = _dot(p.astype(BF16), vh) / denom
        o_ref[:, h * HEAD_DIM:(h + 1) * HEAD_DIM] = o.astype(BF16)


def _attn_ctx_kernel(qkv_ref, gq_ref, gk_ref, ones_ref, o_ref, k_ref, v_ref):
    qkv = qkv_ref[...]
    ones = ones_ref[...]
    q = _head_norm(qkv[:, :ATTN_WIDTH], ones, gq_ref[...])
    k = _head_norm(qkv[:, ATTN_WIDTH:ATTN_WIDTH + KV_WIDTH], ones[:KV_WIDTH, :KV_WIDTH], gk_ref[...])
    v = qkv[:, ATTN_WIDTH + KV_WIDTH:]
    k_ref[...] = k
    v_ref[...] = v
    _attend(q, k.astype(BF16), v.astype(BF16), o_ref)


def _attention_ctx(qkv, consts, lw):
    full = lambda shape: pl.BlockSpec(shape, lambda b: (0,) * len(shape))
    return pl.pallas_call(
        _attn_ctx_kernel,
        grid=(BATCH,),
        in_specs=[pl.BlockSpec((SEQ, QKV_WIDTH), lambda b: (b, 0)),
                  full((1, ATTN_WIDTH)), full((1, KV_WIDTH)), full((ATTN_WIDTH, ATTN_WIDTH))],
        out_specs=[pl.BlockSpec((SEQ, ATTN_WIDTH), lambda b: (b, 0)),
                   pl.BlockSpec((SEQ, KV_WIDTH), lambda b: (b, 0)),
                   pl.BlockSpec((SEQ, KV_WIDTH), lambda b: (b, 0))],
        out_shape=[jax.ShapeDtypeStruct((T_CTX, ATTN_WIDTH), BF16),
                   jax.ShapeDtypeStruct((T_CTX, KV_WIDTH), F32),
                   jax.ShapeDtypeStruct((T_CTX, KV_WIDTH), F32)],
        compiler_params=_params("arbitrary"),
        name="attention_ctx",
    )(qkv, lw["gq"], lw["gk"], consts["ones64"])


LAT_QBLK = 256


def _attn_lat_kernel(q_ref, kv_ref, ck_ref, cv_ref, gq_ref, gk_ref, ones_ref, cos_q, sn_q, sp_q,
                     cos_k, sn_k, sp_k, o_ref, keys, vals):
    ones = ones_ref[...]

    @pl.when(pl.program_id(1) == 0)
    def _():
        kv = kv_ref[...]
        k = _head_norm(kv[:, :KV_WIDTH], ones[:KV_WIDTH, :KV_WIDTH], gk_ref[...])
        keys[0:DEC_SEQ, :] = _rope(k, cos_k[...], sn_k[...], sp_k[...]).astype(BF16)
        keys[DEC_SEQ:, :] = ck_ref[...].astype(BF16)
        vals[0:DEC_SEQ, :] = kv[:, KV_WIDTH:].astype(BF16)
        vals[DEC_SEQ:, :] = cv_ref[...].astype(BF16)

    q = _head_norm(q_ref[...], ones, gq_ref[...])
    q = _rope(q, cos_q[...], sn_q[...], sp_q[...])
    _attend(q, keys[...], vals[...], o_ref)


def _attention_lat(qkv, cache_k_l, cache_v_l, consts, lw):
    nq = DEC_SEQ // LAT_QBLK
    q0 = T_CTX // LAT_QBLK
    s0 = T_CTX // DEC_SEQ
    full = lambda shape: pl.BlockSpec(shape, lambda b, j: (0,) * len(shape))
    rope_q = lambda: pl.BlockSpec((LAT_QBLK, 128), lambda b, j: (j, 0))
    rope_k = lambda: pl.BlockSpec((DEC_SEQ, 128), lambda b, j: (0, 0))
    return pl.pallas_call(
        _attn_lat_kernel,
        grid=(DEC_BATCH, nq),
        in_specs=[
            pl.BlockSpec((LAT_QBLK, ATTN_WIDTH), lambda b, j: (q0 + b * nq + j, 0)),
            pl.BlockSpec((DEC_SEQ, 2 * KV_WIDTH), lambda b, j: (s0 + b, ATTN_WIDTH // (2 * KV_WIDTH))),
            pl.BlockSpec((None, PAST_LEN, KV_WIDTH), lambda b, j: (b, 0, 0)),
            pl.BlockSpec((None, PAST_LEN, KV_WIDTH), lambda b, j: (b, 0, 0)),
            full((1, ATTN_WIDTH)), full((1, KV_WIDTH)), full((ATTN_WIDTH, ATTN_WIDTH)),
            rope_q(), rope_q(), rope_q(), rope_k(), rope_k(), rope_k(),
        ],
        out_specs=pl.BlockSpec((LAT_QBLK, ATTN_WIDTH), lambda b, j: (b * nq + j, 0)),
        out_shape=jax.ShapeDtypeStruct((T_LAT, ATTN_WIDTH), BF16),
        scratch_shapes=[pltpu.VMEM((DEC_SEQ + PAST_LEN, KV_WIDTH), BF16),
                        pltpu.VMEM((DEC_SEQ + PAST_LEN, KV_WIDTH), BF16)],
        compiler_params=_params("arbitrary", "arbitrary"),
        name="attention_lat",
    )(qkv, qkv, cache_k_l, cache_v_l, lw["gq"], lw["gk"], consts["ones64"],
      consts["rope_cos"], consts["rope_sin_next"], consts["rope_sin_prev"],
      consts["rope_cos"], consts["rope_sin_next"], consts["rope_sin_prev"])


MERGE_TM = 512


def _merge_kernel(*refs, n_x):
    branch_refs, gate_ref, x_refs = refs[:8], refs[8], refs[9:9 + n_x]
    (mod_ref, g_ref, wp_ref, wf_ref, wa_ref, ws_ref, wo_ref, rw_ref, rb_ref,
     x_o, h_o, idx_o, wgt_o) = refs[9 + n_x:]
    merged = None
    for i, w_ref in enumerate((wp_ref, wf_ref, wa_ref, ws_ref)):
        br = _dot(_token_load(branch_refs[2 * i:2 * i + 2], MERGE_TM), w_ref[...])
        term = gate_ref[:, i * D_MODEL:(i + 1) * D_MODEL] * br
        merged = term if merged is None else merged + term
    mix = _dot(merged.astype(BF16), wo_ref[...])
    x = _token_load(x_refs, MERGE_TM) + mod_ref[2:3, :] * mix
    x_o[...] = x
    h = _ada_norm(x, g_ref[...], mod_ref[3:4, :], mod_ref[4:5, :])
    for c in range(LANE_TILES):
        h_o[:, c, :, :] = h[:, c * 128:(c + 1) * 128].reshape(MERGE_TM // ROW_GROUP, ROW_GROUP, 128)

    hh, hl = _split_bf16(h)
    rh, rl = _split_bf16(rw_ref[...])
    logits = _dot_nt(rh, hh) + _dot_nt(rl, hh) + _dot_nt(rh, hl) + rb_ref[...]
    expert = lax.broadcasted_iota(jnp.int32, logits.shape, 0).astype(F32)
    work = logits
    top = jnp.max(logits, axis=0, keepdims=True)
    idx, wgt = [], []
    denom = jnp.zeros_like(top)
    for _ in range(TOP_K):
        m = jnp.max(work, axis=0, keepdims=True)
        first = jnp.min(jnp.where(work == m, expert, float(N_EXPERTS)), axis=0, keepdims=True)
        e = jnp.exp(m - top)
        idx.append(first)
        wgt.append(e)
        denom = denom + e
        work = jnp.where(expert == first, -jnp.inf, work)
    idx_o[...] = jnp.concatenate(idx, axis=0).astype(jnp.int32)
    wgt_o[...] = jnp.concatenate(wgt, axis=0) / denom


def _merge(pool, four, attn, sgu, gates, x_parts, mod_l, g_ffn, lw):
    tm = MERGE_TM
    row = functools.partial(_cond_row, blocks_ctx=T_CTX // tm, blocks_per_latent=DEC_SEQ // tm)
    full = lambda shape: pl.BlockSpec(shape, lambda i: (0,) * len(shape))
    tok = lambda w: pl.BlockSpec((tm, w), lambda i: (i, 0))
    branch_specs = []
    for pair, width in ((pool, POOL_WIDTH), (four, FFT_WIDTH), (attn, ATTN_WIDTH), (sgu, SGU_WIDTH)):
        branch_specs += _token_specs(pair, tm, width)
    return pl.pallas_call(
        functools.partial(_merge_kernel, n_x=len(x_parts)),
        grid=(T_ALL // tm,),
        in_specs=branch_specs + [tok(GATE_WIDTH)] + _token_specs(x_parts, tm, D_MODEL) + [
            pl.BlockSpec((None, 6, D_MODEL), lambda i: (row(i), 0, 0)),
            full((1, D_MODEL)),
            full((POOL_WIDTH, D_MODEL)), full((FFT_WIDTH, D_MODEL)), full((ATTN_WIDTH, D_MODEL)),
            full((SGU_WIDTH, D_MODEL)), full((D_MODEL, D_MODEL)),
            full((N_EXPERTS, D_MODEL)), full((N_EXPERTS, 1)),
        ],
        out_specs=[tok(D_MODEL),
                   pl.BlockSpec((tm // ROW_GROUP, LANE_TILES, ROW_GROUP, 128), lambda i: (i, 0, 0, 0)),
                   pl.BlockSpec((TOP_K, tm), lambda i: (0, i)), pl.BlockSpec((TOP_K, tm), lambda i: (0, i))],
        out_shape=[jax.ShapeDtypeStruct((T_ALL, D_MODEL), F32),
                   jax.ShapeDtypeStruct((T_ALL // ROW_GROUP, LANE_TILES, ROW_GROUP, 128), F32),
                   jax.ShapeDtypeStruct((TOP_K, T_ALL), jnp.int32),
                   jax.ShapeDtypeStruct((TOP_K, T_ALL), F32)],
        compiler_params=_params("arbitrary"),
        name="merge_router",
    )(*pool, *four, *attn, *sgu, gates, *x_parts, mod_l, g_ffn.reshape(1, D_MODEL),
      lw["w_br_pool"], lw["w_br_fourier"], lw["w_br_attn"], lw["w_br_sgu"], lw["w_out"],
      lw["router_w"], lw["router_b"])


N_PAIRS = T_ALL * TOP_K
FFN_TM = 512
N_ROW_TILES = N_PAIRS // FFN_TM + N_EXPERTS
N_SLOTS = N_ROW_TILES * FFN_TM
ROUTE_TB = 512


def _route(top_idx):
    experts = jnp.arange(N_EXPERTS, dtype=jnp.int32)
    onehot = (top_idx.T[:, :, None] == experts[None, None, :]).astype(jnp.int32)
    per_token = jnp.sum(onehot, axis=1)
    csum = jnp.cumsum(per_token, axis=0)
    counts = csum[-1]
    tiles = (counts + FFN_TM - 1) // FFN_TM
    tile_end = jnp.cumsum(tiles)
    row_start = (tile_end - tiles) * FFN_TM
    first_row = csum - per_token + row_start[None, :]
    slot = jnp.sum(onehot * first_row[:, None, :], axis=2)
    tile_ids = jnp.arange(N_ROW_TILES, dtype=jnp.int32)
    tile_expert = jnp.sum((tile_end[None, :] <= tile_ids[:, None]).astype(jnp.int32), axis=1)
    tile_expert = jnp.minimum(tile_expert, N_EXPERTS - 1)
    rows_before = (tile_ids - (tile_end - tiles)[tile_expert]) * FFN_TM
    tile_rows = jnp.where(tile_ids < tile_end[-1],
                          jnp.clip(counts[tile_expert] - rows_before, 0, FFN_TM), 0)
    slot = slot.astype(jnp.int32).reshape(T_ALL // ROUTE_TB, 1, ROUTE_TB * TOP_K)
    used = (tiles > 0).astype(jnp.int32)
    rank = jnp.cumsum(used) - 1
    group_expert = jnp.sum(jnp.where((rank[None, :] == experts[:, None]) & (used[None, :] > 0),
                                     experts[None, :], 0), axis=1)
    tiles_info = dict(expert=tile_expert.astype(jnp.int32), rows=tile_rows.astype(jnp.int32),
                      group=rank[tile_expert].astype(jnp.int32),
                      group_expert=group_expert.astype(jnp.int32),
                      n_groups=jnp.sum(used).reshape(1).astype(jnp.int32),
                      n_used=tile_end[-1:].astype(jnp.int32))
    return slot, tiles_info


def _start_pair_rows(copy):
    def body(g, carry):
        for u in range(ROW_GROUP):
            for k in range(TOP_K):
                copy(g, u, k).start(priority=k % 2)
        return carry

    lax.fori_loop(0, ROUTE_TB // ROW_GROUP, body, 0)


def _pair_slot(slot_ref, g, u, k):
    return slot_ref[0, (g * ROW_GROUP + u) * TOP_K + k]


def _dispatch_kernel(tr_ref, slot_ref, h_ref, xs_out, zeros, sem, zero_sem):
    @pl.when(pl.program_id(0) == 0)
    def _():
        zeros[...] = jnp.zeros_like(zeros)

        def for_each_unfilled_tile(fn):
            def body(i, carry):
                @pl.when(tr_ref[i] < FFN_TM)
                def _():
                    fn(pltpu.make_async_copy(zeros, xs_out.at[pl.ds(i * FFN_TM, FFN_TM)], zero_sem))
                return carry

            lax.fori_loop(0, N_ROW_TILES, body, 0)

        for_each_unfilled_tile(lambda c: c.start())
        for_each_unfilled_tile(lambda c: c.wait())

    def copy(g, u, k):
        return pltpu.make_async_copy(h_ref.at[g, :, u, :], xs_out.at[_pair_slot(slot_ref, g, u, k)], sem)

    _start_pair_rows(copy)
    for _ in range(TOP_K):
        pltpu.make_async_copy(xs_out.at[pl.ds(0, ROUTE_TB)], xs_out.at[pl.ds(0, ROUTE_TB)], sem).wait()


def _dispatch(tile_rows, slot, h):
    return pl.pallas_call(
        _dispatch_kernel,
        grid_spec=pltpu.PrefetchScalarGridSpec(
            num_scalar_prefetch=1,
            grid=(T_ALL // ROUTE_TB,),
            in_specs=[
                pl.BlockSpec((None, 1, ROUTE_TB * TOP_K), lambda i, tr: (i, 0, 0),
                             memory_space=pltpu.SMEM),
                pl.BlockSpec((ROUTE_TB // ROW_GROUP, LANE_TILES, ROW_GROUP, 128),
                             lambda i, tr: (i, 0, 0, 0)),
            ],
            out_specs=pl.BlockSpec(memory_space=pl.ANY),
            scratch_shapes=[pltpu.VMEM((FFN_TM, LANE_TILES, 128), F32), pltpu.SemaphoreType.DMA(()),
                            pltpu.SemaphoreType.DMA(())],
        ),
        out_shape=jax.ShapeDtypeStruct((N_SLOTS, LANE_TILES, 128), F32),
        compiler_params=_params("arbitrary"),
        name="moe_dispatch",
    )(tile_rows, slot, h)


def _expert_ffn(xb, wg, bg, wu, bu, wd, bd):
    gate = jnp.minimum(_dot(xb, wg) + bg, SWIGLU_LIMIT)
    up = jnp.clip(_dot(xb, wu) + bu, -SWIGLU_LIMIT, SWIGLU_LIMIT)
    glu = gate * jax.nn.sigmoid(SWIGLU_ALPHA * gate)
    return _dot(((up + 1.0) * glu).astype(BF16), wd) + bd


def _ffn_kernel(te_ref, tr_ref, tg_ref, ge_ref, ng_ref, nu_ref, xs_ref, wg_hbm, bg_ref, wu_hbm,
                bu_ref, wd_hbm, bd_ref, ys_ref, stage, w_bf16, xb, sems, *, layer):
    i = pl.program_id(0)
    n_rows = tr_ref[i]
    group = tg_ref[i]
    slot = group % 2
    half = FFN_TM // 2

    def weight_copies(g, s):
        e = ge_ref[g]
        return [pltpu.make_async_copy(w.at[layer, e], stage.at[s, j], sems.at[s])
                for j, w in enumerate((wg_hbm, wu_hbm, wd_hbm))]

    def lane_tile(ref, c, rows):
        return ref.at[pl.ds(c, rows, stride=LANE_TILES), :]

    def ffn(rows):
        for c in range(LANE_TILES):
            xb[:rows, c * 128:(c + 1) * 128] = lane_tile(xs_ref, c, rows)[...].astype(BF16)
        y = _expert_ffn(xb[:rows, :], w_bf16[0], bg_ref[...], w_bf16[1], bu_ref[...], w_bf16[2],
                        bd_ref[...])
        for c in range(LANE_TILES):
            lane_tile(ys_ref, c, rows)[...] = y[:, c * 128:(c + 1) * 128]

    @pl.when(i == 0)
    def _():
        for c in weight_copies(0, 0):
            c.start()

    @pl.when((n_rows > 0) & ((i == 0) | (te_ref[i] != te_ref[jnp.maximum(i - 1, 0)])))
    def _():
        for c in weight_copies(group, slot):
            c.wait()
        for j in range(3):
            w_bf16[j] = stage[slot, j].astype(BF16)

        @pl.when(group + 1 < ng_ref[0])
        def _():
            for c in weight_copies(group + 1, 1 - slot):
                c.start()

    @pl.when(n_rows > half)
    def _():
        ffn(FFN_TM)

    @pl.when(n_rows <= half)
    def _():
        ys_ref[half * LANE_TILES:, :] = jnp.zeros((half * LANE_TILES, 128), F32)

        @pl.when(n_rows > 0)
        def _():
            ffn(half)

        @pl.when(n_rows == 0)
        def _():
            ys_ref[:half * LANE_TILES, :] = jnp.zeros((half * LANE_TILES, 128), F32)


def _routed_ffn(tiles, xs, l, p):
    n_prefetch = 6
    rows = pl.BlockSpec((FFN_TM * LANE_TILES, 128),
                        lambda i, te, tr, tg, ge, ng, nu: (jnp.minimum(i, nu[0] - 1), 0))
    bias_spec = lambda w: pl.BlockSpec((None, None, 1, w),
                                       lambda i, te, tr, tg, ge, ng, nu: (l, te[i], 0, 0))
    bias = lambda a: a.reshape(DEPTH, N_EXPERTS, 1, a.shape[-1])
    hbm = pl.BlockSpec(memory_space=pl.ANY)
    return pl.pallas_call(
        functools.partial(_ffn_kernel, layer=l),
        grid_spec=pltpu.PrefetchScalarGridSpec(
            num_scalar_prefetch=n_prefetch,
            grid=(N_ROW_TILES,),
            in_specs=[rows, hbm, bias_spec(D_FF), hbm, bias_spec(D_FF), hbm, bias_spec(D_MODEL)],
            out_specs=pl.BlockSpec((FFN_TM * LANE_TILES, 128),
                                   lambda i, te, tr, tg, ge, ng, nu: (i, 0)),
            scratch_shapes=[pltpu.VMEM((2, 3, D_MODEL, D_FF), F32), pltpu.VMEM((3, D_MODEL, D_FF), BF16),
                            pltpu.VMEM((FFN_TM, D_MODEL), BF16), pltpu.SemaphoreType.DMA((2,))],
        ),
        out_shape=jax.ShapeDtypeStruct((N_SLOTS * LANE_TILES, 128), F32),
        compiler_params=_params("arbitrary"),
        name="moe_ffn",
    )(tiles["expert"], tiles["rows"], tiles["group"], tiles["group_expert"], tiles["n_groups"],
      tiles["n_used"], xs.reshape(N_SLOTS * LANE_TILES, 128), p["moe_w_gate"], bias(p["moe_b_gate"]),
      p["moe_w_up"], bias(p["moe_b_up"]), p["moe_w_down"], bias(p["moe_b_down"]))


def _combine_kernel(slot_ref, slot_next_ref, w_ref, x_ref, mod_ref, fg_ref, ys_ref, *rest, final_norm):
    out_refs, (acc, buf, sems) = rest[:-3], rest[-3:]
    i = pl.program_id(0)
    cur = i % 2

    def copy(slots, b, g, u, k):
        return pltpu.make_async_copy(ys_ref.at[_pair_slot(slots, g, u, k)], buf.at[b, k, g, :, u, :],
                                     sems.at[b])

    @pl.when(i == 0)
    def _():
        _start_pair_rows(functools.partial(copy, slot_ref, 0))

    @pl.when(i + 1 < pl.num_programs(0))
    def _():
        _start_pair_rows(functools.partial(copy, slot_next_ref, 1 - cur))

    for k in range(TOP_K):
        pltpu.make_async_copy(ys_ref.at[pl.ds(0, ROUTE_TB)], ys_ref.at[pl.ds(0, ROUTE_TB)],
                              sems.at[cur]).wait()

    def rows_of(k, c):
        return buf[cur, k, :, c, :, :].reshape(ROUTE_TB, 128)

    w = w_ref[...]
    sum_sq = jnp.zeros((ROUTE_TB, 1), F32)
    for c in range(LANE_TILES):
        cols = slice(c * 128, (c + 1) * 128)
        mix = w[:, 0:1] * rows_of(0, c)
        for k in range(1, TOP_K):
            mix = mix + w[:, k:k + 1] * rows_of(k, c)
        piece = x_ref[:, cols] + mod_ref[5:6, cols] * mix
        sum_sq = sum_sq + jnp.sum(piece * piece, axis=-1, keepdims=True)
        acc[:, cols] = piece

    if not final_norm:
        out_refs[0][...] = acc[...]
        return
    normed = acc[...] * lax.rsqrt(sum_sq * (1.0 / D_MODEL) + EPS) * fg_ref[...]

    @pl.when(i < T_CTX // ROUTE_TB)
    def _():
        out_refs[0][...] = normed

    @pl.when(i >= T_CTX // ROUTE_TB)
    def _():
        out_refs[1][...] = normed


def _combine(slot, top_w, x, mod_l, final_g, ys, final_norm):
    tb = ROUTE_TB
    nb = T_ALL // tb
    nc = T_CTX // tb
    row = functools.partial(_cond_row, blocks_ctx=nc, blocks_per_latent=DEC_SEQ // tb)
    tok = lambda w: pl.BlockSpec((tb, w), lambda i: (i, 0))
    slots = lambda ahead: pl.BlockSpec((None, 1, tb * TOP_K),
                                       lambda i: (jnp.minimum(i + ahead, nb - 1), 0, 0),
                                       memory_space=pltpu.SMEM)
    if final_norm:
        out_specs = [pl.BlockSpec((tb, D_MODEL), lambda i: (jnp.minimum(i, nc - 1), 0)),
                     pl.BlockSpec((tb, D_MODEL), lambda i: (jnp.maximum(i - nc, 0), 0))]
        out_shape = [jax.ShapeDtypeStruct((T_CTX, D_MODEL), F32),
                     jax.ShapeDtypeStruct((T_LAT, D_MODEL), F32)]
    else:
        out_specs = [tok(D_MODEL)]
        out_shape = [jax.ShapeDtypeStruct((T_ALL, D_MODEL), F32)]
    return pl.pallas_call(
        functools.partial(_combine_kernel, final_norm=final_norm),
        grid=(nb,),
        in_specs=[
            slots(0), slots(1), tok(TOP_K), tok(D_MODEL),
            pl.BlockSpec((None, 6, D_MODEL), lambda i: (row(i), 0, 0)),
            pl.BlockSpec((1, D_MODEL), lambda i: (0, 0)),
            pl.BlockSpec(memory_space=pl.ANY),
        ],
        out_specs=out_specs,
        out_shape=out_shape,
        scratch_shapes=[pltpu.VMEM((tb, D_MODEL), F32),
                        pltpu.VMEM((2, TOP_K, tb // ROW_GROUP, LANE_TILES, ROW_GROUP, 128), F32),
                        pltpu.SemaphoreType.DMA((2,))],
        compiler_params=_params("arbitrary"),
        name="moe_combine",
    )(slot, slot, top_w, x, mod_l, final_g.reshape(1, D_MODEL), ys.reshape(N_SLOTS, LANE_TILES, 128))


def _block_diag(blocks):
    g, a, b = blocks.shape
    eye = jnp.eye(g, dtype=blocks.dtype)
    return (eye[:, None, :, None] * blocks[:, :, None, :]).reshape(g * a, g * b)


def _dft_tables(n):
    k = np.arange(n, dtype=np.int64)
    ang = 2.0 * np.pi * ((k[:, None] * k[None, :]) % n).astype(np.float64) / n
    return np.cos(ang) / math.sqrt(n), np.sin(ang) / math.sqrt(n)


def _seq_constants(seq):
    t = np.arange(seq)[:, None]
    win = np.array(POOL_WINDOWS)[None, :]
    lo = np.clip(t - win // 2, 0, seq)
    hi = np.clip(t - win // 2 + win, 0, seq)
    invcnt = np.repeat(1.0 / (hi - lo).astype(np.float64), POOL_CH, axis=1)
    cos_p, sin_p = _dft_tables(seq)
    cos_c, sin_c = _dft_tables(FFT_CH)
    eye = np.eye(POOL_GROUPS)
    as_bf16 = lambda a: jnp.asarray(a, F32).astype(BF16)
    return {
        "invcnt": jnp.asarray(invcnt, F32),
        "f_pos": as_bf16(np.concatenate([cos_p, -sin_p], axis=1)),
        "f_cos_ch": as_bf16(np.kron(eye, cos_c)),
        "f_sin_ch": as_bf16(np.kron(eye, sin_c)),
        "ones96": as_bf16(np.kron(eye, np.ones((SGU_CH, SGU_CH)))),
    }


def _rope_constants():
    rows = DEC_SEQ // GRID_W
    row = jnp.repeat(jnp.arange(rows, dtype=F32), GRID_W)
    col = jnp.tile(jnp.arange(GRID_W, dtype=F32), rows)
    freqs = ROPE_THETA ** (-jnp.arange(ROPE_PAIRS, dtype=F32) / ROPE_PAIRS)
    ang = jnp.stack([row[:, None] * freqs, col[:, None] * freqs], axis=1)
    ang = jnp.repeat(ang.reshape(DEC_SEQ, 2 * ROPE_PAIRS), 2, axis=1)
    ang = jnp.tile(ang, (1, 128 // HEAD_DIM))
    even = (jnp.arange(128) % 2 == 0)[None, :]
    sin = jnp.sin(ang)
    return {
        "rope_cos": jnp.cos(ang),
        "rope_sin_next": jnp.where(even, -sin, 0.0),
        "rope_sin_prev": jnp.where(even, 0.0, sin),
    }


def _layer_weights(p, l):
    group_of_lane = np.arange(SGU_WIDTH) // SGU_CH
    return {
        "w_in": p["w_in"][l].astype(BF16),
        "pool_w_bd": _block_diag(p["pool_w"][l]).astype(BF16),
        "pool_scale": p["pool_scale"][l].reshape(1, POOL_WIDTH),
        "sgu_w_stack": p["sgu_w"][l].reshape(SGU_GROUPS * SGU_CHUNK, SGU_CHUNK).astype(BF16),
        "sgu_bias": p["sgu_b"][l].T[:, group_of_lane],
        "gq": jnp.tile(p["q_norm_g"][l], N_HEADS).reshape(1, ATTN_WIDTH),
        "gk": jnp.tile(p["k_norm_g"][l], N_KV_HEADS).reshape(1, KV_WIDTH),
        "w_br_pool": p["w_br_pool"][l].astype(BF16),
        "w_br_fourier": p["w_br_fourier"][l].astype(BF16),
        "w_br_attn": p["w_br_attn"][l].astype(BF16),
        "w_br_sgu": p["w_br_sgu"][l].astype(BF16),
        "w_out": p["w_out"][l].astype(BF16),
        "router_w": p["router_w"][l].T,
        "router_b": p["router_b"][l].reshape(N_EXPERTS, 1),
    }


def kernel(x_prompt, x_sample, cache_k, cache_v, c, c_ctx, w_mod, b_mod, norm_mix_g, norm_ffn_g, w_in, pool_w, pool_scale, q_norm_g, k_norm_g, sgu_w, sgu_b, w_br_pool, w_br_fourier, w_br_attn, w_br_sgu, w_out, router_w, router_b, moe_w_gate, moe_b_gate, moe_w_up, moe_b_up, moe_w_down, moe_b_down, final_norm_g):
    p = dict(w_in=w_in, pool_w=pool_w, pool_scale=pool_scale, q_norm_g=q_norm_g, k_norm_g=k_norm_g,
             sgu_w=sgu_w, sgu_b=sgu_b, w_br_pool=w_br_pool, w_br_fourier=w_br_fourier,
             w_br_attn=w_br_attn, w_br_sgu=w_br_sgu, w_out=w_out, router_w=router_w,
             router_b=router_b, moe_w_gate=moe_w_gate, moe_b_gate=moe_b_gate, moe_w_up=moe_w_up,
             moe_b_up=moe_b_up, moe_w_down=moe_w_down, moe_b_down=moe_b_down)

    cond = jnp.concatenate([c_ctx[None, :], c, jnp.zeros((N_COND - 1 - DEC_BATCH, D_MODEL), F32)])
    mod = _modulation(cond, w_mod, b_mod).reshape(DEPTH, N_COND, 6, D_MODEL)

    ones64 = jnp.asarray(np.kron(np.eye(N_HEADS), np.ones((HEAD_DIM, HEAD_DIM))), BF16)
    consts_ctx = dict(_seq_constants(SEQ), ones64=ones64)
    consts_lat = dict(_seq_constants(DEC_SEQ), ones64=ones64, **_rope_constants())
    ck = cache_k.reshape(DEC_BATCH, DEPTH, PAST_LEN, KV_WIDTH)
    cv = cache_v.reshape(DEC_BATCH, DEPTH, PAST_LEN, KV_WIDTH)

    x_parts = (x_prompt.reshape(T_CTX, D_MODEL), x_sample.reshape(T_LAT, D_MODEL))
    new_k, new_v = [], []
    for l in range(DEPTH):
        lw = _layer_weights(p, l)
        xp, xq, qkv, uv, gates = _in_projection(x_parts, mod[l], norm_mix_g[l], lw["w_in"])
        pool_c, four_c, sgu_c = _mixers(xp, xq, uv, consts_ctx, lw, SEQ, BATCH, 0)
        pool_l, four_l, sgu_l = _mixers(xp, xq, uv, consts_lat, lw, DEC_SEQ, DEC_BATCH,
                                        T_CTX // DEC_SEQ)
        attn_c, k_c, v_c = _attention_ctx(qkv, consts_ctx, lw)
        attn_l = _attention_lat(qkv, ck[:, l], cv[:, l], consts_lat, lw)
        new_k.append(k_c.reshape(BATCH, SEQ, N_KV_HEADS, HEAD_DIM))
        new_v.append(v_c.reshape(BATCH, SEQ, N_KV_HEADS, HEAD_DIM))
        x, h, top_idx, top_w = _merge((pool_c, pool_l), (four_c, four_l), (attn_c, attn_l),
                                      (sgu_c, sgu_l), gates, x_parts, mod[l], norm_ffn_g[l], lw)
        slot, tiles = _route(top_idx)
        ys = _routed_ffn(tiles, _dispatch(tiles["rows"], slot, h), l, p)
        x_parts = tuple(_combine(slot, top_w.T, x, mod[l], final_norm_g, ys,
                                 final_norm=(l == DEPTH - 1)))

    y_prompt = x_parts[0].reshape(BATCH, SEQ, D_MODEL)
    y_sample = x_parts[1].reshape(DEC_BATCH, DEC_SEQ, D_MODEL)
    return (y_prompt, y_sample, jnp.stack(new_k, axis=1), jnp.stack(new_v, axis=1))
```

```python
import functools
import math

import numpy as np
import jax
import jax.numpy as jnp
from jax import lax
from jax.experimental import pallas as pl
from jax.experimental.pallas import tpu as pltpu

F32 = jnp.float32
BF16 = jnp.bfloat16

D_MODEL = 1024
BATCH = 32
SEQ = 256
DEPTH = 2
DEC_BATCH = 2
DEC_SEQ = 1024
PAST_LEN = 256
GRID_W = 64
EPS = 1e-6
POOL_GROUPS = 4
POOL_CH = 96
POOL_WIDTH = 384
POOL_WINDOWS = (2, 4, 8, 16)
POOL_PAD = 16
FFT_CH = 96
FFT_WIDTH = 384
N_HEADS = 8
N_KV_HEADS = 2
HEAD_DIM = 64
ATTN_WIDTH = 512
KV_WIDTH = 128
QKV_WIDTH = ATTN_WIDTH + 2 * KV_WIDTH
ROPE_THETA = 10000.0
ROPE_PAIRS = 16
SGU_GROUPS = 4
SGU_CH = 96
SGU_WIDTH = 384
SGU_CHUNK = 128
N_BRANCHES = 4
GATE_WIDTH = N_BRANCHES * D_MODEL
IN_COLS = POOL_WIDTH + FFT_WIDTH + QKV_WIDTH + 2 * SGU_WIDTH + GATE_WIDTH
N_EXPERTS = 32
TOP_K = 4
D_FF = 1024
SWIGLU_LIMIT = 7.0
SWIGLU_ALPHA = 1.702

ROW_GROUP = 8
LANE_TILES = D_MODEL // 128

T_CTX = BATCH * SEQ
T_LAT = DEC_BATCH * DEC_SEQ
T_ALL = T_CTX + T_LAT
N_COND = 8
VMEM_LIMIT = 56 * 1024 * 1024

COL_XP = 0
COL_XQ = COL_XP + POOL_WIDTH
COL_QKV = COL_XQ + FFT_WIDTH
COL_UV = COL_QKV + QKV_WIDTH
COL_GATE = COL_UV + 2 * SGU_WIDTH


def _params(*sem):
    return pltpu.CompilerParams(dimension_semantics=sem, vmem_limit_bytes=VMEM_LIMIT)


def _split_bf16(x):
    hi = x.astype(BF16)
    lo = (x - hi.astype(F32)).astype(BF16)
    return hi, lo


def _dot(a, b):
    return jnp.dot(a, b, preferred_element_type=F32)


def _dot_nt(a, b):
    return lax.dot_general(a, b, (((1,), (1,)), ((), ())), preferred_element_type=F32)


def _group_lane_select(lane, vals, width):
    out = vals[-1]
    for g in range(len(vals) - 2, -1, -1):
        out = jnp.where(lane < (g + 1) * width, vals[g], out)
    return out


def _cond_row(blk, blocks_ctx, blocks_per_latent):
    return jnp.where(blk < blocks_ctx, 0, 1 + (blk - blocks_ctx) // blocks_per_latent)


MOD_TN = 1536


def _mod_kernel(c_ref, w_ref, b_ref, o_ref):
    c = c_ref[...]
    s = c * jax.nn.sigmoid(c)
    sh, sl = _split_bf16(s)
    wh, wl = _split_bf16(w_ref[...])
    o_ref[...] = _dot(sh, wh) + _dot(sh, wl) + _dot(sl, wh) + b_ref[...]


def _modulation(cond, w_mod, b_mod):
    n_cols = 6 * D_MODEL
    return pl.pallas_call(
        _mod_kernel,
        grid=(DEPTH, n_cols // MOD_TN),
        in_specs=[
            pl.BlockSpec((N_COND, D_MODEL), lambda l, j: (0, 0)),
            pl.BlockSpec((None, D_MODEL, MOD_TN), lambda l, j: (l, 0, j)),
            pl.BlockSpec((None, 1, MOD_TN), lambda l, j: (l, 0, j)),
        ],
        out_specs=pl.BlockSpec((None, N_COND, MOD_TN), lambda l, j: (l, 0, j)),
        out_shape=jax.ShapeDtypeStruct((DEPTH, N_COND, n_cols), F32),
        compiler_params=_params("arbitrary", "arbitrary"),
        name="modulation",
    )(cond, w_mod, b_mod.reshape(DEPTH, 1, n_cols))


INPROJ_TM = 256
INPROJ_SEGMENTS = (
    (COL_XP, POOL_WIDTH), (COL_XQ, FFT_WIDTH), (COL_QKV, QKV_WIDTH),
    (COL_UV, 2 * SGU_WIDTH), (COL_GATE, GATE_WIDTH))
INPROJ_CHUNK = 1024


def _ada_norm(x, g, shift, scale):
    xn = x * lax.rsqrt(jnp.mean(x * x, axis=-1, keepdims=True) + EPS)
    return xn * g * (1.0 + scale) + shift


def _token_specs(parts, tm, width):
    if len(parts) == 1:
        return [pl.BlockSpec((tm, width), lambda i: (i, 0))]
    nc = T_CTX // tm
    return [pl.BlockSpec((tm, width), lambda i: (jnp.minimum(i, nc - 1), 0)),
            pl.BlockSpec((tm, width), lambda i: (jnp.maximum(i - nc, 0), 0))]


def _token_load(refs, tm):
    if len(refs) == 1:
        return refs[0][...]
    return jnp.where(pl.program_id(0) < T_CTX // tm, refs[0][...], refs[1][...])


def _inproj_kernel(*refs, n_x):
    x_refs, (mod_ref, g_ref, w_ref), out_refs = refs[:n_x], refs[n_x:n_x + 3], refs[n_x + 3:]
    h = _ada_norm(_token_load(x_refs, INPROJ_TM), g_ref[...], mod_ref[0:1, :], mod_ref[1:2, :])
    hb = h.astype(BF16)
    for (col, width), o_ref in zip(INPROJ_SEGMENTS, out_refs):
        for c0 in range(0, width, INPROJ_CHUNK):
            c1 = min(c0 + INPROJ_CHUNK, width)
            proj = _dot(hb, w_ref[:, col + c0:col + c1])
            o_ref[:, c0:c1] = jax.nn.sigmoid(proj) if col == COL_GATE else proj


def _in_projection(x_parts, mod_l, g, w_in_bf16):
    tm = INPROJ_TM
    row = functools.partial(_cond_row, blocks_ctx=T_CTX // tm, blocks_per_latent=DEC_SEQ // tm)
    return pl.pallas_call(
        functools.partial(_inproj_kernel, n_x=len(x_parts)),
        grid=(T_ALL // tm,),
        in_specs=_token_specs(x_parts, tm, D_MODEL) + [
            pl.BlockSpec((None, 6, D_MODEL), lambda i: (row(i), 0, 0)),
            pl.BlockSpec((1, D_MODEL), lambda i: (0, 0)),
            pl.BlockSpec((D_MODEL, IN_COLS), lambda i: (0, 0)),
        ],
        out_specs=[pl.BlockSpec((tm, w), lambda i: (i, 0)) for _, w in INPROJ_SEGMENTS],
        out_shape=[jax.ShapeDtypeStruct((T_ALL, w), F32) for _, w in INPROJ_SEGMENTS],
        compiler_params=_params("arbitrary"),
        name="in_projection",
    )(*x_parts, mod_l, g.reshape(1, D_MODEL), w_in_bf16)


def _pool_mixer(xp, invcnt, w_bd, scale):
    s = xp.shape[0]
    n = s + 2 * POOL_PAD
    zeros = jnp.zeros((POOL_PAD, POOL_WIDTH), F32)
    xe = jnp.concatenate([zeros, xp, zeros], axis=0)

    def shift(a, k):
        return pltpu.roll(a, k % n, 0)

    s2 = xe + shift(xe, 1)
    s4 = shift(s2, 1) + shift(s2, -1)
    s8 = shift(s4, 2) + shift(s4, -2)
    s16 = shift(s8, 4) + shift(s8, -4)
    lane = lax.broadcasted_iota(jnp.int32, (1, POOL_WIDTH), 1)
    total = _group_lane_select(lane, [s2, s4, s8, s16], POOL_CH)[POOL_PAD:POOL_PAD + s]
    pooled = total * invcnt - xp
    return _dot(pooled.astype(BF16), w_bd) * scale


def _fourier_mixer(xq, f_cos_ch, f_sin_ch, f_pos):
    xb = xq.astype(BF16)
    a = _dot(xb, f_cos_ch).astype(BF16)
    b = _dot(xb, f_sin_ch).astype(BF16)
    return _dot(f_pos, jnp.concatenate([a, b], axis=0))


def _group_mean_sq(x, ones_bd, width):
    hi, lo = _split_bf16(x * x)
    return (_dot(hi, ones_bd) + _dot(lo, ones_bd)) * (1.0 / width)


def _mixers_kernel(xp_ref, xq_ref, uv_ref, invcnt_ref, pool_w_ref, pool_s_ref, fcc_ref, fsc_ref,
                   fpos_ref, ones_ref, sgu_w_ref, sgu_b_ref, pool_o, four_o, sgu_o):
    s = xp_ref.shape[0]
    pool_o[...] = _pool_mixer(xp_ref[...], invcnt_ref[...], pool_w_ref[...],
                              pool_s_ref[...]).astype(BF16)
    four_o[...] = _fourier_mixer(xq_ref[...], fcc_ref[...], fsc_ref[...], fpos_ref[...]).astype(BF16)

    act = jax.nn.gelu(uv_ref[...], approximate=True)
    u = act[:, :SGU_WIDTH]
    v = act[:, SGU_WIDTH:]
    vg = (v * lax.rsqrt(_group_mean_sq(v, ones_ref[...], SGU_CH) + EPS)).astype(BF16)
    lane = lax.broadcasted_iota(jnp.int32, (1, SGU_WIDTH), 1)
    w_stack = sgu_w_ref[...]
    bias = sgu_b_ref[...]
    for n in range(s // SGU_CHUNK):
        rows = slice(n * SGU_CHUNK, (n + 1) * SGU_CHUNK)
        r = _dot(w_stack, vg[rows])
        per_group = [r[g * SGU_CHUNK:(g + 1) * SGU_CHUNK] for g in range(SGU_GROUPS)]
        spatial = _group_lane_select(lane, per_group, SGU_CH) + bias
        sgu_o[rows, :] = (u[rows] * spatial).astype(BF16)


def _mixers(xp, xq, uv, consts, lw, seq, n_seq, block0):
    full = lambda shape: pl.BlockSpec(shape, lambda b: (0,) * len(shape))
    tok = lambda w: pl.BlockSpec((seq, w), lambda b: (block0 + b, 0))
    out = lambda: pl.BlockSpec((seq, POOL_WIDTH), lambda b: (b, 0))
    return pl.pallas_call(
        _mixers_kernel,
        grid=(n_seq,),
        in_specs=[
            tok(POOL_WIDTH), tok(FFT_WIDTH), tok(2 * SGU_WIDTH),
            full((seq, POOL_WIDTH)), full((POOL_WIDTH, POOL_WIDTH)), full((1, POOL_WIDTH)),
            full((FFT_WIDTH, FFT_WIDTH)), full((FFT_WIDTH, FFT_WIDTH)), full((seq, 2 * seq)),
            full((SGU_WIDTH, SGU_WIDTH)), full((SGU_GROUPS * SGU_CHUNK, SGU_CHUNK)),
            full((SGU_CHUNK, SGU_WIDTH)),
        ],
        out_specs=[out(), out(), out()],
        out_shape=[jax.ShapeDtypeStruct((n_seq * seq, POOL_WIDTH), BF16)] * 3,
        compiler_params=_params("arbitrary"),
        name=f"mixers_s{seq}",
    )(xp, xq, uv, consts["invcnt"], lw["pool_w_bd"], lw["pool_scale"], consts["f_cos_ch"],
      consts["f_sin_ch"], consts["f_pos"], consts["ones96"], lw["sgu_w_stack"], lw["sgu_bias"])


def _head_norm(x, ones_bd, g):
    return x * lax.rsqrt(_group_mean_sq(x, ones_bd, HEAD_DIM) + EPS) * g


def _rope(x, cos, sin_next, sin_prev):
    cols = []
    for c in range(x.shape[1] // 128):
        xc = x[:, c * 128:(c + 1) * 128]
        nxt = pltpu.roll(xc, 127, 1)
        prv = pltpu.roll(xc, 1, 1)
        cols.append(xc * cos + nxt * sin_next + prv * sin_prev)
    return cols[0] if len(cols) == 1 else jnp.concatenate(cols, axis=1)


def _attend(q, keys, vals, o_ref):
    qb = (q * (HEAD_DIM ** -0.5)).astype(BF16)
    group = N_HEADS // N_KV_HEADS
    for h in range(N_HEADS):
        j = h // group
        kh = keys[:, j * HEAD_DIM:(j + 1) * HEAD_DIM]
        vh = vals[:, j * HEAD_DIM:(j + 1) * HEAD_DIM]
        s = _dot_nt(qb[:, h * HEAD_DIM:(h + 1) * HEAD_DIM], kh)
        p = jnp.exp(s - jnp.max(s, axis=-1, keepdims=True))
        denom = jnp.sum(p, axis=-1, keepdims=True)
        o = _dot(p.astype(BF16), vh) / denom
        o_ref[:, h * HEAD_DIM:(h + 1) * HEAD_DIM] = o.astype(BF16)


def _attn_ctx_kernel(qkv_ref, gq_ref, gk_ref, ones_ref, o_ref, k_ref, v_ref):
    qkv = qkv_ref[...]
    ones = ones_ref[...]
    q = _head_norm(qkv[:, :ATTN_WIDTH], ones, gq_ref[...])
    k = _head_norm(qkv[:, ATTN_WIDTH:ATTN_WIDTH + KV_WIDTH], ones[:KV_WIDTH, :KV_WIDTH], gk_ref[...])
    v = qkv[:, ATTN_WIDTH + KV_WIDTH:]
    k_ref[...] = k
    v_ref[...] = v
    _attend(q, k.astype(BF16), v.astype(BF16), o_ref)


def _attention_ctx(qkv, consts, lw):
    full = lambda shape: pl.BlockSpec(shape, lambda b: (0,) * len(shape))
    return pl.pallas_call(
        _attn_ctx_kernel,
        grid=(BATCH,),
        in_specs=[pl.BlockSpec((SEQ, QKV_WIDTH), lambda b: (b, 0)),
                  full((1, ATTN_WIDTH)), full((1, KV_WIDTH)), full((ATTN_WIDTH, ATTN_WIDTH))],
        out_specs=[pl.BlockSpec((SEQ, ATTN_WIDTH), lambda b: (b, 0)),
                   pl.BlockSpec((SEQ, KV_WIDTH), lambda b: (b, 0)),
                   pl.BlockSpec((SEQ, KV_WIDTH), lambda b: (b, 0))],
        out_shape=[jax.ShapeDtypeStruct((T_CTX, ATTN_WIDTH), BF16),
                   jax.ShapeDtypeStruct((T_CTX, KV_WIDTH), F32),
                   jax.ShapeDtypeStruct((T_CTX, KV_WIDTH), F32)],
        compiler_params=_params("arbitrary"),
        name="attention_ctx",
    )(qkv, lw["gq"], lw["gk"], consts["ones64"])


LAT_QBLK = 256


def _attn_lat_kernel(q_ref, kv_ref, ck_ref, cv_ref, gq_ref, gk_ref, ones_ref, cos_q, sn_q, sp_q,
                     cos_k, sn_k, sp_k, o_ref, keys, vals):
    ones = ones_ref[...]

    @pl.when(pl.program_id(1) == 0)
    def _():
        kv = kv_ref[...]
        k = _head_norm(kv[:, :KV_WIDTH], ones[:KV_WIDTH, :KV_WIDTH], gk_ref[...])
        keys[0:DEC_SEQ, :] = _rope(k, cos_k[...], sn_k[...], sp_k[...]).astype(BF16)
        keys[DEC_SEQ:, :] = ck_ref[...].astype(BF16)
        vals[0:DEC_SEQ, :] = kv[:, KV_WIDTH:].astype(BF16)
        vals[DEC_SEQ:, :] = cv_ref[...].astype(BF16)

    q = _head_norm(q_ref[...], ones, gq_ref[...])
    q = _rope(q, cos_q[...], sn_q[...], sp_q[...])
    _attend(q, keys[...], vals[...], o_ref)


def _attention_lat(qkv, cache_k_l, cache_v_l, consts, lw):
    nq = DEC_SEQ // LAT_QBLK
    q0 = T_CTX // LAT_QBLK
    s0 = T_CTX // DEC_SEQ
    full = lambda shape: pl.BlockSpec(shape, lambda b, j: (0,) * len(shape))
    rope_q = lambda: pl.BlockSpec((LAT_QBLK, 128), lambda b, j: (j, 0))
    rope_k = lambda: pl.BlockSpec((DEC_SEQ, 128), lambda b, j: (0, 0))
    return pl.pallas_call(
        _attn_lat_kernel,
        grid=(DEC_BATCH, nq),
        in_specs=[
            pl.BlockSpec((LAT_QBLK, ATTN_WIDTH), lambda b, j: (q0 + b * nq + j, 0)),
            pl.BlockSpec((DEC_SEQ, 2 * KV_WIDTH), lambda b, j: (s0 + b, ATTN_WIDTH // (2 * KV_WIDTH))),
            pl.BlockSpec((None, PAST_LEN, KV_WIDTH), lambda b, j: (b, 0, 0)),
            pl.BlockSpec((None, PAST_LEN, KV_WIDTH), lambda b, j: (b, 0, 0)),
            full((1, ATTN_WIDTH)), full((1, KV_WIDTH)), full((ATTN_WIDTH, ATTN_WIDTH)),
            rope_q(), rope_q(), rope_q(), rope_k(), rope_k(), rope_k(),
        ],
        out_specs=pl.BlockSpec((LAT_QBLK, ATTN_WIDTH), lambda b, j: (b * nq + j, 0)),
        out_shape=jax.ShapeDtypeStruct((T_LAT, ATTN_WIDTH), BF16),
        scratch_shapes=[pltpu.VMEM((DEC_SEQ + PAST_LEN, KV_WIDTH), BF16),
                        pltpu.VMEM((DEC_SEQ + PAST_LEN, KV_WIDTH), BF16)],
        compiler_params=_params("arbitrary", "arbitrary"),
        name="attention_lat",
    )(qkv, qkv, cache_k_l, cache_v_l, lw["gq"], lw["gk"], consts["ones64"],
      consts["rope_cos"], consts["rope_sin_next"], consts["rope_sin_prev"],
      consts["rope_cos"], consts["rope_sin_next"], consts["rope_sin_prev"])


MERGE_TM = 512


def _merge_kernel(*refs, n_x):
    branch_refs, gate_ref, x_refs = refs[:8], refs[8], refs[9:9 + n_x]
    (mod_ref, g_ref, wp_ref, wf_ref, wa_ref, ws_ref, wo_ref, rw_ref, rb_ref,
     x_o, h_o, idx_o, wgt_o) = refs[9 + n_x:]
    merged = None
    for i, w_ref in enumerate((wp_ref, wf_ref, wa_ref, ws_ref)):
        br = _dot(_token_load(branch_refs[2 * i:2 * i + 2], MERGE_TM), w_ref[...])
        term = gate_ref[:, i * D_MODEL:(i + 1) * D_MODEL] * br
        merged = term if merged is None else merged + term
    mix = _dot(merged.astype(BF16), wo_ref[...])
    x = _token_load(x_refs, MERGE_TM) + mod_ref[2:3, :] * mix
    x_o[...] = x
    h = _ada_norm(x, g_ref[...], mod_ref[3:4, :], mod_ref[4:5, :])
    for c in range(LANE_TILES):
        h_o[:, c, :, :] = h[:, c * 128:(c + 1) * 128].reshape(MERGE_TM // ROW_GROUP, ROW_GROUP, 128)

    hh, hl = _split_bf16(h)
    rh, rl = _split_bf16(rw_ref[...])
    logits = _dot_nt(rh, hh) + _dot_nt(rl, hh) + _dot_nt(rh, hl) + rb_ref[...]
    expert = lax.broadcasted_iota(jnp.int32, logits.shape, 0).astype(F32)
    work = logits
    top = jnp.max(logits, axis=0, keepdims=True)
    idx, wgt = [], []
    denom = jnp.zeros_like(top)
    for _ in range(TOP_K):
        m = jnp.max(work, axis=0, keepdims=True)
        first = jnp.min(jnp.where(work == m, expert, float(N_EXPERTS)), axis=0, keepdims=True)
        e = jnp.exp(m - top)
        idx.append(first)
        wgt.append(e)
        denom = denom + e
        work = jnp.where(expert == first, -jnp.inf, work)
    idx_o[...] = jnp.concatenate(idx, axis=0).astype(jnp.int32)
    wgt_o[...] = jnp.concatenate(wgt, axis=0) / denom


def _merge(pool, four, attn, sgu, gates, x_parts, mod_l, g_ffn, lw):
    tm = MERGE_TM
    row = functools.partial(_cond_row, blocks_ctx=T_CTX // tm, blocks_per_latent=DEC_SEQ // tm)
    full = lambda shape: pl.BlockSpec(shape, lambda i: (0,) * len(shape))
    tok = lambda w: pl.BlockSpec((tm, w), lambda i: (i, 0))
    branch_specs = []
    for pair, width in ((pool, POOL_WIDTH), (four, FFT_WIDTH), (attn, ATTN_WIDTH), (sgu, SGU_WIDTH)):
        branch_specs += _token_specs(pair, tm, width)
    return pl.pallas_call(
        functools.partial(_merge_kernel, n_x=len(x_parts)),
        grid=(T_ALL // tm,),
        in_specs=branch_specs + [tok(GATE_WIDTH)] + _token_specs(x_parts, tm, D_MODEL) + [
            pl.BlockSpec((None, 6, D_MODEL), lambda i: (row(i), 0, 0)),
            full((1, D_MODEL)),
            full((POOL_WIDTH, D_MODEL)), full((FFT_WIDTH, D_MODEL)), full((ATTN_WIDTH, D_MODEL)),
            full((SGU_WIDTH, D_MODEL)), full((D_MODEL, D_MODEL)),
            full((N_EXPERTS, D_MODEL)), full((N_EXPERTS, 1)),
        ],
        out_specs=[tok(D_MODEL),
                   pl.BlockSpec((tm // ROW_GROUP, LANE_TILES, ROW_GROUP, 128), lambda i: (i, 0, 0, 0)),
                   pl.BlockSpec((TOP_K, tm), lambda i: (0, i)), pl.BlockSpec((TOP_K, tm), lambda i: (0, i))],
        out_shape=[jax.ShapeDtypeStruct((T_ALL, D_MODEL), F32),
                   jax.ShapeDtypeStruct((T_ALL // ROW_GROUP, LANE_TILES, ROW_GROUP, 128), F32),
                   jax.ShapeDtypeStruct((TOP_K, T_ALL), jnp.int32),
                   jax.ShapeDtypeStruct((TOP_K, T_ALL), F32)],
        compiler_params=_params("arbitrary"),
        name="merge_router",
    )(*pool, *four, *attn, *sgu, gates, *x_parts, mod_l, g_ffn.reshape(1, D_MODEL),
      lw["w_br_pool"], lw["w_br_fourier"], lw["w_br_attn"], lw["w_br_sgu"], lw["w_out"],
      lw["router_w"], lw["router_b"])


N_PAIRS = T_ALL * TOP_K
FFN_TM = 512
CAST_ROWS = 32
FFN_ROW_OPTIONS = (128, 256, 384, 512)
N_ROW_TILES = N_PAIRS // FFN_TM + N_EXPERTS
N_SLOTS = N_ROW_TILES * FFN_TM
ROUTE_TB = 512


def _route(top_idx):
    experts = jnp.arange(N_EXPERTS, dtype=jnp.int32)
    onehot = (top_idx.T[:, :, None] == experts[None, None, :]).astype(jnp.int32)
    per_token = jnp.sum(onehot, axis=1)
    csum = jnp.cumsum(per_token, axis=0)
    counts = csum[-1]
    tiles = (counts + FFN_TM - 1) // FFN_TM
    tile_end = jnp.cumsum(tiles)
    row_start = (tile_end - tiles) * FFN_TM
    first_row = csum - per_token + row_start[None, :]
    slot = jnp.sum(onehot * first_row[:, None, :], axis=2)
    tile_ids = jnp.arange(N_ROW_TILES, dtype=jnp.int32)
    tile_expert = jnp.sum((tile_end[None, :] <= tile_ids[:, None]).astype(jnp.int32), axis=1)
    tile_expert = jnp.minimum(tile_expert, N_EXPERTS - 1)
    rows_before = (tile_ids - (tile_end - tiles)[tile_expert]) * FFN_TM
    tile_rows = jnp.where(tile_ids < tile_end[-1],
                          jnp.clip(counts[tile_expert] - rows_before, 0, FFN_TM), 0)
    slot = slot.astype(jnp.int32).reshape(T_ALL // ROUTE_TB, 1, ROUTE_TB * TOP_K)
    used = (tiles > 0).astype(jnp.int32)
    rank = jnp.cumsum(used) - 1
    group_expert = jnp.sum(jnp.where((rank[None, :] == experts[:, None]) & (used[None, :] > 0),
                                     experts[None, :], 0), axis=1)
    tiles_info = dict(expert=tile_expert.astype(jnp.int32), rows=tile_rows.astype(jnp.int32),
                      group=rank[tile_expert].astype(jnp.int32),
                      group_expert=group_expert.astype(jnp.int32),
                      n_groups=jnp.sum(used).reshape(1).astype(jnp.int32),
                      n_used=tile_end[-1:].astype(jnp.int32))
    return slot, tiles_info


def _start_pair_rows(copy):
    def body(g, carry):
        for u in range(ROW_GROUP):
            for k in range(TOP_K):
                copy(g, u, k).start(priority=k % 2)
        return carry

    lax.fori_loop(0, ROUTE_TB // ROW_GROUP, body, 0)


def _pair_slot(slot_ref, g, u, k):
    return slot_ref[0, (g * ROW_GROUP + u) * TOP_K + k]


def _dispatch_kernel(tr_ref, slot_ref, h_ref, xs_out, zeros, sem, zero_sem):
    @pl.when(pl.program_id(0) == 0)
    def _():
        zeros[...] = jnp.zeros_like(zeros)

        def for_each_unfilled_tile(fn):
            def body(i, carry):
                @pl.when(tr_ref[i] < FFN_TM)
                def _():
                    fn(pltpu.make_async_copy(zeros, xs_out.at[pl.ds(i * FFN_TM, FFN_TM)], zero_sem))
                return carry

            lax.fori_loop(0, N_ROW_TILES, body, 0)

        for_each_unfilled_tile(lambda c: c.start())
        for_each_unfilled_tile(lambda c: c.wait())

    def copy(g, u, k):
        return pltpu.make_async_copy(h_ref.at[g, :, u, :], xs_out.at[_pair_slot(slot_ref, g, u, k)], sem)

    _start_pair_rows(copy)
    for _ in range(TOP_K):
        pltpu.make_async_copy(xs_out.at[pl.ds(0, ROUTE_TB)], xs_out.at[pl.ds(0, ROUTE_TB)], sem).wait()


def _dispatch(tile_rows, slot, h):
    return pl.pallas_call(
        _dispatch_kernel,
        grid_spec=pltpu.PrefetchScalarGridSpec(
            num_scalar_prefetch=1,
            grid=(T_ALL // ROUTE_TB,),
            in_specs=[
                pl.BlockSpec((None, 1, ROUTE_TB * TOP_K), lambda i, tr: (i, 0, 0),
                             memory_space=pltpu.SMEM),
                pl.BlockSpec((ROUTE_TB // ROW_GROUP, LANE_TILES, ROW_GROUP, 128),
                             lambda i, tr: (i, 0, 0, 0)),
            ],
            out_specs=pl.BlockSpec(memory_space=pl.ANY),
            scratch_shapes=[pltpu.VMEM((FFN_TM, LANE_TILES, 128), F32), pltpu.SemaphoreType.DMA(()),
                            pltpu.SemaphoreType.DMA(())],
        ),
        out_shape=jax.ShapeDtypeStruct((N_SLOTS, LANE_TILES, 128), F32),
        compiler_params=_params("arbitrary"),
        name="moe_dispatch",
    )(tile_rows, slot, h)


def _expert_ffn(xb, wg, bg, wu, bu, wd, bd):
    gate = jnp.minimum(_dot(xb, wg) + bg, SWIGLU_LIMIT)
    up = jnp.clip(_dot(xb, wu) + bu, -SWIGLU_LIMIT, SWIGLU_LIMIT)
    glu = gate * jax.nn.sigmoid(SWIGLU_ALPHA * gate)
    return _dot(((up + 1.0) * glu).astype(BF16), wd) + bd


def _ffn_kernel(te_ref, tr_ref, tg_ref, ge_ref, ng_ref, nu_ref, xs_ref, wg_hbm, bg_ref, wu_hbm,
                bu_ref, wd_hbm, bd_ref, ys_ref, stage, w_bf16, xb, sems, *, layer):
    i = pl.program_id(0)
    n_rows = tr_ref[i]
    group = tg_ref[i]
    slot = group % 2

    def weight_copies(g, s):
        e = ge_ref[g]
        return [pltpu.make_async_copy(w.at[layer, e], stage.at[s, j], sems.at[s])
                for j, w in enumerate((wg_hbm, wu_hbm, wd_hbm))]

    def lane_tile(ref, c, rows):
        return ref.at[pl.ds(c, rows, stride=LANE_TILES), :]

    def ffn(rows):
        for c in range(LANE_TILES):
            xb[:rows, c * 128:(c + 1) * 128] = lane_tile(xs_ref, c, rows)[...].astype(BF16)
        y = _expert_ffn(xb[:rows, :], w_bf16[0], bg_ref[...], w_bf16[1], bu_ref[...], w_bf16[2],
                        bd_ref[...])
        for c in range(LANE_TILES):
            lane_tile(ys_ref, c, rows)[...] = y[:, c * 128:(c + 1) * 128]
        if rows < FFN_TM:
            ys_ref[rows * LANE_TILES:, :] = jnp.zeros(((FFN_TM - rows) * LANE_TILES, 128), F32)

    @pl.when(i == 0)
    def _():
        for c in weight_copies(0, 0):
            c.start()

    @pl.when((n_rows > 0) & ((i == 0) | (te_ref[i] != te_ref[jnp.maximum(i - 1, 0)])))
    def _():
        for c in weight_copies(group, slot):
            c.wait()
        def cast_rows(r, carry):
            rows = pl.ds(pl.multiple_of(r * CAST_ROWS, CAST_ROWS), CAST_ROWS)
            for j in range(3):
                w_bf16[j, rows, :] = stage[slot, j, rows, :].astype(BF16)
            return carry

        lax.fori_loop(0, D_MODEL // CAST_ROWS, cast_rows, 0)

        @pl.when(group + 1 < ng_ref[0])
        def _():
            for c in weight_copies(group + 1, 1 - slot):
                c.start()

    @pl.when(n_rows == 0)
    def _():
        ys_ref[...] = jnp.zeros_like(ys_ref)

    lower = 0
    for rows in FFN_ROW_OPTIONS:
        @pl.when((n_rows > lower) & (n_rows <= rows))
        def _(rows=rows):
            ffn(rows)
        lower = rows


def _routed_ffn(tiles, xs, l, p):
    n_prefetch = 6
    rows = pl.BlockSpec((FFN_TM * LANE_TILES, 128),
                        lambda i, te, tr, tg, ge, ng, nu: (jnp.minimum(i, nu[0] - 1), 0))
    bias_spec = lambda w: pl.BlockSpec((None, None, 1, w),
                                       lambda i, te, tr, tg, ge, ng, nu: (l, te[i], 0, 0))
    bias = lambda a: a.reshape(DEPTH, N_EXPERTS, 1, a.shape[-1])
    hbm = pl.BlockSpec(memory_space=pl.ANY)
    return pl.pallas_call(
        functools.partial(_ffn_kernel, layer=l),
        grid_spec=pltpu.PrefetchScalarGridSpec(
            num_scalar_prefetch=n_prefetch,
            grid=(N_ROW_TILES,),
            in_specs=[rows, hbm, bias_spec(D_FF), hbm, bias_spec(D_FF), hbm, bias_spec(D_MODEL)],
            out_specs=pl.BlockSpec((FFN_TM * LANE_TILES, 128),
                                   lambda i, te, tr, tg, ge, ng, nu: (i, 0)),
            scratch_shapes=[pltpu.VMEM((2, 3, D_MODEL, D_FF), F32), pltpu.VMEM((3, D_MODEL, D_FF), BF16),
                            pltpu.VMEM((FFN_TM, D_MODEL), BF16), pltpu.SemaphoreType.DMA((2,))],
        ),
        out_shape=jax.ShapeDtypeStruct((N_SLOTS * LANE_TILES, 128), F32),
        compiler_params=_params("arbitrary"),
        name="moe_ffn",
    )(tiles["expert"], tiles["rows"], tiles["group"], tiles["group_expert"], tiles["n_groups"],
      tiles["n_used"], xs.reshape(N_SLOTS * LANE_TILES, 128), p["moe_w_gate"], bias(p["moe_b_gate"]),
      p["moe_w_up"], bias(p["moe_b_up"]), p["moe_w_down"], bias(p["moe_b_down"]))


def _combine_kernel(slot_ref, slot_next_ref, w_ref, x_ref, mod_ref, fg_ref, ys_ref, *rest, final_norm):
    out_refs, (acc, buf, sems) = rest[:-3], rest[-3:]
    i = pl.program_id(0)
    cur = i % 2

    def copy(slots, b, g, u, k):
        return pltpu.make_async_copy(ys_ref.at[_pair_slot(slots, g, u, k)], buf.at[b, k, g, :, u, :],
                                     sems.at[b])

    @pl.when(i == 0)
    def _():
        _start_pair_rows(functools.partial(copy, slot_ref, 0))

    @pl.when(i + 1 < pl.num_programs(0))
    def _():
        _start_pair_rows(functools.partial(copy, slot_next_ref, 1 - cur))

    for k in range(TOP_K):
        pltpu.make_async_copy(ys_ref.at[pl.ds(0, ROUTE_TB)], ys_ref.at[pl.ds(0, ROUTE_TB)],
                              sems.at[cur]).wait()

    def rows_of(k, c):
        return buf[cur, k, :, c, :, :].reshape(ROUTE_TB, 128)

    w = w_ref[...]
    sum_sq = jnp.zeros((ROUTE_TB, 1), F32)
    for c in range(LANE_TILES):
        cols = slice(c * 128, (c + 1) * 128)
        mix = w[:, 0:1] * rows_of(0, c)
        for k in range(1, TOP_K):
            mix = mix + w[:, k:k + 1] * rows_of(k, c)
        piece = x_ref[:, cols] + mod_ref[5:6, cols] * mix
        sum_sq = sum_sq + jnp.sum(piece * piece, axis=-1, keepdims=True)
        acc[:, cols] = piece

    if not final_norm:
        out_refs[0][...] = acc[...]
        return
    normed = acc[...] * lax.rsqrt(sum_sq * (1.0 / D_MODEL) + EPS) * fg_ref[...]

    @pl.when(i < T_CTX // ROUTE_TB)
    def _():
        out_refs[0][...] = normed

    @pl.when(i >= T_CTX // ROUTE_TB)
    def _():
        out_refs[1][...] = normed


def _combine(slot, top_w, x, mod_l, final_g, ys, final_norm):
    tb = ROUTE_TB
    nb = T_ALL // tb
    nc = T_CTX // tb
    row = functools.partial(_cond_row, blocks_ctx=nc, blocks_per_latent=DEC_SEQ // tb)
    tok = lambda w: pl.BlockSpec((tb, w), lambda i: (i, 0))
    slots = lambda ahead: pl.BlockSpec((None, 1, tb * TOP_K),
                                       lambda i: (jnp.minimum(i + ahead, nb - 1), 0, 0),
                                       memory_space=pltpu.SMEM)
    if final_norm:
        out_specs = [pl.BlockSpec((tb, D_MODEL), lambda i: (jnp.minimum(i, nc - 1), 0)),
                     pl.BlockSpec((tb, D_MODEL), lambda i: (jnp.maximum(i - nc, 0), 0))]
        out_shape = [jax.ShapeDtypeStruct((T_CTX, D_MODEL), F32),
                     jax.ShapeDtypeStruct((T_LAT, D_MODEL), F32)]
    else:
        out_specs = [tok(D_MODEL)]
        out_shape = [jax.ShapeDtypeStruct((T_ALL, D_MODEL), F32)]
    return pl.pallas_call(
        functools.partial(_combine_kernel, final_norm=final_norm),
        grid=(nb,),
        in_specs=[
            slots(0), slots(1), tok(TOP_K), tok(D_MODEL),
            pl.BlockSpec((None, 6, D_MODEL), lambda i: (row(i), 0, 0)),
            pl.BlockSpec((1, D_MODEL), lambda i: (0, 0)),
            pl.BlockSpec(memory_space=pl.ANY),
        ],
        out_specs=out_specs,
        out_shape=out_shape,
        scratch_shapes=[pltpu.VMEM((tb, D_MODEL), F32),
                        pltpu.VMEM((2, TOP_K, tb // ROW_GROUP, LANE_TILES, ROW_GROUP, 128), F32),
                        pltpu.SemaphoreType.DMA((2,))],
        compiler_params=_params("arbitrary"),
        name="moe_combine",
    )(slot, slot, top_w, x, mod_l, final_g.reshape(1, D_MODEL), ys.reshape(N_SLOTS, LANE_TILES, 128))


def _block_diag(blocks):
    g, a, b = blocks.shape
    eye = jnp.eye(g, dtype=blocks.dtype)
    return (eye[:, None, :, None] * blocks[:, :, None, :]).reshape(g * a, g * b)


def _dft_tables(n):
    k = np.arange(n, dtype=np.int64)
    ang = 2.0 * np.pi * ((k[:, None] * k[None, :]) % n).astype(np.float64) / n
    return np.cos(ang) / math.sqrt(n), np.sin(ang) / math.sqrt(n)


def _seq_constants(seq):
    t = np.arange(seq)[:, None]
    win = np.array(POOL_WINDOWS)[None, :]
    lo = np.clip(t - win // 2, 0, seq)
    hi = np.clip(t - win // 2 + win, 0, seq)
    invcnt = np.repeat(1.0 / (hi - lo).astype(np.float64), POOL_CH, axis=1)
    cos_p, sin_p = _dft_tables(seq)
    cos_c, sin_c = _dft_tables(FFT_CH)
    eye = np.eye(POOL_GROUPS)
    as_bf16 = lambda a: jnp.asarray(a, F32).astype(BF16)
    return {
        "invcnt": jnp.asarray(invcnt, F32),
        "f_pos": as_bf16(np.concatenate([cos_p, -sin_p], axis=1)),
        "f_cos_ch": as_bf16(np.kron(eye, cos_c)),
        "f_sin_ch": as_bf16(np.kron(eye, sin_c)),
        "ones96": as_bf16(np.kron(eye, np.ones((SGU_CH, SGU_CH)))),
    }


def _rope_constants():
    rows = DEC_SEQ // GRID_W
    row = jnp.repeat(jnp.arange(rows, dtype=F32), GRID_W)
    col = jnp.tile(jnp.arange(GRID_W, dtype=F32), rows)
    freqs = ROPE_THETA ** (-jnp.arange(ROPE_PAIRS, dtype=F32) / ROPE_PAIRS)
    ang = jnp.stack([row[:, None] * freqs, col[:, None] * freqs], axis=1)
    ang = jnp.repeat(ang.reshape(DEC_SEQ, 2 * ROPE_PAIRS), 2, axis=1)
    ang = jnp.tile(ang, (1, 128 // HEAD_DIM))
    even = (jnp.arange(128) % 2 == 0)[None, :]
    sin = jnp.sin(ang)
    return {
        "rope_cos": jnp.cos(ang),
        "rope_sin_next": jnp.where(even, -sin, 0.0),
        "rope_sin_prev": jnp.where(even, 0.0, sin),
    }


def _layer_weights(p, l):
    group_of_lane = np.arange(SGU_WIDTH) // SGU_CH
    return {
        "w_in": p["w_in"][l].astype(BF16),
        "pool_w_bd": _block_diag(p["pool_w"][l]).astype(BF16),
        "pool_scale": p["pool_scale"][l].reshape(1, POOL_WIDTH),
        "sgu_w_stack": p["sgu_w"][l].reshape(SGU_GROUPS * SGU_CHUNK, SGU_CHUNK).astype(BF16),
        "sgu_bias": p["sgu_b"][l].T[:, group_of_lane],
        "gq": jnp.tile(p["q_norm_g"][l], N_HEADS).reshape(1, ATTN_WIDTH),
        "gk": jnp.tile(p["k_norm_g"][l], N_KV_HEADS).reshape(1, KV_WIDTH),
        "w_br_pool": p["w_br_pool"][l].astype(BF16),
        "w_br_fourier": p["w_br_fourier"][l].astype(BF16),
        "w_br_attn": p["w_br_attn"][l].astype(BF16),
        "w_br_sgu": p["w_br_sgu"][l].astype(BF16),
        "w_out": p["w_out"][l].astype(BF16),
        "router_w": p["router_w"][l].T,
        "router_b": p["router_b"][l].reshape(N_EXPERTS, 1),
    }


def kernel(x_prompt, x_sample, cache_k, cache_v, c, c_ctx, w_mod, b_mod, norm_mix_g, norm_ffn_g, w_in, pool_w, pool_scale, q_norm_g, k_norm_g, sgu_w, sgu_b, w_br_pool, w_br_fourier, w_br_attn, w_br_sgu, w_out, router_w, router_b, moe_w_gate, moe_b_gate, moe_w_up, moe_b_up, moe_w_down, moe_b_down, final_norm_g):
    p = dict(w_in=w_in, pool_w=pool_w, pool_scale=pool_scale, q_norm_g=q_norm_g, k_norm_g=k_norm_g,
             sgu_w=sgu_w, sgu_b=sgu_b, w_br_pool=w_br_pool, w_br_fourier=w_br_fourier,
             w_br_attn=w_br_attn, w_br_sgu=w_br_sgu, w_out=w_out, router_w=router_w,
             router_b=router_b, moe_w_gate=moe_w_gate, moe_b_gate=moe_b_gate, moe_w_up=moe_w_up,
             moe_b_up=moe_b_up, moe_w_down=moe_w_down, moe_b_down=moe_b_down)

    cond = jnp.concatenate([c_ctx[None, :], c, jnp.zeros((N_COND - 1 - DEC_BATCH, D_MODEL), F32)])
    mod = _modulation(cond, w_mod, b_mod).reshape(DEPTH, N_COND, 6, D_MODEL)

    ones64 = jnp.asarray(np.kron(np.eye(N_HEADS), np.ones((HEAD_DIM, HEAD_DIM))), BF16)
    consts_ctx = dict(_seq_constants(SEQ), ones64=ones64)
    consts_lat = dict(_seq_constants(DEC_SEQ), ones64=ones64, **_rope_constants())
    ck = cache_k.reshape(DEC_BATCH, DEPTH, PAST_LEN, KV_WIDTH)
    cv = cache_v.reshape(DEC_BATCH, DEPTH, PAST_LEN, KV_WIDTH)

    x_parts = (x_prompt.reshape(T_CTX, D_MODEL), x_sample.reshape(T_LAT, D_MODEL))
    new_k, new_v = [], []
    for l in range(DEPTH):
        lw = _layer_weights(p, l)
        xp, xq, qkv, uv, gates = _in_projection(x_parts, mod[l], norm_mix_g[l], lw["w_in"])
        pool_c, four_c, sgu_c = _mixers(xp, xq, uv, consts_ctx, lw, SEQ, BATCH, 0)
        pool_l, four_l, sgu_l = _mixers(xp, xq, uv, consts_lat, lw, DEC_SEQ, DEC_BATCH,
                                        T_CTX // DEC_SEQ)
        attn_c, k_c, v_c = _attention_ctx(qkv, consts_ctx, lw)
        attn_l = _attention_lat(qkv, ck[:, l], cv[:, l], consts_lat, lw)
        new_k.append(k_c.reshape(BATCH, SEQ, N_KV_HEADS, HEAD_DIM))
        new_v.append(v_c.reshape(BATCH, SEQ, N_KV_HEADS, HEAD_DIM))
        x, h, top_idx, top_w = _merge((pool_c, pool_l), (four_c, four_l), (attn_c, attn_l),
                                      (sgu_c, sgu_l), gates, x_parts, mod[l], norm_ffn_g[l], lw)
        slot, tiles = _route(top_idx)
        ys = _routed_ffn(tiles, _dispatch(tiles["rows"], slot, h), l, p)
        x_parts = tuple(_combine(slot, top_w.T, x, mod[l], final_norm_g, ys,
                                 final_norm=(l == DEPTH - 1)))

    y_prompt = x_parts[0].reshape(BATCH, SEQ, D_MODEL)
    y_sample = x_parts[1].reshape(DEC_BATCH, DEC_SEQ, D_MODEL)
    return (y_prompt, y_sample, jnp.stack(new_k, axis=1), jnp.stack(new_v, axis=1))
```

```python
import functools
import math

import numpy as np
import jax
import jax.numpy as jnp
from jax import lax
from jax.experimental import pallas as pl
from jax.experimental.pallas import tpu as pltpu

F32 = jnp.float32
BF16 = jnp.bfloat16

D_MODEL = 1024
BATCH = 32
SEQ = 256
DEPTH = 2
DEC_BATCH = 2
DEC_SEQ = 1024
PAST_LEN = 256
GRID_W = 64
EPS = 1e-6
POOL_GROUPS = 4
POOL_CH = 96
POOL_WIDTH = 384
POOL_WINDOWS = (2, 4, 8, 16)
POOL_PAD = 16
FFT_CH = 96
FFT_WIDTH = 384
N_HEADS = 8
N_KV_HEADS = 2
HEAD_DIM = 64
ATTN_WIDTH = 512
KV_WIDTH = 128
QKV_WIDTH = ATTN_WIDTH + 2 * KV_WIDTH
ROPE_THETA = 10000.0
ROPE_PAIRS = 16
SGU_GROUPS = 4
SGU_CH = 96
SGU_WIDTH = 384
SGU_CHUNK = 128
N_BRANCHES = 4
GATE_WIDTH = N_BRANCHES * D_MODEL
IN_COLS = POOL_WIDTH + FFT_WIDTH + QKV_WIDTH + 2 * SGU_WIDTH + GATE_WIDTH
N_EXPERTS = 32
TOP_K = 4
D_FF = 1024
SWIGLU_LIMIT = 7.0
SWIGLU_ALPHA = 1.702

ROW_GROUP = 8
LANE_TILES = D_MODEL // 128

T_CTX = BATCH * SEQ
T_LAT = DEC_BATCH * DEC_SEQ
T_ALL = T_CTX + T_LAT
N_COND = 8
VMEM_LIMIT = 56 * 1024 * 1024
CAST_ROWS = 32

COL_XP = 0
COL_XQ = COL_XP + POOL_WIDTH
COL_QKV = COL_XQ + FFT_WIDTH
COL_UV = COL_QKV + QKV_WIDTH
COL_GATE = COL_UV + 2 * SGU_WIDTH


def _params(*sem):
    return pltpu.CompilerParams(dimension_semantics=sem, vmem_limit_bytes=VMEM_LIMIT)


def _split_bf16(x):
    hi = x.astype(BF16)
    lo = (x - hi.astype(F32)).astype(BF16)
    return hi, lo


def _dot(a, b):
    return jnp.dot(a, b, preferred_element_type=F32)


def _dot_nt(a, b):
    return lax.dot_general(a, b, (((1,), (1,)), ((), ())), preferred_element_type=F32)


def _group_lane_select(lane, vals, width):
    out = vals[-1]
    for g in range(len(vals) - 2, -1, -1):
        out = jnp.where(lane < (g + 1) * width, vals[g], out)
    return out


def _cond_row(blk, blocks_ctx, blocks_per_latent):
    return jnp.where(blk < blocks_ctx, 0, 1 + (blk - blocks_ctx) // blocks_per_latent)


MOD_TN = 1536


def _mod_kernel(c_ref, w_ref, b_ref, o_ref):
    c = c_ref[...]
    s = c * jax.nn.sigmoid(c)
    sh, sl = _split_bf16(s)
    wh, wl = _split_bf16(w_ref[...])
    o_ref[...] = _dot(sh, wh) + _dot(sh, wl) + _dot(sl, wh) + b_ref[...]


def _modulation(cond, w_mod, b_mod):
    n_cols = 6 * D_MODEL
    return pl.pallas_call(
        _mod_kernel,
        grid=(DEPTH, n_cols // MOD_TN),
        in_specs=[
            pl.BlockSpec((N_COND, D_MODEL), lambda l, j: (0, 0)),
            pl.BlockSpec((None, D_MODEL, MOD_TN), lambda l, j: (l, 0, j)),
            pl.BlockSpec((None, 1, MOD_TN), lambda l, j: (l, 0, j)),
        ],
        out_specs=pl.BlockSpec((None, N_COND, MOD_TN), lambda l, j: (l, 0, j)),
        out_shape=jax.ShapeDtypeStruct((DEPTH, N_COND, n_cols), F32),
        compiler_params=_params("arbitrary", "arbitrary"),
        name="modulation",
    )(cond, w_mod, b_mod.reshape(DEPTH, 1, n_cols))


INPROJ_TM = 256
INPROJ_SEGMENTS = (
    (COL_XP, POOL_WIDTH), (COL_XQ, FFT_WIDTH), (COL_QKV, QKV_WIDTH),
    (COL_UV, 2 * SGU_WIDTH), (COL_GATE, GATE_WIDTH))
INPROJ_CHUNK = 1024


def _ada_norm(x, g, shift, scale):
    xn = x * lax.rsqrt(jnp.mean(x * x, axis=-1, keepdims=True) + EPS)
    return xn * g * (1.0 + scale) + shift


def _token_specs(parts, tm, width):
    if len(parts) == 1:
        return [pl.BlockSpec((tm, width), lambda i: (i, 0))]
    nc = T_CTX // tm
    return [pl.BlockSpec((tm, width), lambda i: (jnp.minimum(i, nc - 1), 0)),
            pl.BlockSpec((tm, width), lambda i: (jnp.maximum(i - nc, 0), 0))]


def _token_load(refs, tm):
    if len(refs) == 1:
        return refs[0][...]
    return jnp.where(pl.program_id(0) < T_CTX // tm, refs[0][...], refs[1][...])


W_IN_CHUNK = 640


def _inproj_kernel(*refs, n_x, layer):
    x_refs, (mod_ref, g_ref, w_hbm) = refs[:n_x], refs[n_x:n_x + 3]
    out_refs, (w_ref, stage, sems) = refs[n_x + 3:-3], refs[-3:]

    @pl.when(pl.program_id(0) == 0)
    def _():
        def chunk(c):
            return pltpu.make_async_copy(w_hbm.at[layer, :, pl.ds(c * W_IN_CHUNK, W_IN_CHUNK)],
                                         stage.at[c % 2], sems.at[c % 2])

        n_chunks = IN_COLS // W_IN_CHUNK
        chunk(0).start()
        for c in range(n_chunks):
            if c + 1 < n_chunks:
                chunk(c + 1).start()
            chunk(c).wait()

            def cast_rows(r, carry, c=c):
                rows = pl.ds(pl.multiple_of(r * CAST_ROWS, CAST_ROWS), CAST_ROWS)
                w_ref[rows, c * W_IN_CHUNK:(c + 1) * W_IN_CHUNK] = stage[c % 2, rows, :].astype(BF16)
                return carry

            lax.fori_loop(0, D_MODEL // CAST_ROWS, cast_rows, 0)

    h = _ada_norm(_token_load(x_refs, INPROJ_TM), g_ref[...], mod_ref[0:1, :], mod_ref[1:2, :])
    hb = h.astype(BF16)
    for (col, width), o_ref in zip(INPROJ_SEGMENTS, out_refs):
        for c0 in range(0, width, INPROJ_CHUNK):
            c1 = min(c0 + INPROJ_CHUNK, width)
            proj = _dot(hb, w_ref[:, col + c0:col + c1])
            o_ref[:, c0:c1] = jax.nn.sigmoid(proj) if col == COL_GATE else proj


def _in_projection(x_parts, mod_l, g, w_in, l):
    tm = INPROJ_TM
    row = functools.partial(_cond_row, blocks_ctx=T_CTX // tm, blocks_per_latent=DEC_SEQ // tm)
    return pl.pallas_call(
        functools.partial(_inproj_kernel, n_x=len(x_parts), layer=l),
        grid=(T_ALL // tm,),
        in_specs=_token_specs(x_parts, tm, D_MODEL) + [
            pl.BlockSpec((None, 6, D_MODEL), lambda i: (row(i), 0, 0)),
            pl.BlockSpec((1, D_MODEL), lambda i: (0, 0)),
            pl.BlockSpec(memory_space=pl.ANY),
        ],
        out_specs=[pl.BlockSpec((tm, w), lambda i: (i, 0)) for _, w in INPROJ_SEGMENTS],
        out_shape=[jax.ShapeDtypeStruct((T_ALL, w), F32) for _, w in INPROJ_SEGMENTS],
        scratch_shapes=[pltpu.VMEM((D_MODEL, IN_COLS), BF16), pltpu.VMEM((2, D_MODEL, W_IN_CHUNK), F32),
                        pltpu.SemaphoreType.DMA((2,))],
        compiler_params=_params("arbitrary"),
        name="in_projection",
    )(*x_parts, mod_l, g.reshape(1, D_MODEL), w_in)


def _pool_mixer(xp, invcnt, w_bd, scale):
    s = xp.shape[0]
    n = s + 2 * POOL_PAD
    zeros = jnp.zeros((POOL_PAD, POOL_WIDTH), F32)
    xe = jnp.concatenate([zeros, xp, zeros], axis=0)

    def shift(a, k):
        return pltpu.roll(a, k % n, 0)

    s2 = xe + shift(xe, 1)
    s4 = shift(s2, 1) + shift(s2, -1)
    s8 = shift(s4, 2) + shift(s4, -2)
    s16 = shift(s8, 4) + shift(s8, -4)
    lane = lax.broadcasted_iota(jnp.int32, (1, POOL_WIDTH), 1)
    total = _group_lane_select(lane, [s2, s4, s8, s16], POOL_CH)[POOL_PAD:POOL_PAD + s]
    pooled = total * invcnt - xp
    return _dot(pooled.astype(BF16), w_bd) * scale


def _fourier_mixer(xq, f_cos_ch, f_sin_ch, f_pos):
    xb = xq.astype(BF16)
    a = _dot(xb, f_cos_ch).astype(BF16)
    b = _dot(xb, f_sin_ch).astype(BF16)
    return _dot(f_pos, jnp.concatenate([a, b], axis=0))


def _group_mean_sq(x, ones_bd, width):
    hi, lo = _split_bf16(x * x)
    return (_dot(hi, ones_bd) + _dot(lo, ones_bd)) * (1.0 / width)


def _mixers_kernel(xp_ref, xq_ref, uv_ref, invcnt_ref, pool_w_ref, pool_s_ref, fcc_ref, fsc_ref,
                   fpos_ref, ones_ref, sgu_w_ref, sgu_b_ref, pool_o, four_o, sgu_o):
    s = xp_ref.shape[0]
    pool_o[...] = _pool_mixer(xp_ref[...], invcnt_ref[...], pool_w_ref[...],
                              pool_s_ref[...]).astype(BF16)
    four_o[...] = _fourier_mixer(xq_ref[...], fcc_ref[...], fsc_ref[...], fpos_ref[...]).astype(BF16)

    act = jax.nn.gelu(uv_ref[...], approximate=True)
    u = act[:, :SGU_WIDTH]
    v = act[:, SGU_WIDTH:]
    vg = (v * lax.rsqrt(_group_mean_sq(v, ones_ref[...], SGU_CH) + EPS)).astype(BF16)
    lane = lax.broadcasted_iota(jnp.int32, (1, SGU_WIDTH), 1)
    w_stack = sgu_w_ref[...]
    bias = sgu_b_ref[...]
    for n in range(s // SGU_CHUNK):
        rows = slice(n * SGU_CHUNK, (n + 1) * SGU_CHUNK)
        r = _dot(w_stack, vg[rows])
        per_group = [r[g * SGU_CHUNK:(g + 1) * SGU_CHUNK] for g in range(SGU_GROUPS)]
        spatial = _group_lane_select(lane, per_group, SGU_CH) + bias
        sgu_o[rows, :] = (u[rows] * spatial).astype(BF16)


def _mixers(xp, xq, uv, consts, lw, seq, n_seq, block0):
    full = lambda shape: pl.BlockSpec(shape, lambda b: (0,) * len(shape))
    tok = lambda w: pl.BlockSpec((seq, w), lambda b: (block0 + b, 0))
    out = lambda: pl.BlockSpec((seq, POOL_WIDTH), lambda b: (b, 0))
    return pl.pallas_call(
        _mixers_kernel,
        grid=(n_seq,),
        in_specs=[
            tok(POOL_WIDTH), tok(FFT_WIDTH), tok(2 * SGU_WIDTH),
            full((seq, POOL_WIDTH)), full((POOL_WIDTH, POOL_WIDTH)), full((1, POOL_WIDTH)),
            full((FFT_WIDTH, FFT_WIDTH)), full((FFT_WIDTH, FFT_WIDTH)), full((seq, 2 * seq)),
            full((SGU_WIDTH, SGU_WIDTH)), full((SGU_GROUPS * SGU_CHUNK, SGU_CHUNK)),
            full((SGU_CHUNK, SGU_WIDTH)),
        ],
        out_specs=[out(), out(), out()],
        out_shape=[jax.ShapeDtypeStruct((n_seq * seq, POOL_WIDTH), BF16)] * 3,
        compiler_params=_params("arbitrary"),
        name=f"mixers_s{seq}",
    )(xp, xq, uv, consts["invcnt"], lw["pool_w_bd"], lw["pool_scale"], consts["f_cos_ch"],
      consts["f_sin_ch"], consts["f_pos"], consts["ones96"], lw["sgu_w_stack"], lw["sgu_bias"])


def _head_norm(x, ones_bd, g):
    return x * lax.rsqrt(_group_mean_sq(x, ones_bd, HEAD_DIM) + EPS) * g


def _rope(x, cos, sin_next, sin_prev):
    cols = []
    for c in range(x.shape[1] // 128):
        xc = x[:, c * 128:(c + 1) * 128]
        nxt = pltpu.roll(xc, 127, 1)
        prv = pltpu.roll(xc, 1, 1)
        cols.append(xc * cos + nxt * sin_next + prv * sin_prev)
    return cols[0] if len(cols) == 1 else jnp.concatenate(cols, axis=1)


def _attend(q, keys, vals, o_ref):
    qb = (q * (HEAD_DIM ** -0.5)).astype(BF16)
    group = N_HEADS // N_KV_HEADS
    for h in range(N_HEADS):
        j = h // group
        kh = keys[:, j * HEAD_DIM:(j + 1) * HEAD_DIM]
        vh = vals[:, j * HEAD_DIM:(j + 1) * HEAD_DIM]
        s = _dot_nt(qb[:, h * HEAD_DIM:(h + 1) * HEAD_DIM], kh)
        p = jnp.exp(s - jnp.max(s, axis=-1, keepdims=True))
        denom = jnp.sum(p, axis=-1, keepdims=True)
        o = _dot(p.astype(BF16), vh) / denom
        o_ref[:, h * HEAD_DIM:(h + 1) * HEAD_DIM] = o.astype(BF16)


def _attn_ctx_kernel(qkv_ref, gq_ref, gk_ref, ones_ref, o_ref, k_ref, v_ref):
    qkv = qkv_ref[...]
    ones = ones_ref[...]
    q = _head_norm(qkv[:, :ATTN_WIDTH], ones, gq_ref[...])
    k = _head_norm(qkv[:, ATTN_WIDTH:ATTN_WIDTH + KV_WIDTH], ones[:KV_WIDTH, :KV_WIDTH], gk_ref[...])
    v = qkv[:, ATTN_WIDTH + KV_WIDTH:]
    k_ref[...] = k
    v_ref[...] = v
    _attend(q, k.astype(BF16), v.astype(BF16), o_ref)


def _attention_ctx(qkv, consts, lw):
    full = lambda shape: pl.BlockSpec(shape, lambda b: (0,) * len(shape))
    return pl.pallas_call(
        _attn_ctx_kernel,
        grid=(BATCH,),
        in_specs=[pl.BlockSpec((SEQ, QKV_WIDTH), lambda b: (b, 0)),
                  full((1, ATTN_WIDTH)), full((1, KV_WIDTH)), full((ATTN_WIDTH, ATTN_WIDTH))],
        out_specs=[pl.BlockSpec((SEQ, ATTN_WIDTH), lambda b: (b, 0)),
                   pl.BlockSpec((SEQ, KV_WIDTH), lambda b: (b, 0)),
                   pl.BlockSpec((SEQ, KV_WIDTH), lambda b: (b, 0))],
        out_shape=[jax.ShapeDtypeStruct((T_CTX, ATTN_WIDTH), BF16),
                   jax.ShapeDtypeStruct((T_CTX, KV_WIDTH), F32),
                   jax.ShapeDtypeStruct((T_CTX, KV_WIDTH), F32)],
        compiler_params=_params("arbitrary"),
        name="attention_ctx",
    )(qkv, lw["gq"], lw["gk"], consts["ones64"])


LAT_QBLK = 256


def _attn_lat_kernel(q_ref, kv_ref, ck_ref, cv_ref, gq_ref, gk_ref, ones_ref, cos_q, sn_q, sp_q,
                     cos_k, sn_k, sp_k, o_ref, keys, vals):
    ones = ones_ref[...]

    @pl.when(pl.program_id(1) == 0)
    def _():
        kv = kv_ref[...]
        k = _head_norm(kv[:, :KV_WIDTH], ones[:KV_WIDTH, :KV_WIDTH], gk_ref[...])
        keys[0:DEC_SEQ, :] = _rope(k, cos_k[...], sn_k[...], sp_k[...]).astype(BF16)
        keys[DEC_SEQ:, :] = ck_ref[...].astype(BF16)
        vals[0:DEC_SEQ, :] = kv[:, KV_WIDTH:].astype(BF16)
        vals[DEC_SEQ:, :] = cv_ref[...].astype(BF16)

    q = _head_norm(q_ref[...], ones, gq_ref[...])
    q = _rope(q, cos_q[...], sn_q[...], sp_q[...])
    _attend(q, keys[...], vals[...], o_ref)


def _attention_lat(qkv, cache_k_l, cache_v_l, consts, lw):
    nq = DEC_SEQ // LAT_QBLK
    q0 = T_CTX // LAT_QBLK
    s0 = T_CTX // DEC_SEQ
    full = lambda shape: pl.BlockSpec(shape, lambda b, j: (0,) * len(shape))
    rope_q = lambda: pl.BlockSpec((LAT_QBLK, 128), lambda b, j: (j, 0))
    rope_k = lambda: pl.BlockSpec((DEC_SEQ, 128), lambda b, j: (0, 0))
    return pl.pallas_call(
        _attn_lat_kernel,
        grid=(DEC_BATCH, nq),
        in_specs=[
            pl.BlockSpec((LAT_QBLK, ATTN_WIDTH), lambda b, j: (q0 + b * nq + j, 0)),
            pl.BlockSpec((DEC_SEQ, 2 * KV_WIDTH), lambda b, j: (s0 + b, ATTN_WIDTH // (2 * KV_WIDTH))),
            pl.BlockSpec((None, PAST_LEN, KV_WIDTH), lambda b, j: (b, 0, 0)),
            pl.BlockSpec((None, PAST_LEN, KV_WIDTH), lambda b, j: (b, 0, 0)),
            full((1, ATTN_WIDTH)), full((1, KV_WIDTH)), full((ATTN_WIDTH, ATTN_WIDTH)),
            rope_q(), rope_q(), rope_q(), rope_k(), rope_k(), rope_k(),
        ],
        out_specs=pl.BlockSpec((LAT_QBLK, ATTN_WIDTH), lambda b, j: (b * nq + j, 0)),
        out_shape=jax.ShapeDtypeStruct((T_LAT, ATTN_WIDTH), BF16),
        scratch_shapes=[pltpu.VMEM((DEC_SEQ + PAST_LEN, KV_WIDTH), BF16),
                        pltpu.VMEM((DEC_SEQ + PAST_LEN, KV_WIDTH), BF16)],
        compiler_params=_params("arbitrary", "arbitrary"),
        name="attention_lat",
    )(qkv, qkv, cache_k_l, cache_v_l, lw["gq"], lw["gk"], consts["ones64"],
      consts["rope_cos"], consts["rope_sin_next"], consts["rope_sin_prev"],
      consts["rope_cos"], consts["rope_sin_next"], consts["rope_sin_prev"])


MERGE_TM = 512


def _merge_kernel(*refs, n_x):
    branch_refs, gate_ref, x_refs = refs[:8], refs[8], refs[9:9 + n_x]
    (mod_ref, g_ref, wp_ref, wf_ref, wa_ref, ws_ref, wo_ref, rw_ref, rb_ref,
     x_o, h_o, idx_o, wgt_o) = refs[9 + n_x:]
    merged = None
    for i, w_ref in enumerate((wp_ref, wf_ref, wa_ref, ws_ref)):
        br = _dot(_token_load(branch_refs[2 * i:2 * i + 2], MERGE_TM), w_ref[...])
        term = gate_ref[:, i * D_MODEL:(i + 1) * D_MODEL] * br
        merged = term if merged is None else merged + term
    mix = _dot(merged.astype(BF16), wo_ref[...])
    x = _token_load(x_refs, MERGE_TM) + mod_ref[2:3, :] * mix
    x_o[...] = x
    h = _ada_norm(x, g_ref[...], mod_ref[3:4, :], mod_ref[4:5, :])
    for c in range(LANE_TILES):
        h_o[:, c, :, :] = h[:, c * 128:(c + 1) * 128].reshape(MERGE_TM // ROW_GROUP, ROW_GROUP, 128)

    hh, hl = _split_bf16(h)
    rh, rl = _split_bf16(rw_ref[...])
    logits = _dot_nt(rh, hh) + _dot_nt(rl, hh) + _dot_nt(rh, hl) + rb_ref[...]
    expert = lax.broadcasted_iota(jnp.int32, logits.shape, 0).astype(F32)
    work = logits
    top = jnp.max(logits, axis=0, keepdims=True)
    idx, wgt = [], []
    denom = jnp.zeros_like(top)
    for _ in range(TOP_K):
        m = jnp.max(work, axis=0, keepdims=True)
        first = jnp.min(jnp.where(work == m, expert, float(N_EXPERTS)), axis=0, keepdims=True)
        e = jnp.exp(m - top)
        idx.append(first)
        wgt.append(e)
        denom = denom + e
        work = jnp.where(expert == first, -jnp.inf, work)
    idx_o[...] = jnp.concatenate(idx, axis=0).astype(jnp.int32)
    wgt_o[...] = jnp.concatenate(wgt, axis=0) / denom


def _merge(pool, four, attn, sgu, gates, x_parts, mod_l, g_ffn, lw):
    tm = MERGE_TM
    row = functools.partial(_cond_row, blocks_ctx=T_CTX // tm, blocks_per_latent=DEC_SEQ // tm)
    full = lambda shape: pl.BlockSpec(shape, lambda i: (0,) * len(shape))
    tok = lambda w: pl.BlockSpec((tm, w), lambda i: (i, 0))
    branch_specs = []
    for pair, width in ((pool, POOL_WIDTH), (four, FFT_WIDTH), (attn, ATTN_WIDTH), (sgu, SGU_WIDTH)):
        branch_specs += _token_specs(pair, tm, width)
    return pl.pallas_call(
        functools.partial(_merge_kernel, n_x=len(x_parts)),
        grid=(T_ALL // tm,),
        in_specs=branch_specs + [tok(GATE_WIDTH)] + _token_specs(x_parts, tm, D_MODEL) + [
            pl.BlockSpec((None, 6, D_MODEL), lambda i: (row(i), 0, 0)),
            full((1, D_MODEL)),
            full((POOL_WIDTH, D_MODEL)), full((FFT_WIDTH, D_MODEL)), full((ATTN_WIDTH, D_MODEL)),
            full((SGU_WIDTH, D_MODEL)), full((D_MODEL, D_MODEL)),
            full((N_EXPERTS, D_MODEL)), full((N_EXPERTS, 1)),
        ],
        out_specs=[tok(D_MODEL),
                   pl.BlockSpec((tm // ROW_GROUP, LANE_TILES, ROW_GROUP, 128), lambda i: (i, 0, 0, 0)),
                   pl.BlockSpec((TOP_K, tm), lambda i: (0, i)), pl.BlockSpec((TOP_K, tm), lambda i: (0, i))],
        out_shape=[jax.ShapeDtypeStruct((T_ALL, D_MODEL), F32),
                   jax.ShapeDtypeStruct((T_ALL // ROW_GROUP, LANE_TILES, ROW_GROUP, 128), F32),
                   jax.ShapeDtypeStruct((TOP_K, T_ALL), jnp.int32),
                   jax.ShapeDtypeStruct((TOP_K, T_ALL), F32)],
        compiler_params=_params("arbitrary"),
        name="merge_router",
    )(*pool, *four, *attn, *sgu, gates, *x_parts, mod_l, g_ffn.reshape(1, D_MODEL),
      lw["w_br_pool"], lw["w_br_fourier"], lw["w_br_attn"], lw["w_br_sgu"], lw["w_out"],
      lw["router_w"], lw["router_b"])


N_PAIRS = T_ALL * TOP_K
FFN_TM = 512
FFN_ROW_OPTIONS = (128, 256, 384, 512)
N_ROW_TILES = N_PAIRS // FFN_TM + N_EXPERTS
N_SLOTS = N_ROW_TILES * FFN_TM
ROUTE_TB = 512


def _route(top_idx):
    experts = jnp.arange(N_EXPERTS, dtype=jnp.int32)
    onehot = (top_idx.T[:, :, None] == experts[None, None, :]).astype(jnp.int32)
    per_token = jnp.sum(onehot, axis=1)
    csum = jnp.cumsum(per_token, axis=0)
    counts = csum[-1]
    tiles = (counts + FFN_TM - 1) // FFN_TM
    tile_end = jnp.cumsum(tiles)
    row_start = (tile_end - tiles) * FFN_TM
    first_row = csum - per_token + row_start[None, :]
    slot = jnp.sum(onehot * first_row[:, None, :], axis=2)
    tile_ids = jnp.arange(N_ROW_TILES, dtype=jnp.int32)
    tile_expert = jnp.sum((tile_end[None, :] <= tile_ids[:, None]).astype(jnp.int32), axis=1)
    tile_expert = jnp.minimum(tile_expert, N_EXPERTS - 1)
    rows_before = (tile_ids - (tile_end - tiles)[tile_expert]) * FFN_TM
    tile_rows = jnp.where(tile_ids < tile_end[-1],
                          jnp.clip(counts[tile_expert] - rows_before, 0, FFN_TM), 0)
    slot = slot.astype(jnp.int32).reshape(T_ALL // ROUTE_TB, 1, ROUTE_TB * TOP_K)
    used = (tiles > 0).astype(jnp.int32)
    rank = jnp.cumsum(used) - 1
    group_expert = jnp.sum(jnp.where((rank[None, :] == experts[:, None]) & (used[None, :] > 0),
                                     experts[None, :], 0), axis=1)
    tiles_info = dict(expert=tile_expert.astype(jnp.int32), rows=tile_rows.astype(jnp.int32),
                      group=rank[tile_expert].astype(jnp.int32),
                      group_expert=group_expert.astype(jnp.int32),
                      n_groups=jnp.sum(used).reshape(1).astype(jnp.int32),
                      n_used=tile_end[-1:].astype(jnp.int32))
    return slot, tiles_info


def _start_pair_rows(copy):
    def body(g, carry):
        for u in range(ROW_GROUP):
            for k in range(TOP_K):
                copy(g, u, k).start(priority=k % 2)
        return carry

    lax.fori_loop(0, ROUTE_TB // ROW_GROUP, body, 0)


def _pair_slot(slot_ref, g, u, k):
    return slot_ref[0, (g * ROW_GROUP + u) * TOP_K + k]


def _dispatch_kernel(tr_ref, slot_ref, h_ref, xs_out, zeros, sem, zero_sem):
    @pl.when(pl.program_id(0) == 0)
    def _():
        zeros[...] = jnp.zeros_like(zeros)

        def for_each_unfilled_tile(fn):
            def body(i, carry):
                @pl.when(tr_ref[i] < FFN_TM)
                def _():
                    fn(pltpu.make_async_copy(zeros, xs_out.at[pl.ds(i * FFN_TM, FFN_TM)], zero_sem))
                return carry

            lax.fori_loop(0, N_ROW_TILES, body, 0)

        for_each_unfilled_tile(lambda c: c.start())
        for_each_unfilled_tile(lambda c: c.wait())

    def copy(g, u, k):
        return pltpu.make_async_copy(h_ref.at[g, :, u, :], xs_out.at[_pair_slot(slot_ref, g, u, k)], sem)

    _start_pair_rows(copy)
    for _ in range(TOP_K):
        pltpu.make_async_copy(xs_out.at[pl.ds(0, ROUTE_TB)], xs_out.at[pl.ds(0, ROUTE_TB)], sem).wait()


def _dispatch(tile_rows, slot, h):
    return pl.pallas_call(
        _dispatch_kernel,
        grid_spec=pltpu.PrefetchScalarGridSpec(
            num_scalar_prefetch=1,
            grid=(T_ALL // ROUTE_TB,),
            in_specs=[
                pl.BlockSpec((None, 1, ROUTE_TB * TOP_K), lambda i, tr: (i, 0, 0),
                             memory_space=pltpu.SMEM),
                pl.BlockSpec((ROUTE_TB // ROW_GROUP, LANE_TILES, ROW_GROUP, 128),
                             lambda i, tr: (i, 0, 0, 0)),
            ],
            out_specs=pl.BlockSpec(memory_space=pl.ANY),
            scratch_shapes=[pltpu.VMEM((FFN_TM, LANE_TILES, 128), F32), pltpu.SemaphoreType.DMA(()),
                            pltpu.SemaphoreType.DMA(())],
        ),
        out_shape=jax.ShapeDtypeStruct((N_SLOTS, LANE_TILES, 128), F32),
        compiler_params=_params("arbitrary"),
        name="moe_dispatch",
    )(tile_rows, slot, h)


def _expert_ffn(xb, wg, bg, wu, bu, wd, bd):
    gate = jnp.minimum(_dot(xb, wg) + bg, SWIGLU_LIMIT)
    up = jnp.clip(_dot(xb, wu) + bu, -SWIGLU_LIMIT, SWIGLU_LIMIT)
    glu = gate * jax.nn.sigmoid(SWIGLU_ALPHA * gate)
    return _dot(((up + 1.0) * glu).astype(BF16), wd) + bd


def _ffn_kernel(te_ref, tr_ref, tg_ref, ge_ref, ng_ref, nu_ref, xs_ref, wg_hbm, bg_ref, wu_hbm,
                bu_ref, wd_hbm, bd_ref, ys_ref, stage, w_bf16, xb, sems, *, layer):
    i = pl.program_id(0)
    n_rows = tr_ref[i]
    group = tg_ref[i]
    slot = group % 2

    def weight_copies(g, s):
        e = ge_ref[g]
        return [pltpu.make_async_copy(w.at[layer, e], stage.at[s, j], sems.at[s])
                for j, w in enumerate((wg_hbm, wu_hbm, wd_hbm))]

    def lane_tile(ref, c, rows):
        return ref.at[pl.ds(c, rows, stride=LANE_TILES), :]

    def ffn(rows):
        for c in range(LANE_TILES):
            xb[:rows, c * 128:(c + 1) * 128] = lane_tile(xs_ref, c, rows)[...].astype(BF16)
        y = _expert_ffn(xb[:rows, :], w_bf16[0], bg_ref[...], w_bf16[1], bu_ref[...], w_bf16[2],
                        bd_ref[...])
        for c in range(LANE_TILES):
            lane_tile(ys_ref, c, rows)[...] = y[:, c * 128:(c + 1) * 128]
        if rows < FFN_TM:
            ys_ref[rows * LANE_TILES:, :] = jnp.zeros(((FFN_TM - rows) * LANE_TILES, 128), F32)

    @pl.when(i == 0)
    def _():
        for c in weight_copies(0, 0):
            c.start()

    @pl.when((n_rows > 0) & ((i == 0) | (te_ref[i] != te_ref[jnp.maximum(i - 1, 0)])))
    def _():
        for c in weight_copies(group, slot):
            c.wait()
        def cast_rows(r, carry):
            rows = pl.ds(pl.multiple_of(r * CAST_ROWS, CAST_ROWS), CAST_ROWS)
            for j in range(3):
                w_bf16[j, rows, :] = stage[slot, j, rows, :].astype(BF16)
            return carry

        lax.fori_loop(0, D_MODEL // CAST_ROWS, cast_rows, 0)

        @pl.when(group + 1 < ng_ref[0])
        def _():
            for c in weight_copies(group + 1, 1 - slot):
                c.start()

    @pl.when(n_rows == 0)
    def _():
        ys_ref[...] = jnp.zeros_like(ys_ref)

    lower = 0
    for rows in FFN_ROW_OPTIONS:
        @pl.when((n_rows > lower) & (n_rows <= rows))
        def _(rows=rows):
            ffn(rows)
        lower = rows


def _routed_ffn(tiles, xs, l, p):
    n_prefetch = 6
    rows = pl.BlockSpec((FFN_TM * LANE_TILES, 128),
                        lambda i, te, tr, tg, ge, ng, nu: (jnp.minimum(i, nu[0] - 1), 0))
    bias_spec = lambda w: pl.BlockSpec((None, None, 1, w),
                                       lambda i, te, tr, tg, ge, ng, nu: (l, te[i], 0, 0))
    bias = lambda a: a.reshape(DEPTH, N_EXPERTS, 1, a.shape[-1])
    hbm = pl.BlockSpec(memory_space=pl.ANY)
    return pl.pallas_call(
        functools.partial(_ffn_kernel, layer=l),
        grid_spec=pltpu.PrefetchScalarGridSpec(
            num_scalar_prefetch=n_prefetch,
            grid=(N_ROW_TILES,),
            in_specs=[rows, hbm, bias_spec(D_FF), hbm, bias_spec(D_FF), hbm, bias_spec(D_MODEL)],
            out_specs=pl.BlockSpec((FFN_TM * LANE_TILES, 128),
                                   lambda i, te, tr, tg, ge, ng, nu: (i, 0)),
            scratch_shapes=[pltpu.VMEM((2, 3, D_MODEL, D_FF), F32), pltpu.VMEM((3, D_MODEL, D_FF), BF16),
                            pltpu.VMEM((FFN_TM, D_MODEL), BF16), pltpu.SemaphoreType.DMA((2,))],
        ),
        out_shape=jax.ShapeDtypeStruct((N_SLOTS * LANE_TILES, 128), F32),
        compiler_params=_params("arbitrary"),
        name="moe_ffn",
    )(tiles["expert"], tiles["rows"], tiles["group"], tiles["group_expert"], tiles["n_groups"],
      tiles["n_used"], xs.reshape(N_SLOTS * LANE_TILES, 128), p["moe_w_gate"], bias(p["moe_b_gate"]),
      p["moe_w_up"], bias(p["moe_b_up"]), p["moe_w_down"], bias(p["moe_b_down"]))


def _combine_kernel(slot_ref, slot_next_ref, w_ref, x_ref, mod_ref, fg_ref, ys_ref, *rest, final_norm):
    out_refs, (acc, buf, sems) = rest[:-3], rest[-3:]
    i = pl.program_id(0)
    cur = i % 2

    def copy(slots, b, g, u, k):
        return pltpu.make_async_copy(ys_ref.at[_pair_slot(slots, g, u, k)], buf.at[b, k, g, :, u, :],
                                     sems.at[b])

    @pl.when(i == 0)
    def _():
        _start_pair_rows(functools.partial(copy, slot_ref, 0))

    @pl.when(i + 1 < pl.num_programs(0))
    def _():
        _start_pair_rows(functools.partial(copy, slot_next_ref, 1 - cur))

    for k in range(TOP_K):
        pltpu.make_async_copy(ys_ref.at[pl.ds(0, ROUTE_TB)], ys_ref.at[pl.ds(0, ROUTE_TB)],
                              sems.at[cur]).wait()

    def rows_of(k, c):
        return buf[cur, k, :, c, :, :].reshape(ROUTE_TB, 128)

    w = w_ref[...]
    sum_sq = jnp.zeros((ROUTE_TB, 1), F32)
    for c in range(LANE_TILES):
        cols = slice(c * 128, (c + 1) * 128)
        mix = w[:, 0:1] * rows_of(0, c)
        for k in range(1, TOP_K):
            mix = mix + w[:, k:k + 1] * rows_of(k, c)
        piece = x_ref[:, cols] + mod_ref[5:6, cols] * mix
        sum_sq = sum_sq + jnp.sum(piece * piece, axis=-1, keepdims=True)
        acc[:, cols] = piece

    if not final_norm:
        out_refs[0][...] = acc[...]
        return
    normed = acc[...] * lax.rsqrt(sum_sq * (1.0 / D_MODEL) + EPS) * fg_ref[...]

    @pl.when(i < T_CTX // ROUTE_TB)
    def _():
        out_refs[0][...] = normed

    @pl.when(i >= T_CTX // ROUTE_TB)
    def _():
        out_refs[1][...] = normed


def _combine(slot, top_w, x, mod_l, final_g, ys, final_norm):
    tb = ROUTE_TB
    nb = T_ALL // tb
    nc = T_CTX // tb
    row = functools.partial(_cond_row, blocks_ctx=nc, blocks_per_latent=DEC_SEQ // tb)
    tok = lambda w: pl.BlockSpec((tb, w), lambda i: (i, 0))
    slots = lambda ahead: pl.BlockSpec((None, 1, tb * TOP_K),
                                       lambda i: (jnp.minimum(i + ahead, nb - 1), 0, 0),
                                       memory_space=pltpu.SMEM)
    if final_norm:
        out_specs = [pl.BlockSpec((tb, D_MODEL), lambda i: (jnp.minimum(i, nc - 1), 0)),
                     pl.BlockSpec((tb, D_MODEL), lambda i: (jnp.maximum(i - nc, 0), 0))]
        out_shape = [jax.ShapeDtypeStruct((T_CTX, D_MODEL), F32),
                     jax.ShapeDtypeStruct((T_LAT, D_MODEL), F32)]
    else:
        out_specs = [tok(D_MODEL)]
        out_shape = [jax.ShapeDtypeStruct((T_ALL, D_MODEL), F32)]
    return pl.pallas_call(
        functools.partial(_combine_kernel, final_norm=final_norm),
        grid=(nb,),
        in_specs=[
            slots(0), slots(1), tok(TOP_K), tok(D_MODEL),
            pl.BlockSpec((None, 6, D_MODEL), lambda i: (row(i), 0, 0)),
            pl.BlockSpec((1, D_MODEL), lambda i: (0, 0)),
            pl.BlockSpec(memory_space=pl.ANY),
        ],
        out_specs=out_specs,
        out_shape=out_shape,
        scratch_shapes=[pltpu.VMEM((tb, D_MODEL), F32),
                        pltpu.VMEM((2, TOP_K, tb // ROW_GROUP, LANE_TILES, ROW_GROUP, 128), F32),
                        pltpu.SemaphoreType.DMA((2,))],
        compiler_params=_params("arbitrary"),
        name="moe_combine",
    )(slot, slot, top_w, x, mod_l, final_g.reshape(1, D_MODEL), ys.reshape(N_SLOTS, LANE_TILES, 128))


def _block_diag(blocks):
    g, a, b = blocks.shape
    eye = jnp.eye(g, dtype=blocks.dtype)
    return (eye[:, None, :, None] * blocks[:, :, None, :]).reshape(g * a, g * b)


def _dft_tables(n):
    k = np.arange(n, dtype=np.int64)
    ang = 2.0 * np.pi * ((k[:, None] * k[None, :]) % n).astype(np.float64) / n
    return np.cos(ang) / math.sqrt(n), np.sin(ang) / math.sqrt(n)


def _seq_constants(seq):
    t = np.arange(seq)[:, None]
    win = np.array(POOL_WINDOWS)[None, :]
    lo = np.clip(t - win // 2, 0, seq)
    hi = np.clip(t - win // 2 + win, 0, seq)
    invcnt = np.repeat(1.0 / (hi - lo).astype(np.float64), POOL_CH, axis=1)
    cos_p, sin_p = _dft_tables(seq)
    cos_c, sin_c = _dft_tables(FFT_CH)
    eye = np.eye(POOL_GROUPS)
    as_bf16 = lambda a: jnp.asarray(a, F32).astype(BF16)
    return {
        "invcnt": jnp.asarray(invcnt, F32),
        "f_pos": as_bf16(np.concatenate([cos_p, -sin_p], axis=1)),
        "f_cos_ch": as_bf16(np.kron(eye, cos_c)),
        "f_sin_ch": as_bf16(np.kron(eye, sin_c)),
        "ones96": as_bf16(np.kron(eye, np.ones((SGU_CH, SGU_CH)))),
    }


def _rope_constants():
    rows = DEC_SEQ // GRID_W
    row = jnp.repeat(jnp.arange(rows, dtype=F32), GRID_W)
    col = jnp.tile(jnp.arange(GRID_W, dtype=F32), rows)
    freqs = ROPE_THETA ** (-jnp.arange(ROPE_PAIRS, dtype=F32) / ROPE_PAIRS)
    ang = jnp.stack([row[:, None] * freqs, col[:, None] * freqs], axis=1)
    ang = jnp.repeat(ang.reshape(DEC_SEQ, 2 * ROPE_PAIRS), 2, axis=1)
    ang = jnp.tile(ang, (1, 128 // HEAD_DIM))
    even = (jnp.arange(128) % 2 == 0)[None, :]
    sin = jnp.sin(ang)
    return {
        "rope_cos": jnp.cos(ang),
        "rope_sin_next": jnp.where(even, -sin, 0.0),
        "rope_sin_prev": jnp.where(even, 0.0, sin),
    }


def _layer_weights(p, l):
    group_of_lane = np.arange(SGU_WIDTH) // SGU_CH
    return {
        "pool_w_bd": _block_diag(p["pool_w"][l]).astype(BF16),
        "pool_scale": p["pool_scale"][l].reshape(1, POOL_WIDTH),
        "sgu_w_stack": p["sgu_w"][l].reshape(SGU_GROUPS * SGU_CHUNK, SGU_CHUNK).astype(BF16),
        "sgu_bias": p["sgu_b"][l].T[:, group_of_lane],
        "gq": jnp.tile(p["q_norm_g"][l], N_HEADS).reshape(1, ATTN_WIDTH),
        "gk": jnp.tile(p["k_norm_g"][l], N_KV_HEADS).reshape(1, KV_WIDTH),
        "w_br_pool": p["w_br_pool"][l].astype(BF16),
        "w_br_fourier": p["w_br_fourier"][l].astype(BF16),
        "w_br_attn": p["w_br_attn"][l].astype(BF16),
        "w_br_sgu": p["w_br_sgu"][l].astype(BF16),
        "w_out": p["w_out"][l].astype(BF16),
        "router_w": p["router_w"][l].T,
        "router_b": p["router_b"][l].reshape(N_EXPERTS, 1),
    }


def kernel(x_prompt, x_sample, cache_k, cache_v, c, c_ctx, w_mod, b_mod, norm_mix_g, norm_ffn_g, w_in, pool_w, pool_scale, q_norm_g, k_norm_g, sgu_w, sgu_b, w_br_pool, w_br_fourier, w_br_attn, w_br_sgu, w_out, router_w, router_b, moe_w_gate, moe_b_gate, moe_w_up, moe_b_up, moe_w_down, moe_b_down, final_norm_g):
    p = dict(w_in=w_in, pool_w=pool_w, pool_scale=pool_scale, q_norm_g=q_norm_g, k_norm_g=k_norm_g,
             sgu_w=sgu_w, sgu_b=sgu_b, w_br_pool=w_br_pool, w_br_fourier=w_br_fourier,
             w_br_attn=w_br_attn, w_br_sgu=w_br_sgu, w_out=w_out, router_w=router_w,
             router_b=router_b, moe_w_gate=moe_w_gate, moe_b_gate=moe_b_gate, moe_w_up=moe_w_up,
             moe_b_up=moe_b_up, moe_w_down=moe_w_down, moe_b_down=moe_b_down)

    cond = jnp.concatenate([c_ctx[None, :], c, jnp.zeros((N_COND - 1 - DEC_BATCH, D_MODEL), F32)])
    mod = _modulation(cond, w_mod, b_mod).reshape(DEPTH, N_COND, 6, D_MODEL)

    ones64 = jnp.asarray(np.kron(np.eye(N_HEADS), np.ones((HEAD_DIM, HEAD_DIM))), BF16)
    consts_ctx = dict(_seq_constants(SEQ), ones64=ones64)
    consts_lat = dict(_seq_constants(DEC_SEQ), ones64=ones64, **_rope_constants())
    ck = cache_k.reshape(DEC_BATCH, DEPTH, PAST_LEN, KV_WIDTH)
    cv = cache_v.reshape(DEC_BATCH, DEPTH, PAST_LEN, KV_WIDTH)

    x_parts = (x_prompt.reshape(T_CTX, D_MODEL), x_sample.reshape(T_LAT, D_MODEL))
    new_k, new_v = [], []
    for l in range(DEPTH):
        lw = _layer_weights(p, l)
        xp, xq, qkv, uv, gates = _in_projection(x_parts, mod[l], norm_mix_g[l], w_in, l)
        pool_c, four_c, sgu_c = _mixers(xp, xq, uv, consts_ctx, lw, SEQ, BATCH, 0)
        pool_l, four_l, sgu_l = _mixers(xp, xq, uv, consts_lat, lw, DEC_SEQ, DEC_BATCH,
                                        T_CTX // DEC_SEQ)
        attn_c, k_c, v_c = _attention_ctx(qkv, consts_ctx, lw)
        attn_l = _attention_lat(qkv, ck[:, l], cv[:, l], consts_lat, lw)
        new_k.append(k_c.reshape(BATCH, SEQ, N_KV_HEADS, HEAD_DIM))
        new_v.append(v_c.reshape(BATCH, SEQ, N_KV_HEADS, HEAD_DIM))
        x, h, top_idx, top_w = _merge((pool_c, pool_l), (four_c, four_l), (attn_c, attn_l),
                                      (sgu_c, sgu_l), gates, x_parts, mod[l], norm_ffn_g[l], lw)
        slot, tiles = _route(top_idx)
        ys = _routed_ffn(tiles, _dispatch(tiles["rows"], slot, h), l, p)
        x_parts = tuple(_combine(slot, top_w.T, x, mod[l], final_norm_g, ys,
                                 final_norm=(l == DEPTH - 1)))

    y_prompt = x_parts[0].reshape(BATCH, SEQ, D_MODEL)
    y_sample = x_parts[1].reshape(DEC_BATCH, DEC_SEQ, D_MODEL)
    return (y_prompt, y_sample, jnp.stack(new_k, axis=1), jnp.stack(new_v, axis=1))
```

```python
import functools
import math

import numpy as np
import jax
import jax.numpy as jnp
from jax import lax
from jax.experimental import pallas as pl
from jax.experimental.pallas import tpu as pltpu

F32 = jnp.float32
BF16 = jnp.bfloat16

D_MODEL = 1024
BATCH = 32
SEQ = 256
DEPTH = 2
DEC_BATCH = 2
DEC_SEQ = 1024
PAST_LEN = 256
GRID_W = 64
EPS = 1e-6
POOL_GROUPS = 4
POOL_CH = 96
POOL_WIDTH = 384
POOL_WINDOWS = (2, 4, 8, 16)
POOL_PAD = 16
FFT_CH = 96
FFT_WIDTH = 384
N_HEADS = 8
N_KV_HEADS = 2
HEAD_DIM = 64
ATTN_WIDTH = 512
KV_WIDTH = 128
QKV_WIDTH = ATTN_WIDTH + 2 * KV_WIDTH
ROPE_THETA = 10000.0
ROPE_PAIRS = 16
SGU_GROUPS = 4
SGU_CH = 96
SGU_WIDTH = 384
SGU_CHUNK = 128
N_BRANCHES = 4
GATE_WIDTH = N_BRANCHES * D_MODEL
IN_COLS = POOL_WIDTH + FFT_WIDTH + QKV_WIDTH + 2 * SGU_WIDTH + GATE_WIDTH
N_EXPERTS = 32
TOP_K = 4
D_FF = 1024
SWIGLU_LIMIT = 7.0
SWIGLU_ALPHA = 1.702

ROW_GROUP = 8
LANE_TILES = D_MODEL // 128

T_CTX = BATCH * SEQ
T_LAT = DEC_BATCH * DEC_SEQ
T_ALL = T_CTX + T_LAT
N_COND = 8
VMEM_LIMIT = 56 * 1024 * 1024
CAST_ROWS = 32

COL_XP = 0
COL_XQ = COL_XP + POOL_WIDTH
COL_QKV = COL_XQ + FFT_WIDTH
COL_UV = COL_QKV + QKV_WIDTH
COL_GATE = COL_UV + 2 * SGU_WIDTH


def _params(*sem):
    return pltpu.CompilerParams(dimension_semantics=sem, vmem_limit_bytes=VMEM_LIMIT)


def _split_bf16(x):
    hi = x.astype(BF16)
    lo = (x - hi.astype(F32)).astype(BF16)
    return hi, lo


def _dot(a, b):
    return jnp.dot(a, b, preferred_element_type=F32)


def _dot_nt(a, b):
    return lax.dot_general(a, b, (((1,), (1,)), ((), ())), preferred_element_type=F32)


def _group_lane_select(lane, vals, width):
    out = vals[-1]
    for g in range(len(vals) - 2, -1, -1):
        out = jnp.where(lane < (g + 1) * width, vals[g], out)
    return out


def _cond_row(blk, blocks_ctx, blocks_per_latent):
    return jnp.where(blk < blocks_ctx, 0, 1 + (blk - blocks_ctx) // blocks_per_latent)


MOD_TN = 1536


def _mod_kernel(c_ref, w_ref, b_ref, o_ref):
    c = c_ref[...]
    s = c * jax.nn.sigmoid(c)
    sh, sl = _split_bf16(s)
    wh, wl = _split_bf16(w_ref[...])
    o_ref[...] = _dot(sh, wh) + _dot(sh, wl) + _dot(sl, wh) + b_ref[...]


def _modulation(cond, w_mod, b_mod):
    n_cols = 6 * D_MODEL
    return pl.pallas_call(
        _mod_kernel,
        grid=(DEPTH, n_cols // MOD_TN),
        in_specs=[
            pl.BlockSpec((N_COND, D_MODEL), lambda l, j: (0, 0)),
            pl.BlockSpec((None, D_MODEL, MOD_TN), lambda l, j: (l, 0, j)),
            pl.BlockSpec((None, 1, MOD_TN), lambda l, j: (l, 0, j)),
        ],
        out_specs=pl.BlockSpec((None, N_COND, MOD_TN), lambda l, j: (l, 0, j)),
        out_shape=jax.ShapeDtypeStruct((DEPTH, N_COND, n_cols), F32),
        compiler_params=_params("arbitrary", "arbitrary"),
        name="modulation",
    )(cond, w_mod, b_mod.reshape(DEPTH, 1, n_cols))


INPROJ_TM = 256
INPROJ_SEGMENTS = (
    (COL_XP, POOL_WIDTH), (COL_XQ, FFT_WIDTH), (COL_QKV, QKV_WIDTH),
    (COL_UV, 2 * SGU_WIDTH), (COL_GATE, GATE_WIDTH))
INPROJ_CHUNK = 1024


def _ada_norm(x, g, shift, scale):
    xn = x * lax.rsqrt(jnp.mean(x * x, axis=-1, keepdims=True) + EPS)
    return xn * g * (1.0 + scale) + shift


def _token_specs(parts, tm, width):
    if len(parts) == 1:
        return [pl.BlockSpec((tm, width), lambda i: (i, 0))]
    nc = T_CTX // tm
    return [pl.BlockSpec((tm, width), lambda i: (jnp.minimum(i, nc - 1), 0)),
            pl.BlockSpec((tm, width), lambda i: (jnp.maximum(i - nc, 0), 0))]


def _token_load(refs, tm):
    if len(refs) == 1:
        return refs[0][...]
    return jnp.where(pl.program_id(0) < T_CTX // tm, refs[0][...], refs[1][...])


W_IN_CHUNK = 640


def _inproj_kernel(*refs, n_x, layer):
    x_refs, (mod_ref, g_ref, w_hbm) = refs[:n_x], refs[n_x:n_x + 3]
    out_refs, (w_ref, stage, sems) = refs[n_x + 3:-3], refs[-3:]

    @pl.when(pl.program_id(0) == 0)
    def _():
        def chunk(c):
            return pltpu.make_async_copy(w_hbm.at[layer, :, pl.ds(c * W_IN_CHUNK, W_IN_CHUNK)],
                                         stage.at[c % 2], sems.at[c % 2])

        n_chunks = IN_COLS // W_IN_CHUNK
        chunk(0).start()
        for c in range(n_chunks):
            if c + 1 < n_chunks:
                chunk(c + 1).start()
            chunk(c).wait()

            def cast_rows(r, carry, c=c):
                rows = pl.ds(pl.multiple_of(r * CAST_ROWS, CAST_ROWS), CAST_ROWS)
                w_ref[rows, c * W_IN_CHUNK:(c + 1) * W_IN_CHUNK] = stage[c % 2, rows, :].astype(BF16)
                return carry

            lax.fori_loop(0, D_MODEL // CAST_ROWS, cast_rows, 0)

    h = _ada_norm(_token_load(x_refs, INPROJ_TM), g_ref[...], mod_ref[0:1, :], mod_ref[1:2, :])
    hb = h.astype(BF16)
    for (col, width), o_ref in zip(INPROJ_SEGMENTS, out_refs):
        for c0 in range(0, width, INPROJ_CHUNK):
            c1 = min(c0 + INPROJ_CHUNK, width)
            proj = _dot(hb, w_ref[:, col + c0:col + c1])
            o_ref[:, c0:c1] = jax.nn.sigmoid(proj) if col == COL_GATE else proj


def _in_projection(x_parts, mod_l, g, w_in, l):
    tm = INPROJ_TM
    row = functools.partial(_cond_row, blocks_ctx=T_CTX // tm, blocks_per_latent=DEC_SEQ // tm)
    return pl.pallas_call(
        functools.partial(_inproj_kernel, n_x=len(x_parts), layer=l),
        grid=(T_ALL // tm,),
        in_specs=_token_specs(x_parts, tm, D_MODEL) + [
            pl.BlockSpec((None, 6, D_MODEL), lambda i: (row(i), 0, 0)),
            pl.BlockSpec((1, D_MODEL), lambda i: (0, 0)),
            pl.BlockSpec(memory_space=pl.ANY),
        ],
        out_specs=[pl.BlockSpec((tm, w), lambda i: (i, 0)) for _, w in INPROJ_SEGMENTS],
        out_shape=[jax.ShapeDtypeStruct((T_ALL, w), F32) for _, w in INPROJ_SEGMENTS],
        scratch_shapes=[pltpu.VMEM((D_MODEL, IN_COLS), BF16), pltpu.VMEM((2, D_MODEL, W_IN_CHUNK), F32),
                        pltpu.SemaphoreType.DMA((2,))],
        compiler_params=_params("arbitrary"),
        name="in_projection",
    )(*x_parts, mod_l, g.reshape(1, D_MODEL), w_in)


CTX_SEQS_PER_STEP = 2


def _pool_mixer(xp, invcnt, w_bd, scale):
    s = xp.shape[0]
    n = s + 2 * POOL_PAD
    zeros = jnp.zeros((POOL_PAD, POOL_WIDTH), F32)
    xe = jnp.concatenate([zeros, xp, zeros], axis=0)

    def shift(a, k):
        return pltpu.roll(a, k % n, 0)

    s2 = xe + shift(xe, 1)
    s4 = shift(s2, 1) + shift(s2, -1)
    s8 = shift(s4, 2) + shift(s4, -2)
    s16 = shift(s8, 4) + shift(s8, -4)
    lane = lax.broadcasted_iota(jnp.int32, (1, POOL_WIDTH), 1)
    total = _group_lane_select(lane, [s2, s4, s8, s16], POOL_CH)[POOL_PAD:POOL_PAD + s]
    pooled = total * invcnt - xp
    return _dot(pooled.astype(BF16), w_bd) * scale


def _fourier_mixer(xq, f_cos_ch, f_sin_ch, f_pos):
    xb = xq.astype(BF16)
    a = _dot(xb, f_cos_ch).astype(BF16)
    b = _dot(xb, f_sin_ch).astype(BF16)
    return _dot(f_pos, jnp.concatenate([a, b], axis=0))


def _group_mean_sq(x, ones_bd, width):
    hi, lo = _split_bf16(x * x)
    return (_dot(hi, ones_bd) + _dot(lo, ones_bd)) * (1.0 / width)


def _mixers_kernel(xp_ref, xq_ref, uv_ref, invcnt_ref, pool_w_ref, pool_s_ref, fcc_ref, fsc_ref,
                   fpos_ref, ones_ref, sgu_w_ref, sgu_b_ref, pool_o, four_o, sgu_o, *, seq):
    lane = lax.broadcasted_iota(jnp.int32, (1, SGU_WIDTH), 1)
    w_stack = sgu_w_ref[...]
    bias = sgu_b_ref[...]
    for b in range(xp_ref.shape[0] // seq):
        rows = slice(b * seq, (b + 1) * seq)
        pool_o[rows, :] = _pool_mixer(xp_ref[rows, :], invcnt_ref[...], pool_w_ref[...],
                                      pool_s_ref[...]).astype(BF16)
        four_o[rows, :] = _fourier_mixer(xq_ref[rows, :], fcc_ref[...], fsc_ref[...],
                                         fpos_ref[...]).astype(BF16)

        act = jax.nn.gelu(uv_ref[rows, :], approximate=True)
        u = act[:, :SGU_WIDTH]
        v = act[:, SGU_WIDTH:]
        vg = (v * lax.rsqrt(_group_mean_sq(v, ones_ref[...], SGU_CH) + EPS)).astype(BF16)
        for n in range(seq // SGU_CHUNK):
            chunk = slice(n * SGU_CHUNK, (n + 1) * SGU_CHUNK)
            r = _dot(w_stack, vg[chunk])
            per_group = [r[g * SGU_CHUNK:(g + 1) * SGU_CHUNK] for g in range(SGU_GROUPS)]
            spatial = _group_lane_select(lane, per_group, SGU_CH) + bias
            out_rows = slice(b * seq + n * SGU_CHUNK, b * seq + (n + 1) * SGU_CHUNK)
            sgu_o[out_rows, :] = (u[chunk] * spatial).astype(BF16)


def _mixers(xp, xq, uv, consts, lw, seq, n_seq, block0, per_step):
    rows = per_step * seq
    full = lambda shape: pl.BlockSpec(shape, lambda b: (0,) * len(shape))
    tok = lambda w: pl.BlockSpec((rows, w), lambda b: (block0 + b, 0))
    out = lambda: pl.BlockSpec((rows, POOL_WIDTH), lambda b: (b, 0))
    return pl.pallas_call(
        functools.partial(_mixers_kernel, seq=seq),
        grid=(n_seq // per_step,),
        in_specs=[
            tok(POOL_WIDTH), tok(FFT_WIDTH), tok(2 * SGU_WIDTH),
            full((seq, POOL_WIDTH)), full((POOL_WIDTH, POOL_WIDTH)), full((1, POOL_WIDTH)),
            full((FFT_WIDTH, FFT_WIDTH)), full((FFT_WIDTH, FFT_WIDTH)), full((seq, 2 * seq)),
            full((SGU_WIDTH, SGU_WIDTH)), full((SGU_GROUPS * SGU_CHUNK, SGU_CHUNK)),
            full((SGU_CHUNK, SGU_WIDTH)),
        ],
        out_specs=[out(), out(), out()],
        out_shape=[jax.ShapeDtypeStruct((n_seq * seq, POOL_WIDTH), BF16)] * 3,
        compiler_params=_params("arbitrary"),
        name=f"mixers_s{seq}",
    )(xp, xq, uv, consts["invcnt"], lw["pool_w_bd"], lw["pool_scale"], consts["f_cos_ch"],
      consts["f_sin_ch"], consts["f_pos"], consts["ones96"], lw["sgu_w_stack"], lw["sgu_bias"])


def _head_norm(x, ones_bd, g):
    return x * lax.rsqrt(_group_mean_sq(x, ones_bd, HEAD_DIM) + EPS) * g


def _rope(x, cos, sin_next, sin_prev):
    cols = []
    for c in range(x.shape[1] // 128):
        xc = x[:, c * 128:(c + 1) * 128]
        nxt = pltpu.roll(xc, 127, 1)
        prv = pltpu.roll(xc, 1, 1)
        cols.append(xc * cos + nxt * sin_next + prv * sin_prev)
    return cols[0] if len(cols) == 1 else jnp.concatenate(cols, axis=1)


def _attend(q, keys, vals, o_ref):
    qb = (q * (HEAD_DIM ** -0.5)).astype(BF16)
    group = N_HEADS // N_KV_HEADS
    for h in range(N_HEADS):
        j = h // group
        kh = keys[:, j * HEAD_DIM:(j + 1) * HEAD_DIM]
        vh = vals[:, j * HEAD_DIM:(j + 1) * HEAD_DIM]
        s = _dot_nt(qb[:, h * HEAD_DIM:(h + 1) * HEAD_DIM], kh)
        p = jnp.exp(s - jnp.max(s, axis=-1, keepdims=True))
        denom = jnp.sum(p, axis=-1, keepdims=True)
        o = _dot(p.astype(BF16), vh) / denom
        o_ref[:, h * HEAD_DIM:(h + 1) * HEAD_DIM] = o.astype(BF16)


def _attn_ctx_kernel(qkv_ref, gq_ref, gk_ref, ones_ref, o_ref, k_ref, v_ref):
    qkv = qkv_ref[...]
    ones = ones_ref[...]
    q = _head_norm(qkv[:, :ATTN_WIDTH], ones, gq_ref[...])
    k = _head_norm(qkv[:, ATTN_WIDTH:ATTN_WIDTH + KV_WIDTH], ones[:KV_WIDTH, :KV_WIDTH], gk_ref[...])
    v = qkv[:, ATTN_WIDTH + KV_WIDTH:]
    k_ref[...] = k
    v_ref[...] = v
    _attend(q, k.astype(BF16), v.astype(BF16), o_ref)


def _attention_ctx(qkv, consts, lw):
    full = lambda shape: pl.BlockSpec(shape, lambda b: (0,) * len(shape))
    return pl.pallas_call(
        _attn_ctx_kernel,
        grid=(BATCH,),
        in_specs=[pl.BlockSpec((SEQ, QKV_WIDTH), lambda b: (b, 0)),
                  full((1, ATTN_WIDTH)), full((1, KV_WIDTH)), full((ATTN_WIDTH, ATTN_WIDTH))],
        out_specs=[pl.BlockSpec((SEQ, ATTN_WIDTH), lambda b: (b, 0)),
                   pl.BlockSpec((SEQ, KV_WIDTH), lambda b: (b, 0)),
                   pl.BlockSpec((SEQ, KV_WIDTH), lambda b: (b, 0))],
        out_shape=[jax.ShapeDtypeStruct((T_CTX, ATTN_WIDTH), BF16),
                   jax.ShapeDtypeStruct((T_CTX, KV_WIDTH), F32),
                   jax.ShapeDtypeStruct((T_CTX, KV_WIDTH), F32)],
        compiler_params=_params("arbitrary"),
        name="attention_ctx",
    )(qkv, lw["gq"], lw["gk"], consts["ones64"])


LAT_QBLK = 256


def _attn_lat_kernel(q_ref, kv_ref, ck_ref, cv_ref, gq_ref, gk_ref, ones_ref, cos_q, sn_q, sp_q,
                     cos_k, sn_k, sp_k, o_ref, keys, vals):
    ones = ones_ref[...]

    @pl.when(pl.program_id(1) == 0)
    def _():
        kv = kv_ref[...]
        k = _head_norm(kv[:, :KV_WIDTH], ones[:KV_WIDTH, :KV_WIDTH], gk_ref[...])
        keys[0:DEC_SEQ, :] = _rope(k, cos_k[...], sn_k[...], sp_k[...]).astype(BF16)
        keys[DEC_SEQ:, :] = ck_ref[...].astype(BF16)
        vals[0:DEC_SEQ, :] = kv[:, KV_WIDTH:].astype(BF16)
        vals[DEC_SEQ:, :] = cv_ref[...].astype(BF16)

    q = _head_norm(q_ref[...], ones, gq_ref[...])
    q = _rope(q, cos_q[...], sn_q[...], sp_q[...])
    _attend(q, keys[...], vals[...], o_ref)


def _attention_lat(qkv, cache_k_l, cache_v_l, consts, lw):
    nq = DEC_SEQ // LAT_QBLK
    q0 = T_CTX // LAT_QBLK
    s0 = T_CTX // DEC_SEQ
    full = lambda shape: pl.BlockSpec(shape, lambda b, j: (0,) * len(shape))
    rope_q = lambda: pl.BlockSpec((LAT_QBLK, 128), lambda b, j: (j, 0))
    rope_k = lambda: pl.BlockSpec((DEC_SEQ, 128), lambda b, j: (0, 0))
    return pl.pallas_call(
        _attn_lat_kernel,
        grid=(DEC_BATCH, nq),
        in_specs=[
            pl.BlockSpec((LAT_QBLK, ATTN_WIDTH), lambda b, j: (q0 + b * nq + j, 0)),
            pl.BlockSpec((DEC_SEQ, 2 * KV_WIDTH), lambda b, j: (s0 + b, ATTN_WIDTH // (2 * KV_WIDTH))),
            pl.BlockSpec((None, PAST_LEN, KV_WIDTH), lambda b, j: (b, 0, 0)),
            pl.BlockSpec((None, PAST_LEN, KV_WIDTH), lambda b, j: (b, 0, 0)),
            full((1, ATTN_WIDTH)), full((1, KV_WIDTH)), full((ATTN_WIDTH, ATTN_WIDTH)),
            rope_q(), rope_q(), rope_q(), rope_k(), rope_k(), rope_k(),
        ],
        out_specs=pl.BlockSpec((LAT_QBLK, ATTN_WIDTH), lambda b, j: (b * nq + j, 0)),
        out_shape=jax.ShapeDtypeStruct((T_LAT, ATTN_WIDTH), BF16),
        scratch_shapes=[pltpu.VMEM((DEC_SEQ + PAST_LEN, KV_WIDTH), BF16),
                        pltpu.VMEM((DEC_SEQ + PAST_LEN, KV_WIDTH), BF16)],
        compiler_params=_params("arbitrary", "arbitrary"),
        name="attention_lat",
    )(qkv, qkv, cache_k_l, cache_v_l, lw["gq"], lw["gk"], consts["ones64"],
      consts["rope_cos"], consts["rope_sin_next"], consts["rope_sin_prev"],
      consts["rope_cos"], consts["rope_sin_next"], consts["rope_sin_prev"])


MERGE_TM = 512


def _merge_kernel(*refs, n_x):
    branch_refs, gate_ref, x_refs = refs[:8], refs[8], refs[9:9 + n_x]
    (mod_ref, g_ref, wp_ref, wf_ref, wa_ref, ws_ref, wo_ref, rw_ref, rb_ref,
     x_o, h_o, idx_o, wgt_o) = refs[9 + n_x:]
    merged = None
    for i, w_ref in enumerate((wp_ref, wf_ref, wa_ref, ws_ref)):
        br = _dot(_token_load(branch_refs[2 * i:2 * i + 2], MERGE_TM), w_ref[...])
        term = gate_ref[:, i * D_MODEL:(i + 1) * D_MODEL] * br
        merged = term if merged is None else merged + term
    mix = _dot(merged.astype(BF16), wo_ref[...])
    x = _token_load(x_refs, MERGE_TM) + mod_ref[2:3, :] * mix
    x_o[...] = x
    h = _ada_norm(x, g_ref[...], mod_ref[3:4, :], mod_ref[4:5, :])
    for c in range(LANE_TILES):
        h_o[:, c, :, :] = h[:, c * 128:(c + 1) * 128].reshape(MERGE_TM // ROW_GROUP, ROW_GROUP, 128)

    hh, hl = _split_bf16(h)
    rh, rl = _split_bf16(rw_ref[...])
    logits = _dot_nt(rh, hh) + _dot_nt(rl, hh) + _dot_nt(rh, hl) + rb_ref[...]
    expert = lax.broadcasted_iota(jnp.int32, logits.shape, 0).astype(F32)
    work = logits
    top = jnp.max(logits, axis=0, keepdims=True)
    idx, wgt = [], []
    denom = jnp.zeros_like(top)
    for _ in range(TOP_K):
        m = jnp.max(work, axis=0, keepdims=True)
        first = jnp.min(jnp.where(work == m, expert, float(N_EXPERTS)), axis=0, keepdims=True)
        e = jnp.exp(m - top)
        idx.append(first)
        wgt.append(e)
        denom = denom + e
        work = jnp.where(expert == first, -jnp.inf, work)
    idx_o[...] = jnp.concatenate(idx, axis=0).astype(jnp.int32)
    wgt_o[...] = jnp.concatenate(wgt, axis=0) / denom


def _merge(pool, four, attn, sgu, gates, x_parts, mod_l, g_ffn, lw):
    tm = MERGE_TM
    row = functools.partial(_cond_row, blocks_ctx=T_CTX // tm, blocks_per_latent=DEC_SEQ // tm)
    full = lambda shape: pl.BlockSpec(shape, lambda i: (0,) * len(shape))
    tok = lambda w: pl.BlockSpec((tm, w), lambda i: (i, 0))
    branch_specs = []
    for pair, width in ((pool, POOL_WIDTH), (four, FFT_WIDTH), (attn, ATTN_WIDTH), (sgu, SGU_WIDTH)):
        branch_specs += _token_specs(pair, tm, width)
    return pl.pallas_call(
        functools.partial(_merge_kernel, n_x=len(x_parts)),
        grid=(T_ALL // tm,),
        in_specs=branch_specs + [tok(GATE_WIDTH)] + _token_specs(x_parts, tm, D_MODEL) + [
            pl.BlockSpec((None, 6, D_MODEL), lambda i: (row(i), 0, 0)),
            full((1, D_MODEL)),
            full((POOL_WIDTH, D_MODEL)), full((FFT_WIDTH, D_MODEL)), full((ATTN_WIDTH, D_MODEL)),
            full((SGU_WIDTH, D_MODEL)), full((D_MODEL, D_MODEL)),
            full((N_EXPERTS, D_MODEL)), full((N_EXPERTS, 1)),
        ],
        out_specs=[tok(D_MODEL),
                   pl.BlockSpec((tm // ROW_GROUP, LANE_TILES, ROW_GROUP, 128), lambda i: (i, 0, 0, 0)),
                   pl.BlockSpec((TOP_K, tm), lambda i: (0, i)), pl.BlockSpec((TOP_K, tm), lambda i: (0, i))],
        out_shape=[jax.ShapeDtypeStruct((T_ALL, D_MODEL), F32),
                   jax.ShapeDtypeStruct((T_ALL // ROW_GROUP, LANE_TILES, ROW_GROUP, 128), F32),
                   jax.ShapeDtypeStruct((TOP_K, T_ALL), jnp.int32),
                   jax.ShapeDtypeStruct((TOP_K, T_ALL), F32)],
        compiler_params=_params("arbitrary"),
        name="merge_router",
    )(*pool, *four, *attn, *sgu, gates, *x_parts, mod_l, g_ffn.reshape(1, D_MODEL),
      lw["w_br_pool"], lw["w_br_fourier"], lw["w_br_attn"], lw["w_br_sgu"], lw["w_out"],
      lw["router_w"], lw["router_b"])


N_PAIRS = T_ALL * TOP_K
FFN_TM = 512
FFN_ROW_OPTIONS = (128, 256, 384, 512)
N_ROW_TILES = N_PAIRS // FFN_TM + N_EXPERTS
N_SLOTS = N_ROW_TILES * FFN_TM
ROUTE_TB = 512


def _route(top_idx):
    experts = jnp.arange(N_EXPERTS, dtype=jnp.int32)
    onehot = (top_idx.T[:, :, None] == experts[None, None, :]).astype(jnp.int32)
    per_token = jnp.sum(onehot, axis=1)
    csum = jnp.cumsum(per_token, axis=0)
    counts = csum[-1]
    tiles = (counts + FFN_TM - 1) // FFN_TM
    tile_end = jnp.cumsum(tiles)
    row_start = (tile_end - tiles) * FFN_TM
    first_row = csum - per_token + row_start[None, :]
    slot = jnp.sum(onehot * first_row[:, None, :], axis=2)
    tile_ids = jnp.arange(N_ROW_TILES, dtype=jnp.int32)
    tile_expert = jnp.sum((tile_end[None, :] <= tile_ids[:, None]).astype(jnp.int32), axis=1)
    tile_expert = jnp.minimum(tile_expert, N_EXPERTS - 1)
    rows_before = (tile_ids - (tile_end - tiles)[tile_expert]) * FFN_TM
    tile_rows = jnp.where(tile_ids < tile_end[-1],
                          jnp.clip(counts[tile_expert] - rows_before, 0, FFN_TM), 0)
    slot = slot.astype(jnp.int32).reshape(T_ALL // ROUTE_TB, 1, ROUTE_TB * TOP_K)
    used = (tiles > 0).astype(jnp.int32)
    rank = jnp.cumsum(used) - 1
    group_expert = jnp.sum(jnp.where((rank[None, :] == experts[:, None]) & (used[None, :] > 0),
                                     experts[None, :], 0), axis=1)
    tiles_info = dict(expert=tile_expert.astype(jnp.int32), rows=tile_rows.astype(jnp.int32),
                      group=rank[tile_expert].astype(jnp.int32),
                      group_expert=group_expert.astype(jnp.int32),
                      n_groups=jnp.sum(used).reshape(1).astype(jnp.int32),
                      n_used=tile_end[-1:].astype(jnp.int32))
    return slot, tiles_info


def _start_pair_rows(copy):
    def body(g, carry):
        for u in range(ROW_GROUP):
            for k in range(TOP_K):
                copy(g, u, k).start(priority=k % 2)
        return carry

    lax.fori_loop(0, ROUTE_TB // ROW_GROUP, body, 0)


def _pair_slot(slot_ref, g, u, k):
    return slot_ref[0, (g * ROW_GROUP + u) * TOP_K + k]


def _dispatch_kernel(tr_ref, slot_ref, h_ref, xs_out, zeros, sem, zero_sem):
    @pl.when(pl.program_id(0) == 0)
    def _():
        zeros[...] = jnp.zeros_like(zeros)

        def for_each_unfilled_tile(fn):
            group = FFN_ROW_OPTIONS[0]

            def body(i, carry):
                for full_groups in range(FFN_TM // group):
                    first = full_groups * group

                    @pl.when(tr_ref[i] // group == full_groups)
                    def _():
                        fn(pltpu.make_async_copy(zeros.at[pl.ds(0, FFN_TM - first)],
                                                 xs_out.at[pl.ds(i * FFN_TM + first, FFN_TM - first)],
                                                 zero_sem))
                return carry

            lax.fori_loop(0, N_ROW_TILES, body, 0)

        for_each_unfilled_tile(lambda c: c.start())
        for_each_unfilled_tile(lambda c: c.wait())

    def copy(g, u, k):
        return pltpu.make_async_copy(h_ref.at[g, :, u, :], xs_out.at[_pair_slot(slot_ref, g, u, k)], sem)

    _start_pair_rows(copy)
    for _ in range(TOP_K):
        pltpu.make_async_copy(xs_out.at[pl.ds(0, ROUTE_TB)], xs_out.at[pl.ds(0, ROUTE_TB)], sem).wait()


def _dispatch(tile_rows, slot, h):
    return pl.pallas_call(
        _dispatch_kernel,
        grid_spec=pltpu.PrefetchScalarGridSpec(
            num_scalar_prefetch=1,
            grid=(T_ALL // ROUTE_TB,),
            in_specs=[
                pl.BlockSpec((None, 1, ROUTE_TB * TOP_K), lambda i, tr: (i, 0, 0),
                             memory_space=pltpu.SMEM),
                pl.BlockSpec((ROUTE_TB // ROW_GROUP, LANE_TILES, ROW_GROUP, 128),
                             lambda i, tr: (i, 0, 0, 0)),
            ],
            out_specs=pl.BlockSpec(memory_space=pl.ANY),
            scratch_shapes=[pltpu.VMEM((FFN_TM, LANE_TILES, 128), F32), pltpu.SemaphoreType.DMA(()),
                            pltpu.SemaphoreType.DMA(())],
        ),
        out_shape=jax.ShapeDtypeStruct((N_SLOTS, LANE_TILES, 128), F32),
        compiler_params=_params("arbitrary"),
        name="moe_dispatch",
    )(tile_rows, slot, h)


def _expert_ffn(xb, wg, bg, wu, bu, wd, bd):
    gate = jnp.minimum(_dot(xb, wg) + bg, SWIGLU_LIMIT)
    up = jnp.clip(_dot(xb, wu) + bu, -SWIGLU_LIMIT, SWIGLU_LIMIT)
    glu = gate * jax.nn.sigmoid(SWIGLU_ALPHA * gate)
    return _dot(((up + 1.0) * glu).astype(BF16), wd) + bd


def _ffn_kernel(te_ref, tr_ref, tg_ref, ge_ref, ng_ref, nu_ref, xs_ref, wg_hbm, bg_ref, wu_hbm,
                bu_ref, wd_hbm, bd_ref, ys_ref, stage, w_bf16, xb, sems, *, layer):
    i = pl.program_id(0)
    n_rows = tr_ref[i]
    group = tg_ref[i]
    slot = group % 2

    def weight_copies(g, s):
        e = ge_ref[g]
        return [pltpu.make_async_copy(w.at[layer, e], stage.at[s, j], sems.at[s])
                for j, w in enumerate((wg_hbm, wu_hbm, wd_hbm))]

    def lane_tile(ref, c, rows):
        return ref.at[pl.ds(c, rows, stride=LANE_TILES), :]

    def ffn(rows):
        for c in range(LANE_TILES):
            xb[:rows, c * 128:(c + 1) * 128] = lane_tile(xs_ref, c, rows)[...].astype(BF16)
        y = _expert_ffn(xb[:rows, :], w_bf16[0], bg_ref[...], w_bf16[1], bu_ref[...], w_bf16[2],
                        bd_ref[...])
        for c in range(LANE_TILES):
            lane_tile(ys_ref, c, rows)[...] = y[:, c * 128:(c + 1) * 128]
        if rows < FFN_TM:
            ys_ref[rows * LANE_TILES:, :] = jnp.zeros(((FFN_TM - rows) * LANE_TILES, 128), F32)

    @pl.when(i == 0)
    def _():
        for c in weight_copies(0, 0):
            c.start()

    @pl.when((n_rows > 0) & ((i == 0) | (te_ref[i] != te_ref[jnp.maximum(i - 1, 0)])))
    def _():
        for c in weight_copies(group, slot):
            c.wait()
        def cast_rows(r, carry):
            rows = pl.ds(pl.multiple_of(r * CAST_ROWS, CAST_ROWS), CAST_ROWS)
            for j in range(3):
                w_bf16[j, rows, :] = stage[slot, j, rows, :].astype(BF16)
            return carry

        lax.fori_loop(0, D_MODEL // CAST_ROWS, cast_rows, 0)

        @pl.when(group + 1 < ng_ref[0])
        def _():
            for c in weight_copies(group + 1, 1 - slot):
                c.start()

    @pl.when(n_rows == 0)
    def _():
        ys_ref[...] = jnp.zeros_like(ys_ref)

    lower = 0
    for rows in FFN_ROW_OPTIONS:
        @pl.when((n_rows > lower) & (n_rows <= rows))
        def _(rows=rows):
            ffn(rows)
        lower = rows


def _routed_ffn(tiles, xs, l, p):
    n_prefetch = 6
    rows = pl.BlockSpec((FFN_TM * LANE_TILES, 128),
                        lambda i, te, tr, tg, ge, ng, nu: (jnp.minimum(i, nu[0] - 1), 0))
    bias_spec = lambda w: pl.BlockSpec((None, None, 1, w),
                                       lambda i, te, tr, tg, ge, ng, nu: (l, te[i], 0, 0))
    bias = lambda a: a.reshape(DEPTH, N_EXPERTS, 1, a.shape[-1])
    hbm = pl.BlockSpec(memory_space=pl.ANY)
    return pl.pallas_call(
        functools.partial(_ffn_kernel, layer=l),
        grid_spec=pltpu.PrefetchScalarGridSpec(
            num_scalar_prefetch=n_prefetch,
            grid=(N_ROW_TILES,),
            in_specs=[rows, hbm, bias_spec(D_FF), hbm, bias_spec(D_FF), hbm, bias_spec(D_MODEL)],
            out_specs=pl.BlockSpec((FFN_TM * LANE_TILES, 128),
                                   lambda i, te, tr, tg, ge, ng, nu: (i, 0)),
            scratch_shapes=[pltpu.VMEM((2, 3, D_MODEL, D_FF), F32), pltpu.VMEM((3, D_MODEL, D_FF), BF16),
                            pltpu.VMEM((FFN_TM, D_MODEL), BF16), pltpu.SemaphoreType.DMA((2,))],
        ),
        out_shape=jax.ShapeDtypeStruct((N_SLOTS * LANE_TILES, 128), F32),
        compiler_params=_params("arbitrary"),
        name="moe_ffn",
    )(tiles["expert"], tiles["rows"], tiles["group"], tiles["group_expert"], tiles["n_groups"],
      tiles["n_used"], xs.reshape(N_SLOTS * LANE_TILES, 128), p["moe_w_gate"], bias(p["moe_b_gate"]),
      p["moe_w_up"], bias(p["moe_b_up"]), p["moe_w_down"], bias(p["moe_b_down"]))


def _combine_kernel(slot_ref, slot_next_ref, w_ref, x_ref, mod_ref, fg_ref, ys_ref, *rest, final_norm):
    out_refs, (acc, buf, sems) = rest[:-3], rest[-3:]
    i = pl.program_id(0)
    cur = i % 2

    def copy(slots, b, g, u, k):
        return pltpu.make_async_copy(ys_ref.at[_pair_slot(slots, g, u, k)], buf.at[b, k, g, :, u, :],
                                     sems.at[b])

    @pl.when(i == 0)
    def _():
        _start_pair_rows(functools.partial(copy, slot_ref, 0))

    @pl.when(i + 1 < pl.num_programs(0))
    def _():
        _start_pair_rows(functools.partial(copy, slot_next_ref, 1 - cur))

    for k in range(TOP_K):
        pltpu.make_async_copy(ys_ref.at[pl.ds(0, ROUTE_TB)], ys_ref.at[pl.ds(0, ROUTE_TB)],
                              sems.at[cur]).wait()

    def rows_of(k, c):
        return buf[cur, k, :, c, :, :].reshape(ROUTE_TB, 128)

    w = w_ref[...]
    sum_sq = jnp.zeros((ROUTE_TB, 1), F32)
    for c in range(LANE_TILES):
        cols = slice(c * 128, (c + 1) * 128)
        mix = w[:, 0:1] * rows_of(0, c)
        for k in range(1, TOP_K):
            mix = mix + w[:, k:k + 1] * rows_of(k, c)
        piece = x_ref[:, cols] + mod_ref[5:6, cols] * mix
        sum_sq = sum_sq + jnp.sum(piece * piece, axis=-1, keepdims=True)
        acc[:, cols] = piece

    if not final_norm:
        out_refs[0][...] = acc[...]
        return
    normed = acc[...] * lax.rsqrt(sum_sq * (1.0 / D_MODEL) + EPS) * fg_ref[...]

    @pl.when(i < T_CTX // ROUTE_TB)
    def _():
        out_refs[0][...] = normed

    @pl.when(i >= T_CTX // ROUTE_TB)
    def _():
        out_refs[1][...] = normed


def _combine(slot, top_w, x, mod_l, final_g, ys, final_norm):
    tb = ROUTE_TB
    nb = T_ALL // tb
    nc = T_CTX // tb
    row = functools.partial(_cond_row, blocks_ctx=nc, blocks_per_latent=DEC_SEQ // tb)
    tok = lambda w: pl.BlockSpec((tb, w), lambda i: (i, 0))
    slots = lambda ahead: pl.BlockSpec((None, 1, tb * TOP_K),
                                       lambda i: (jnp.minimum(i + ahead, nb - 1), 0, 0),
                                       memory_space=pltpu.SMEM)
    if final_norm:
        out_specs = [pl.BlockSpec((tb, D_MODEL), lambda i: (jnp.minimum(i, nc - 1), 0)),
                     pl.BlockSpec((tb, D_MODEL), lambda i: (jnp.maximum(i - nc, 0), 0))]
        out_shape = [jax.ShapeDtypeStruct((T_CTX, D_MODEL), F32),
                     jax.ShapeDtypeStruct((T_LAT, D_MODEL), F32)]
    else:
        out_specs = [tok(D_MODEL)]
        out_shape = [jax.ShapeDtypeStruct((T_ALL, D_MODEL), F32)]
    return pl.pallas_call(
        functools.partial(_combine_kernel, final_norm=final_norm),
        grid=(nb,),
        in_specs=[
            slots(0), slots(1), tok(TOP_K), tok(D_MODEL),
            pl.BlockSpec((None, 6, D_MODEL), lambda i: (row(i), 0, 0)),
            pl.BlockSpec((1, D_MODEL), lambda i: (0, 0)),
            pl.BlockSpec(memory_space=pl.ANY),
        ],
        out_specs=out_specs,
        out_shape=out_shape,
        scratch_shapes=[pltpu.VMEM((tb, D_MODEL), F32),
                        pltpu.VMEM((2, TOP_K, tb // ROW_GROUP, LANE_TILES, ROW_GROUP, 128), F32),
                        pltpu.SemaphoreType.DMA((2,))],
        compiler_params=_params("arbitrary"),
        name="moe_combine",
    )(slot, slot, top_w, x, mod_l, final_g.reshape(1, D_MODEL), ys.reshape(N_SLOTS, LANE_TILES, 128))


def _block_diag(blocks):
    g, a, b = blocks.shape
    eye = jnp.eye(g, dtype=blocks.dtype)
    return (eye[:, None, :, None] * blocks[:, :, None, :]).reshape(g * a, g * b)


def _dft_tables(n):
    k = np.arange(n, dtype=np.int64)
    ang = 2.0 * np.pi * ((k[:, None] * k[None, :]) % n).astype(np.float64) / n
    return np.cos(ang) / math.sqrt(n), np.sin(ang) / math.sqrt(n)


def _seq_constants(seq):
    t = np.arange(seq)[:, None]
    win = np.array(POOL_WINDOWS)[None, :]
    lo = np.clip(t - win // 2, 0, seq)
    hi = np.clip(t - win // 2 + win, 0, seq)
    invcnt = np.repeat(1.0 / (hi - lo).astype(np.float64), POOL_CH, axis=1)
    cos_p, sin_p = _dft_tables(seq)
    cos_c, sin_c = _dft_tables(FFT_CH)
    eye = np.eye(POOL_GROUPS)
    as_bf16 = lambda a: jnp.asarray(a, F32).astype(BF16)
    return {
        "invcnt": jnp.asarray(invcnt, F32),
        "f_pos": as_bf16(np.concatenate([cos_p, -sin_p], axis=1)),
        "f_cos_ch": as_bf16(np.kron(eye, cos_c)),
        "f_sin_ch": as_bf16(np.kron(eye, sin_c)),
        "ones96": as_bf16(np.kron(eye, np.ones((SGU_CH, SGU_CH)))),
    }


def _rope_constants():
    rows = DEC_SEQ // GRID_W
    row = jnp.repeat(jnp.arange(rows, dtype=F32), GRID_W)
    col = jnp.tile(jnp.arange(GRID_W, dtype=F32), rows)
    freqs = ROPE_THETA ** (-jnp.arange(ROPE_PAIRS, dtype=F32) / ROPE_PAIRS)
    ang = jnp.stack([row[:, None] * freqs, col[:, None] * freqs], axis=1)
    ang = jnp.repeat(ang.reshape(DEC_SEQ, 2 * ROPE_PAIRS), 2, axis=1)
    ang = jnp.tile(ang, (1, 128 // HEAD_DIM))
    even = (jnp.arange(128) % 2 == 0)[None, :]
    sin = jnp.sin(ang)
    return {
        "rope_cos": jnp.cos(ang),
        "rope_sin_next": jnp.where(even, -sin, 0.0),
        "rope_sin_prev": jnp.where(even, 0.0, sin),
    }


def _layer_weights(p, l):
    group_of_lane = np.arange(SGU_WIDTH) // SGU_CH
    return {
        "pool_w_bd": _block_diag(p["pool_w"][l]).astype(BF16),
        "pool_scale": p["pool_scale"][l].reshape(1, POOL_WIDTH),
        "sgu_w_stack": p["sgu_w"][l].reshape(SGU_GROUPS * SGU_CHUNK, SGU_CHUNK).astype(BF16),
        "sgu_bias": p["sgu_b"][l].T[:, group_of_lane],
        "gq": jnp.tile(p["q_norm_g"][l], N_HEADS).reshape(1, ATTN_WIDTH),
        "gk": jnp.tile(p["k_norm_g"][l], N_KV_HEADS).reshape(1, KV_WIDTH),
        "w_br_pool": p["w_br_pool"][l].astype(BF16),
        "w_br_fourier": p["w_br_fourier"][l].astype(BF16),
        "w_br_attn": p["w_br_attn"][l].astype(BF16),
        "w_br_sgu": p["w_br_sgu"][l].astype(BF16),
        "w_out": p["w_out"][l].astype(BF16),
        "router_w": p["router_w"][l].T,
        "router_b": p["router_b"][l].reshape(N_EXPERTS, 1),
    }


def kernel(x_prompt, x_sample, cache_k, cache_v, c, c_ctx, w_mod, b_mod, norm_mix_g, norm_ffn_g, w_in, pool_w, pool_scale, q_norm_g, k_norm_g, sgu_w, sgu_b, w_br_pool, w_br_fourier, w_br_attn, w_br_sgu, w_out, router_w, router_b, moe_w_gate, moe_b_gate, moe_w_up, moe_b_up, moe_w_down, moe_b_down, final_norm_g):
    p = dict(w_in=w_in, pool_w=pool_w, pool_scale=pool_scale, q_norm_g=q_norm_g, k_norm_g=k_norm_g,
             sgu_w=sgu_w, sgu_b=sgu_b, w_br_pool=w_br_pool, w_br_fourier=w_br_fourier,
             w_br_attn=w_br_attn, w_br_sgu=w_br_sgu, w_out=w_out, router_w=router_w,
             router_b=router_b, moe_w_gate=moe_w_gate, moe_b_gate=moe_b_gate, moe_w_up=moe_w_up,
             moe_b_up=moe_b_up, moe_w_down=moe_w_down, moe_b_down=moe_b_down)

    cond = jnp.concatenate([c_ctx[None, :], c, jnp.zeros((N_COND - 1 - DEC_BATCH, D_MODEL), F32)])
    mod = _modulation(cond, w_mod, b_mod).reshape(DEPTH, N_COND, 6, D_MODEL)

    ones64 = jnp.asarray(np.kron(np.eye(N_HEADS), np.ones((HEAD_DIM, HEAD_DIM))), BF16)
    consts_ctx = dict(_seq_constants(SEQ), ones64=ones64)
    consts_lat = dict(_seq_constants(DEC_SEQ), ones64=ones64, **_rope_constants())
    ck = cache_k.reshape(DEC_BATCH, DEPTH, PAST_LEN, KV_WIDTH)
    cv = cache_v.reshape(DEC_BATCH, DEPTH, PAST_LEN, KV_WIDTH)

    x_parts = (x_prompt.reshape(T_CTX, D_MODEL), x_sample.reshape(T_LAT, D_MODEL))
    new_k, new_v = [], []
    for l in range(DEPTH):
        lw = _layer_weights(p, l)
        xp, xq, qkv, uv, gates = _in_projection(x_parts, mod[l], norm_mix_g[l], w_in, l)
        pool_c, four_c, sgu_c = _mixers(xp, xq, uv, consts_ctx, lw, SEQ, BATCH, 0, CTX_SEQS_PER_STEP)
        pool_l, four_l, sgu_l = _mixers(xp, xq, uv, consts_lat, lw, DEC_SEQ, DEC_BATCH,
                                        T_CTX // DEC_SEQ, 1)
        attn_c, k_c, v_c = _attention_ctx(qkv, consts_ctx, lw)
        attn_l = _attention_lat(qkv, ck[:, l], cv[:, l], consts_lat, lw)
        new_k.append(k_c.reshape(BATCH, SEQ, N_KV_HEADS, HEAD_DIM))
        new_v.append(v_c.reshape(BATCH, SEQ, N_KV_HEADS, HEAD_DIM))
        x, h, top_idx, top_w = _merge((pool_c, pool_l), (four_c, four_l), (attn_c, attn_l),
                                      (sgu_c, sgu_l), gates, x_parts, mod[l], norm_ffn_g[l], lw)
        slot, tiles = _route(top_idx)
        ys = _routed_ffn(tiles, _dispatch(tiles["rows"], slot, h), l, p)
        x_parts = tuple(_combine(slot, top_w.T, x, mod[l], final_norm_g, ys,
                                 final_norm=(l == DEPTH - 1)))

    y_prompt = x_parts[0].reshape(BATCH, SEQ, D_MODEL)
    y_sample = x_parts[1].reshape(DEC_BATCH, DEC_SEQ, D_MODEL)
    return (y_prompt, y_sample, jnp.stack(new_k, axis=1), jnp.stack(new_v, axis=1))
```

```python
import functools
import math

import numpy as np
import jax
import jax.numpy as jnp
from jax import lax
from jax.experimental import pallas as pl
from jax.experimental.pallas import tpu as pltpu

F32 = jnp.float32
BF16 = jnp.bfloat16

D_MODEL = 1024
BATCH = 32
SEQ = 256
DEPTH = 2
DEC_BATCH = 2
DEC_SEQ = 1024
PAST_LEN = 256
GRID_W = 64
EPS = 1e-6
POOL_GROUPS = 4
POOL_CH = 96
POOL_WIDTH = 384
POOL_WINDOWS = (2, 4, 8, 16)
POOL_PAD = 16
FFT_CH = 96
FFT_WIDTH = 384
N_HEADS = 8
N_KV_HEADS = 2
HEAD_DIM = 64
ATTN_WIDTH = 512
KV_WIDTH = 128
QKV_WIDTH = ATTN_WIDTH + 2 * KV_WIDTH
ROPE_THETA = 10000.0
ROPE_PAIRS = 16
SGU_GROUPS = 4
SGU_CH = 96
SGU_WIDTH = 384
SGU_CHUNK = 128
N_BRANCHES = 4
GATE_WIDTH = N_BRANCHES * D_MODEL
IN_COLS = POOL_WIDTH + FFT_WIDTH + QKV_WIDTH + 2 * SGU_WIDTH + GATE_WIDTH
N_EXPERTS = 32
TOP_K = 4
D_FF = 1024
SWIGLU_LIMIT = 7.0
SWIGLU_ALPHA = 1.702

ROW_GROUP = 8
LANE_TILES = D_MODEL // 128

T_CTX = BATCH * SEQ
T_LAT = DEC_BATCH * DEC_SEQ
T_ALL = T_CTX + T_LAT
N_COND = 8
VMEM_LIMIT = 56 * 1024 * 1024
CAST_ROWS = 32

COL_XP = 0
COL_XQ = COL_XP + POOL_WIDTH
COL_QKV = COL_XQ + FFT_WIDTH
COL_UV = COL_QKV + QKV_WIDTH
COL_GATE = COL_UV + 2 * SGU_WIDTH


def _params(*sem):
    return pltpu.CompilerParams(dimension_semantics=sem, vmem_limit_bytes=VMEM_LIMIT)


def _split_bf16(x):
    hi = x.astype(BF16)
    lo = (x - hi.astype(F32)).astype(BF16)
    return hi, lo


def _dot(a, b):
    return jnp.dot(a, b, preferred_element_type=F32)


def _dot_nt(a, b):
    return lax.dot_general(a, b, (((1,), (1,)), ((), ())), preferred_element_type=F32)


def _group_lane_select(lane, vals, width):
    out = vals[-1]
    for g in range(len(vals) - 2, -1, -1):
        out = jnp.where(lane < (g + 1) * width, vals[g], out)
    return out


def _cond_row(blk, blocks_ctx, blocks_per_latent):
    return jnp.where(blk < blocks_ctx, 0, 1 + (blk - blocks_ctx) // blocks_per_latent)


MOD_TN = 1536


def _mod_kernel(c_ref, w_ref, b_ref, o_ref):
    c = c_ref[...]
    s = c * jax.nn.sigmoid(c)
    sh, sl = _split_bf16(s)
    wh, wl = _split_bf16(w_ref[...])
    o_ref[...] = _dot(sh, wh) + _dot(sh, wl) + _dot(sl, wh) + b_ref[...]


def _modulation(cond, w_mod, b_mod):
    n_cols = 6 * D_MODEL
    return pl.pallas_call(
        _mod_kernel,
        grid=(DEPTH, n_cols // MOD_TN),
        in_specs=[
            pl.BlockSpec((N_COND, D_MODEL), lambda l, j: (0, 0)),
            pl.BlockSpec((None, D_MODEL, MOD_TN), lambda l, j: (l, 0, j)),
            pl.BlockSpec((None, 1, MOD_TN), lambda l, j: (l, 0, j)),
        ],
        out_specs=pl.BlockSpec((None, N_COND, MOD_TN), lambda l, j: (l, 0, j)),
        out_shape=jax.ShapeDtypeStruct((DEPTH, N_COND, n_cols), F32),
        compiler_params=_params("arbitrary", "arbitrary"),
        name="modulation",
    )(cond, w_mod, b_mod.reshape(DEPTH, 1, n_cols))


INPROJ_TM = 256
INPROJ_SEGMENTS = (
    (COL_XP, POOL_WIDTH), (COL_XQ, FFT_WIDTH), (COL_QKV, QKV_WIDTH),
    (COL_UV, 2 * SGU_WIDTH), (COL_GATE, GATE_WIDTH))
INPROJ_CHUNK = 1024


def _ada_norm(x, g, shift, scale):
    xn = x * lax.rsqrt(jnp.mean(x * x, axis=-1, keepdims=True) + EPS)
    return xn * g * (1.0 + scale) + shift


def _token_specs(parts, tm, width):
    if len(parts) == 1:
        return [pl.BlockSpec((tm, width), lambda i: (i, 0))]
    nc = T_CTX // tm
    return [pl.BlockSpec((tm, width), lambda i: (jnp.minimum(i, nc - 1), 0)),
            pl.BlockSpec((tm, width), lambda i: (jnp.maximum(i - nc, 0), 0))]


def _token_load(refs, tm):
    if len(refs) == 1:
        return refs[0][...]
    return jnp.where(pl.program_id(0) < T_CTX // tm, refs[0][...], refs[1][...])


W_IN_CHUNK = 640


def _inproj_kernel(*refs, n_x, layer):
    x_refs, (mod_ref, g_ref, w_hbm) = refs[:n_x], refs[n_x:n_x + 3]
    out_refs, (w_ref, stage, sems) = refs[n_x + 3:-3], refs[-3:]

    @pl.when(pl.program_id(0) == 0)
    def _():
        def chunk(c):
            return pltpu.make_async_copy(w_hbm.at[layer, :, pl.ds(c * W_IN_CHUNK, W_IN_CHUNK)],
                                         stage.at[c % 2], sems.at[c % 2])

        n_chunks = IN_COLS // W_IN_CHUNK
        chunk(0).start()
        for c in range(n_chunks):
            if c + 1 < n_chunks:
                chunk(c + 1).start()
            chunk(c).wait()

            def cast_rows(r, carry, c=c):
                rows = pl.ds(pl.multiple_of(r * CAST_ROWS, CAST_ROWS), CAST_ROWS)
                w_ref[rows, c * W_IN_CHUNK:(c + 1) * W_IN_CHUNK] = stage[c % 2, rows, :].astype(BF16)
                return carry

            lax.fori_loop(0, D_MODEL // CAST_ROWS, cast_rows, 0)

    h = _ada_norm(_token_load(x_refs, INPROJ_TM), g_ref[...], mod_ref[0:1, :], mod_ref[1:2, :])
    hb = h.astype(BF16)
    for (col, width), o_ref in zip(INPROJ_SEGMENTS, out_refs):
        for c0 in range(0, width, INPROJ_CHUNK):
            c1 = min(c0 + INPROJ_CHUNK, width)
            proj = _dot(hb, w_ref[:, col + c0:col + c1])
            o_ref[:, c0:c1] = jax.nn.sigmoid(proj) if col == COL_GATE else proj


def _in_projection(x_parts, mod_l, g, w_in, l):
    tm = INPROJ_TM
    row = functools.partial(_cond_row, blocks_ctx=T_CTX // tm, blocks_per_latent=DEC_SEQ // tm)
    return pl.pallas_call(
        functools.partial(_inproj_kernel, n_x=len(x_parts), layer=l),
        grid=(T_ALL // tm,),
        in_specs=_token_specs(x_parts, tm, D_MODEL) + [
            pl.BlockSpec((None, 6, D_MODEL), lambda i: (row(i), 0, 0)),
            pl.BlockSpec((1, D_MODEL), lambda i: (0, 0)),
            pl.BlockSpec(memory_space=pl.ANY),
        ],
        out_specs=[pl.BlockSpec((tm, w), lambda i: (i, 0)) for _, w in INPROJ_SEGMENTS],
        out_shape=[jax.ShapeDtypeStruct((T_ALL, w), F32) for _, w in INPROJ_SEGMENTS],
        scratch_shapes=[pltpu.VMEM((D_MODEL, IN_COLS), BF16), pltpu.VMEM((2, D_MODEL, W_IN_CHUNK), F32),
                        pltpu.SemaphoreType.DMA((2,))],
        compiler_params=_params("arbitrary"),
        name="in_projection",
    )(*x_parts, mod_l, g.reshape(1, D_MODEL), w_in)


CTX_SEQS_PER_STEP = 4


def _pool_mixer(xp, invcnt, w_bd, scale):
    s = xp.shape[0]
    n = s + 2 * POOL_PAD
    zeros = jnp.zeros((POOL_PAD, POOL_WIDTH), F32)
    xe = jnp.concatenate([zeros, xp, zeros], axis=0)

    def shift(a, k):
        return pltpu.roll(a, k % n, 0)

    s2 = xe + shift(xe, 1)
    s4 = shift(s2, 1) + shift(s2, -1)
    s8 = shift(s4, 2) + shift(s4, -2)
    s16 = shift(s8, 4) + shift(s8, -4)
    lane = lax.broadcasted_iota(jnp.int32, (1, POOL_WIDTH), 1)
    total = _group_lane_select(lane, [s2, s4, s8, s16], POOL_CH)[POOL_PAD:POOL_PAD + s]
    pooled = total * invcnt - xp
    return _dot(pooled.astype(BF16), w_bd) * scale


def _fourier_mixer(xq, f_cos_ch, f_sin_ch, f_pos):
    xb = xq.astype(BF16)
    a = _dot(xb, f_cos_ch).astype(BF16)
    b = _dot(xb, f_sin_ch).astype(BF16)
    return _dot(f_pos, jnp.concatenate([a, b], axis=0))


def _group_mean_sq(x, ones_bd, width):
    hi, lo = _split_bf16(x * x)
    return (_dot(hi, ones_bd) + _dot(lo, ones_bd)) * (1.0 / width)


def _mixers_kernel(xp_ref, xq_ref, uv_ref, invcnt_ref, pool_w_ref, pool_s_ref, fcc_ref, fsc_ref,
                   fpos_ref, ones_ref, sgu_w_ref, sgu_b_ref, pool_o, four_o, sgu_o, *, seq):
    lane = lax.broadcasted_iota(jnp.int32, (1, SGU_WIDTH), 1)
    w_stack = sgu_w_ref[...]
    bias = sgu_b_ref[...]
    for b in range(xp_ref.shape[0] // seq):
        rows = slice(b * seq, (b + 1) * seq)
        pool_o[rows, :] = _pool_mixer(xp_ref[rows, :], invcnt_ref[...], pool_w_ref[...],
                                      pool_s_ref[...]).astype(BF16)
        four_o[rows, :] = _fourier_mixer(xq_ref[rows, :], fcc_ref[...], fsc_ref[...],
                                         fpos_ref[...]).astype(BF16)

        act = jax.nn.gelu(uv_ref[rows, :], approximate=True)
        u = act[:, :SGU_WIDTH]
        v = act[:, SGU_WIDTH:]
        vg = (v * lax.rsqrt(_group_mean_sq(v, ones_ref[...], SGU_CH) + EPS)).astype(BF16)
        for n in range(seq // SGU_CHUNK):
            chunk = slice(n * SGU_CHUNK, (n + 1) * SGU_CHUNK)
            r = _dot(w_stack, vg[chunk])
            per_group = [r[g * SGU_CHUNK:(g + 1) * SGU_CHUNK] for g in range(SGU_GROUPS)]
            spatial = _group_lane_select(lane, per_group, SGU_CH) + bias
            out_rows = slice(b * seq + n * SGU_CHUNK, b * seq + (n + 1) * SGU_CHUNK)
            sgu_o[out_rows, :] = (u[chunk] * spatial).astype(BF16)


def _mixers(xp, xq, uv, consts, lw, seq, n_seq, block0, per_step):
    rows = per_step * seq
    full = lambda shape: pl.BlockSpec(shape, lambda b: (0,) * len(shape))
    tok = lambda w: pl.BlockSpec((rows, w), lambda b: (block0 + b, 0))
    out = lambda: pl.BlockSpec((rows, POOL_WIDTH), lambda b: (b, 0))
    return pl.pallas_call(
        functools.partial(_mixers_kernel, seq=seq),
        grid=(n_seq // per_step,),
        in_specs=[
            tok(POOL_WIDTH), tok(FFT_WIDTH), tok(2 * SGU_WIDTH),
            full((seq, POOL_WIDTH)), full((POOL_WIDTH, POOL_WIDTH)), full((1, POOL_WIDTH)),
            full((FFT_WIDTH, FFT_WIDTH)), full((FFT_WIDTH, FFT_WIDTH)), full((seq, 2 * seq)),
            full((SGU_WIDTH, SGU_WIDTH)), full((SGU_GROUPS * SGU_CHUNK, SGU_CHUNK)),
            full((SGU_CHUNK, SGU_WIDTH)),
        ],
        out_specs=[out(), out(), out()],
        out_shape=[jax.ShapeDtypeStruct((n_seq * seq, POOL_WIDTH), BF16)] * 3,
        compiler_params=_params("arbitrary"),
        name=f"mixers_s{seq}",
    )(xp, xq, uv, consts["invcnt"], lw["pool_w_bd"], lw["pool_scale"], consts["f_cos_ch"],
      consts["f_sin_ch"], consts["f_pos"], consts["ones96"], lw["sgu_w_stack"], lw["sgu_bias"])


def _head_norm(x, ones_bd, g):
    return x * lax.rsqrt(_group_mean_sq(x, ones_bd, HEAD_DIM) + EPS) * g


def _rope(x, cos, sin_next, sin_prev):
    cols = []
    for c in range(x.shape[1] // 128):
        xc = x[:, c * 128:(c + 1) * 128]
        nxt = pltpu.roll(xc, 127, 1)
        prv = pltpu.roll(xc, 1, 1)
        cols.append(xc * cos + nxt * sin_next + prv * sin_prev)
    return cols[0] if len(cols) == 1 else jnp.concatenate(cols, axis=1)


def _attend(q, keys, vals, o_ref):
    qb = (q * (HEAD_DIM ** -0.5)).astype(BF16)
    group = N_HEADS // N_KV_HEADS
    for h in range(N_HEADS):
        j = h // group
        kh = keys[:, j * HEAD_DIM:(j + 1) * HEAD_DIM]
        vh = vals[:, j * HEAD_DIM:(j + 1) * HEAD_DIM]
        s = _dot_nt(qb[:, h * HEAD_DIM:(h + 1) * HEAD_DIM], kh)
        p = jnp.exp(s - jnp.max(s, axis=-1, keepdims=True))
        denom = jnp.sum(p, axis=-1, keepdims=True)
        o = _dot(p.astype(BF16), vh) / denom
        o_ref[:, h * HEAD_DIM:(h + 1) * HEAD_DIM] = o.astype(BF16)


def _attn_ctx_kernel(qkv_ref, gq_ref, gk_ref, ones_ref, o_ref, k_ref, v_ref):
    qkv = qkv_ref[...]
    ones = ones_ref[...]
    q = _head_norm(qkv[:, :ATTN_WIDTH], ones, gq_ref[...])
    k = _head_norm(qkv[:, ATTN_WIDTH:ATTN_WIDTH + KV_WIDTH], ones[:KV_WIDTH, :KV_WIDTH], gk_ref[...])
    v = qkv[:, ATTN_WIDTH + KV_WIDTH:]
    k_ref[...] = k
    v_ref[...] = v
    _attend(q, k.astype(BF16), v.astype(BF16), o_ref)


def _attention_ctx(qkv, consts, lw):
    full = lambda shape: pl.BlockSpec(shape, lambda b: (0,) * len(shape))
    return pl.pallas_call(
        _attn_ctx_kernel,
        grid=(BATCH,),
        in_specs=[pl.BlockSpec((SEQ, QKV_WIDTH), lambda b: (b, 0)),
                  full((1, ATTN_WIDTH)), full((1, KV_WIDTH)), full((ATTN_WIDTH, ATTN_WIDTH))],
        out_specs=[pl.BlockSpec((SEQ, ATTN_WIDTH), lambda b: (b, 0)),
                   pl.BlockSpec((SEQ, KV_WIDTH), lambda b: (b, 0)),
                   pl.BlockSpec((SEQ, KV_WIDTH), lambda b: (b, 0))],
        out_shape=[jax.ShapeDtypeStruct((T_CTX, ATTN_WIDTH), BF16),
                   jax.ShapeDtypeStruct((T_CTX, KV_WIDTH), F32),
                   jax.ShapeDtypeStruct((T_CTX, KV_WIDTH), F32)],
        compiler_params=_params("arbitrary"),
        name="attention_ctx",
    )(qkv, lw["gq"], lw["gk"], consts["ones64"])


LAT_QBLK = 512


def _attn_lat_kernel(q_ref, kv_ref, ck_ref, cv_ref, gq_ref, gk_ref, ones_ref, cos_q, sn_q, sp_q,
                     cos_k, sn_k, sp_k, o_ref, keys, vals):
    ones = ones_ref[...]

    @pl.when(pl.program_id(1) == 0)
    def _():
        kv = kv_ref[...]
        k = _head_norm(kv[:, :KV_WIDTH], ones[:KV_WIDTH, :KV_WIDTH], gk_ref[...])
        keys[0:DEC_SEQ, :] = _rope(k, cos_k[...], sn_k[...], sp_k[...]).astype(BF16)
        keys[DEC_SEQ:, :] = ck_ref[...].astype(BF16)
        vals[0:DEC_SEQ, :] = kv[:, KV_WIDTH:].astype(BF16)
        vals[DEC_SEQ:, :] = cv_ref[...].astype(BF16)

    q = _head_norm(q_ref[...], ones, gq_ref[...])
    q = _rope(q, cos_q[...], sn_q[...], sp_q[...])
    _attend(q, keys[...], vals[...], o_ref)


def _attention_lat(qkv, cache_k_l, cache_v_l, consts, lw):
    nq = DEC_SEQ // LAT_QBLK
    q0 = T_CTX // LAT_QBLK
    s0 = T_CTX // DEC_SEQ
    full = lambda shape: pl.BlockSpec(shape, lambda b, j: (0,) * len(shape))
    rope_q = lambda: pl.BlockSpec((LAT_QBLK, 128), lambda b, j: (j, 0))
    rope_k = lambda: pl.BlockSpec((DEC_SEQ, 128), lambda b, j: (0, 0))
    return pl.pallas_call(
        _attn_lat_kernel,
        grid=(DEC_BATCH, nq),
        in_specs=[
            pl.BlockSpec((LAT_QBLK, ATTN_WIDTH), lambda b, j: (q0 + b * nq + j, 0)),
            pl.BlockSpec((DEC_SEQ, 2 * KV_WIDTH), lambda b, j: (s0 + b, ATTN_WIDTH // (2 * KV_WIDTH))),
            pl.BlockSpec((None, PAST_LEN, KV_WIDTH), lambda b, j: (b, 0, 0)),
            pl.BlockSpec((None, PAST_LEN, KV_WIDTH), lambda b, j: (b, 0, 0)),
            full((1, ATTN_WIDTH)), full((1, KV_WIDTH)), full((ATTN_WIDTH, ATTN_WIDTH)),
            rope_q(), rope_q(), rope_q(), rope_k(), rope_k(), rope_k(),
        ],
        out_specs=pl.BlockSpec((LAT_QBLK, ATTN_WIDTH), lambda b, j: (b * nq + j, 0)),
        out_shape=jax.ShapeDtypeStruct((T_LAT, ATTN_WIDTH), BF16),
        scratch_shapes=[pltpu.VMEM((DEC_SEQ + PAST_LEN, KV_WIDTH), BF16),
                        pltpu.VMEM((DEC_SEQ + PAST_LEN, KV_WIDTH), BF16)],
        compiler_params=_params("arbitrary", "arbitrary"),
        name="attention_lat",
    )(qkv, qkv, cache_k_l, cache_v_l, lw["gq"], lw["gk"], consts["ones64"],
      consts["rope_cos"], consts["rope_sin_next"], consts["rope_sin_prev"],
      consts["rope_cos"], consts["rope_sin_next"], consts["rope_sin_prev"])


MERGE_TM = 512


def _merge_kernel(*refs, n_x):
    branch_refs, gate_ref, x_refs = refs[:8], refs[8], refs[9:9 + n_x]
    (mod_ref, g_ref, wp_ref, wf_ref, wa_ref, ws_ref, wo_ref, rw_ref, rb_ref,
     x_o, h_o, idx_o, wgt_o) = refs[9 + n_x:]
    merged = None
    for i, w_ref in enumerate((wp_ref, wf_ref, wa_ref, ws_ref)):
        br = _dot(_token_load(branch_refs[2 * i:2 * i + 2], MERGE_TM), w_ref[...])
        term = gate_ref[:, i * D_MODEL:(i + 1) * D_MODEL] * br
        merged = term if merged is None else merged + term
    mix = _dot(merged.astype(BF16), wo_ref[...])
    x = _token_load(x_refs, MERGE_TM) + mod_ref[2:3, :] * mix
    x_o[...] = x
    h = _ada_norm(x, g_ref[...], mod_ref[3:4, :], mod_ref[4:5, :])
    for c in range(LANE_TILES):
        h_o[:, c, :, :] = h[:, c * 128:(c + 1) * 128].reshape(MERGE_TM // ROW_GROUP, ROW_GROUP, 128)

    hh, hl = _split_bf16(h)
    rh, rl = _split_bf16(rw_ref[...])
    logits = _dot_nt(rh, hh) + _dot_nt(rl, hh) + _dot_nt(rh, hl) + rb_ref[...]
    expert = lax.broadcasted_iota(jnp.int32, logits.shape, 0).astype(F32)
    work = logits
    top = jnp.max(logits, axis=0, keepdims=True)
    idx, wgt = [], []
    denom = jnp.zeros_like(top)
    for _ in range(TOP_K):
        m = jnp.max(work, axis=0, keepdims=True)
        first = jnp.min(jnp.where(work == m, expert, float(N_EXPERTS)), axis=0, keepdims=True)
        e = jnp.exp(m - top)
        idx.append(first)
        wgt.append(e)
        denom = denom + e
        work = jnp.where(expert == first, -jnp.inf, work)
    idx_o[...] = jnp.concatenate(idx, axis=0).astype(jnp.int32)
    wgt_o[...] = jnp.concatenate(wgt, axis=0) / denom


def _merge(pool, four, attn, sgu, gates, x_parts, mod_l, g_ffn, lw):
    tm = MERGE_TM
    row = functools.partial(_cond_row, blocks_ctx=T_CTX // tm, blocks_per_latent=DEC_SEQ // tm)
    full = lambda shape: pl.BlockSpec(shape, lambda i: (0,) * len(shape))
    tok = lambda w: pl.BlockSpec((tm, w), lambda i: (i, 0))
    branch_specs = []
    for pair, width in ((pool, POOL_WIDTH), (four, FFT_WIDTH), (attn, ATTN_WIDTH), (sgu, SGU_WIDTH)):
        branch_specs += _token_specs(pair, tm, width)
    return pl.pallas_call(
        functools.partial(_merge_kernel, n_x=len(x_parts)),
        grid=(T_ALL // tm,),
        in_specs=branch_specs + [tok(GATE_WIDTH)] + _token_specs(x_parts, tm, D_MODEL) + [
            pl.BlockSpec((None, 6, D_MODEL), lambda i: (row(i), 0, 0)),
            full((1, D_MODEL)),
            full((POOL_WIDTH, D_MODEL)), full((FFT_WIDTH, D_MODEL)), full((ATTN_WIDTH, D_MODEL)),
            full((SGU_WIDTH, D_MODEL)), full((D_MODEL, D_MODEL)),
            full((N_EXPERTS, D_MODEL)), full((N_EXPERTS, 1)),
        ],
        out_specs=[tok(D_MODEL),
                   pl.BlockSpec((tm // ROW_GROUP, LANE_TILES, ROW_GROUP, 128), lambda i: (i, 0, 0, 0)),
                   pl.BlockSpec((TOP_K, tm), lambda i: (0, i)), pl.BlockSpec((TOP_K, tm), lambda i: (0, i))],
        out_shape=[jax.ShapeDtypeStruct((T_ALL, D_MODEL), F32),
                   jax.ShapeDtypeStruct((T_ALL // ROW_GROUP, LANE_TILES, ROW_GROUP, 128), F32),
                   jax.ShapeDtypeStruct((TOP_K, T_ALL), jnp.int32),
                   jax.ShapeDtypeStruct((TOP_K, T_ALL), F32)],
        compiler_params=_params("arbitrary"),
        name="merge_router",
    )(*pool, *four, *attn, *sgu, gates, *x_parts, mod_l, g_ffn.reshape(1, D_MODEL),
      lw["w_br_pool"], lw["w_br_fourier"], lw["w_br_attn"], lw["w_br_sgu"], lw["w_out"],
      lw["router_w"], lw["router_b"])


N_PAIRS = T_ALL * TOP_K
FFN_TM = 512
FFN_ROW_OPTIONS = (128, 256, 384, 512)
N_ROW_TILES = N_PAIRS // FFN_TM + N_EXPERTS
N_SLOTS = N_ROW_TILES * FFN_TM
ROUTE_TB = 512


def _route(top_idx):
    experts = jnp.arange(N_EXPERTS, dtype=jnp.int32)
    onehot = (top_idx.T[:, :, None] == experts[None, None, :]).astype(jnp.int32)
    per_token = jnp.sum(onehot, axis=1)
    csum = jnp.cumsum(per_token, axis=0)
    counts = csum[-1]
    tiles = (counts + FFN_TM - 1) // FFN_TM
    tile_end = jnp.cumsum(tiles)
    row_start = (tile_end - tiles) * FFN_TM
    first_row = csum - per_token + row_start[None, :]
    slot = jnp.sum(onehot * first_row[:, None, :], axis=2)
    tile_ids = jnp.arange(N_ROW_TILES, dtype=jnp.int32)
    tile_expert = jnp.sum((tile_end[None, :] <= tile_ids[:, None]).astype(jnp.int32), axis=1)
    tile_expert = jnp.minimum(tile_expert, N_EXPERTS - 1)
    rows_before = (tile_ids - (tile_end - tiles)[tile_expert]) * FFN_TM
    tile_rows = jnp.where(tile_ids < tile_end[-1],
                          jnp.clip(counts[tile_expert] - rows_before, 0, FFN_TM), 0)
    slot = slot.astype(jnp.int32).reshape(T_ALL // ROUTE_TB, 1, ROUTE_TB * TOP_K)
    used = (tiles > 0).astype(jnp.int32)
    rank = jnp.cumsum(used) - 1
    group_expert = jnp.sum(jnp.where((rank[None, :] == experts[:, None]) & (used[None, :] > 0),
                                     experts[None, :], 0), axis=1)
    tiles_info = dict(expert=tile_expert.astype(jnp.int32), rows=tile_rows.astype(jnp.int32),
                      group=rank[tile_expert].astype(jnp.int32),
                      group_expert=group_expert.astype(jnp.int32),
                      n_groups=jnp.sum(used).reshape(1).astype(jnp.int32),
                      n_used=tile_end[-1:].astype(jnp.int32))
    return slot, tiles_info


def _start_pair_rows(copy):
    def body(g, carry):
        for u in range(ROW_GROUP):
            for k in range(TOP_K):
                copy(g, u, k).start(priority=k % 2)
        return carry

    lax.fori_loop(0, ROUTE_TB // ROW_GROUP, body, 0)


def _pair_slot(slot_ref, g, u, k):
    return slot_ref[0, (g * ROW_GROUP + u) * TOP_K + k]


def _dispatch_kernel(tr_ref, slot_ref, h_ref, xs_out, zeros, sem, zero_sem):
    @pl.when(pl.program_id(0) == 0)
    def _():
        zeros[...] = jnp.zeros_like(zeros)

        def for_each_unfilled_tile(fn):
            group = FFN_ROW_OPTIONS[0]

            def body(i, carry):
                for full_groups in range(FFN_TM // group):
                    first = full_groups * group

                    @pl.when(tr_ref[i] // group == full_groups)
                    def _():
                        fn(pltpu.make_async_copy(zeros.at[pl.ds(0, FFN_TM - first)],
                                                 xs_out.at[pl.ds(i * FFN_TM + first, FFN_TM - first)],
                                                 zero_sem))
                return carry

            lax.fori_loop(0, N_ROW_TILES, body, 0)

        for_each_unfilled_tile(lambda c: c.start())
        for_each_unfilled_tile(lambda c: c.wait())

    def copy(g, u, k):
        return pltpu.make_async_copy(h_ref.at[g, :, u, :], xs_out.at[_pair_slot(slot_ref, g, u, k)], sem)

    _start_pair_rows(copy)
    for _ in range(TOP_K):
        pltpu.make_async_copy(xs_out.at[pl.ds(0, ROUTE_TB)], xs_out.at[pl.ds(0, ROUTE_TB)], sem).wait()


def _dispatch(tile_rows, slot, h):
    return pl.pallas_call(
        _dispatch_kernel,
        grid_spec=pltpu.PrefetchScalarGridSpec(
            num_scalar_prefetch=1,
            grid=(T_ALL // ROUTE_TB,),
            in_specs=[
                pl.BlockSpec((None, 1, ROUTE_TB * TOP_K), lambda i, tr: (i, 0, 0),
                             memory_space=pltpu.SMEM),
                pl.BlockSpec((ROUTE_TB // ROW_GROUP, LANE_TILES, ROW_GROUP, 128),
                             lambda i, tr: (i, 0, 0, 0)),
            ],
            out_specs=pl.BlockSpec(memory_space=pl.ANY),
            scratch_shapes=[pltpu.VMEM((FFN_TM, LANE_TILES, 128), F32), pltpu.SemaphoreType.DMA(()),
                            pltpu.SemaphoreType.DMA(())],
        ),
        out_shape=jax.ShapeDtypeStruct((N_SLOTS, LANE_TILES, 128), F32),
        compiler_params=_params("arbitrary"),
        name="moe_dispatch",
    )(tile_rows, slot, h)


def _expert_ffn(xb, wg, bg, wu, bu, wd, bd):
    gate = jnp.minimum(_dot(xb, wg) + bg, SWIGLU_LIMIT)
    up = jnp.clip(_dot(xb, wu) + bu, -SWIGLU_LIMIT, SWIGLU_LIMIT)
    glu = gate * jax.nn.sigmoid(SWIGLU_ALPHA * gate)
    return _dot(((up + 1.0) * glu).astype(BF16), wd) + bd


def _ffn_kernel(te_ref, tr_ref, tg_ref, ge_ref, ng_ref, nu_ref, xs_ref, wg_hbm, bg_ref, wu_hbm,
                bu_ref, wd_hbm, bd_ref, ys_ref, stage, w_bf16, xb, sems, *, layer):
    i = pl.program_id(0)
    n_rows = tr_ref[i]
    group = tg_ref[i]
    slot = group % 2

    def weight_copies(g, s):
        e = ge_ref[g]
        return [pltpu.make_async_copy(w.at[layer, e], stage.at[s, j], sems.at[s])
                for j, w in enumerate((wg_hbm, wu_hbm, wd_hbm))]

    def lane_tile(ref, c, rows):
        return ref.at[pl.ds(c, rows, stride=LANE_TILES), :]

    def ffn(rows):
        for c in range(LANE_TILES):
            xb[:rows, c * 128:(c + 1) * 128] = lane_tile(xs_ref, c, rows)[...].astype(BF16)
        y = _expert_ffn(xb[:rows, :], w_bf16[0], bg_ref[...], w_bf16[1], bu_ref[...], w_bf16[2],
                        bd_ref[...])
        for c in range(LANE_TILES):
            lane_tile(ys_ref, c, rows)[...] = y[:, c * 128:(c + 1) * 128]
        if rows < FFN_TM:
            ys_ref[rows * LANE_TILES:, :] = jnp.zeros(((FFN_TM - rows) * LANE_TILES, 128), F32)

    @pl.when(i == 0)
    def _():
        for c in weight_copies(0, 0):
            c.start()

    @pl.when((n_rows > 0) & ((i == 0) | (te_ref[i] != te_ref[jnp.maximum(i - 1, 0)])))
    def _():
        for c in weight_copies(group, slot):
            c.wait()
        def cast_rows(r, carry):
            rows = pl.ds(pl.multiple_of(r * CAST_ROWS, CAST_ROWS), CAST_ROWS)
            for j in range(3):
                w_bf16[j, rows, :] = stage[slot, j, rows, :].astype(BF16)
            return carry

        lax.fori_loop(0, D_MODEL // CAST_ROWS, cast_rows, 0)

        @pl.when(group + 1 < ng_ref[0])
        def _():
            for c in weight_copies(group + 1, 1 - slot):
                c.start()

    @pl.when(n_rows == 0)
    def _():
        ys_ref[...] = jnp.zeros_like(ys_ref)

    lower = 0
    for rows in FFN_ROW_OPTIONS:
        @pl.when((n_rows > lower) & (n_rows <= rows))
        def _(rows=rows):
            ffn(rows)
        lower = rows


def _routed_ffn(tiles, xs, l, p):
    n_prefetch = 6
    rows = pl.BlockSpec((FFN_TM * LANE_TILES, 128),
                        lambda i, te, tr, tg, ge, ng, nu: (jnp.minimum(i, nu[0] - 1), 0))
    bias_spec = lambda w: pl.BlockSpec((None, None, 1, w),
                                       lambda i, te, tr, tg, ge, ng, nu: (l, te[i], 0, 0))
    bias = lambda a: a.reshape(DEPTH, N_EXPERTS, 1, a.shape[-1])
    hbm = pl.BlockSpec(memory_space=pl.ANY)
    return pl.pallas_call(
        functools.partial(_ffn_kernel, layer=l),
        grid_spec=pltpu.PrefetchScalarGridSpec(
            num_scalar_prefetch=n_prefetch,
            grid=(N_ROW_TILES,),
            in_specs=[rows, hbm, bias_spec(D_FF), hbm, bias_spec(D_FF), hbm, bias_spec(D_MODEL)],
            out_specs=pl.BlockSpec((FFN_TM * LANE_TILES, 128),
                                   lambda i, te, tr, tg, ge, ng, nu: (i, 0)),
            scratch_shapes=[pltpu.VMEM((2, 3, D_MODEL, D_FF), F32), pltpu.VMEM((3, D_MODEL, D_FF), BF16),
                            pltpu.VMEM((FFN_TM, D_MODEL), BF16), pltpu.SemaphoreType.DMA((2,))],
        ),
        out_shape=jax.ShapeDtypeStruct((N_SLOTS * LANE_TILES, 128), F32),
        compiler_params=_params("arbitrary"),
        name="moe_ffn",
    )(tiles["expert"], tiles["rows"], tiles["group"], tiles["group_expert"], tiles["n_groups"],
      tiles["n_used"], xs.reshape(N_SLOTS * LANE_TILES, 128), p["moe_w_gate"], bias(p["moe_b_gate"]),
      p["moe_w_up"], bias(p["moe_b_up"]), p["moe_w_down"], bias(p["moe_b_down"]))


def _combine_kernel(slot_ref, slot_next_ref, w_ref, x_ref, mod_ref, fg_ref, ys_ref, *rest, final_norm):
    out_refs, (acc, buf, sems) = rest[:-3], rest[-3:]
    i = pl.program_id(0)
    cur = i % 2

    def copy(slots, b, g, u, k):
        return pltpu.make_async_copy(ys_ref.at[_pair_slot(slots, g, u, k)], buf.at[b, k, g, :, u, :],
                                     sems.at[b])

    @pl.when(i == 0)
    def _():
        _start_pair_rows(functools.partial(copy, slot_ref, 0))

    @pl.when(i + 1 < pl.num_programs(0))
    def _():
        _start_pair_rows(functools.partial(copy, slot_next_ref, 1 - cur))

    for k in range(TOP_K):
        pltpu.make_async_copy(ys_ref.at[pl.ds(0, ROUTE_TB)], ys_ref.at[pl.ds(0, ROUTE_TB)],
                              sems.at[cur]).wait()

    def rows_of(k, c):
        return buf[cur, k, :, c, :, :].reshape(ROUTE_TB, 128)

    w = w_ref[...]
    sum_sq = jnp.zeros((ROUTE_TB, 1), F32)
    for c in range(LANE_TILES):
        cols = slice(c * 128, (c + 1) * 128)
        mix = w[:, 0:1] * rows_of(0, c)
        for k in range(1, TOP_K):
            mix = mix + w[:, k:k + 1] * rows_of(k, c)
        piece = x_ref[:, cols] + mod_ref[5:6, cols] * mix
        sum_sq = sum_sq + jnp.sum(piece * piece, axis=-1, keepdims=True)
        acc[:, cols] = piece

    if not final_norm:
        out_refs[0][...] = acc[...]
        return
    normed = acc[...] * lax.rsqrt(sum_sq * (1.0 / D_MODEL) + EPS) * fg_ref[...]

    @pl.when(i < T_CTX // ROUTE_TB)
    def _():
        out_refs[0][...] = normed

    @pl.when(i >= T_CTX // ROUTE_TB)
    def _():
        out_refs[1][...] = normed


def _combine(slot, top_w, x, mod_l, final_g, ys, final_norm):
    tb = ROUTE_TB
    nb = T_ALL // tb
    nc = T_CTX // tb
    row = functools.partial(_cond_row, blocks_ctx=nc, blocks_per_latent=DEC_SEQ // tb)
    tok = lambda w: pl.BlockSpec((tb, w), lambda i: (i, 0))
    slots = lambda ahead: pl.BlockSpec((None, 1, tb * TOP_K),
                                       lambda i: (jnp.minimum(i + ahead, nb - 1), 0, 0),
                                       memory_space=pltpu.SMEM)
    if final_norm:
        out_specs = [pl.BlockSpec((tb, D_MODEL), lambda i: (jnp.minimum(i, nc - 1), 0)),
                     pl.BlockSpec((tb, D_MODEL), lambda i: (jnp.maximum(i - nc, 0), 0))]
        out_shape = [jax.ShapeDtypeStruct((T_CTX, D_MODEL), F32),
                     jax.ShapeDtypeStruct((T_LAT, D_MODEL), F32)]
    else:
        out_specs = [tok(D_MODEL)]
        out_shape = [jax.ShapeDtypeStruct((T_ALL, D_MODEL), F32)]
    return pl.pallas_call(
        functools.partial(_combine_kernel, final_norm=final_norm),
        grid=(nb,),
        in_specs=[
            slots(0), slots(1), tok(TOP_K), tok(D_MODEL),
            pl.BlockSpec((None, 6, D_MODEL), lambda i: (row(i), 0, 0)),
            pl.BlockSpec((1, D_MODEL), lambda i: (0, 0)),
            pl.BlockSpec(memory_space=pl.ANY),
        ],
        out_specs=out_specs,
        out_shape=out_shape,
        scratch_shapes=[pltpu.VMEM((tb, D_MODEL), F32),
                        pltpu.VMEM((2, TOP_K, tb // ROW_GROUP, LANE_TILES, ROW_GROUP, 128), F32),
                        pltpu.SemaphoreType.DMA((2,))],
        compiler_params=_params("arbitrary"),
        name="moe_combine",
    )(slot, slot, top_w, x, mod_l, final_g.reshape(1, D_MODEL), ys.reshape(N_SLOTS, LANE_TILES, 128))


def _block_diag(blocks):
    g, a, b = blocks.shape
    eye = jnp.eye(g, dtype=blocks.dtype)
    return (eye[:, None, :, None] * blocks[:, :, None, :]).reshape(g * a, g * b)


def _dft_tables(n):
    k = np.arange(n, dtype=np.int64)
    ang = 2.0 * np.pi * ((k[:, None] * k[None, :]) % n).astype(np.float64) / n
    return np.cos(ang) / math.sqrt(n), np.sin(ang) / math.sqrt(n)


def _seq_constants(seq):
    t = np.arange(seq)[:, None]
    win = np.array(POOL_WINDOWS)[None, :]
    lo = np.clip(t - win // 2, 0, seq)
    hi = np.clip(t - win // 2 + win, 0, seq)
    invcnt = np.repeat(1.0 / (hi - lo).astype(np.float64), POOL_CH, axis=1)
    cos_p, sin_p = _dft_tables(seq)
    cos_c, sin_c = _dft_tables(FFT_CH)
    eye = np.eye(POOL_GROUPS)
    as_bf16 = lambda a: jnp.asarray(a, F32).astype(BF16)
    return {
        "invcnt": jnp.asarray(invcnt, F32),
        "f_pos": as_bf16(np.concatenate([cos_p, -sin_p], axis=1)),
        "f_cos_ch": as_bf16(np.kron(eye, cos_c)),
        "f_sin_ch": as_bf16(np.kron(eye, sin_c)),
        "ones96": as_bf16(np.kron(eye, np.ones((SGU_CH, SGU_CH)))),
    }


def _rope_constants():
    rows = DEC_SEQ // GRID_W
    row = jnp.repeat(jnp.arange(rows, dtype=F32), GRID_W)
    col = jnp.tile(jnp.arange(GRID_W, dtype=F32), rows)
    freqs = ROPE_THETA ** (-jnp.arange(ROPE_PAIRS, dtype=F32) / ROPE_PAIRS)
    ang = jnp.stack([row[:, None] * freqs, col[:, None] * freqs], axis=1)
    ang = jnp.repeat(ang.reshape(DEC_SEQ, 2 * ROPE_PAIRS), 2, axis=1)
    ang = jnp.tile(ang, (1, 128 // HEAD_DIM))
    even = (jnp.arange(128) % 2 == 0)[None, :]
    sin = jnp.sin(ang)
    return {
        "rope_cos": jnp.cos(ang),
        "rope_sin_next": jnp.where(even, -sin, 0.0),
        "rope_sin_prev": jnp.where(even, 0.0, sin),
    }


def _layer_weights(p, l):
    group_of_lane = np.arange(SGU_WIDTH) // SGU_CH
    return {
        "pool_w_bd": _block_diag(p["pool_w"][l]).astype(BF16),
        "pool_scale": p["pool_scale"][l].reshape(1, POOL_WIDTH),
        "sgu_w_stack": p["sgu_w"][l].reshape(SGU_GROUPS * SGU_CHUNK, SGU_CHUNK).astype(BF16),
        "sgu_bias": p["sgu_b"][l].T[:, group_of_lane],
        "gq": jnp.tile(p["q_norm_g"][l], N_HEADS).reshape(1, ATTN_WIDTH),
        "gk": jnp.tile(p["k_norm_g"][l], N_KV_HEADS).reshape(1, KV_WIDTH),
        "w_br_pool": p["w_br_pool"][l].astype(BF16),
        "w_br_fourier": p["w_br_fourier"][l].astype(BF16),
        "w_br_attn": p["w_br_attn"][l].astype(BF16),
        "w_br_sgu": p["w_br_sgu"][l].astype(BF16),
        "w_out": p["w_out"][l].astype(BF16),
        "router_w": p["router_w"][l].T,
        "router_b": p["router_b"][l].reshape(N_EXPERTS, 1),
    }


def kernel(x_prompt, x_sample, cache_k, cache_v, c, c_ctx, w_mod, b_mod, norm_mix_g, norm_ffn_g, w_in, pool_w, pool_scale, q_norm_g, k_norm_g, sgu_w, sgu_b, w_br_pool, w_br_fourier, w_br_attn, w_br_sgu, w_out, router_w, router_b, moe_w_gate, moe_b_gate, moe_w_up, moe_b_up, moe_w_down, moe_b_down, final_norm_g):
    p = dict(w_in=w_in, pool_w=pool_w, pool_scale=pool_scale, q_norm_g=q_norm_g, k_norm_g=k_norm_g,
             sgu_w=sgu_w, sgu_b=sgu_b, w_br_pool=w_br_pool, w_br_fourier=w_br_fourier,
             w_br_attn=w_br_attn, w_br_sgu=w_br_sgu, w_out=w_out, router_w=router_w,
             router_b=router_b, moe_w_gate=moe_w_gate, moe_b_gate=moe_b_gate, moe_w_up=moe_w_up,
             moe_b_up=moe_b_up, moe_w_down=moe_w_down, moe_b_down=moe_b_down)

    cond = jnp.concatenate([c_ctx[None, :], c, jnp.zeros((N_COND - 1 - DEC_BATCH, D_MODEL), F32)])
    mod = _modulation(cond, w_mod, b_mod).reshape(DEPTH, N_COND, 6, D_MODEL)

    ones64 = jnp.asarray(np.kron(np.eye(N_HEADS), np.ones((HEAD_DIM, HEAD_DIM))), BF16)
    consts_ctx = dict(_seq_constants(SEQ), ones64=ones64)
    consts_lat = dict(_seq_constants(DEC_SEQ), ones64=ones64, **_rope_constants())
    ck = cache_k.reshape(DEC_BATCH, DEPTH, PAST_LEN, KV_WIDTH)
    cv = cache_v.reshape(DEC_BATCH, DEPTH, PAST_LEN, KV_WIDTH)

    x_parts = (x_prompt.reshape(T_CTX, D_MODEL), x_sample.reshape(T_LAT, D_MODEL))
    new_k, new_v = [], []
    for l in range(DEPTH):
        lw = _layer_weights(p, l)
        xp, xq, qkv, uv, gates = _in_projection(x_parts, mod[l], norm_mix_g[l], w_in, l)
        pool_c, four_c, sgu_c = _mixers(xp, xq, uv, consts_ctx, lw, SEQ, BATCH, 0, CTX_SEQS_PER_STEP)
        pool_l, four_l, sgu_l = _mixers(xp, xq, uv, consts_lat, lw, DEC_SEQ, DEC_BATCH,
                                        T_CTX // DEC_SEQ, 1)
        attn_c, k_c, v_c = _attention_ctx(qkv, consts_ctx, lw)
        attn_l = _attention_lat(qkv, ck[:, l], cv[:, l], consts_lat, lw)
        new_k.append(k_c.reshape(BATCH, SEQ, N_KV_HEADS, HEAD_DIM))
        new_v.append(v_c.reshape(BATCH, SEQ, N_KV_HEADS, HEAD_DIM))
        x, h, top_idx, top_w = _merge((pool_c, pool_l), (four_c, four_l), (attn_c, attn_l),
                                      (sgu_c, sgu_l), gates, x_parts, mod[l], norm_ffn_g[l], lw)
        slot, tiles = _route(top_idx)
        ys = _routed_ffn(tiles, _dispatch(tiles["rows"], slot, h), l, p)
        x_parts = tuple(_combine(slot, top_w.T, x, mod[l], final_norm_g, ys,
                                 final_norm=(l == DEPTH - 1)))

    y_prompt = x_parts[0].reshape(BATCH, SEQ, D_MODEL)
    y_sample = x_parts[1].reshape(DEC_BATCH, DEC_SEQ, D_MODEL)
    return (y_prompt, y_sample, jnp.stack(new_k, axis=1), jnp.stack(new_v, axis=1))
```

```python
import functools
import math

import numpy as np
import jax
import jax.numpy as jnp
from jax import lax
from jax.experimental import pallas as pl
from jax.experimental.pallas import tpu as pltpu

F32 = jnp.float32
BF16 = jnp.bfloat16

D_MODEL = 1024
BATCH = 32
SEQ = 256
DEPTH = 2
DEC_BATCH = 2
DEC_SEQ = 1024
PAST_LEN = 256
GRID_W = 64
EPS = 1e-6
POOL_GROUPS = 4
POOL_CH = 96
POOL_WIDTH = 384
POOL_WINDOWS = (2, 4, 8, 16)
POOL_PAD = 16
FFT_CH = 96
FFT_WIDTH = 384
N_HEADS = 8
N_KV_HEADS = 2
HEAD_DIM = 64
ATTN_WIDTH = 512
KV_WIDTH = 128
QKV_WIDTH = ATTN_WIDTH + 2 * KV_WIDTH
ROPE_THETA = 10000.0
ROPE_PAIRS = 16
SGU_GROUPS = 4
SGU_CH = 96
SGU_WIDTH = 384
SGU_CHUNK = 128
N_BRANCHES = 4
GATE_WIDTH = N_BRANCHES * D_MODEL
IN_COLS = POOL_WIDTH + FFT_WIDTH + QKV_WIDTH + 2 * SGU_WIDTH + GATE_WIDTH
N_EXPERTS = 32
TOP_K = 4
D_FF = 1024
SWIGLU_LIMIT = 7.0
SWIGLU_ALPHA = 1.702

ROW_GROUP = 8
LANE_TILES = D_MODEL // 128

T_CTX = BATCH * SEQ
T_LAT = DEC_BATCH * DEC_SEQ
T_ALL = T_CTX + T_LAT
N_COND = 8
VMEM_LIMIT = 56 * 1024 * 1024
CAST_ROWS = 32

COL_XP = 0
COL_XQ = COL_XP + POOL_WIDTH
COL_QKV = COL_XQ + FFT_WIDTH
COL_UV = COL_QKV + QKV_WIDTH
COL_GATE = COL_UV + 2 * SGU_WIDTH


def _params(*sem):
    return pltpu.CompilerParams(dimension_semantics=sem, vmem_limit_bytes=VMEM_LIMIT)


def _split_bf16(x):
    hi = x.astype(BF16)
    lo = (x - hi.astype(F32)).astype(BF16)
    return hi, lo


def _dot(a, b):
    return jnp.dot(a, b, preferred_element_type=F32)


def _dot_nt(a, b):
    return lax.dot_general(a, b, (((1,), (1,)), ((), ())), preferred_element_type=F32)


def _group_lane_select(lane, vals, width):
    out = vals[-1]
    for g in range(len(vals) - 2, -1, -1):
        out = jnp.where(lane < (g + 1) * width, vals[g], out)
    return out


def _cond_row(blk, blocks_ctx, blocks_per_latent):
    return jnp.where(blk < blocks_ctx, 0, 1 + (blk - blocks_ctx) // blocks_per_latent)


MOD_TN = 3072


def _mod_kernel(c_ref, w_ref, b_ref, o_ref):
    c = c_ref[...]
    s = c * jax.nn.sigmoid(c)
    sh, sl = _split_bf16(s)
    wh, wl = _split_bf16(w_ref[...])
    o_ref[...] = _dot(sh, wh) + _dot(sh, wl) + _dot(sl, wh) + b_ref[...]


def _modulation(cond, w_mod, b_mod):
    n_cols = 6 * D_MODEL
    return pl.pallas_call(
        _mod_kernel,
        grid=(DEPTH, n_cols // MOD_TN),
        in_specs=[
            pl.BlockSpec((N_COND, D_MODEL), lambda l, j: (0, 0)),
            pl.BlockSpec((None, D_MODEL, MOD_TN), lambda l, j: (l, 0, j)),
            pl.BlockSpec((None, 1, MOD_TN), lambda l, j: (l, 0, j)),
        ],
        out_specs=pl.BlockSpec((None, N_COND, MOD_TN), lambda l, j: (l, 0, j)),
        out_shape=jax.ShapeDtypeStruct((DEPTH, N_COND, n_cols), F32),
        compiler_params=_params("arbitrary", "arbitrary"),
        name="modulation",
    )(cond, w_mod, b_mod.reshape(DEPTH, 1, n_cols))


INPROJ_TM = 256
INPROJ_SEGMENTS = (
    (COL_XP, POOL_WIDTH), (COL_XQ, FFT_WIDTH), (COL_QKV, QKV_WIDTH),
    (COL_UV, 2 * SGU_WIDTH), (COL_GATE, GATE_WIDTH))
INPROJ_CHUNK = 1024


def _ada_norm(x, g, shift, scale):
    xn = x * lax.rsqrt(jnp.mean(x * x, axis=-1, keepdims=True) + EPS)
    return xn * g * (1.0 + scale) + shift


def _token_specs(parts, tm, width):
    if len(parts) == 1:
        return [pl.BlockSpec((tm, width), lambda i: (i, 0))]
    nc = T_CTX // tm
    return [pl.BlockSpec((tm, width), lambda i: (jnp.minimum(i, nc - 1), 0)),
            pl.BlockSpec((tm, width), lambda i: (jnp.maximum(i - nc, 0), 0))]


def _token_load(refs, tm):
    if len(refs) == 1:
        return refs[0][...]
    return jnp.where(pl.program_id(0) < T_CTX // tm, refs[0][...], refs[1][...])


W_IN_CHUNK = 640


def _inproj_kernel(*refs, n_x, layer):
    x_refs, (mod_ref, g_ref, w_hbm) = refs[:n_x], refs[n_x:n_x + 3]
    out_refs, (w_ref, stage, sems) = refs[n_x + 3:-3], refs[-3:]

    @pl.when(pl.program_id(0) == 0)
    def _():
        def chunk(c):
            return pltpu.make_async_copy(w_hbm.at[layer, :, pl.ds(c * W_IN_CHUNK, W_IN_CHUNK)],
                                         stage.at[c % 2], sems.at[c % 2])

        n_chunks = IN_COLS // W_IN_CHUNK
        chunk(0).start()
        for c in range(n_chunks):
            if c + 1 < n_chunks:
                chunk(c + 1).start()
            chunk(c).wait()

            def cast_rows(r, carry, c=c):
                rows = pl.ds(pl.multiple_of(r * CAST_ROWS, CAST_ROWS), CAST_ROWS)
                w_ref[rows, c * W_IN_CHUNK:(c + 1) * W_IN_CHUNK] = stage[c % 2, rows, :].astype(BF16)
                return carry

            lax.fori_loop(0, D_MODEL // CAST_ROWS, cast_rows, 0)

    h = _ada_norm(_token_load(x_refs, INPROJ_TM), g_ref[...], mod_ref[0:1, :], mod_ref[1:2, :])
    hb = h.astype(BF16)
    for (col, width), o_ref in zip(INPROJ_SEGMENTS, out_refs):
        for c0 in range(0, width, INPROJ_CHUNK):
            c1 = min(c0 + INPROJ_CHUNK, width)
            proj = _dot(hb, w_ref[:, col + c0:col + c1])
            o_ref[:, c0:c1] = jax.nn.sigmoid(proj) if col == COL_GATE else proj


def _in_projection(x_parts, mod_l, g, w_in, l):
    tm = INPROJ_TM
    row = functools.partial(_cond_row, blocks_ctx=T_CTX // tm, blocks_per_latent=DEC_SEQ // tm)
    return pl.pallas_call(
        functools.partial(_inproj_kernel, n_x=len(x_parts), layer=l),
        grid=(T_ALL // tm,),
        in_specs=_token_specs(x_parts, tm, D_MODEL) + [
            pl.BlockSpec((None, 6, D_MODEL), lambda i: (row(i), 0, 0)),
            pl.BlockSpec((1, D_MODEL), lambda i: (0, 0)),
            pl.BlockSpec(memory_space=pl.ANY),
        ],
        out_specs=[pl.BlockSpec((tm, w), lambda i: (i, 0)) for _, w in INPROJ_SEGMENTS],
        out_shape=[jax.ShapeDtypeStruct((T_ALL, w), F32) for _, w in INPROJ_SEGMENTS],
        scratch_shapes=[pltpu.VMEM((D_MODEL, IN_COLS), BF16), pltpu.VMEM((2, D_MODEL, W_IN_CHUNK), F32),
                        pltpu.SemaphoreType.DMA((2,))],
        compiler_params=_params("arbitrary"),
        name="in_projection",
    )(*x_parts, mod_l, g.reshape(1, D_MODEL), w_in)


CTX_SEQS_PER_STEP = 4


def _pool_mixer(xp, invcnt, w_bd, scale):
    s = xp.shape[0]
    n = s + 2 * POOL_PAD
    zeros = jnp.zeros((POOL_PAD, POOL_WIDTH), F32)
    xe = jnp.concatenate([zeros, xp, zeros], axis=0)

    def shift(a, k):
        return pltpu.roll(a, k % n, 0)

    s2 = xe + shift(xe, 1)
    s4 = shift(s2, 1) + shift(s2, -1)
    s8 = shift(s4, 2) + shift(s4, -2)
    s16 = shift(s8, 4) + shift(s8, -4)
    lane = lax.broadcasted_iota(jnp.int32, (1, POOL_WIDTH), 1)
    total = _group_lane_select(lane, [s2, s4, s8, s16], POOL_CH)[POOL_PAD:POOL_PAD + s]
    pooled = total * invcnt - xp
    return _dot(pooled.astype(BF16), w_bd) * scale


def _fourier_mixer(xq, f_cos_ch, f_sin_ch, f_pos):
    xb = xq.astype(BF16)
    a = _dot(xb, f_cos_ch).astype(BF16)
    b = _dot(xb, f_sin_ch).astype(BF16)
    return _dot(f_pos, jnp.concatenate([a, b], axis=0))


def _group_mean_sq(x, ones_bd, width):
    hi, lo = _split_bf16(x * x)
    return (_dot(hi, ones_bd) + _dot(lo, ones_bd)) * (1.0 / width)


def _mixers_kernel(xp_ref, xq_ref, uv_ref, invcnt_ref, pool_w_ref, pool_s_ref, fcc_ref, fsc_ref,
                   fpos_ref, ones_ref, sgu_w_ref, sgu_b_ref, pool_o, four_o, sgu_o, *, seq):
    lane = lax.broadcasted_iota(jnp.int32, (1, SGU_WIDTH), 1)
    w_stack = sgu_w_ref[...]
    bias = sgu_b_ref[...]
    for b in range(xp_ref.shape[0] // seq):
        rows = slice(b * seq, (b + 1) * seq)
        pool_o[rows, :] = _pool_mixer(xp_ref[rows, :], invcnt_ref[...], pool_w_ref[...],
                                      pool_s_ref[...]).astype(BF16)
        four_o[rows, :] = _fourier_mixer(xq_ref[rows, :], fcc_ref[...], fsc_ref[...],
                                         fpos_ref[...]).astype(BF16)

        act = jax.nn.gelu(uv_ref[rows, :], approximate=True)
        u = act[:, :SGU_WIDTH]
        v = act[:, SGU_WIDTH:]
        vg = (v * lax.rsqrt(_group_mean_sq(v, ones_ref[...], SGU_CH) + EPS)).astype(BF16)
        for n in range(seq // SGU_CHUNK):
            chunk = slice(n * SGU_CHUNK, (n + 1) * SGU_CHUNK)
            r = _dot(w_stack, vg[chunk])
            per_group = [r[g * SGU_CHUNK:(g + 1) * SGU_CHUNK] for g in range(SGU_GROUPS)]
            spatial = _group_lane_select(lane, per_group, SGU_CH) + bias
            out_rows = slice(b * seq + n * SGU_CHUNK, b * seq + (n + 1) * SGU_CHUNK)
            sgu_o[out_rows, :] = (u[chunk] * spatial).astype(BF16)


def _mixers(xp, xq, uv, consts, lw, seq, n_seq, block0, per_step):
    rows = per_step * seq
    full = lambda shape: pl.BlockSpec(shape, lambda b: (0,) * len(shape))
    tok = lambda w: pl.BlockSpec((rows, w), lambda b: (block0 + b, 0))
    out = lambda: pl.BlockSpec((rows, POOL_WIDTH), lambda b: (b, 0))
    return pl.pallas_call(
        functools.partial(_mixers_kernel, seq=seq),
        grid=(n_seq // per_step,),
        in_specs=[
            tok(POOL_WIDTH), tok(FFT_WIDTH), tok(2 * SGU_WIDTH),
            full((seq, POOL_WIDTH)), full((POOL_WIDTH, POOL_WIDTH)), full((1, POOL_WIDTH)),
            full((FFT_WIDTH, FFT_WIDTH)), full((FFT_WIDTH, FFT_WIDTH)), full((seq, 2 * seq)),
            full((SGU_WIDTH, SGU_WIDTH)), full((SGU_GROUPS * SGU_CHUNK, SGU_CHUNK)),
            full((SGU_CHUNK, SGU_WIDTH)),
        ],
        out_specs=[out(), out(), out()],
        out_shape=[jax.ShapeDtypeStruct((n_seq * seq, POOL_WIDTH), BF16)] * 3,
        compiler_params=_params("arbitrary"),
        name=f"mixers_s{seq}",
    )(xp, xq, uv, consts["invcnt"], lw["pool_w_bd"], lw["pool_scale"], consts["f_cos_ch"],
      consts["f_sin_ch"], consts["f_pos"], consts["ones96"], lw["sgu_w_stack"], lw["sgu_bias"])


def _head_norm(x, ones_bd, g):
    return x * lax.rsqrt(_group_mean_sq(x, ones_bd, HEAD_DIM) + EPS) * g


def _rope(x, cos, sin_next, sin_prev):
    cols = []
    for c in range(x.shape[1] // 128):
        xc = x[:, c * 128:(c + 1) * 128]
        nxt = pltpu.roll(xc, 127, 1)
        prv = pltpu.roll(xc, 1, 1)
        cols.append(xc * cos + nxt * sin_next + prv * sin_prev)
    return cols[0] if len(cols) == 1 else jnp.concatenate(cols, axis=1)


def _attend(q, keys, vals, o_ref, stack):
    sq = q.shape[0]
    qb = (q * (HEAD_DIM ** -0.5)).astype(BF16)
    group = N_HEADS // N_KV_HEADS
    head = lambda a, h: a[:, h * HEAD_DIM:(h + 1) * HEAD_DIM]
    for h0 in range(0, N_HEADS, stack):
        j = h0 // group
        qs = head(qb, h0) if stack == 1 else jnp.concatenate(
            [head(qb, h0 + g) for g in range(stack)], axis=0)
        s = _dot_nt(qs, head(keys, j))
        p = jnp.exp(s - jnp.max(s, axis=-1, keepdims=True))
        denom = jnp.sum(p, axis=-1, keepdims=True)
        o = (_dot(p.astype(BF16), head(vals, j)) / denom).astype(BF16)
        for g in range(stack):
            o_ref[:, (h0 + g) * HEAD_DIM:(h0 + g + 1) * HEAD_DIM] = o[g * sq:(g + 1) * sq]


def _attn_ctx_kernel(qkv_ref, gq_ref, gk_ref, ones_ref, o_ref, k_ref, v_ref):
    qkv = qkv_ref[...]
    ones = ones_ref[...]
    q = _head_norm(qkv[:, :ATTN_WIDTH], ones, gq_ref[...])
    k = _head_norm(qkv[:, ATTN_WIDTH:ATTN_WIDTH + KV_WIDTH], ones[:KV_WIDTH, :KV_WIDTH], gk_ref[...])
    v = qkv[:, ATTN_WIDTH + KV_WIDTH:]
    k_ref[...] = k
    v_ref[...] = v
    _attend(q, k.astype(BF16), v.astype(BF16), o_ref, stack=2)


def _attention_ctx(qkv, consts, lw):
    full = lambda shape: pl.BlockSpec(shape, lambda b: (0,) * len(shape))
    return pl.pallas_call(
        _attn_ctx_kernel,
        grid=(BATCH,),
        in_specs=[pl.BlockSpec((SEQ, QKV_WIDTH), lambda b: (b, 0)),
                  full((1, ATTN_WIDTH)), full((1, KV_WIDTH)), full((ATTN_WIDTH, ATTN_WIDTH))],
        out_specs=[pl.BlockSpec((SEQ, ATTN_WIDTH), lambda b: (b, 0)),
                   pl.BlockSpec((SEQ, KV_WIDTH), lambda b: (b, 0)),
                   pl.BlockSpec((SEQ, KV_WIDTH), lambda b: (b, 0))],
        out_shape=[jax.ShapeDtypeStruct((T_CTX, ATTN_WIDTH), BF16),
                   jax.ShapeDtypeStruct((T_CTX, KV_WIDTH), F32),
                   jax.ShapeDtypeStruct((T_CTX, KV_WIDTH), F32)],
        compiler_params=_params("arbitrary"),
        name="attention_ctx",
    )(qkv, lw["gq"], lw["gk"], consts["ones64"])


LAT_QBLK = 512


def _attn_lat_kernel(q_ref, kv_ref, ck_ref, cv_ref, gq_ref, gk_ref, ones_ref, cos_q, sn_q, sp_q,
                     cos_k, sn_k, sp_k, o_ref, keys, vals):
    ones = ones_ref[...]

    @pl.when(pl.program_id(1) == 0)
    def _():
        kv = kv_ref[...]
        k = _head_norm(kv[:, :KV_WIDTH], ones[:KV_WIDTH, :KV_WIDTH], gk_ref[...])
        keys[0:DEC_SEQ, :] = _rope(k, cos_k[...], sn_k[...], sp_k[...]).astype(BF16)
        keys[DEC_SEQ:, :] = ck_ref[...].astype(BF16)
        vals[0:DEC_SEQ, :] = kv[:, KV_WIDTH:].astype(BF16)
        vals[DEC_SEQ:, :] = cv_ref[...].astype(BF16)

    q = _head_norm(q_ref[...], ones, gq_ref[...])
    q = _rope(q, cos_q[...], sn_q[...], sp_q[...])
    _attend(q, keys[...], vals[...], o_ref, stack=1)


def _attention_lat(qkv, cache_k_l, cache_v_l, consts, lw):
    nq = DEC_SEQ // LAT_QBLK
    q0 = T_CTX // LAT_QBLK
    s0 = T_CTX // DEC_SEQ
    full = lambda shape: pl.BlockSpec(shape, lambda b, j: (0,) * len(shape))
    rope_q = lambda: pl.BlockSpec((LAT_QBLK, 128), lambda b, j: (j, 0))
    rope_k = lambda: pl.BlockSpec((DEC_SEQ, 128), lambda b, j: (0, 0))
    return pl.pallas_call(
        _attn_lat_kernel,
        grid=(DEC_BATCH, nq),
        in_specs=[
            pl.BlockSpec((LAT_QBLK, ATTN_WIDTH), lambda b, j: (q0 + b * nq + j, 0)),
            pl.BlockSpec((DEC_SEQ, 2 * KV_WIDTH), lambda b, j: (s0 + b, ATTN_WIDTH // (2 * KV_WIDTH))),
            pl.BlockSpec((None, PAST_LEN, KV_WIDTH), lambda b, j: (b, 0, 0)),
            pl.BlockSpec((None, PAST_LEN, KV_WIDTH), lambda b, j: (b, 0, 0)),
            full((1, ATTN_WIDTH)), full((1, KV_WIDTH)), full((ATTN_WIDTH, ATTN_WIDTH)),
            rope_q(), rope_q(), rope_q(), rope_k(), rope_k(), rope_k(),
        ],
        out_specs=pl.BlockSpec((LAT_QBLK, ATTN_WIDTH), lambda b, j: (b * nq + j, 0)),
        out_shape=jax.ShapeDtypeStruct((T_LAT, ATTN_WIDTH), BF16),
        scratch_shapes=[pltpu.VMEM((DEC_SEQ + PAST_LEN, KV_WIDTH), BF16),
                        pltpu.VMEM((DEC_SEQ + PAST_LEN, KV_WIDTH), BF16)],
        compiler_params=_params("arbitrary", "arbitrary"),
        name="attention_lat",
    )(qkv, qkv, cache_k_l, cache_v_l, lw["gq"], lw["gk"], consts["ones64"],
      consts["rope_cos"], consts["rope_sin_next"], consts["rope_sin_prev"],
      consts["rope_cos"], consts["rope_sin_next"], consts["rope_sin_prev"])


MERGE_TM = 512


def _merge_kernel(*refs, n_x):
    branch_refs, gate_ref, x_refs = refs[:8], refs[8], refs[9:9 + n_x]
    (mod_ref, g_ref, wp_ref, wf_ref, wa_ref, ws_ref, wo_ref, rw_ref, rb_ref,
     x_o, h_o, idx_o, wgt_o) = refs[9 + n_x:]
    merged = None
    for i, w_ref in enumerate((wp_ref, wf_ref, wa_ref, ws_ref)):
        br = _dot(_token_load(branch_refs[2 * i:2 * i + 2], MERGE_TM), w_ref[...])
        term = gate_ref[:, i * D_MODEL:(i + 1) * D_MODEL] * br
        merged = term if merged is None else merged + term
    mix = _dot(merged.astype(BF16), wo_ref[...])
    x = _token_load(x_refs, MERGE_TM) + mod_ref[2:3, :] * mix
    x_o[...] = x
    h = _ada_norm(x, g_ref[...], mod_ref[3:4, :], mod_ref[4:5, :])
    for c in range(LANE_TILES):
        h_o[:, c, :, :] = h[:, c * 128:(c + 1) * 128].reshape(MERGE_TM // ROW_GROUP, ROW_GROUP, 128)

    hh, hl = _split_bf16(h)
    rh, rl = _split_bf16(rw_ref[...])
    logits = _dot_nt(rh, hh) + _dot_nt(rl, hh) + _dot_nt(rh, hl) + rb_ref[...]
    expert = lax.broadcasted_iota(jnp.int32, logits.shape, 0).astype(F32)
    work = logits
    top = jnp.max(logits, axis=0, keepdims=True)
    idx, wgt = [], []
    denom = jnp.zeros_like(top)
    for _ in range(TOP_K):
        m = jnp.max(work, axis=0, keepdims=True)
        first = jnp.min(jnp.where(work == m, expert, float(N_EXPERTS)), axis=0, keepdims=True)
        e = jnp.exp(m - top)
        idx.append(first)
        wgt.append(e)
        denom = denom + e
        work = jnp.where(expert == first, -jnp.inf, work)
    idx_o[...] = jnp.concatenate(idx, axis=0).astype(jnp.int32)
    wgt_o[...] = jnp.concatenate(wgt, axis=0) / denom


def _merge(pool, four, attn, sgu, gates, x_parts, mod_l, g_ffn, lw):
    tm = MERGE_TM
    row = functools.partial(_cond_row, blocks_ctx=T_CTX // tm, blocks_per_latent=DEC_SEQ // tm)
    full = lambda shape: pl.BlockSpec(shape, lambda i: (0,) * len(shape))
    tok = lambda w: pl.BlockSpec((tm, w), lambda i: (i, 0))
    branch_specs = []
    for pair, width in ((pool, POOL_WIDTH), (four, FFT_WIDTH), (attn, ATTN_WIDTH), (sgu, SGU_WIDTH)):
        branch_specs += _token_specs(pair, tm, width)
    return pl.pallas_call(
        functools.partial(_merge_kernel, n_x=len(x_parts)),
        grid=(T_ALL // tm,),
        in_specs=branch_specs + [tok(GATE_WIDTH)] + _token_specs(x_parts, tm, D_MODEL) + [
            pl.BlockSpec((None, 6, D_MODEL), lambda i: (row(i), 0, 0)),
            full((1, D_MODEL)),
            full((POOL_WIDTH, D_MODEL)), full((FFT_WIDTH, D_MODEL)), full((ATTN_WIDTH, D_MODEL)),
            full((SGU_WIDTH, D_MODEL)), full((D_MODEL, D_MODEL)),
            full((N_EXPERTS, D_MODEL)), full((N_EXPERTS, 1)),
        ],
        out_specs=[tok(D_MODEL),
                   pl.BlockSpec((tm // ROW_GROUP, LANE_TILES, ROW_GROUP, 128), lambda i: (i, 0, 0, 0)),
                   pl.BlockSpec((TOP_K, tm), lambda i: (0, i)), pl.BlockSpec((TOP_K, tm), lambda i: (0, i))],
        out_shape=[jax.ShapeDtypeStruct((T_ALL, D_MODEL), F32),
                   jax.ShapeDtypeStruct((T_ALL // ROW_GROUP, LANE_TILES, ROW_GROUP, 128), F32),
                   jax.ShapeDtypeStruct((TOP_K, T_ALL), jnp.int32),
                   jax.ShapeDtypeStruct((TOP_K, T_ALL), F32)],
        compiler_params=_params("arbitrary"),
        name="merge_router",
    )(*pool, *four, *attn, *sgu, gates, *x_parts, mod_l, g_ffn.reshape(1, D_MODEL),
      lw["w_br_pool"], lw["w_br_fourier"], lw["w_br_attn"], lw["w_br_sgu"], lw["w_out"],
      lw["router_w"], lw["router_b"])


N_PAIRS = T_ALL * TOP_K
FFN_TM = 512
FFN_ROW_OPTIONS = (128, 256, 384, 512)
N_ROW_TILES = N_PAIRS // FFN_TM + N_EXPERTS
N_SLOTS = N_ROW_TILES * FFN_TM
ROUTE_TB = 512


def _route(top_idx):
    experts = jnp.arange(N_EXPERTS, dtype=jnp.int32)
    onehot = (top_idx.T[:, :, None] == experts[None, None, :]).astype(jnp.int32)
    per_token = jnp.sum(onehot, axis=1)
    csum = jnp.cumsum(per_token, axis=0)
    counts = csum[-1]
    tiles = (counts + FFN_TM - 1) // FFN_TM
    tile_end = jnp.cumsum(tiles)
    row_start = (tile_end - tiles) * FFN_TM
    first_row = csum - per_token + row_start[None, :]
    slot = jnp.sum(onehot * first_row[:, None, :], axis=2)
    tile_ids = jnp.arange(N_ROW_TILES, dtype=jnp.int32)
    tile_expert = jnp.sum((tile_end[None, :] <= tile_ids[:, None]).astype(jnp.int32), axis=1)
    tile_expert = jnp.minimum(tile_expert, N_EXPERTS - 1)
    rows_before = (tile_ids - (tile_end - tiles)[tile_expert]) * FFN_TM
    tile_rows = jnp.where(tile_ids < tile_end[-1],
                          jnp.clip(counts[tile_expert] - rows_before, 0, FFN_TM), 0)
    slot = slot.astype(jnp.int32).reshape(T_ALL // ROUTE_TB, 1, ROUTE_TB * TOP_K)
    used = (tiles > 0).astype(jnp.int32)
    rank = jnp.cumsum(used) - 1
    group_expert = jnp.sum(jnp.where((rank[None, :] == experts[:, None]) & (used[None, :] > 0),
                                     experts[None, :], 0), axis=1)
    tiles_info = dict(expert=tile_expert.astype(jnp.int32), rows=tile_rows.astype(jnp.int32),
                      group=rank[tile_expert].astype(jnp.int32),
                      group_expert=group_expert.astype(jnp.int32),
                      n_groups=jnp.sum(used).reshape(1).astype(jnp.int32),
                      n_used=tile_end[-1:].astype(jnp.int32))
    return slot, tiles_info


def _start_pair_rows(copy):
    def body(g, carry):
        for u in range(ROW_GROUP):
            for k in range(TOP_K):
                copy(g, u, k).start(priority=k % 2)
        return carry

    lax.fori_loop(0, ROUTE_TB // ROW_GROUP, body, 0)


def _pair_slot(slot_ref, g, u, k):
    return slot_ref[0, (g * ROW_GROUP + u) * TOP_K + k]


def _dispatch_kernel(tr_ref, slot_ref, h_ref, xs_out, zeros, sem, zero_sem):
    @pl.when(pl.program_id(0) == 0)
    def _():
        zeros[...] = jnp.zeros_like(zeros)

        def for_each_unfilled_tile(fn):
            group = FFN_ROW_OPTIONS[0]

            def body(i, carry):
                for full_groups in range(FFN_TM // group):
                    first = full_groups * group

                    @pl.when(tr_ref[i] // group == full_groups)
                    def _():
                        fn(pltpu.make_async_copy(zeros.at[pl.ds(0, FFN_TM - first)],
                                                 xs_out.at[pl.ds(i * FFN_TM + first, FFN_TM - first)],
                                                 zero_sem))
                return carry

            lax.fori_loop(0, N_ROW_TILES, body, 0)

        for_each_unfilled_tile(lambda c: c.start())
        for_each_unfilled_tile(lambda c: c.wait())

    def copy(g, u, k):
        return pltpu.make_async_copy(h_ref.at[g, :, u, :], xs_out.at[_pair_slot(slot_ref, g, u, k)], sem)

    _start_pair_rows(copy)
    for _ in range(TOP_K):
        pltpu.make_async_copy(xs_out.at[pl.ds(0, ROUTE_TB)], xs_out.at[pl.ds(0, ROUTE_TB)], sem).wait()


def _dispatch(tile_rows, slot, h):
    return pl.pallas_call(
        _dispatch_kernel,
        grid_spec=pltpu.PrefetchScalarGridSpec(
            num_scalar_prefetch=1,
            grid=(T_ALL // ROUTE_TB,),
            in_specs=[
                pl.BlockSpec((None, 1, ROUTE_TB * TOP_K), lambda i, tr: (i, 0, 0),
                             memory_space=pltpu.SMEM),
                pl.BlockSpec((ROUTE_TB // ROW_GROUP, LANE_TILES, ROW_GROUP, 128),
                             lambda i, tr: (i, 0, 0, 0)),
            ],
            out_specs=pl.BlockSpec(memory_space=pl.ANY),
            scratch_shapes=[pltpu.VMEM((FFN_TM, LANE_TILES, 128), F32), pltpu.SemaphoreType.DMA(()),
                            pltpu.SemaphoreType.DMA(())],
        ),
        out_shape=jax.ShapeDtypeStruct((N_SLOTS, LANE_TILES, 128), F32),
        compiler_params=_params("arbitrary"),
        name="moe_dispatch",
    )(tile_rows, slot, h)


def _expert_ffn(xb, wg, bg, wu, bu, wd, bd):
    gate = jnp.minimum(_dot(xb, wg) + bg, SWIGLU_LIMIT)
    up = jnp.clip(_dot(xb, wu) + bu, -SWIGLU_LIMIT, SWIGLU_LIMIT)
    glu = gate * jax.nn.sigmoid(SWIGLU_ALPHA * gate)
    return _dot(((up + 1.0) * glu).astype(BF16), wd) + bd


def _ffn_kernel(te_ref, tr_ref, tg_ref, ge_ref, ng_ref, nu_ref, xs_ref, wg_hbm, bg_ref, wu_hbm,
                bu_ref, wd_hbm, bd_ref, ys_ref, stage, w_bf16, xb, sems, *, layer):
    i = pl.program_id(0)
    n_rows = tr_ref[i]
    group = tg_ref[i]
    slot = group % 2

    def weight_copies(g, s):
        e = ge_ref[g]
        return [pltpu.make_async_copy(w.at[layer, e], stage.at[s, j], sems.at[s])
                for j, w in enumerate((wg_hbm, wu_hbm, wd_hbm))]

    def lane_tile(ref, c, rows):
        return ref.at[pl.ds(c, rows, stride=LANE_TILES), :]

    def ffn(rows):
        for c in range(LANE_TILES):
            xb[:rows, c * 128:(c + 1) * 128] = lane_tile(xs_ref, c, rows)[...].astype(BF16)
        y = _expert_ffn(xb[:rows, :], w_bf16[0], bg_ref[...], w_bf16[1], bu_ref[...], w_bf16[2],
                        bd_ref[...])
        for c in range(LANE_TILES):
            lane_tile(ys_ref, c, rows)[...] = y[:, c * 128:(c + 1) * 128]
        if rows < FFN_TM:
            ys_ref[rows * LANE_TILES:, :] = jnp.zeros(((FFN_TM - rows) * LANE_TILES, 128), F32)

    @pl.when(i == 0)
    def _():
        for c in weight_copies(0, 0):
            c.start()

    @pl.when((n_rows > 0) & ((i == 0) | (te_ref[i] != te_ref[jnp.maximum(i - 1, 0)])))
    def _():
        for c in weight_copies(group, slot):
            c.wait()
        def cast_rows(r, carry):
            rows = pl.ds(pl.multiple_of(r * CAST_ROWS, CAST_ROWS), CAST_ROWS)
            for j in range(3):
                w_bf16[j, rows, :] = stage[slot, j, rows, :].astype(BF16)
            return carry

        lax.fori_loop(0, D_MODEL // CAST_ROWS, cast_rows, 0)

        @pl.when(group + 1 < ng_ref[0])
        def _():
            for c in weight_copies(group + 1, 1 - slot):
                c.start()

    @pl.when(n_rows == 0)
    def _():
        ys_ref[...] = jnp.zeros_like(ys_ref)

    lower = 0
    for rows in FFN_ROW_OPTIONS:
        @pl.when((n_rows > lower) & (n_rows <= rows))
        def _(rows=rows):
            ffn(rows)
        lower = rows


def _routed_ffn(tiles, xs, l, p):
    n_prefetch = 6
    rows = pl.BlockSpec((FFN_TM * LANE_TILES, 128),
                        lambda i, te, tr, tg, ge, ng, nu: (jnp.minimum(i, nu[0] - 1), 0))
    bias_spec = lambda w: pl.BlockSpec((None, None, 1, w),
                                       lambda i, te, tr, tg, ge, ng, nu: (l, te[i], 0, 0))
    bias = lambda a: a.reshape(DEPTH, N_EXPERTS, 1, a.shape[-1])
    hbm = pl.BlockSpec(memory_space=pl.ANY)
    return pl.pallas_call(
        functools.partial(_ffn_kernel, layer=l),
        grid_spec=pltpu.PrefetchScalarGridSpec(
            num_scalar_prefetch=n_prefetch,
            grid=(N_ROW_TILES,),
            in_specs=[rows, hbm, bias_spec(D_FF), hbm, bias_spec(D_FF), hbm, bias_spec(D_MODEL)],
            out_specs=pl.BlockSpec((FFN_TM * LANE_TILES, 128),
                                   lambda i, te, tr, tg, ge, ng, nu: (i, 0)),
            scratch_shapes=[pltpu.VMEM((2, 3, D_MODEL, D_FF), F32), pltpu.VMEM((3, D_MODEL, D_FF), BF16),
                            pltpu.VMEM((FFN_TM, D_MODEL), BF16), pltpu.SemaphoreType.DMA((2,))],
        ),
        out_shape=jax.ShapeDtypeStruct((N_SLOTS * LANE_TILES, 128), F32),
        compiler_params=_params("arbitrary"),
        name="moe_ffn",
    )(tiles["expert"], tiles["rows"], tiles["group"], tiles["group_expert"], tiles["n_groups"],
      tiles["n_used"], xs.reshape(N_SLOTS * LANE_TILES, 128), p["moe_w_gate"], bias(p["moe_b_gate"]),
      p["moe_w_up"], bias(p["moe_b_up"]), p["moe_w_down"], bias(p["moe_b_down"]))


def _combine_kernel(slot_ref, slot_next_ref, w_ref, x_ref, mod_ref, fg_ref, ys_ref, *rest, final_norm):
    out_refs, (acc, buf, sems) = rest[:-3], rest[-3:]
    i = pl.program_id(0)
    cur = i % 2

    def copy(slots, b, g, u, k):
        return pltpu.make_async_copy(ys_ref.at[_pair_slot(slots, g, u, k)], buf.at[b, k, g, :, u, :],
                                     sems.at[b])

    @pl.when(i == 0)
    def _():
        _start_pair_rows(functools.partial(copy, slot_ref, 0))

    @pl.when(i + 1 < pl.num_programs(0))
    def _():
        _start_pair_rows(functools.partial(copy, slot_next_ref, 1 - cur))

    for k in range(TOP_K):
        pltpu.make_async_copy(ys_ref.at[pl.ds(0, ROUTE_TB)], ys_ref.at[pl.ds(0, ROUTE_TB)],
                              sems.at[cur]).wait()

    def rows_of(k, c):
        return buf[cur, k, :, c, :, :].reshape(ROUTE_TB, 128)

    w = w_ref[...]
    sum_sq = jnp.zeros((ROUTE_TB, 1), F32)
    for c in range(LANE_TILES):
        cols = slice(c * 128, (c + 1) * 128)
        mix = w[:, 0:1] * rows_of(0, c)
        for k in range(1, TOP_K):
            mix = mix + w[:, k:k + 1] * rows_of(k, c)
        piece = x_ref[:, cols] + mod_ref[5:6, cols] * mix
        sum_sq = sum_sq + jnp.sum(piece * piece, axis=-1, keepdims=True)
        acc[:, cols] = piece

    if not final_norm:
        out_refs[0][...] = acc[...]
        return
    normed = acc[...] * lax.rsqrt(sum_sq * (1.0 / D_MODEL) + EPS) * fg_ref[...]

    @pl.when(i < T_CTX // ROUTE_TB)
    def _():
        out_refs[0][...] = normed

    @pl.when(i >= T_CTX // ROUTE_TB)
    def _():
        out_refs[1][...] = normed


def _combine(slot, top_w, x, mod_l, final_g, ys, final_norm):
    tb = ROUTE_TB
    nb = T_ALL // tb
    nc = T_CTX // tb
    row = functools.partial(_cond_row, blocks_ctx=nc, blocks_per_latent=DEC_SEQ // tb)
    tok = lambda w: pl.BlockSpec((tb, w), lambda i: (i, 0))
    slots = lambda ahead: pl.BlockSpec((None, 1, tb * TOP_K),
                                       lambda i: (jnp.minimum(i + ahead, nb - 1), 0, 0),
                                       memory_space=pltpu.SMEM)
    if final_norm:
        out_specs = [pl.BlockSpec((tb, D_MODEL), lambda i: (jnp.minimum(i, nc - 1), 0)),
                     pl.BlockSpec((tb, D_MODEL), lambda i: (jnp.maximum(i - nc, 0), 0))]
        out_shape = [jax.ShapeDtypeStruct((T_CTX, D_MODEL), F32),
                     jax.ShapeDtypeStruct((T_LAT, D_MODEL), F32)]
    else:
        out_specs = [tok(D_MODEL)]
        out_shape = [jax.ShapeDtypeStruct((T_ALL, D_MODEL), F32)]
    return pl.pallas_call(
        functools.partial(_combine_kernel, final_norm=final_norm),
        grid=(nb,),
        in_specs=[
            slots(0), slots(1), tok(TOP_K), tok(D_MODEL),
            pl.BlockSpec((None, 6, D_MODEL), lambda i: (row(i), 0, 0)),
            pl.BlockSpec((1, D_MODEL), lambda i: (0, 0)),
            pl.BlockSpec(memory_space=pl.ANY),
        ],
        out_specs=out_specs,
        out_shape=out_shape,
        scratch_shapes=[pltpu.VMEM((tb, D_MODEL), F32),
                        pltpu.VMEM((2, TOP_K, tb // ROW_GROUP, LANE_TILES, ROW_GROUP, 128), F32),
                        pltpu.SemaphoreType.DMA((2,))],
        compiler_params=_params("arbitrary"),
        name="moe_combine",
    )(slot, slot, top_w, x, mod_l, final_g.reshape(1, D_MODEL), ys.reshape(N_SLOTS, LANE_TILES, 128))


def _block_diag(blocks):
    g, a, b = blocks.shape
    eye = jnp.eye(g, dtype=blocks.dtype)
    return (eye[:, None, :, None] * blocks[:, :, None, :]).reshape(g * a, g * b)


def _dft_tables(n):
    k = np.arange(n, dtype=np.int64)
    ang = 2.0 * np.pi * ((k[:, None] * k[None, :]) % n).astype(np.float64) / n
    return np.cos(ang) / math.sqrt(n), np.sin(ang) / math.sqrt(n)


def _seq_constants(seq):
    t = np.arange(seq)[:, None]
    win = np.array(POOL_WINDOWS)[None, :]
    lo = np.clip(t - win // 2, 0, seq)
    hi = np.clip(t - win // 2 + win, 0, seq)
    invcnt = np.repeat(1.0 / (hi - lo).astype(np.float64), POOL_CH, axis=1)
    cos_p, sin_p = _dft_tables(seq)
    cos_c, sin_c = _dft_tables(FFT_CH)
    eye = np.eye(POOL_GROUPS)
    as_bf16 = lambda a: jnp.asarray(a, F32).astype(BF16)
    return {
        "invcnt": jnp.asarray(invcnt, F32),
        "f_pos": as_bf16(np.concatenate([cos_p, -sin_p], axis=1)),
        "f_cos_ch": as_bf16(np.kron(eye, cos_c)),
        "f_sin_ch": as_bf16(np.kron(eye, sin_c)),
        "ones96": as_bf16(np.kron(eye, np.ones((SGU_CH, SGU_CH)))),
    }


def _rope_constants():
    rows = DEC_SEQ // GRID_W
    row = jnp.repeat(jnp.arange(rows, dtype=F32), GRID_W)
    col = jnp.tile(jnp.arange(GRID_W, dtype=F32), rows)
    freqs = ROPE_THETA ** (-jnp.arange(ROPE_PAIRS, dtype=F32) / ROPE_PAIRS)
    ang = jnp.stack([row[:, None] * freqs, col[:, None] * freqs], axis=1)
    ang = jnp.repeat(ang.reshape(DEC_SEQ, 2 * ROPE_PAIRS), 2, axis=1)
    ang = jnp.tile(ang, (1, 128 // HEAD_DIM))
    even = (jnp.arange(128) % 2 == 0)[None, :]
    sin = jnp.sin(ang)
    return {
        "rope_cos": jnp.cos(ang),
        "rope_sin_next": jnp.where(even, -sin, 0.0),
        "rope_sin_prev": jnp.where(even, 0.0, sin),
    }


def _layer_weights(p, l):
    group_of_lane = np.arange(SGU_WIDTH) // SGU_CH
    return {
        "pool_w_bd": _block_diag(p["pool_w"][l]).astype(BF16),
        "pool_scale": p["pool_scale"][l].reshape(1, POOL_WIDTH),
        "sgu_w_stack": p["sgu_w"][l].reshape(SGU_GROUPS * SGU_CHUNK, SGU_CHUNK).astype(BF16),
        "sgu_bias": p["sgu_b"][l].T[:, group_of_lane],
        "gq": jnp.tile(p["q_norm_g"][l], N_HEADS).reshape(1, ATTN_WIDTH),
        "gk": jnp.tile(p["k_norm_g"][l], N_KV_HEADS).reshape(1, KV_WIDTH),
        "w_br_pool": p["w_br_pool"][l].astype(BF16),
        "w_br_fourier": p["w_br_fourier"][l].astype(BF16),
        "w_br_attn": p["w_br_attn"][l].astype(BF16),
        "w_br_sgu": p["w_br_sgu"][l].astype(BF16),
        "w_out": p["w_out"][l].astype(BF16),
        "router_w": p["router_w"][l].T,
        "router_b": p["router_b"][l].reshape(N_EXPERTS, 1),
    }


def kernel(x_prompt, x_sample, cache_k, cache_v, c, c_ctx, w_mod, b_mod, norm_mix_g, norm_ffn_g, w_in, pool_w, pool_scale, q_norm_g, k_norm_g, sgu_w, sgu_b, w_br_pool, w_br_fourier, w_br_attn, w_br_sgu, w_out, router_w, router_b, moe_w_gate, moe_b_gate, moe_w_up, moe_b_up, moe_w_down, moe_b_down, final_norm_g):
    p = dict(w_in=w_in, pool_w=pool_w, pool_scale=pool_scale, q_norm_g=q_norm_g, k_norm_g=k_norm_g,
             sgu_w=sgu_w, sgu_b=sgu_b, w_br_pool=w_br_pool, w_br_fourier=w_br_fourier,
             w_br_attn=w_br_attn, w_br_sgu=w_br_sgu, w_out=w_out, router_w=router_w,
             router_b=router_b, moe_w_gate=moe_w_gate, moe_b_gate=moe_b_gate, moe_w_up=moe_w_up,
             moe_b_up=moe_b_up, moe_w_down=moe_w_down, moe_b_down=moe_b_down)

    cond = jnp.concatenate([c_ctx[None, :], c, jnp.zeros((N_COND - 1 - DEC_BATCH, D_MODEL), F32)])
    mod = _modulation(cond, w_mod, b_mod).reshape(DEPTH, N_COND, 6, D_MODEL)

    ones64 = jnp.asarray(np.kron(np.eye(N_HEADS), np.ones((HEAD_DIM, HEAD_DIM))), BF16)
    consts_ctx = dict(_seq_constants(SEQ), ones64=ones64)
    consts_lat = dict(_seq_constants(DEC_SEQ), ones64=ones64, **_rope_constants())
    ck = cache_k.reshape(DEC_BATCH, DEPTH, PAST_LEN, KV_WIDTH)
    cv = cache_v.reshape(DEC_BATCH, DEPTH, PAST_LEN, KV_WIDTH)

    x_parts = (x_prompt.reshape(T_CTX, D_MODEL), x_sample.reshape(T_LAT, D_MODEL))
    new_k, new_v = [], []
    for l in range(DEPTH):
        lw = _layer_weights(p, l)
        xp, xq, qkv, uv, gates = _in_projection(x_parts, mod[l], norm_mix_g[l], w_in, l)
        pool_c, four_c, sgu_c = _mixers(xp, xq, uv, consts_ctx, lw, SEQ, BATCH, 0, CTX_SEQS_PER_STEP)
        pool_l, four_l, sgu_l = _mixers(xp, xq, uv, consts_lat, lw, DEC_SEQ, DEC_BATCH,
                                        T_CTX // DEC_SEQ, 1)
        attn_c, k_c, v_c = _attention_ctx(qkv, consts_ctx, lw)
        attn_l = _attention_lat(qkv, ck[:, l], cv[:, l], consts_lat, lw)
        new_k.append(k_c.reshape(BATCH, SEQ, N_KV_HEADS, HEAD_DIM))
        new_v.append(v_c.reshape(BATCH, SEQ, N_KV_HEADS, HEAD_DIM))
        x, h, top_idx, top_w = _merge((pool_c, pool_l), (four_c, four_l), (attn_c, attn_l),
                                      (sgu_c, sgu_l), gates, x_parts, mod[l], norm_ffn_g[l], lw)
        slot, tiles = _route(top_idx)
        ys = _routed_ffn(tiles, _dispatch(tiles["rows"], slot, h), l, p)
        x_parts = tuple(_combine(slot, top_w.T, x, mod[l], final_norm_g, ys,
                                 final_norm=(l == DEPTH - 1)))

    y_prompt = x_parts[0].reshape(BATCH, SEQ, D_MODEL)
    y_sample = x_parts[1].reshape(DEC_BATCH, DEC_SEQ, D_MODEL)
    return (y_prompt, y_sample, jnp.stack(new_k, axis=1), jnp.stack(new_v, axis=1))
```

```python
import functools
import math

import numpy as np
import jax
import jax.numpy as jnp
from jax import lax
from jax.experimental import pallas as pl
from jax.experimental.pallas import tpu as pltpu

F32 = jnp.float32
BF16 = jnp.bfloat16

D_MODEL = 1024
BATCH = 32
SEQ = 256
DEPTH = 2
DEC_BATCH = 2
DEC_SEQ = 1024
PAST_LEN = 256
GRID_W = 64
EPS = 1e-6
POOL_GROUPS = 4
POOL_CH = 96
POOL_WIDTH = 384
POOL_WINDOWS = (2, 4, 8, 16)
POOL_PAD = 16
FFT_CH = 96
FFT_WIDTH = 384
N_HEADS = 8
N_KV_HEADS = 2
HEAD_DIM = 64
ATTN_WIDTH = 512
KV_WIDTH = 128
QKV_WIDTH = ATTN_WIDTH + 2 * KV_WIDTH
ROPE_THETA = 10000.0
ROPE_PAIRS = 16
SGU_GROUPS = 4
SGU_CH = 96
SGU_WIDTH = 384
SGU_CHUNK = 128
N_BRANCHES = 4
GATE_WIDTH = N_BRANCHES * D_MODEL
IN_COLS = POOL_WIDTH + FFT_WIDTH + QKV_WIDTH + 2 * SGU_WIDTH + GATE_WIDTH
N_EXPERTS = 32
TOP_K = 4
D_FF = 1024
SWIGLU_LIMIT = 7.0
SWIGLU_ALPHA = 1.702

ROW_GROUP = 8
LANE_TILES = D_MODEL // 128

T_CTX = BATCH * SEQ
T_LAT = DEC_BATCH * DEC_SEQ
T_ALL = T_CTX + T_LAT
N_COND = 8
VMEM_LIMIT = 56 * 1024 * 1024
CAST_ROWS = 32

COL_XP = 0
COL_XQ = COL_XP + POOL_WIDTH
COL_QKV = COL_XQ + FFT_WIDTH
COL_UV = COL_QKV + QKV_WIDTH
COL_GATE = COL_UV + 2 * SGU_WIDTH


def _params(*sem):
    return pltpu.CompilerParams(dimension_semantics=sem, vmem_limit_bytes=VMEM_LIMIT)


def _split_bf16(x):
    hi = x.astype(BF16)
    lo = (x - hi.astype(F32)).astype(BF16)
    return hi, lo


def _dot(a, b):
    return jnp.dot(a, b, preferred_element_type=F32)


def _dot_nt(a, b):
    return lax.dot_general(a, b, (((1,), (1,)), ((), ())), preferred_element_type=F32)


def _group_lane_select(lane, vals, width):
    out = vals[-1]
    for g in range(len(vals) - 2, -1, -1):
        out = jnp.where(lane < (g + 1) * width, vals[g], out)
    return out


def _cond_row(blk, blocks_ctx, blocks_per_latent):
    return jnp.where(blk < blocks_ctx, 0, 1 + (blk - blocks_ctx) // blocks_per_latent)


MOD_TN = 3072


def _mod_kernel(c_ref, w_ref, b_ref, o_ref):
    c = c_ref[...]
    s = c * jax.nn.sigmoid(c)
    sh, sl = _split_bf16(s)
    wh, wl = _split_bf16(w_ref[...])
    o_ref[...] = _dot(sh, wh) + _dot(sh, wl) + _dot(sl, wh) + b_ref[...]


def _modulation(cond, w_mod, b_mod):
    n_cols = 6 * D_MODEL
    return pl.pallas_call(
        _mod_kernel,
        grid=(DEPTH, n_cols // MOD_TN),
        in_specs=[
            pl.BlockSpec((N_COND, D_MODEL), lambda l, j: (0, 0)),
            pl.BlockSpec((None, D_MODEL, MOD_TN), lambda l, j: (l, 0, j)),
            pl.BlockSpec((None, 1, MOD_TN), lambda l, j: (l, 0, j)),
        ],
        out_specs=pl.BlockSpec((None, N_COND, MOD_TN), lambda l, j: (l, 0, j)),
        out_shape=jax.ShapeDtypeStruct((DEPTH, N_COND, n_cols), F32),
        compiler_params=_params("arbitrary", "arbitrary"),
        name="modulation",
    )(cond, w_mod, b_mod.reshape(DEPTH, 1, n_cols))


INPROJ_TM = 256
INPROJ_SEGMENTS = (
    (COL_XP, POOL_WIDTH), (COL_XQ, FFT_WIDTH), (COL_QKV, QKV_WIDTH),
    (COL_UV, 2 * SGU_WIDTH), (COL_GATE, GATE_WIDTH))
INPROJ_CHUNK = 1024


def _ada_norm(x, g, shift, scale):
    xn = x * lax.rsqrt(jnp.mean(x * x, axis=-1, keepdims=True) + EPS)
    return xn * g * (1.0 + scale) + shift


def _token_specs(parts, tm, width):
    if len(parts) == 1:
        return [pl.BlockSpec((tm, width), lambda i: (i, 0))]
    nc = T_CTX // tm
    return [pl.BlockSpec((tm, width), lambda i: (jnp.minimum(i, nc - 1), 0)),
            pl.BlockSpec((tm, width), lambda i: (jnp.maximum(i - nc, 0), 0))]


def _token_load(refs, tm):
    if len(refs) == 1:
        return refs[0][...]
    return jnp.where(pl.program_id(0) < T_CTX // tm, refs[0][...], refs[1][...])


W_IN_CHUNK = 640


def _inproj_kernel(*refs, n_x, layer):
    x_refs, (mod_ref, g_ref, w_hbm) = refs[:n_x], refs[n_x:n_x + 3]
    out_refs, (w_ref, stage, sems) = refs[n_x + 3:-3], refs[-3:]

    @pl.when(pl.program_id(0) == 0)
    def _():
        def chunk(c):
            return pltpu.make_async_copy(w_hbm.at[layer, :, pl.ds(c * W_IN_CHUNK, W_IN_CHUNK)],
                                         stage.at[c % 2], sems.at[c % 2])

        n_chunks = IN_COLS // W_IN_CHUNK
        chunk(0).start()
        for c in range(n_chunks):
            if c + 1 < n_chunks:
                chunk(c + 1).start()
            chunk(c).wait()

            def cast_rows(r, carry, c=c):
                rows = pl.ds(pl.multiple_of(r * CAST_ROWS, CAST_ROWS), CAST_ROWS)
                w_ref[rows, c * W_IN_CHUNK:(c + 1) * W_IN_CHUNK] = stage[c % 2, rows, :].astype(BF16)
                return carry

            lax.fori_loop(0, D_MODEL // CAST_ROWS, cast_rows, 0)

    h = _ada_norm(_token_load(x_refs, INPROJ_TM), g_ref[...], mod_ref[0:1, :], mod_ref[1:2, :])
    hb = h.astype(BF16)
    for (col, width), o_ref in zip(INPROJ_SEGMENTS, out_refs):
        for c0 in range(0, width, INPROJ_CHUNK):
            c1 = min(c0 + INPROJ_CHUNK, width)
            proj = _dot(hb, w_ref[:, col + c0:col + c1])
            o_ref[:, c0:c1] = jax.nn.sigmoid(proj) if col == COL_GATE else proj


def _in_projection(x_parts, mod_l, g, w_in, l):
    tm = INPROJ_TM
    row = functools.partial(_cond_row, blocks_ctx=T_CTX // tm, blocks_per_latent=DEC_SEQ // tm)
    return pl.pallas_call(
        functools.partial(_inproj_kernel, n_x=len(x_parts), layer=l),
        grid=(T_ALL // tm,),
        in_specs=_token_specs(x_parts, tm, D_MODEL) + [
            pl.BlockSpec((None, 6, D_MODEL), lambda i: (row(i), 0, 0)),
            pl.BlockSpec((1, D_MODEL), lambda i: (0, 0)),
            pl.BlockSpec(memory_space=pl.ANY),
        ],
        out_specs=[pl.BlockSpec((tm, w), lambda i: (i, 0)) for _, w in INPROJ_SEGMENTS],
        out_shape=[jax.ShapeDtypeStruct((T_ALL, w), F32) for _, w in INPROJ_SEGMENTS],
        scratch_shapes=[pltpu.VMEM((D_MODEL, IN_COLS), BF16), pltpu.VMEM((2, D_MODEL, W_IN_CHUNK), F32),
                        pltpu.SemaphoreType.DMA((2,))],
        compiler_params=_params("arbitrary"),
        name="in_projection",
    )(*x_parts, mod_l, g.reshape(1, D_MODEL), w_in)


CTX_SEQS_PER_STEP = 4


def _pool_mixer(xp, invcnt, w_bd, scale):
    s = xp.shape[0]
    n = s + 2 * POOL_PAD
    zeros = jnp.zeros((POOL_PAD, POOL_WIDTH), F32)
    xe = jnp.concatenate([zeros, xp, zeros], axis=0)

    def shift(a, k):
        return pltpu.roll(a, k % n, 0)

    s2 = xe + shift(xe, 1)
    s4 = shift(s2, 1) + shift(s2, -1)
    s8 = shift(s4, 2) + shift(s4, -2)
    s16 = shift(s8, 4) + shift(s8, -4)
    lane = lax.broadcasted_iota(jnp.int32, (1, POOL_WIDTH), 1)
    total = _group_lane_select(lane, [s2, s4, s8, s16], POOL_CH)[POOL_PAD:POOL_PAD + s]
    pooled = total * invcnt - xp
    return _dot(pooled.astype(BF16), w_bd) * scale


def _fourier_mixer(xq, f_cos_ch, f_sin_ch, f_pos):
    xb = xq.astype(BF16)
    a = _dot(xb, f_cos_ch).astype(BF16)
    b = _dot(xb, f_sin_ch).astype(BF16)
    return _dot(f_pos, jnp.concatenate([a, b], axis=0))


def _group_mean_sq(x, ones_bd, width):
    hi, lo = _split_bf16(x * x)
    return (_dot(hi, ones_bd) + _dot(lo, ones_bd)) * (1.0 / width)


def _mixers_kernel(xp_ref, xq_ref, uv_ref, invcnt_ref, pool_w_ref, pool_s_ref, fcc_ref, fsc_ref,
                   fpos_ref, ones_ref, sgu_w_ref, sgu_b_ref, pool_o, four_o, sgu_o, *, seq):
    lane = lax.broadcasted_iota(jnp.int32, (1, SGU_WIDTH), 1)
    w_stack = sgu_w_ref[...]
    bias = sgu_b_ref[...]
    for b in range(xp_ref.shape[0] // seq):
        rows = slice(b * seq, (b + 1) * seq)
        pool_o[rows, :] = _pool_mixer(xp_ref[rows, :], invcnt_ref[...], pool_w_ref[...],
                                      pool_s_ref[...]).astype(BF16)
        four_o[rows, :] = _fourier_mixer(xq_ref[rows, :], fcc_ref[...], fsc_ref[...],
                                         fpos_ref[...]).astype(BF16)

        act = jax.nn.gelu(uv_ref[rows, :], approximate=True)
        u = act[:, :SGU_WIDTH]
        v = act[:, SGU_WIDTH:]
        vg = (v * lax.rsqrt(_group_mean_sq(v, ones_ref[...], SGU_CH) + EPS)).astype(BF16)
        for n in range(seq // SGU_CHUNK):
            chunk = slice(n * SGU_CHUNK, (n + 1) * SGU_CHUNK)
            r = _dot(w_stack, vg[chunk])
            per_group = [r[g * SGU_CHUNK:(g + 1) * SGU_CHUNK] for g in range(SGU_GROUPS)]
            spatial = _group_lane_select(lane, per_group, SGU_CH) + bias
            out_rows = slice(b * seq + n * SGU_CHUNK, b * seq + (n + 1) * SGU_CHUNK)
            sgu_o[out_rows, :] = (u[chunk] * spatial).astype(BF16)


def _mixers(xp, xq, uv, consts, lw, seq, n_seq, block0, per_step):
    rows = per_step * seq
    full = lambda shape: pl.BlockSpec(shape, lambda b: (0,) * len(shape))
    tok = lambda w: pl.BlockSpec((rows, w), lambda b: (block0 + b, 0))
    out = lambda: pl.BlockSpec((rows, POOL_WIDTH), lambda b: (b, 0))
    return pl.pallas_call(
        functools.partial(_mixers_kernel, seq=seq),
        grid=(n_seq // per_step,),
        in_specs=[
            tok(POOL_WIDTH), tok(FFT_WIDTH), tok(2 * SGU_WIDTH),
            full((seq, POOL_WIDTH)), full((POOL_WIDTH, POOL_WIDTH)), full((1, POOL_WIDTH)),
            full((FFT_WIDTH, FFT_WIDTH)), full((FFT_WIDTH, FFT_WIDTH)), full((seq, 2 * seq)),
            full((SGU_WIDTH, SGU_WIDTH)), full((SGU_GROUPS * SGU_CHUNK, SGU_CHUNK)),
            full((SGU_CHUNK, SGU_WIDTH)),
        ],
        out_specs=[out(), out(), out()],
        out_shape=[jax.ShapeDtypeStruct((n_seq * seq, POOL_WIDTH), BF16)] * 3,
        compiler_params=_params("arbitrary"),
        name=f"mixers_s{seq}",
    )(xp, xq, uv, consts["invcnt"], lw["pool_w_bd"], lw["pool_scale"], consts["f_cos_ch"],
      consts["f_sin_ch"], consts["f_pos"], consts["ones96"], lw["sgu_w_stack"], lw["sgu_bias"])


def _head_norm(x, ones_bd, g):
    return x * lax.rsqrt(_group_mean_sq(x, ones_bd, HEAD_DIM) + EPS) * g


def _rope(x, cos, sin_next, sin_prev):
    cols = []
    for c in range(x.shape[1] // 128):
        xc = x[:, c * 128:(c + 1) * 128]
        nxt = pltpu.roll(xc, 127, 1)
        prv = pltpu.roll(xc, 1, 1)
        cols.append(xc * cos + nxt * sin_next + prv * sin_prev)
    return cols[0] if len(cols) == 1 else jnp.concatenate(cols, axis=1)


def _attend(q, keys, vals, o_ref, stack):
    sq = q.shape[0]
    qb = (q * (HEAD_DIM ** -0.5)).astype(BF16)
    group = N_HEADS // N_KV_HEADS
    head = lambda a, h: a[:, h * HEAD_DIM:(h + 1) * HEAD_DIM]
    for h0 in range(0, N_HEADS, stack):
        j = h0 // group
        qs = head(qb, h0) if stack == 1 else jnp.concatenate(
            [head(qb, h0 + g) for g in range(stack)], axis=0)
        s = _dot_nt(qs, head(keys, j))
        p = jnp.exp(s - jnp.max(s, axis=-1, keepdims=True))
        denom = jnp.sum(p, axis=-1, keepdims=True)
        o = (_dot(p.astype(BF16), head(vals, j)) / denom).astype(BF16)
        for g in range(stack):
            o_ref[:, (h0 + g) * HEAD_DIM:(h0 + g + 1) * HEAD_DIM] = o[g * sq:(g + 1) * sq]


def _attn_ctx_kernel(qkv_ref, gq_ref, gk_ref, ones_ref, o_ref, k_ref, v_ref):
    qkv = qkv_ref[...]
    ones = ones_ref[...]
    q = _head_norm(qkv[:, :ATTN_WIDTH], ones, gq_ref[...])
    k = _head_norm(qkv[:, ATTN_WIDTH:ATTN_WIDTH + KV_WIDTH], ones[:KV_WIDTH, :KV_WIDTH], gk_ref[...])
    v = qkv[:, ATTN_WIDTH + KV_WIDTH:]
    k_ref[...] = k
    v_ref[...] = v
    _attend(q, k.astype(BF16), v.astype(BF16), o_ref, stack=2)


def _attention_ctx(qkv, consts, lw):
    full = lambda shape: pl.BlockSpec(shape, lambda b: (0,) * len(shape))
    return pl.pallas_call(
        _attn_ctx_kernel,
        grid=(BATCH,),
        in_specs=[pl.BlockSpec((SEQ, QKV_WIDTH), lambda b: (b, 0)),
                  full((1, ATTN_WIDTH)), full((1, KV_WIDTH)), full((ATTN_WIDTH, ATTN_WIDTH))],
        out_specs=[pl.BlockSpec((SEQ, ATTN_WIDTH), lambda b: (b, 0)),
                   pl.BlockSpec((SEQ, KV_WIDTH), lambda b: (b, 0)),
                   pl.BlockSpec((SEQ, KV_WIDTH), lambda b: (b, 0))],
        out_shape=[jax.ShapeDtypeStruct((T_CTX, ATTN_WIDTH), BF16),
                   jax.ShapeDtypeStruct((T_CTX, KV_WIDTH), F32),
                   jax.ShapeDtypeStruct((T_CTX, KV_WIDTH), F32)],
        compiler_params=_params("arbitrary"),
        name="attention_ctx",
    )(qkv, lw["gq"], lw["gk"], consts["ones64"])


LAT_QBLK = 512


def _attn_lat_kernel(q_ref, kv_ref, ck_ref, cv_ref, gq_ref, gk_ref, ones_ref, cos_q, sn_q, sp_q,
                     cos_k, sn_k, sp_k, o_ref, keys, vals):
    ones = ones_ref[...]

    @pl.when(pl.program_id(1) == 0)
    def _():
        kv = kv_ref[...]
        k = _head_norm(kv[:, :KV_WIDTH], ones[:KV_WIDTH, :KV_WIDTH], gk_ref[...])
        keys[0:DEC_SEQ, :] = _rope(k, cos_k[...], sn_k[...], sp_k[...]).astype(BF16)
        keys[DEC_SEQ:, :] = ck_ref[...].astype(BF16)
        vals[0:DEC_SEQ, :] = kv[:, KV_WIDTH:].astype(BF16)
        vals[DEC_SEQ:, :] = cv_ref[...].astype(BF16)

    q = _head_norm(q_ref[...], ones, gq_ref[...])
    q = _rope(q, cos_q[...], sn_q[...], sp_q[...])
    _attend(q, keys[...], vals[...], o_ref, stack=1)


def _attention_lat(qkv, cache_k_l, cache_v_l, consts, lw):
    nq = DEC_SEQ // LAT_QBLK
    q0 = T_CTX // LAT_QBLK
    s0 = T_CTX // DEC_SEQ
    full = lambda shape: pl.BlockSpec(shape, lambda b, j: (0,) * len(shape))
    rope_q = lambda: pl.BlockSpec((LAT_QBLK, 128), lambda b, j: (j, 0))
    rope_k = lambda: pl.BlockSpec((DEC_SEQ, 128), lambda b, j: (0, 0))
    return pl.pallas_call(
        _attn_lat_kernel,
        grid=(DEC_BATCH, nq),
        in_specs=[
            pl.BlockSpec((LAT_QBLK, ATTN_WIDTH), lambda b, j: (q0 + b * nq + j, 0)),
            pl.BlockSpec((DEC_SEQ, 2 * KV_WIDTH), lambda b, j: (s0 + b, ATTN_WIDTH // (2 * KV_WIDTH))),
            pl.BlockSpec((None, PAST_LEN, KV_WIDTH), lambda b, j: (b, 0, 0)),
            pl.BlockSpec((None, PAST_LEN, KV_WIDTH), lambda b, j: (b, 0, 0)),
            full((1, ATTN_WIDTH)), full((1, KV_WIDTH)), full((ATTN_WIDTH, ATTN_WIDTH)),
            rope_q(), rope_q(), rope_q(), rope_k(), rope_k(), rope_k(),
        ],
        out_specs=pl.BlockSpec((LAT_QBLK, ATTN_WIDTH), lambda b, j: (b * nq + j, 0)),
        out_shape=jax.ShapeDtypeStruct((T_LAT, ATTN_WIDTH), BF16),
        scratch_shapes=[pltpu.VMEM((DEC_SEQ + PAST_LEN, KV_WIDTH), BF16),
                        pltpu.VMEM((DEC_SEQ + PAST_LEN, KV_WIDTH), BF16)],
        compiler_params=_params("arbitrary", "arbitrary"),
        name="attention_lat",
    )(qkv, qkv, cache_k_l, cache_v_l, lw["gq"], lw["gk"], consts["ones64"],
      consts["rope_cos"], consts["rope_sin_next"], consts["rope_sin_prev"],
      consts["rope_cos"], consts["rope_sin_next"], consts["rope_sin_prev"])


MERGE_TM = 512


MERGE_WEIGHT_ROWS = (POOL_WIDTH, FFT_WIDTH, ATTN_WIDTH, SGU_WIDTH, D_MODEL)


def _merge_kernel(*refs, n_x, layer):
    branch_refs, gate_ref, x_refs = refs[:8], refs[8], refs[9:9 + n_x]
    (mod_ref, g_ref, wp_hbm, wf_hbm, wa_hbm, ws_hbm, wo_hbm, rw_ref, rb_ref,
     x_o, h_o, idx_o, wgt_o, wp_ref, wf_ref, wa_ref, ws_ref, wo_ref, stage, sem) = refs[9 + n_x:]

    @pl.when(pl.program_id(0) == 0)
    def _():
        for w_hbm, w_ref, n in zip((wp_hbm, wf_hbm, wa_hbm, ws_hbm, wo_hbm),
                                   (wp_ref, wf_ref, wa_ref, ws_ref, wo_ref), MERGE_WEIGHT_ROWS):
            copy = pltpu.make_async_copy(w_hbm.at[layer], stage.at[pl.ds(0, n)], sem)
            copy.start()
            copy.wait()

            def cast_rows(r, carry, w_ref=w_ref):
                rows = pl.ds(pl.multiple_of(r * CAST_ROWS, CAST_ROWS), CAST_ROWS)
                w_ref[rows, :] = stage[rows, :].astype(BF16)
                return carry

            lax.fori_loop(0, n // CAST_ROWS, cast_rows, 0)

    merged = None
    for i, w_ref in enumerate((wp_ref, wf_ref, wa_ref, ws_ref)):
        br = _dot(_token_load(branch_refs[2 * i:2 * i + 2], MERGE_TM), w_ref[...])
        term = gate_ref[:, i * D_MODEL:(i + 1) * D_MODEL] * br
        merged = term if merged is None else merged + term
    mix = _dot(merged.astype(BF16), wo_ref[...])
    x = _token_load(x_refs, MERGE_TM) + mod_ref[2:3, :] * mix
    x_o[...] = x
    h = _ada_norm(x, g_ref[...], mod_ref[3:4, :], mod_ref[4:5, :])
    for c in range(LANE_TILES):
        h_o[:, c, :, :] = h[:, c * 128:(c + 1) * 128].reshape(MERGE_TM // ROW_GROUP, ROW_GROUP, 128)

    hh, hl = _split_bf16(h)
    rh, rl = _split_bf16(rw_ref[...])
    logits = _dot_nt(rh, hh) + _dot_nt(rl, hh) + _dot_nt(rh, hl) + rb_ref[...]
    expert = lax.broadcasted_iota(jnp.int32, logits.shape, 0).astype(F32)
    work = logits
    top = jnp.max(logits, axis=0, keepdims=True)
    idx, wgt = [], []
    denom = jnp.zeros_like(top)
    for _ in range(TOP_K):
        m = jnp.max(work, axis=0, keepdims=True)
        first = jnp.min(jnp.where(work == m, expert, float(N_EXPERTS)), axis=0, keepdims=True)
        e = jnp.exp(m - top)
        idx.append(first)
        wgt.append(e)
        denom = denom + e
        work = jnp.where(expert == first, -jnp.inf, work)
    idx_o[...] = jnp.concatenate(idx, axis=0).astype(jnp.int32)
    wgt_o[...] = jnp.concatenate(wgt, axis=0) / denom


def _merge(pool, four, attn, sgu, gates, x_parts, mod_l, g_ffn, lw, p, l):
    tm = MERGE_TM
    hbm = pl.BlockSpec(memory_space=pl.ANY)
    row = functools.partial(_cond_row, blocks_ctx=T_CTX // tm, blocks_per_latent=DEC_SEQ // tm)
    full = lambda shape: pl.BlockSpec(shape, lambda i: (0,) * len(shape))
    tok = lambda w: pl.BlockSpec((tm, w), lambda i: (i, 0))
    branch_specs = []
    for pair, width in ((pool, POOL_WIDTH), (four, FFT_WIDTH), (attn, ATTN_WIDTH), (sgu, SGU_WIDTH)):
        branch_specs += _token_specs(pair, tm, width)
    return pl.pallas_call(
        functools.partial(_merge_kernel, n_x=len(x_parts), layer=l),
        grid=(T_ALL // tm,),
        in_specs=branch_specs + [tok(GATE_WIDTH)] + _token_specs(x_parts, tm, D_MODEL) + [
            pl.BlockSpec((None, 6, D_MODEL), lambda i: (row(i), 0, 0)),
            full((1, D_MODEL)),
            hbm, hbm, hbm, hbm, hbm,
            full((N_EXPERTS, D_MODEL)), full((N_EXPERTS, 1)),
        ],
        out_specs=[tok(D_MODEL),
                   pl.BlockSpec((tm // ROW_GROUP, LANE_TILES, ROW_GROUP, 128), lambda i: (i, 0, 0, 0)),
                   pl.BlockSpec((TOP_K, tm), lambda i: (0, i)), pl.BlockSpec((TOP_K, tm), lambda i: (0, i))],
        out_shape=[jax.ShapeDtypeStruct((T_ALL, D_MODEL), F32),
                   jax.ShapeDtypeStruct((T_ALL // ROW_GROUP, LANE_TILES, ROW_GROUP, 128), F32),
                   jax.ShapeDtypeStruct((TOP_K, T_ALL), jnp.int32),
                   jax.ShapeDtypeStruct((TOP_K, T_ALL), F32)],
        scratch_shapes=[pltpu.VMEM((n, D_MODEL), BF16) for n in MERGE_WEIGHT_ROWS]
        + [pltpu.VMEM((max(MERGE_WEIGHT_ROWS), D_MODEL), F32), pltpu.SemaphoreType.DMA(())],
        compiler_params=_params("arbitrary"),
        name="merge_router",
    )(*pool, *four, *attn, *sgu, gates, *x_parts, mod_l, g_ffn.reshape(1, D_MODEL),
      p["w_br_pool"], p["w_br_fourier"], p["w_br_attn"], p["w_br_sgu"], p["w_out"],
      lw["router_w"], lw["router_b"])


N_PAIRS = T_ALL * TOP_K
FFN_TM = 512
FFN_ROW_OPTIONS = (128, 256, 384, 512)
N_ROW_TILES = N_PAIRS // FFN_TM + N_EXPERTS
N_SLOTS = N_ROW_TILES * FFN_TM
ROUTE_TB = 512


def _route(top_idx):
    experts = jnp.arange(N_EXPERTS, dtype=jnp.int32)
    onehot = (top_idx.T[:, :, None] == experts[None, None, :]).astype(jnp.int32)
    per_token = jnp.sum(onehot, axis=1)
    csum = jnp.cumsum(per_token, axis=0)
    counts = csum[-1]
    tiles = (counts + FFN_TM - 1) // FFN_TM
    tile_end = jnp.cumsum(tiles)
    row_start = (tile_end - tiles) * FFN_TM
    first_row = csum - per_token + row_start[None, :]
    slot = jnp.sum(onehot * first_row[:, None, :], axis=2)
    tile_ids = jnp.arange(N_ROW_TILES, dtype=jnp.int32)
    tile_expert = jnp.sum((tile_end[None, :] <= tile_ids[:, None]).astype(jnp.int32), axis=1)
    tile_expert = jnp.minimum(tile_expert, N_EXPERTS - 1)
    rows_before = (tile_ids - (tile_end - tiles)[tile_expert]) * FFN_TM
    tile_rows = jnp.where(tile_ids < tile_end[-1],
                          jnp.clip(counts[tile_expert] - rows_before, 0, FFN_TM), 0)
    slot = slot.astype(jnp.int32).reshape(T_ALL // ROUTE_TB, 1, ROUTE_TB * TOP_K)
    used = (tiles > 0).astype(jnp.int32)
    rank = jnp.cumsum(used) - 1
    group_expert = jnp.sum(jnp.where((rank[None, :] == experts[:, None]) & (used[None, :] > 0),
                                     experts[None, :], 0), axis=1)
    tiles_info = dict(expert=tile_expert.astype(jnp.int32), rows=tile_rows.astype(jnp.int32),
                      group=rank[tile_expert].astype(jnp.int32),
                      group_expert=group_expert.astype(jnp.int32),
                      n_groups=jnp.sum(used).reshape(1).astype(jnp.int32),
                      n_used=tile_end[-1:].astype(jnp.int32))
    return slot, tiles_info


def _start_pair_rows(copy):
    def body(g, carry):
        for u in range(ROW_GROUP):
            for k in range(TOP_K):
                copy(g, u, k).start(priority=k % 2)
        return carry

    lax.fori_loop(0, ROUTE_TB // ROW_GROUP, body, 0)


def _pair_slot(slot_ref, g, u, k):
    return slot_ref[0, (g * ROW_GROUP + u) * TOP_K + k]


def _dispatch_kernel(tr_ref, slot_ref, h_ref, xs_out, zeros, sem, zero_sem):
    @pl.when(pl.program_id(0) == 0)
    def _():
        zeros[...] = jnp.zeros_like(zeros)

        def for_each_unfilled_tile(fn):
            group = FFN_ROW_OPTIONS[0]

            def body(i, carry):
                for full_groups in range(FFN_TM // group):
                    first = full_groups * group

                    @pl.when(tr_ref[i] // group == full_groups)
                    def _():
                        fn(pltpu.make_async_copy(zeros.at[pl.ds(0, FFN_TM - first)],
                                                 xs_out.at[pl.ds(i * FFN_TM + first, FFN_TM - first)],
                                                 zero_sem))
                return carry

            lax.fori_loop(0, N_ROW_TILES, body, 0)

        for_each_unfilled_tile(lambda c: c.start())
        for_each_unfilled_tile(lambda c: c.wait())

    def copy(g, u, k):
        return pltpu.make_async_copy(h_ref.at[g, :, u, :], xs_out.at[_pair_slot(slot_ref, g, u, k)], sem)

    _start_pair_rows(copy)
    for _ in range(TOP_K):
        pltpu.make_async_copy(xs_out.at[pl.ds(0, ROUTE_TB)], xs_out.at[pl.ds(0, ROUTE_TB)], sem).wait()


def _dispatch(tile_rows, slot, h):
    return pl.pallas_call(
        _dispatch_kernel,
        grid_spec=pltpu.PrefetchScalarGridSpec(
            num_scalar_prefetch=1,
            grid=(T_ALL // ROUTE_TB,),
            in_specs=[
                pl.BlockSpec((None, 1, ROUTE_TB * TOP_K), lambda i, tr: (i, 0, 0),
                             memory_space=pltpu.SMEM),
                pl.BlockSpec((ROUTE_TB // ROW_GROUP, LANE_TILES, ROW_GROUP, 128),
                             lambda i, tr: (i, 0, 0, 0)),
            ],
            out_specs=pl.BlockSpec(memory_space=pl.ANY),
            scratch_shapes=[pltpu.VMEM((FFN_TM, LANE_TILES, 128), F32), pltpu.SemaphoreType.DMA(()),
                            pltpu.SemaphoreType.DMA(())],
        ),
        out_shape=jax.ShapeDtypeStruct((N_SLOTS, LANE_TILES, 128), F32),
        compiler_params=_params("arbitrary"),
        name="moe_dispatch",
    )(tile_rows, slot, h)


def _expert_ffn(xb, wg, bg, wu, bu, wd, bd):
    gate = jnp.minimum(_dot(xb, wg) + bg, SWIGLU_LIMIT)
    up = jnp.clip(_dot(xb, wu) + bu, -SWIGLU_LIMIT, SWIGLU_LIMIT)
    glu = gate * jax.nn.sigmoid(SWIGLU_ALPHA * gate)
    return _dot(((up + 1.0) * glu).astype(BF16), wd) + bd


def _ffn_kernel(te_ref, tr_ref, tg_ref, ge_ref, ng_ref, nu_ref, xs_ref, wg_hbm, bg_ref, wu_hbm,
                bu_ref, wd_hbm, bd_ref, ys_ref, stage, w_bf16, xb, sems, *, layer):
    i = pl.program_id(0)
    n_rows = tr_ref[i]
    group = tg_ref[i]
    slot = group % 2

    def weight_copies(g, s):
        e = ge_ref[g]
        return [pltpu.make_async_copy(w.at[layer, e], stage.at[s, j], sems.at[s])
                for j, w in enumerate((wg_hbm, wu_hbm, wd_hbm))]

    def lane_tile(ref, c, rows):
        return ref.at[pl.ds(c, rows, stride=LANE_TILES), :]

    def ffn(rows):
        for c in range(LANE_TILES):
            xb[:rows, c * 128:(c + 1) * 128] = lane_tile(xs_ref, c, rows)[...].astype(BF16)
        y = _expert_ffn(xb[:rows, :], w_bf16[0], bg_ref[...], w_bf16[1], bu_ref[...], w_bf16[2],
                        bd_ref[...])
        for c in range(LANE_TILES):
            lane_tile(ys_ref, c, rows)[...] = y[:, c * 128:(c + 1) * 128]
        if rows < FFN_TM:
            ys_ref[rows * LANE_TILES:, :] = jnp.zeros(((FFN_TM - rows) * LANE_TILES, 128), F32)

    @pl.when(i == 0)
    def _():
        for c in weight_copies(0, 0):
            c.start()

    @pl.when((n_rows > 0) & ((i == 0) | (te_ref[i] != te_ref[jnp.maximum(i - 1, 0)])))
    def _():
        for c in weight_copies(group, slot):
            c.wait()
        def cast_rows(r, carry):
            rows = pl.ds(pl.multiple_of(r * CAST_ROWS, CAST_ROWS), CAST_ROWS)
            for j in range(3):
                w_bf16[j, rows, :] = stage[slot, j, rows, :].astype(BF16)
            return carry

        lax.fori_loop(0, D_MODEL // CAST_ROWS, cast_rows, 0)

        @pl.when(group + 1 < ng_ref[0])
        def _():
            for c in weight_copies(group + 1, 1 - slot):
                c.start()

    @pl.when(n_rows == 0)
    def _():
        ys_ref[...] = jnp.zeros_like(ys_ref)

    lower = 0
    for rows in FFN_ROW_OPTIONS:
        @pl.when((n_rows > lower) & (n_rows <= rows))
        def _(rows=rows):
            ffn(rows)
        lower = rows


def _routed_ffn(tiles, xs, l, p):
    n_prefetch = 6
    rows = pl.BlockSpec((FFN_TM * LANE_TILES, 128),
                        lambda i, te, tr, tg, ge, ng, nu: (jnp.minimum(i, nu[0] - 1), 0))
    bias_spec = lambda w: pl.BlockSpec((None, None, 1, w),
                                       lambda i, te, tr, tg, ge, ng, nu: (l, te[i], 0, 0))
    bias = lambda a: a.reshape(DEPTH, N_EXPERTS, 1, a.shape[-1])
    hbm = pl.BlockSpec(memory_space=pl.ANY)
    return pl.pallas_call(
        functools.partial(_ffn_kernel, layer=l),
        grid_spec=pltpu.PrefetchScalarGridSpec(
            num_scalar_prefetch=n_prefetch,
            grid=(N_ROW_TILES,),
            in_specs=[rows, hbm, bias_spec(D_FF), hbm, bias_spec(D_FF), hbm, bias_spec(D_MODEL)],
            out_specs=pl.BlockSpec((FFN_TM * LANE_TILES, 128),
                                   lambda i, te, tr, tg, ge, ng, nu: (i, 0)),
            scratch_shapes=[pltpu.VMEM((2, 3, D_MODEL, D_FF), F32), pltpu.VMEM((3, D_MODEL, D_FF), BF16),
                            pltpu.VMEM((FFN_TM, D_MODEL), BF16), pltpu.SemaphoreType.DMA((2,))],
        ),
        out_shape=jax.ShapeDtypeStruct((N_SLOTS * LANE_TILES, 128), F32),
        compiler_params=_params("arbitrary"),
        name="moe_ffn",
    )(tiles["expert"], tiles["rows"], tiles["group"], tiles["group_expert"], tiles["n_groups"],
      tiles["n_used"], xs.reshape(N_SLOTS * LANE_TILES, 128), p["moe_w_gate"], bias(p["moe_b_gate"]),
      p["moe_w_up"], bias(p["moe_b_up"]), p["moe_w_down"], bias(p["moe_b_down"]))


def _combine_kernel(slot_ref, slot_next_ref, w_ref, x_ref, mod_ref, fg_ref, ys_ref, *rest, final_norm):
    out_refs, (acc, buf, sems) = rest[:-3], rest[-3:]
    i = pl.program_id(0)
    cur = i % 2

    def copy(slots, b, g, u, k):
        return pltpu.make_async_copy(ys_ref.at[_pair_slot(slots, g, u, k)], buf.at[b, k, g, :, u, :],
                                     sems.at[b])

    @pl.when(i == 0)
    def _():
        _start_pair_rows(functools.partial(copy, slot_ref, 0))

    @pl.when(i + 1 < pl.num_programs(0))
    def _():
        _start_pair_rows(functools.partial(copy, slot_next_ref, 1 - cur))

    for k in range(TOP_K):
        pltpu.make_async_copy(ys_ref.at[pl.ds(0, ROUTE_TB)], ys_ref.at[pl.ds(0, ROUTE_TB)],
                              sems.at[cur]).wait()

    def rows_of(k, c):
        return buf[cur, k, :, c, :, :].reshape(ROUTE_TB, 128)

    w = w_ref[...]
    sum_sq = jnp.zeros((ROUTE_TB, 1), F32)
    for c in range(LANE_TILES):
        cols = slice(c * 128, (c + 1) * 128)
        mix = w[:, 0:1] * rows_of(0, c)
        for k in range(1, TOP_K):
            mix = mix + w[:, k:k + 1] * rows_of(k, c)
        piece = x_ref[:, cols] + mod_ref[5:6, cols] * mix
        sum_sq = sum_sq + jnp.sum(piece * piece, axis=-1, keepdims=True)
        acc[:, cols] = piece

    if not final_norm:
        out_refs[0][...] = acc[...]
        return
    normed = acc[...] * lax.rsqrt(sum_sq * (1.0 / D_MODEL) + EPS) * fg_ref[...]

    @pl.when(i < T_CTX // ROUTE_TB)
    def _():
        out_refs[0][...] = normed

    @pl.when(i >= T_CTX // ROUTE_TB)
    def _():
        out_refs[1][...] = normed


def _combine(slot, top_w, x, mod_l, final_g, ys, final_norm):
    tb = ROUTE_TB
    nb = T_ALL // tb
    nc = T_CTX // tb
    row = functools.partial(_cond_row, blocks_ctx=nc, blocks_per_latent=DEC_SEQ // tb)
    tok = lambda w: pl.BlockSpec((tb, w), lambda i: (i, 0))
    slots = lambda ahead: pl.BlockSpec((None, 1, tb * TOP_K),
                                       lambda i: (jnp.minimum(i + ahead, nb - 1), 0, 0),
                                       memory_space=pltpu.SMEM)
    if final_norm:
        out_specs = [pl.BlockSpec((tb, D_MODEL), lambda i: (jnp.minimum(i, nc - 1), 0)),
                     pl.BlockSpec((tb, D_MODEL), lambda i: (jnp.maximum(i - nc, 0), 0))]
        out_shape = [jax.ShapeDtypeStruct((T_CTX, D_MODEL), F32),
                     jax.ShapeDtypeStruct((T_LAT, D_MODEL), F32)]
    else:
        out_specs = [tok(D_MODEL)]
        out_shape = [jax.ShapeDtypeStruct((T_ALL, D_MODEL), F32)]
    return pl.pallas_call(
        functools.partial(_combine_kernel, final_norm=final_norm),
        grid=(nb,),
        in_specs=[
            slots(0), slots(1), tok(TOP_K), tok(D_MODEL),
            pl.BlockSpec((None, 6, D_MODEL), lambda i: (row(i), 0, 0)),
            pl.BlockSpec((1, D_MODEL), lambda i: (0, 0)),
            pl.BlockSpec(memory_space=pl.ANY),
        ],
        out_specs=out_specs,
        out_shape=out_shape,
        scratch_shapes=[pltpu.VMEM((tb, D_MODEL), F32),
                        pltpu.VMEM((2, TOP_K, tb // ROW_GROUP, LANE_TILES, ROW_GROUP, 128), F32),
                        pltpu.SemaphoreType.DMA((2,))],
        compiler_params=_params("arbitrary"),
        name="moe_combine",
    )(slot, slot, top_w, x, mod_l, final_g.reshape(1, D_MODEL), ys.reshape(N_SLOTS, LANE_TILES, 128))


def _block_diag(blocks):
    g, a, b = blocks.shape
    eye = jnp.eye(g, dtype=blocks.dtype)
    return (eye[:, None, :, None] * blocks[:, :, None, :]).reshape(g * a, g * b)


def _dft_tables(n):
    k = np.arange(n, dtype=np.int64)
    ang = 2.0 * np.pi * ((k[:, None] * k[None, :]) % n).astype(np.float64) / n
    return np.cos(ang) / math.sqrt(n), np.sin(ang) / math.sqrt(n)


def _seq_constants(seq):
    t = np.arange(seq)[:, None]
    win = np.array(POOL_WINDOWS)[None, :]
    lo = np.clip(t - win // 2, 0, seq)
    hi = np.clip(t - win // 2 + win, 0, seq)
    invcnt = np.repeat(1.0 / (hi - lo).astype(np.float64), POOL_CH, axis=1)
    cos_p, sin_p = _dft_tables(seq)
    cos_c, sin_c = _dft_tables(FFT_CH)
    eye = np.eye(POOL_GROUPS)
    as_bf16 = lambda a: jnp.asarray(a, F32).astype(BF16)
    return {
        "invcnt": jnp.asarray(invcnt, F32),
        "f_pos": as_bf16(np.concatenate([cos_p, -sin_p], axis=1)),
        "f_cos_ch": as_bf16(np.kron(eye, cos_c)),
        "f_sin_ch": as_bf16(np.kron(eye, sin_c)),
        "ones96": as_bf16(np.kron(eye, np.ones((SGU_CH, SGU_CH)))),
    }


def _rope_constants():
    rows = DEC_SEQ // GRID_W
    row = jnp.repeat(jnp.arange(rows, dtype=F32), GRID_W)
    col = jnp.tile(jnp.arange(GRID_W, dtype=F32), rows)
    freqs = ROPE_THETA ** (-jnp.arange(ROPE_PAIRS, dtype=F32) / ROPE_PAIRS)
    ang = jnp.stack([row[:, None] * freqs, col[:, None] * freqs], axis=1)
    ang = jnp.repeat(ang.reshape(DEC_SEQ, 2 * ROPE_PAIRS), 2, axis=1)
    ang = jnp.tile(ang, (1, 128 // HEAD_DIM))
    even = (jnp.arange(128) % 2 == 0)[None, :]
    sin = jnp.sin(ang)
    return {
        "rope_cos": jnp.cos(ang),
        "rope_sin_next": jnp.where(even, -sin, 0.0),
        "rope_sin_prev": jnp.where(even, 0.0, sin),
    }


def _layer_weights(p, l):
    group_of_lane = np.arange(SGU_WIDTH) // SGU_CH
    return {
        "pool_w_bd": _block_diag(p["pool_w"][l]).astype(BF16),
        "pool_scale": p["pool_scale"][l].reshape(1, POOL_WIDTH),
        "sgu_w_stack": p["sgu_w"][l].reshape(SGU_GROUPS * SGU_CHUNK, SGU_CHUNK).astype(BF16),
        "sgu_bias": p["sgu_b"][l].T[:, group_of_lane],
        "gq": jnp.tile(p["q_norm_g"][l], N_HEADS).reshape(1, ATTN_WIDTH),
        "gk": jnp.tile(p["k_norm_g"][l], N_KV_HEADS).reshape(1, KV_WIDTH),
        "router_w": p["router_w"][l].T,
        "router_b": p["router_b"][l].reshape(N_EXPERTS, 1),
    }


def kernel(x_prompt, x_sample, cache_k, cache_v, c, c_ctx, w_mod, b_mod, norm_mix_g, norm_ffn_g, w_in, pool_w, pool_scale, q_norm_g, k_norm_g, sgu_w, sgu_b, w_br_pool, w_br_fourier, w_br_attn, w_br_sgu, w_out, router_w, router_b, moe_w_gate, moe_b_gate, moe_w_up, moe_b_up, moe_w_down, moe_b_down, final_norm_g):
    p = dict(w_in=w_in, pool_w=pool_w, pool_scale=pool_scale, q_norm_g=q_norm_g, k_norm_g=k_norm_g,
             sgu_w=sgu_w, sgu_b=sgu_b, w_br_pool=w_br_pool, w_br_fourier=w_br_fourier,
             w_br_attn=w_br_attn, w_br_sgu=w_br_sgu, w_out=w_out, router_w=router_w,
             router_b=router_b, moe_w_gate=moe_w_gate, moe_b_gate=moe_b_gate, moe_w_up=moe_w_up,
             moe_b_up=moe_b_up, moe_w_down=moe_w_down, moe_b_down=moe_b_down)

    cond = jnp.concatenate([c_ctx[None, :], c, jnp.zeros((N_COND - 1 - DEC_BATCH, D_MODEL), F32)])
    mod = _modulation(cond, w_mod, b_mod).reshape(DEPTH, N_COND, 6, D_MODEL)

    ones64 = jnp.asarray(np.kron(np.eye(N_HEADS), np.ones((HEAD_DIM, HEAD_DIM))), BF16)
    consts_ctx = dict(_seq_constants(SEQ), ones64=ones64)
    consts_lat = dict(_seq_constants(DEC_SEQ), ones64=ones64, **_rope_constants())
    ck = cache_k.reshape(DEC_BATCH, DEPTH, PAST_LEN, KV_WIDTH)
    cv = cache_v.reshape(DEC_BATCH, DEPTH, PAST_LEN, KV_WIDTH)

    x_parts = (x_prompt.reshape(T_CTX, D_MODEL), x_sample.reshape(T_LAT, D_MODEL))
    new_k, new_v = [], []
    for l in range(DEPTH):
        lw = _layer_weights(p, l)
        xp, xq, qkv, uv, gates = _in_projection(x_parts, mod[l], norm_mix_g[l], w_in, l)
        pool_c, four_c, sgu_c = _mixers(xp, xq, uv, consts_ctx, lw, SEQ, BATCH, 0, CTX_SEQS_PER_STEP)
        pool_l, four_l, sgu_l = _mixers(xp, xq, uv, consts_lat, lw, DEC_SEQ, DEC_BATCH,
                                        T_CTX // DEC_SEQ, 1)
        attn_c, k_c, v_c = _attention_ctx(qkv, consts_ctx, lw)
        attn_l = _attention_lat(qkv, ck[:, l], cv[:, l], consts_lat, lw)
        new_k.append(k_c.reshape(BATCH, SEQ, N_KV_HEADS, HEAD_DIM))
        new_v.append(v_c.reshape(BATCH, SEQ, N_KV_HEADS, HEAD_DIM))
        x, h, top_idx, top_w = _merge((pool_c, pool_l), (four_c, four_l), (attn_c, attn_l),
                                      (sgu_c, sgu_l), gates, x_parts, mod[l], norm_ffn_g[l], lw, p, l)
        slot, tiles = _route(top_idx)
        ys = _routed_ffn(tiles, _dispatch(tiles["rows"], slot, h), l, p)
        x_parts = tuple(_combine(slot, top_w.T, x, mod[l], final_norm_g, ys,
                                 final_norm=(l == DEPTH - 1)))

    y_prompt = x_parts[0].reshape(BATCH, SEQ, D_MODEL)
    y_sample = x_parts[1].reshape(DEC_BATCH, DEC_SEQ, D_MODEL)
    return (y_prompt, y_sample, jnp.stack(new_k, axis=1), jnp.stack(new_v, axis=1))
```

```python
import functools
import math

import numpy as np
import jax
import jax.numpy as jnp
from jax import lax
from jax.experimental import pallas as pl
from jax.experimental.pallas import tpu as pltpu

F32 = jnp.float32
BF16 = jnp.bfloat16

D_MODEL = 1024
BATCH = 32
SEQ = 256
DEPTH = 2
DEC_BATCH = 2
DEC_SEQ = 1024
PAST_LEN = 256
GRID_W = 64
EPS = 1e-6
POOL_GROUPS = 4
POOL_CH = 96
POOL_WIDTH = 384
POOL_WINDOWS = (2, 4, 8, 16)
POOL_PAD = 16
FFT_CH = 96
FFT_WIDTH = 384
N_HEADS = 8
N_KV_HEADS = 2
HEAD_DIM = 64
ATTN_WIDTH = 512
KV_WIDTH = 128
QKV_WIDTH = ATTN_WIDTH + 2 * KV_WIDTH
ROPE_THETA = 10000.0
ROPE_PAIRS = 16
SGU_GROUPS = 4
SGU_CH = 96
SGU_WIDTH = 384
SGU_CHUNK = 128
N_BRANCHES = 4
GATE_WIDTH = N_BRANCHES * D_MODEL
IN_COLS = POOL_WIDTH + FFT_WIDTH + QKV_WIDTH + 2 * SGU_WIDTH + GATE_WIDTH
N_EXPERTS = 32
TOP_K = 4
D_FF = 1024
SWIGLU_LIMIT = 7.0
SWIGLU_ALPHA = 1.702

ROW_GROUP = 8
LANE_TILES = D_MODEL // 128

T_CTX = BATCH * SEQ
T_LAT = DEC_BATCH * DEC_SEQ
T_ALL = T_CTX + T_LAT
N_COND = 8
VMEM_LIMIT = 56 * 1024 * 1024
CAST_ROWS = 32

COL_XP = 0
COL_XQ = COL_XP + POOL_WIDTH
COL_QKV = COL_XQ + FFT_WIDTH
COL_UV = COL_QKV + QKV_WIDTH
COL_GATE = COL_UV + 2 * SGU_WIDTH


def _params(*sem):
    return pltpu.CompilerParams(dimension_semantics=sem, vmem_limit_bytes=VMEM_LIMIT)


def _split_bf16(x):
    hi = x.astype(BF16)
    lo = (x - hi.astype(F32)).astype(BF16)
    return hi, lo


def _dot(a, b):
    return jnp.dot(a, b, preferred_element_type=F32)


def _dot_nt(a, b):
    return lax.dot_general(a, b, (((1,), (1,)), ((), ())), preferred_element_type=F32)


def _group_lane_select(lane, vals, width):
    out = vals[-1]
    for g in range(len(vals) - 2, -1, -1):
        out = jnp.where(lane < (g + 1) * width, vals[g], out)
    return out


def _cond_row(blk, blocks_ctx, blocks_per_latent):
    return jnp.where(blk < blocks_ctx, 0, 1 + (blk - blocks_ctx) // blocks_per_latent)


MOD_TN = 3072


def _mod_kernel(c_ref, w_ref, b_ref, o_ref):
    c = c_ref[...]
    s = c * jax.nn.sigmoid(c)
    sh, sl = _split_bf16(s)
    wh, wl = _split_bf16(w_ref[...])
    o_ref[...] = _dot(sh, wh) + _dot(sh, wl) + _dot(sl, wh) + b_ref[...]


def _modulation(cond, w_mod, b_mod):
    n_cols = 6 * D_MODEL
    return pl.pallas_call(
        _mod_kernel,
        grid=(DEPTH, n_cols // MOD_TN),
        in_specs=[
            pl.BlockSpec((N_COND, D_MODEL), lambda l, j: (0, 0)),
            pl.BlockSpec((None, D_MODEL, MOD_TN), lambda l, j: (l, 0, j)),
            pl.BlockSpec((None, 1, MOD_TN), lambda l, j: (l, 0, j)),
        ],
        out_specs=pl.BlockSpec((None, N_COND, MOD_TN), lambda l, j: (l, 0, j)),
        out_shape=jax.ShapeDtypeStruct((DEPTH, N_COND, n_cols), F32),
        compiler_params=_params("arbitrary", "arbitrary"),
        name="modulation",
    )(cond, w_mod, b_mod.reshape(DEPTH, 1, n_cols))


INPROJ_TM = 256
INPROJ_SEGMENTS = (
    (COL_XP, POOL_WIDTH), (COL_XQ, FFT_WIDTH), (COL_QKV, QKV_WIDTH),
    (COL_UV, 2 * SGU_WIDTH), (COL_GATE, GATE_WIDTH))
INPROJ_CHUNK = 1024


def _ada_norm(x, g, shift, scale):
    xn = x * lax.rsqrt(jnp.mean(x * x, axis=-1, keepdims=True) + EPS)
    return xn * g * (1.0 + scale) + shift


def _token_specs(parts, tm, width):
    if len(parts) == 1:
        return [pl.BlockSpec((tm, width), lambda i: (i, 0))]
    nc = T_CTX // tm
    return [pl.BlockSpec((tm, width), lambda i: (jnp.minimum(i, nc - 1), 0)),
            pl.BlockSpec((tm, width), lambda i: (jnp.maximum(i - nc, 0), 0))]


def _token_load(refs, tm):
    if len(refs) == 1:
        return refs[0][...]
    return jnp.where(pl.program_id(0) < T_CTX // tm, refs[0][...], refs[1][...])


W_IN_CHUNK = 640


def _inproj_kernel(*refs, n_x, layer):
    x_refs, (mod_ref, g_ref, w_hbm) = refs[:n_x], refs[n_x:n_x + 3]
    out_refs, (w_ref, stage, sems) = refs[n_x + 3:-3], refs[-3:]

    @pl.when(pl.program_id(0) == 0)
    def _():
        def chunk(c):
            return pltpu.make_async_copy(w_hbm.at[layer, :, pl.ds(c * W_IN_CHUNK, W_IN_CHUNK)],
                                         stage.at[c % 2], sems.at[c % 2])

        n_chunks = IN_COLS // W_IN_CHUNK
        chunk(0).start()
        for c in range(n_chunks):
            if c + 1 < n_chunks:
                chunk(c + 1).start()
            chunk(c).wait()

            def cast_rows(r, carry, c=c):
                rows = pl.ds(pl.multiple_of(r * CAST_ROWS, CAST_ROWS), CAST_ROWS)
                w_ref[rows, c * W_IN_CHUNK:(c + 1) * W_IN_CHUNK] = stage[c % 2, rows, :].astype(BF16)
                return carry

            lax.fori_loop(0, D_MODEL // CAST_ROWS, cast_rows, 0)

    h = _ada_norm(_token_load(x_refs, INPROJ_TM), g_ref[...], mod_ref[0:1, :], mod_ref[1:2, :])
    hb = h.astype(BF16)
    for (col, width), o_ref in zip(INPROJ_SEGMENTS, out_refs):
        for c0 in range(0, width, INPROJ_CHUNK):
            c1 = min(c0 + INPROJ_CHUNK, width)
            proj = _dot(hb, w_ref[:, col + c0:col + c1])
            o_ref[:, c0:c1] = jax.nn.sigmoid(proj) if col == COL_GATE else proj


def _in_projection(x_parts, mod_l, g, w_in, l):
    tm = INPROJ_TM
    row = functools.partial(_cond_row, blocks_ctx=T_CTX // tm, blocks_per_latent=DEC_SEQ // tm)
    return pl.pallas_call(
        functools.partial(_inproj_kernel, n_x=len(x_parts), layer=l),
        grid=(T_ALL // tm,),
        in_specs=_token_specs(x_parts, tm, D_MODEL) + [
            pl.BlockSpec((None, 6, D_MODEL), lambda i: (row(i), 0, 0)),
            pl.BlockSpec((1, D_MODEL), lambda i: (0, 0)),
            pl.BlockSpec(memory_space=pl.ANY),
        ],
        out_specs=[pl.BlockSpec((tm, w), lambda i: (i, 0)) for _, w in INPROJ_SEGMENTS],
        out_shape=[jax.ShapeDtypeStruct((T_ALL, w), F32) for _, w in INPROJ_SEGMENTS],
        scratch_shapes=[pltpu.VMEM((D_MODEL, IN_COLS), BF16), pltpu.VMEM((2, D_MODEL, W_IN_CHUNK), F32),
                        pltpu.SemaphoreType.DMA((2,))],
        compiler_params=_params("arbitrary"),
        name="in_projection",
    )(*x_parts, mod_l, g.reshape(1, D_MODEL), w_in)


CTX_SEQS_PER_STEP = 4


def _pool_mixer(xp, invcnt, w_bd, scale):
    s = xp.shape[0]
    n = s + 2 * POOL_PAD
    zeros = jnp.zeros((POOL_PAD, POOL_WIDTH), F32)
    xe = jnp.concatenate([zeros, xp, zeros], axis=0)

    def shift(a, k):
        return pltpu.roll(a, k % n, 0)

    s2 = xe + shift(xe, 1)
    s4 = shift(s2, 1) + shift(s2, -1)
    s8 = shift(s4, 2) + shift(s4, -2)
    s16 = shift(s8, 4) + shift(s8, -4)
    lane = lax.broadcasted_iota(jnp.int32, (1, POOL_WIDTH), 1)
    total = _group_lane_select(lane, [s2, s4, s8, s16], POOL_CH)[POOL_PAD:POOL_PAD + s]
    pooled = total * invcnt - xp
    return _dot(pooled.astype(BF16), w_bd) * scale


def _fourier_mixer(xq, f_cos_ch, f_sin_ch, f_pos):
    xb = xq.astype(BF16)
    a = _dot(xb, f_cos_ch).astype(BF16)
    b = _dot(xb, f_sin_ch).astype(BF16)
    return _dot(f_pos, jnp.concatenate([a, b], axis=0))


def _group_mean_sq(x, ones_bd, width):
    hi, lo = _split_bf16(x * x)
    return (_dot(hi, ones_bd) + _dot(lo, ones_bd)) * (1.0 / width)


def _mixers_kernel(xp_ref, xq_ref, uv_ref, invcnt_ref, pool_w_ref, pool_s_ref, fcc_ref, fsc_ref,
                   fpos_ref, ones_ref, sgu_w_ref, sgu_b_ref, pool_o, four_o, sgu_o, *, seq):
    lane = lax.broadcasted_iota(jnp.int32, (1, SGU_WIDTH), 1)
    w_stack = sgu_w_ref[...]
    bias = sgu_b_ref[...]
    for b in range(xp_ref.shape[0] // seq):
        rows = slice(b * seq, (b + 1) * seq)
        pool_o[rows, :] = _pool_mixer(xp_ref[rows, :], invcnt_ref[...], pool_w_ref[...],
                                      pool_s_ref[...]).astype(BF16)
        four_o[rows, :] = _fourier_mixer(xq_ref[rows, :], fcc_ref[...], fsc_ref[...],
                                         fpos_ref[...]).astype(BF16)

        act = jax.nn.gelu(uv_ref[rows, :], approximate=True)
        u = act[:, :SGU_WIDTH]
        v = act[:, SGU_WIDTH:]
        vg = (v * lax.rsqrt(_group_mean_sq(v, ones_ref[...], SGU_CH) + EPS)).astype(BF16)
        for n in range(seq // SGU_CHUNK):
            chunk = slice(n * SGU_CHUNK, (n + 1) * SGU_CHUNK)
            r = _dot(w_stack, vg[chunk])
            per_group = [r[g * SGU_CHUNK:(g + 1) * SGU_CHUNK] for g in range(SGU_GROUPS)]
            spatial = _group_lane_select(lane, per_group, SGU_CH) + bias
            out_rows = slice(b * seq + n * SGU_CHUNK, b * seq + (n + 1) * SGU_CHUNK)
            sgu_o[out_rows, :] = (u[chunk] * spatial).astype(BF16)


def _mixers(xp, xq, uv, consts, lw, seq, n_seq, block0, per_step):
    rows = per_step * seq
    full = lambda shape: pl.BlockSpec(shape, lambda b: (0,) * len(shape))
    tok = lambda w: pl.BlockSpec((rows, w), lambda b: (block0 + b, 0))
    out = lambda: pl.BlockSpec((rows, POOL_WIDTH), lambda b: (b, 0))
    return pl.pallas_call(
        functools.partial(_mixers_kernel, seq=seq),
        grid=(n_seq // per_step,),
        in_specs=[
            tok(POOL_WIDTH), tok(FFT_WIDTH), tok(2 * SGU_WIDTH),
            full((seq, POOL_WIDTH)), full((POOL_WIDTH, POOL_WIDTH)), full((1, POOL_WIDTH)),
            full((FFT_WIDTH, FFT_WIDTH)), full((FFT_WIDTH, FFT_WIDTH)), full((seq, 2 * seq)),
            full((SGU_WIDTH, SGU_WIDTH)), full((SGU_GROUPS * SGU_CHUNK, SGU_CHUNK)),
            full((SGU_CHUNK, SGU_WIDTH)),
        ],
        out_specs=[out(), out(), out()],
        out_shape=[jax.ShapeDtypeStruct((n_seq * seq, POOL_WIDTH), BF16)] * 3,
        compiler_params=_params("arbitrary"),
        name=f"mixers_s{seq}",
    )(xp, xq, uv, consts["invcnt"], lw["pool_w_bd"], lw["pool_scale"], consts["f_cos_ch"],
      consts["f_sin_ch"], consts["f_pos"], consts["ones96"], lw["sgu_w_stack"], lw["sgu_bias"])


def _head_norm(x, ones_bd, g):
    return x * lax.rsqrt(_group_mean_sq(x, ones_bd, HEAD_DIM) + EPS) * g


def _rope(x, cos, sin_next, sin_prev):
    cols = []
    for c in range(x.shape[1] // 128):
        xc = x[:, c * 128:(c + 1) * 128]
        nxt = pltpu.roll(xc, 127, 1)
        prv = pltpu.roll(xc, 1, 1)
        cols.append(xc * cos + nxt * sin_next + prv * sin_prev)
    return cols[0] if len(cols) == 1 else jnp.concatenate(cols, axis=1)


def _attend(q, keys, vals, o_ref, stack):
    sq = q.shape[0]
    qb = (q * (HEAD_DIM ** -0.5)).astype(BF16)
    group = N_HEADS // N_KV_HEADS
    head = lambda a, h: a[:, h * HEAD_DIM:(h + 1) * HEAD_DIM]
    for h0 in range(0, N_HEADS, stack):
        j = h0 // group
        qs = head(qb, h0) if stack == 1 else jnp.concatenate(
            [head(qb, h0 + g) for g in range(stack)], axis=0)
        s = _dot_nt(qs, head(keys, j))
        p = jnp.exp(s - jnp.max(s, axis=-1, keepdims=True))
        denom = jnp.sum(p, axis=-1, keepdims=True)
        o = (_dot(p.astype(BF16), head(vals, j)) / denom).astype(BF16)
        for g in range(stack):
            o_ref[:, (h0 + g) * HEAD_DIM:(h0 + g + 1) * HEAD_DIM] = o[g * sq:(g + 1) * sq]


def _attn_ctx_kernel(qkv_ref, gq_ref, gk_ref, ones_ref, o_ref, k_ref, v_ref):
    qkv = qkv_ref[...]
    ones = ones_ref[...]
    q = _head_norm(qkv[:, :ATTN_WIDTH], ones, gq_ref[...])
    k = _head_norm(qkv[:, ATTN_WIDTH:ATTN_WIDTH + KV_WIDTH], ones[:KV_WIDTH, :KV_WIDTH], gk_ref[...])
    v = qkv[:, ATTN_WIDTH + KV_WIDTH:]
    k_ref[...] = k
    v_ref[...] = v
    _attend(q, k.astype(BF16), v.astype(BF16), o_ref, stack=2)


def _attention_ctx(qkv, consts, lw):
    full = lambda shape: pl.BlockSpec(shape, lambda b: (0,) * len(shape))
    return pl.pallas_call(
        _attn_ctx_kernel,
        grid=(BATCH,),
        in_specs=[pl.BlockSpec((SEQ, QKV_WIDTH), lambda b: (b, 0)),
                  full((1, ATTN_WIDTH)), full((1, KV_WIDTH)), full((ATTN_WIDTH, ATTN_WIDTH))],
        out_specs=[pl.BlockSpec((SEQ, ATTN_WIDTH), lambda b: (b, 0)),
                   pl.BlockSpec((SEQ, KV_WIDTH), lambda b: (b, 0)),
                   pl.BlockSpec((SEQ, KV_WIDTH), lambda b: (b, 0))],
        out_shape=[jax.ShapeDtypeStruct((T_CTX, ATTN_WIDTH), BF16),
                   jax.ShapeDtypeStruct((T_CTX, KV_WIDTH), F32),
                   jax.ShapeDtypeStruct((T_CTX, KV_WIDTH), F32)],
        compiler_params=_params("arbitrary"),
        name="attention_ctx",
    )(qkv, lw["gq"], lw["gk"], consts["ones64"])


LAT_QBLK = 512


def _attn_lat_kernel(q_ref, kv_ref, ck_ref, cv_ref, gq_ref, gk_ref, ones_ref, cos_q, sn_q, sp_q,
                     cos_k, sn_k, sp_k, o_ref, keys, vals):
    ones = ones_ref[...]

    @pl.when(pl.program_id(1) == 0)
    def _():
        kv = kv_ref[...]
        k = _head_norm(kv[:, :KV_WIDTH], ones[:KV_WIDTH, :KV_WIDTH], gk_ref[...])
        keys[0:DEC_SEQ, :] = _rope(k, cos_k[...], sn_k[...], sp_k[...]).astype(BF16)
        keys[DEC_SEQ:, :] = ck_ref[...].astype(BF16)
        vals[0:DEC_SEQ, :] = kv[:, KV_WIDTH:].astype(BF16)
        vals[DEC_SEQ:, :] = cv_ref[...].astype(BF16)

    q = _head_norm(q_ref[...], ones, gq_ref[...])
    q = _rope(q, cos_q[...], sn_q[...], sp_q[...])
    _attend(q, keys[...], vals[...], o_ref, stack=1)


def _attention_lat(qkv, cache_k_l, cache_v_l, consts, lw):
    nq = DEC_SEQ // LAT_QBLK
    q0 = T_CTX // LAT_QBLK
    s0 = T_CTX // DEC_SEQ
    full = lambda shape: pl.BlockSpec(shape, lambda b, j: (0,) * len(shape))
    rope_q = lambda: pl.BlockSpec((LAT_QBLK, 128), lambda b, j: (j, 0))
    rope_k = lambda: pl.BlockSpec((DEC_SEQ, 128), lambda b, j: (0, 0))
    return pl.pallas_call(
        _attn_lat_kernel,
        grid=(DEC_BATCH, nq),
        in_specs=[
            pl.BlockSpec((LAT_QBLK, ATTN_WIDTH), lambda b, j: (q0 + b * nq + j, 0)),
            pl.BlockSpec((DEC_SEQ, 2 * KV_WIDTH), lambda b, j: (s0 + b, ATTN_WIDTH // (2 * KV_WIDTH))),
            pl.BlockSpec((None, PAST_LEN, KV_WIDTH), lambda b, j: (b, 0, 0)),
            pl.BlockSpec((None, PAST_LEN, KV_WIDTH), lambda b, j: (b, 0, 0)),
            full((1, ATTN_WIDTH)), full((1, KV_WIDTH)), full((ATTN_WIDTH, ATTN_WIDTH)),
            rope_q(), rope_q(), rope_q(), rope_k(), rope_k(), rope_k(),
        ],
        out_specs=pl.BlockSpec((LAT_QBLK, ATTN_WIDTH), lambda b, j: (b * nq + j, 0)),
        out_shape=jax.ShapeDtypeStruct((T_LAT, ATTN_WIDTH), BF16),
        scratch_shapes=[pltpu.VMEM((DEC_SEQ + PAST_LEN, KV_WIDTH), BF16),
                        pltpu.VMEM((DEC_SEQ + PAST_LEN, KV_WIDTH), BF16)],
        compiler_params=_params("arbitrary", "arbitrary"),
        name="attention_lat",
    )(qkv, qkv, cache_k_l, cache_v_l, lw["gq"], lw["gk"], consts["ones64"],
      consts["rope_cos"], consts["rope_sin_next"], consts["rope_sin_prev"],
      consts["rope_cos"], consts["rope_sin_next"], consts["rope_sin_prev"])


MERGE_TM = 512


def _merge_kernel(*refs, n_x):
    branch_refs, gate_ref, x_refs = refs[:8], refs[8], refs[9:9 + n_x]
    (mod_ref, g_ref, wp_ref, wf_ref, wa_ref, ws_ref, wo_ref, rw_ref, rb_ref,
     x_o, h_o, idx_o, wgt_o) = refs[9 + n_x:]
    merged = None
    for i, w_ref in enumerate((wp_ref, wf_ref, wa_ref, ws_ref)):
        br = _dot(_token_load(branch_refs[2 * i:2 * i + 2], MERGE_TM), w_ref[...])
        term = gate_ref[:, i * D_MODEL:(i + 1) * D_MODEL] * br
        merged = term if merged is None else merged + term
    mix = _dot(merged.astype(BF16), wo_ref[...])
    x = _token_load(x_refs, MERGE_TM) + mod_ref[2:3, :] * mix
    x_o[...] = x
    h = _ada_norm(x, g_ref[...], mod_ref[3:4, :], mod_ref[4:5, :])
    for c in range(LANE_TILES):
        h_o[:, c, :, :] = h[:, c * 128:(c + 1) * 128].reshape(MERGE_TM // ROW_GROUP, ROW_GROUP, 128)

    hh, hl = _split_bf16(h)
    rh, rl = _split_bf16(rw_ref[...])
    logits = _dot_nt(rh, hh) + _dot_nt(rl, hh) + _dot_nt(rh, hl) + rb_ref[...]
    expert = lax.broadcasted_iota(jnp.int32, logits.shape, 0).astype(F32)
    work = logits
    top = jnp.max(logits, axis=0, keepdims=True)
    idx, wgt = [], []
    denom = jnp.zeros_like(top)
    for _ in range(TOP_K):
        m = jnp.max(work, axis=0, keepdims=True)
        first = jnp.min(jnp.where(work == m, expert, float(N_EXPERTS)), axis=0, keepdims=True)
        e = jnp.exp(m - top)
        idx.append(first)
        wgt.append(e)
        denom = denom + e
        work = jnp.where(expert == first, -jnp.inf, work)
    idx_o[...] = jnp.concatenate(idx, axis=0).astype(jnp.int32)
    wgt_o[...] = jnp.concatenate(wgt, axis=0) / denom


def _merge(pool, four, attn, sgu, gates, x_parts, mod_l, g_ffn, lw):
    tm = MERGE_TM
    row = functools.partial(_cond_row, blocks_ctx=T_CTX // tm, blocks_per_latent=DEC_SEQ // tm)
    full = lambda shape: pl.BlockSpec(shape, lambda i: (0,) * len(shape))
    tok = lambda w: pl.BlockSpec((tm, w), lambda i: (i, 0))
    branch_specs = []
    for pair, width in ((pool, POOL_WIDTH), (four, FFT_WIDTH), (attn, ATTN_WIDTH), (sgu, SGU_WIDTH)):
        branch_specs += _token_specs(pair, tm, width)
    return pl.pallas_call(
        functools.partial(_merge_kernel, n_x=len(x_parts)),
        grid=(T_ALL // tm,),
        in_specs=branch_specs + [tok(GATE_WIDTH)] + _token_specs(x_parts, tm, D_MODEL) + [
            pl.BlockSpec((None, 6, D_MODEL), lambda i: (row(i), 0, 0)),
            full((1, D_MODEL)),
            full((POOL_WIDTH, D_MODEL)), full((FFT_WIDTH, D_MODEL)), full((ATTN_WIDTH, D_MODEL)),
            full((SGU_WIDTH, D_MODEL)), full((D_MODEL, D_MODEL)),
            full((N_EXPERTS, D_MODEL)), full((N_EXPERTS, 1)),
        ],
        out_specs=[tok(D_MODEL),
                   pl.BlockSpec((tm // ROW_GROUP, LANE_TILES, ROW_GROUP, 128), lambda i: (i, 0, 0, 0)),
                   pl.BlockSpec((TOP_K, tm), lambda i: (0, i)), pl.BlockSpec((TOP_K, tm), lambda i: (0, i))],
        out_shape=[jax.ShapeDtypeStruct((T_ALL, D_MODEL), F32),
                   jax.ShapeDtypeStruct((T_ALL // ROW_GROUP, LANE_TILES, ROW_GROUP, 128), F32),
                   jax.ShapeDtypeStruct((TOP_K, T_ALL), jnp.int32),
                   jax.ShapeDtypeStruct((TOP_K, T_ALL), F32)],
        compiler_params=_params("arbitrary"),
        name="merge_router",
    )(*pool, *four, *attn, *sgu, gates, *x_parts, mod_l, g_ffn.reshape(1, D_MODEL),
      lw["w_br_pool"], lw["w_br_fourier"], lw["w_br_attn"], lw["w_br_sgu"], lw["w_out"],
      lw["router_w"], lw["router_b"])


N_PAIRS = T_ALL * TOP_K
FFN_TM = 768
FFN_ROW_OPTIONS = (128, 256, 384, 512, 640, 768)
N_ROW_TILES = -(-N_PAIRS // FFN_TM) + N_EXPERTS
N_SLOTS = N_ROW_TILES * FFN_TM
ROUTE_TB = 512


def _route(top_idx):
    experts = jnp.arange(N_EXPERTS, dtype=jnp.int32)
    onehot = (top_idx.T[:, :, None] == experts[None, None, :]).astype(jnp.int32)
    per_token = jnp.sum(onehot, axis=1)
    csum = jnp.cumsum(per_token, axis=0)
    counts = csum[-1]
    tiles = (counts + FFN_TM - 1) // FFN_TM
    tile_end = jnp.cumsum(tiles)
    row_start = (tile_end - tiles) * FFN_TM
    first_row = csum - per_token + row_start[None, :]
    slot = jnp.sum(onehot * first_row[:, None, :], axis=2)
    tile_ids = jnp.arange(N_ROW_TILES, dtype=jnp.int32)
    tile_expert = jnp.sum((tile_end[None, :] <= tile_ids[:, None]).astype(jnp.int32), axis=1)
    tile_expert = jnp.minimum(tile_expert, N_EXPERTS - 1)
    rows_before = (tile_ids - (tile_end - tiles)[tile_expert]) * FFN_TM
    tile_rows = jnp.where(tile_ids < tile_end[-1],
                          jnp.clip(counts[tile_expert] - rows_before, 0, FFN_TM), 0)
    slot = slot.astype(jnp.int32).reshape(T_ALL // ROUTE_TB, 1, ROUTE_TB * TOP_K)
    used = (tiles > 0).astype(jnp.int32)
    rank = jnp.cumsum(used) - 1
    group_expert = jnp.sum(jnp.where((rank[None, :] == experts[:, None]) & (used[None, :] > 0),
                                     experts[None, :], 0), axis=1)
    tiles_info = dict(expert=tile_expert.astype(jnp.int32), rows=tile_rows.astype(jnp.int32),
                      group=rank[tile_expert].astype(jnp.int32),
                      group_expert=group_expert.astype(jnp.int32),
                      n_groups=jnp.sum(used).reshape(1).astype(jnp.int32),
                      n_used=tile_end[-1:].astype(jnp.int32))
    return slot, tiles_info


def _start_pair_rows(copy):
    def body(g, carry):
        for u in range(ROW_GROUP):
            for k in range(TOP_K):
                copy(g, u, k).start(priority=k % 2)
        return carry

    lax.fori_loop(0, ROUTE_TB // ROW_GROUP, body, 0)


def _pair_slot(slot_ref, g, u, k):
    return slot_ref[0, (g * ROW_GROUP + u) * TOP_K + k]


def _dispatch_kernel(tr_ref, slot_ref, h_ref, xs_out, zeros, sem, zero_sem):
    @pl.when(pl.program_id(0) == 0)
    def _():
        zeros[...] = jnp.zeros_like(zeros)

        def for_each_unfilled_tile(fn):
            group = FFN_ROW_OPTIONS[0]

            def body(i, carry):
                for full_groups in range(FFN_TM // group):
                    first = full_groups * group

                    @pl.when(tr_ref[i] // group == full_groups)
                    def _():
                        fn(pltpu.make_async_copy(zeros.at[pl.ds(0, FFN_TM - first)],
                                                 xs_out.at[pl.ds(i * FFN_TM + first, FFN_TM - first)],
                                                 zero_sem))
                return carry

            lax.fori_loop(0, N_ROW_TILES, body, 0)

        for_each_unfilled_tile(lambda c: c.start())
        for_each_unfilled_tile(lambda c: c.wait())

    def copy(g, u, k):
        return pltpu.make_async_copy(h_ref.at[g, :, u, :], xs_out.at[_pair_slot(slot_ref, g, u, k)], sem)

    _start_pair_rows(copy)
    for _ in range(TOP_K):
        pltpu.make_async_copy(xs_out.at[pl.ds(0, ROUTE_TB)], xs_out.at[pl.ds(0, ROUTE_TB)], sem).wait()


def _dispatch(tile_rows, slot, h):
    return pl.pallas_call(
        _dispatch_kernel,
        grid_spec=pltpu.PrefetchScalarGridSpec(
            num_scalar_prefetch=1,
            grid=(T_ALL // ROUTE_TB,),
            in_specs=[
                pl.BlockSpec((None, 1, ROUTE_TB * TOP_K), lambda i, tr: (i, 0, 0),
                             memory_space=pltpu.SMEM),
                pl.BlockSpec((ROUTE_TB // ROW_GROUP, LANE_TILES, ROW_GROUP, 128),
                             lambda i, tr: (i, 0, 0, 0)),
            ],
            out_specs=pl.BlockSpec(memory_space=pl.ANY),
            scratch_shapes=[pltpu.VMEM((FFN_TM, LANE_TILES, 128), F32), pltpu.SemaphoreType.DMA(()),
                            pltpu.SemaphoreType.DMA(())],
        ),
        out_shape=jax.ShapeDtypeStruct((N_SLOTS, LANE_TILES, 128), F32),
        compiler_params=_params("arbitrary"),
        name="moe_dispatch",
    )(tile_rows, slot, h)


def _expert_ffn(xb, wg, bg, wu, bu, wd, bd):
    gate = jnp.minimum(_dot(xb, wg) + bg, SWIGLU_LIMIT)
    up = jnp.clip(_dot(xb, wu) + bu, -SWIGLU_LIMIT, SWIGLU_LIMIT)
    glu = gate * jax.nn.sigmoid(SWIGLU_ALPHA * gate)
    return _dot(((up + 1.0) * glu).astype(BF16), wd) + bd


def _ffn_kernel(te_ref, tr_ref, tg_ref, ge_ref, ng_ref, nu_ref, xs_ref, wg_hbm, bg_ref, wu_hbm,
                bu_ref, wd_hbm, bd_ref, ys_ref, stage, w_bf16, xb, sems, *, layer):
    i = pl.program_id(0)
    n_rows = tr_ref[i]
    group = tg_ref[i]
    slot = group % 2

    def weight_copies(g, s):
        e = ge_ref[g]
        return [pltpu.make_async_copy(w.at[layer, e], stage.at[s, j], sems.at[s])
                for j, w in enumerate((wg_hbm, wu_hbm, wd_hbm))]

    def lane_tile(ref, c, rows):
        return ref.at[pl.ds(c, rows, stride=LANE_TILES), :]

    def ffn(rows):
        for c in range(LANE_TILES):
            xb[:rows, c * 128:(c + 1) * 128] = lane_tile(xs_ref, c, rows)[...].astype(BF16)
        y = _expert_ffn(xb[:rows, :], w_bf16[0], bg_ref[...], w_bf16[1], bu_ref[...], w_bf16[2],
                        bd_ref[...])
        for c in range(LANE_TILES):
            lane_tile(ys_ref, c, rows)[...] = y[:, c * 128:(c + 1) * 128]
        if rows < FFN_TM:
            ys_ref[rows * LANE_TILES:, :] = jnp.zeros(((FFN_TM - rows) * LANE_TILES, 128), F32)

    @pl.when(i == 0)
    def _():
        for c in weight_copies(0, 0):
            c.start()

    @pl.when((n_rows > 0) & ((i == 0) | (te_ref[i] != te_ref[jnp.maximum(i - 1, 0)])))
    def _():
        for c in weight_copies(group, slot):
            c.wait()
        def cast_rows(r, carry):
            rows = pl.ds(pl.multiple_of(r * CAST_ROWS, CAST_ROWS), CAST_ROWS)
            for j in range(3):
                w_bf16[j, rows, :] = stage[slot, j, rows, :].astype(BF16)
            return carry

        lax.fori_loop(0, D_MODEL // CAST_ROWS, cast_rows, 0)

        @pl.when(group + 1 < ng_ref[0])
        def _():
            for c in weight_copies(group + 1, 1 - slot):
                c.start()

    @pl.when(n_rows == 0)
    def _():
        ys_ref[...] = jnp.zeros_like(ys_ref)

    lower = 0
    for rows in FFN_ROW_OPTIONS:
        @pl.when((n_rows > lower) & (n_rows <= rows))
        def _(rows=rows):
            ffn(rows)
        lower = rows


def _routed_ffn(tiles, xs, l, p):
    n_prefetch = 6
    rows = pl.BlockSpec((FFN_TM * LANE_TILES, 128),
                        lambda i, te, tr, tg, ge, ng, nu: (jnp.minimum(i, nu[0] - 1), 0))
    bias_spec = lambda w: pl.BlockSpec((None, None, 1, w),
                                       lambda i, te, tr, tg, ge, ng, nu: (l, te[i], 0, 0))
    bias = lambda a: a.reshape(DEPTH, N_EXPERTS, 1, a.shape[-1])
    hbm = pl.BlockSpec(memory_space=pl.ANY)
    return pl.pallas_call(
        functools.partial(_ffn_kernel, layer=l),
        grid_spec=pltpu.PrefetchScalarGridSpec(
            num_scalar_prefetch=n_prefetch,
            grid=(N_ROW_TILES,),
            in_specs=[rows, hbm, bias_spec(D_FF), hbm, bias_spec(D_FF), hbm, bias_spec(D_MODEL)],
            out_specs=pl.BlockSpec((FFN_TM * LANE_TILES, 128),
                                   lambda i, te, tr, tg, ge, ng, nu: (i, 0)),
            scratch_shapes=[pltpu.VMEM((2, 3, D_MODEL, D_FF), F32), pltpu.VMEM((3, D_MODEL, D_FF), BF16),
                            pltpu.VMEM((FFN_TM, D_MODEL), BF16), pltpu.SemaphoreType.DMA((2,))],
        ),
        out_shape=jax.ShapeDtypeStruct((N_SLOTS * LANE_TILES, 128), F32),
        compiler_params=_params("arbitrary"),
        name="moe_ffn",
    )(tiles["expert"], tiles["rows"], tiles["group"], tiles["group_expert"], tiles["n_groups"],
      tiles["n_used"], xs.reshape(N_SLOTS * LANE_TILES, 128), p["moe_w_gate"], bias(p["moe_b_gate"]),
      p["moe_w_up"], bias(p["moe_b_up"]), p["moe_w_down"], bias(p["moe_b_down"]))


def _combine_kernel(slot_ref, slot_next_ref, w_ref, x_ref, mod_ref, fg_ref, ys_ref, *rest, final_norm):
    out_refs, (acc, buf, sems) = rest[:-3], rest[-3:]
    i = pl.program_id(0)
    cur = i % 2

    def copy(slots, b, g, u, k):
        return pltpu.make_async_copy(ys_ref.at[_pair_slot(slots, g, u, k)], buf.at[b, k, g, :, u, :],
                                     sems.at[b])

    @pl.when(i == 0)
    def _():
        _start_pair_rows(functools.partial(copy, slot_ref, 0))

    @pl.when(i + 1 < pl.num_programs(0))
    def _():
        _start_pair_rows(functools.partial(copy, slot_next_ref, 1 - cur))

    for k in range(TOP_K):
        pltpu.make_async_copy(ys_ref.at[pl.ds(0, ROUTE_TB)], ys_ref.at[pl.ds(0, ROUTE_TB)],
                              sems.at[cur]).wait()

    def rows_of(k, c):
        return buf[cur, k, :, c, :, :].reshape(ROUTE_TB, 128)

    w = w_ref[...]
    sum_sq = jnp.zeros((ROUTE_TB, 1), F32)
    for c in range(LANE_TILES):
        cols = slice(c * 128, (c + 1) * 128)
        mix = w[:, 0:1] * rows_of(0, c)
        for k in range(1, TOP_K):
            mix = mix + w[:, k:k + 1] * rows_of(k, c)
        piece = x_ref[:, cols] + mod_ref[5:6, cols] * mix
        sum_sq = sum_sq + jnp.sum(piece * piece, axis=-1, keepdims=True)
        acc[:, cols] = piece

    if not final_norm:
        out_refs[0][...] = acc[...]
        return
    normed = acc[...] * lax.rsqrt(sum_sq * (1.0 / D_MODEL) + EPS) * fg_ref[...]

    @pl.when(i < T_CTX // ROUTE_TB)
    def _():
        out_refs[0][...] = normed

    @pl.when(i >= T_CTX // ROUTE_TB)
    def _():
        out_refs[1][...] = normed


def _combine(slot, top_w, x, mod_l, final_g, ys, final_norm):
    tb = ROUTE_TB
    nb = T_ALL // tb
    nc = T_CTX // tb
    row = functools.partial(_cond_row, blocks_ctx=nc, blocks_per_latent=DEC_SEQ // tb)
    tok = lambda w: pl.BlockSpec((tb, w), lambda i: (i, 0))
    slots = lambda ahead: pl.BlockSpec((None, 1, tb * TOP_K),
                                       lambda i: (jnp.minimum(i + ahead, nb - 1), 0, 0),
                                       memory_space=pltpu.SMEM)
    if final_norm:
        out_specs = [pl.BlockSpec((tb, D_MODEL), lambda i: (jnp.minimum(i, nc - 1), 0)),
                     pl.BlockSpec((tb, D_MODEL), lambda i: (jnp.maximum(i - nc, 0), 0))]
        out_shape = [jax.ShapeDtypeStruct((T_CTX, D_MODEL), F32),
                     jax.ShapeDtypeStruct((T_LAT, D_MODEL), F32)]
    else:
        out_specs = [tok(D_MODEL)]
        out_shape = [jax.ShapeDtypeStruct((T_ALL, D_MODEL), F32)]
    return pl.pallas_call(
        functools.partial(_combine_kernel, final_norm=final_norm),
        grid=(nb,),
        in_specs=[
            slots(0), slots(1), tok(TOP_K), tok(D_MODEL),
            pl.BlockSpec((None, 6, D_MODEL), lambda i: (row(i), 0, 0)),
            pl.BlockSpec((1, D_MODEL), lambda i: (0, 0)),
            pl.BlockSpec(memory_space=pl.ANY),
        ],
        out_specs=out_specs,
        out_shape=out_shape,
        scratch_shapes=[pltpu.VMEM((tb, D_MODEL), F32),
                        pltpu.VMEM((2, TOP_K, tb // ROW_GROUP, LANE_TILES, ROW_GROUP, 128), F32),
                        pltpu.SemaphoreType.DMA((2,))],
        compiler_params=_params("arbitrary"),
        name="moe_combine",
    )(slot, slot, top_w, x, mod_l, final_g.reshape(1, D_MODEL), ys.reshape(N_SLOTS, LANE_TILES, 128))


def _block_diag(blocks):
    g, a, b = blocks.shape
    eye = jnp.eye(g, dtype=blocks.dtype)
    return (eye[:, None, :, None] * blocks[:, :, None, :]).reshape(g * a, g * b)


def _dft_tables(n):
    k = np.arange(n, dtype=np.int64)
    ang = 2.0 * np.pi * ((k[:, None] * k[None, :]) % n).astype(np.float64) / n
    return np.cos(ang) / math.sqrt(n), np.sin(ang) / math.sqrt(n)


def _seq_constants(seq):
    t = np.arange(seq)[:, None]
    win = np.array(POOL_WINDOWS)[None, :]
    lo = np.clip(t - win // 2, 0, seq)
    hi = np.clip(t - win // 2 + win, 0, seq)
    invcnt = np.repeat(1.0 / (hi - lo).astype(np.float64), POOL_CH, axis=1)
    cos_p, sin_p = _dft_tables(seq)
    cos_c, sin_c = _dft_tables(FFT_CH)
    eye = np.eye(POOL_GROUPS)
    as_bf16 = lambda a: jnp.asarray(a, F32).astype(BF16)
    return {
        "invcnt": jnp.asarray(invcnt, F32),
        "f_pos": as_bf16(np.concatenate([cos_p, -sin_p], axis=1)),
        "f_cos_ch": as_bf16(np.kron(eye, cos_c)),
        "f_sin_ch": as_bf16(np.kron(eye, sin_c)),
        "ones96": as_bf16(np.kron(eye, np.ones((SGU_CH, SGU_CH)))),
    }


def _rope_constants():
    rows = DEC_SEQ // GRID_W
    row = jnp.repeat(jnp.arange(rows, dtype=F32), GRID_W)
    col = jnp.tile(jnp.arange(GRID_W, dtype=F32), rows)
    freqs = ROPE_THETA ** (-jnp.arange(ROPE_PAIRS, dtype=F32) / ROPE_PAIRS)
    ang = jnp.stack([row[:, None] * freqs, col[:, None] * freqs], axis=1)
    ang = jnp.repeat(ang.reshape(DEC_SEQ, 2 * ROPE_PAIRS), 2, axis=1)
    ang = jnp.tile(ang, (1, 128 // HEAD_DIM))
    even = (jnp.arange(128) % 2 == 0)[None, :]
    sin = jnp.sin(ang)
    return {
        "rope_cos": jnp.cos(ang),
        "rope_sin_next": jnp.where(even, -sin, 0.0),
        "rope_sin_prev": jnp.where(even, 0.0, sin),
    }


def _layer_weights(p, l):
    group_of_lane = np.arange(SGU_WIDTH) // SGU_CH
    return {
        "pool_w_bd": _block_diag(p["pool_w"][l]).astype(BF16),
        "pool_scale": p["pool_scale"][l].reshape(1, POOL_WIDTH),
        "sgu_w_stack": p["sgu_w"][l].reshape(SGU_GROUPS * SGU_CHUNK, SGU_CHUNK).astype(BF16),
        "sgu_bias": p["sgu_b"][l].T[:, group_of_lane],
        "gq": jnp.tile(p["q_norm_g"][l], N_HEADS).reshape(1, ATTN_WIDTH),
        "gk": jnp.tile(p["k_norm_g"][l], N_KV_HEADS).reshape(1, KV_WIDTH),
        "w_br_pool": p["w_br_pool"][l].astype(BF16),
        "w_br_fourier": p["w_br_fourier"][l].astype(BF16),
        "w_br_attn": p["w_br_attn"][l].astype(BF16),
        "w_br_sgu": p["w_br_sgu"][l].astype(BF16),
        "w_out": p["w_out"][l].astype(BF16),
        "router_w": p["router_w"][l].T,
        "router_b": p["router_b"][l].reshape(N_EXPERTS, 1),
    }


def kernel(x_prompt, x_sample, cache_k, cache_v, c, c_ctx, w_mod, b_mod, norm_mix_g, norm_ffn_g, w_in, pool_w, pool_scale, q_norm_g, k_norm_g, sgu_w, sgu_b, w_br_pool, w_br_fourier, w_br_attn, w_br_sgu, w_out, router_w, router_b, moe_w_gate, moe_b_gate, moe_w_up, moe_b_up, moe_w_down, moe_b_down, final_norm_g):
    p = dict(w_in=w_in, pool_w=pool_w, pool_scale=pool_scale, q_norm_g=q_norm_g, k_norm_g=k_norm_g,
             sgu_w=sgu_w, sgu_b=sgu_b, w_br_pool=w_br_pool, w_br_fourier=w_br_fourier,
             w_br_attn=w_br_attn, w_br_sgu=w_br_sgu, w_out=w_out, router_w=router_w,
             router_b=router_b, moe_w_gate=moe_w_gate, moe_b_gate=moe_b_gate, moe_w_up=moe_w_up,
             moe_b_up=moe_b_up, moe_w_down=moe_w_down, moe_b_down=moe_b_down)

    cond = jnp.concatenate([c_ctx[None, :], c, jnp.zeros((N_COND - 1 - DEC_BATCH, D_MODEL), F32)])
    mod = _modulation(cond, w_mod, b_mod).reshape(DEPTH, N_COND, 6, D_MODEL)

    ones64 = jnp.asarray(np.kron(np.eye(N_HEADS), np.ones((HEAD_DIM, HEAD_DIM))), BF16)
    consts_ctx = dict(_seq_constants(SEQ), ones64=ones64)
    consts_lat = dict(_seq_constants(DEC_SEQ), ones64=ones64, **_rope_constants())
    ck = cache_k.reshape(DEC_BATCH, DEPTH, PAST_LEN, KV_WIDTH)
    cv = cache_v.reshape(DEC_BATCH, DEPTH, PAST_LEN, KV_WIDTH)

    x_parts = (x_prompt.reshape(T_CTX, D_MODEL), x_sample.reshape(T_LAT, D_MODEL))
    new_k, new_v = [], []
    for l in range(DEPTH):
        lw = _layer_weights(p, l)
        xp, xq, qkv, uv, gates = _in_projection(x_parts, mod[l], norm_mix_g[l], w_in, l)
        pool_c, four_c, sgu_c = _mixers(xp, xq, uv, consts_ctx, lw, SEQ, BATCH, 0, CTX_SEQS_PER_STEP)
        pool_l, four_l, sgu_l = _mixers(xp, xq, uv, consts_lat, lw, DEC_SEQ, DEC_BATCH,
                                        T_CTX // DEC_SEQ, 1)
        attn_c, k_c, v_c = _attention_ctx(qkv, consts_ctx, lw)
        attn_l = _attention_lat(qkv, ck[:, l], cv[:, l], consts_lat, lw)
        new_k.append(k_c.reshape(BATCH, SEQ, N_KV_HEADS, HEAD_DIM))
        new_v.append(v_c.reshape(BATCH, SEQ, N_KV_HEADS, HEAD_DIM))
        x, h, top_idx, top_w = _merge((pool_c, pool_l), (four_c, four_l), (attn_c, attn_l),
                                      (sgu_c, sgu_l), gates, x_parts, mod[l], norm_ffn_g[l], lw)
        slot, tiles = _route(top_idx)
        ys = _routed_ffn(tiles, _dispatch(tiles["rows"], slot, h), l, p)
        x_parts = tuple(_combine(slot, top_w.T, x, mod[l], final_norm_g, ys,
                                 final_norm=(l == DEPTH - 1)))

    y_prompt = x_parts[0].reshape(BATCH, SEQ, D_MODEL)
    y_sample = x_parts[1].reshape(DEC_BATCH, DEC_SEQ, D_MODEL)
    return (y_prompt, y_sample, jnp.stack(new_k, axis=1), jnp.stack(new_v, axis=1))
```

```python
import functools
import math

import numpy as np
import jax
import jax.numpy as jnp
from jax import lax
from jax.experimental import pallas as pl
from jax.experimental.pallas import tpu as pltpu

F32 = jnp.float32
BF16 = jnp.bfloat16

D_MODEL = 1024
BATCH = 32
SEQ = 256
DEPTH = 2
DEC_BATCH = 2
DEC_SEQ = 1024
PAST_LEN = 256
GRID_W = 64
EPS = 1e-6
POOL_GROUPS = 4
POOL_CH = 96
POOL_WIDTH = 384
POOL_WINDOWS = (2, 4, 8, 16)
POOL_PAD = 16
FFT_CH = 96
FFT_WIDTH = 384
N_HEADS = 8
N_KV_HEADS = 2
HEAD_DIM = 64
ATTN_WIDTH = 512
KV_WIDTH = 128
QKV_WIDTH = ATTN_WIDTH + 2 * KV_WIDTH
ROPE_THETA = 10000.0
ROPE_PAIRS = 16
SGU_GROUPS = 4
SGU_CH = 96
SGU_WIDTH = 384
SGU_CHUNK = 128
N_BRANCHES = 4
GATE_WIDTH = N_BRANCHES * D_MODEL
IN_COLS = POOL_WIDTH + FFT_WIDTH + QKV_WIDTH + 2 * SGU_WIDTH + GATE_WIDTH
N_EXPERTS = 32
TOP_K = 4
D_FF = 1024
SWIGLU_LIMIT = 7.0
SWIGLU_ALPHA = 1.702

ROW_GROUP = 8
LANE_TILES = D_MODEL // 128

T_CTX = BATCH * SEQ
T_LAT = DEC_BATCH * DEC_SEQ
T_ALL = T_CTX + T_LAT
N_COND = 8
VMEM_LIMIT = 56 * 1024 * 1024
CAST_ROWS = 32

COL_XP = 0
COL_XQ = COL_XP + POOL_WIDTH
COL_QKV = COL_XQ + FFT_WIDTH
COL_UV = COL_QKV + QKV_WIDTH
COL_GATE = COL_UV + 2 * SGU_WIDTH


def _params(*sem):
    return pltpu.CompilerParams(dimension_semantics=sem, vmem_limit_bytes=VMEM_LIMIT)


def _split_bf16(x):
    hi = x.astype(BF16)
    lo = (x - hi.astype(F32)).astype(BF16)
    return hi, lo


def _dot(a, b):
    return jnp.dot(a, b, preferred_element_type=F32)


def _dot_nt(a, b):
    return lax.dot_general(a, b, (((1,), (1,)), ((), ())), preferred_element_type=F32)


def _group_lane_select(lane, vals, width):
    out = vals[-1]
    for g in range(len(vals) - 2, -1, -1):
        out = jnp.where(lane < (g + 1) * width, vals[g], out)
    return out


def _cond_row(blk, blocks_ctx, blocks_per_latent):
    return jnp.where(blk < blocks_ctx, 0, 1 + (blk - blocks_ctx) // blocks_per_latent)


MOD_TN = 3072


def _mod_kernel(c_ref, w_ref, b_ref, o_ref):
    c = c_ref[...]
    s = c * jax.nn.sigmoid(c)
    sh, sl = _split_bf16(s)
    wh, wl = _split_bf16(w_ref[...])
    o_ref[...] = _dot(sh, wh) + _dot(sh, wl) + _dot(sl, wh) + b_ref[...]


def _modulation(cond, w_mod, b_mod):
    n_cols = 6 * D_MODEL
    return pl.pallas_call(
        _mod_kernel,
        grid=(DEPTH, n_cols // MOD_TN),
        in_specs=[
            pl.BlockSpec((N_COND, D_MODEL), lambda l, j: (0, 0)),
            pl.BlockSpec((None, D_MODEL, MOD_TN), lambda l, j: (l, 0, j)),
            pl.BlockSpec((None, 1, MOD_TN), lambda l, j: (l, 0, j)),
        ],
        out_specs=pl.BlockSpec((None, N_COND, MOD_TN), lambda l, j: (l, 0, j)),
        out_shape=jax.ShapeDtypeStruct((DEPTH, N_COND, n_cols), F32),
        compiler_params=_params("arbitrary", "arbitrary"),
        name="modulation",
    )(cond, w_mod, b_mod.reshape(DEPTH, 1, n_cols))


INPROJ_TM = 256
INPROJ_SEGMENTS = (
    (COL_XP, POOL_WIDTH), (COL_XQ, FFT_WIDTH), (COL_QKV, QKV_WIDTH),
    (COL_UV, 2 * SGU_WIDTH), (COL_GATE, GATE_WIDTH))
INPROJ_CHUNK = 1024


def _ada_norm(x, g, shift, scale):
    xn = x * lax.rsqrt(jnp.mean(x * x, axis=-1, keepdims=True) + EPS)
    return xn * g * (1.0 + scale) + shift


def _token_specs(parts, tm, width):
    if len(parts) == 1:
        return [pl.BlockSpec((tm, width), lambda i: (i, 0))]
    nc = T_CTX // tm
    return [pl.BlockSpec((tm, width), lambda i: (jnp.minimum(i, nc - 1), 0)),
            pl.BlockSpec((tm, width), lambda i: (jnp.maximum(i - nc, 0), 0))]


def _token_load(refs, tm):
    if len(refs) == 1:
        return refs[0][...]
    return jnp.where(pl.program_id(0) < T_CTX // tm, refs[0][...], refs[1][...])


W_IN_CHUNK = 640


def _inproj_kernel(*refs, n_x, layer):
    x_refs, (mod_ref, g_ref, w_hbm) = refs[:n_x], refs[n_x:n_x + 3]
    out_refs, (w_ref, stage, sems) = refs[n_x + 3:-3], refs[-3:]

    @pl.when(pl.program_id(0) == 0)
    def _():
        def chunk(c):
            return pltpu.make_async_copy(w_hbm.at[layer, :, pl.ds(c * W_IN_CHUNK, W_IN_CHUNK)],
                                         stage.at[c % 2], sems.at[c % 2])

        n_chunks = IN_COLS // W_IN_CHUNK
        chunk(0).start()
        for c in range(n_chunks):
            if c + 1 < n_chunks:
                chunk(c + 1).start()
            chunk(c).wait()

            def cast_rows(r, carry, c=c):
                rows = pl.ds(pl.multiple_of(r * CAST_ROWS, CAST_ROWS), CAST_ROWS)
                w_ref[rows, c * W_IN_CHUNK:(c + 1) * W_IN_CHUNK] = stage[c % 2, rows, :].astype(BF16)
                return carry

            lax.fori_loop(0, D_MODEL // CAST_ROWS, cast_rows, 0)

    h = _ada_norm(_token_load(x_refs, INPROJ_TM), g_ref[...], mod_ref[0:1, :], mod_ref[1:2, :])
    hb = h.astype(BF16)
    for (col, width), o_ref in zip(INPROJ_SEGMENTS, out_refs):
        for c0 in range(0, width, INPROJ_CHUNK):
            c1 = min(c0 + INPROJ_CHUNK, width)
            proj = _dot(hb, w_ref[:, col + c0:col + c1])
            o_ref[:, c0:c1] = jax.nn.sigmoid(proj) if col == COL_GATE else proj


def _in_projection(x_parts, mod_l, g, w_in, l):
    tm = INPROJ_TM
    row = functools.partial(_cond_row, blocks_ctx=T_CTX // tm, blocks_per_latent=DEC_SEQ // tm)
    return pl.pallas_call(
        functools.partial(_inproj_kernel, n_x=len(x_parts), layer=l),
        grid=(T_ALL // tm,),
        in_specs=_token_specs(x_parts, tm, D_MODEL) + [
            pl.BlockSpec((None, 6, D_MODEL), lambda i: (row(i), 0, 0)),
            pl.BlockSpec((1, D_MODEL), lambda i: (0, 0)),
            pl.BlockSpec(memory_space=pl.ANY),
        ],
        out_specs=[pl.BlockSpec((tm, w), lambda i: (i, 0)) for _, w in INPROJ_SEGMENTS],
        out_shape=[jax.ShapeDtypeStruct((T_ALL, w), F32) for _, w in INPROJ_SEGMENTS],
        scratch_shapes=[pltpu.VMEM((D_MODEL, IN_COLS), BF16), pltpu.VMEM((2, D_MODEL, W_IN_CHUNK), F32),
                        pltpu.SemaphoreType.DMA((2,))],
        compiler_params=_params("arbitrary"),
        name="in_projection",
    )(*x_parts, mod_l, g.reshape(1, D_MODEL), w_in)


CTX_SEQS_PER_STEP = 4


def _pool_mixer(xp, invcnt, w_bd, scale):
    s = xp.shape[0]
    n = s + 2 * POOL_PAD
    zeros = jnp.zeros((POOL_PAD, POOL_WIDTH), F32)
    xe = jnp.concatenate([zeros, xp, zeros], axis=0)

    def shift(a, k):
        return pltpu.roll(a, k % n, 0)

    s2 = xe + shift(xe, 1)
    s4 = shift(s2, 1) + shift(s2, -1)
    s8 = shift(s4, 2) + shift(s4, -2)
    s16 = shift(s8, 4) + shift(s8, -4)
    lane = lax.broadcasted_iota(jnp.int32, (1, POOL_WIDTH), 1)
    total = _group_lane_select(lane, [s2, s4, s8, s16], POOL_CH)[POOL_PAD:POOL_PAD + s]
    pooled = total * invcnt - xp
    return _dot(pooled.astype(BF16), w_bd) * scale


def _fourier_mixer(xq, f_cos_ch, f_sin_ch, f_pos):
    xb = xq.astype(BF16)
    a = _dot(xb, f_cos_ch).astype(BF16)
    b = _dot(xb, f_sin_ch).astype(BF16)
    return _dot(f_pos, jnp.concatenate([a, b], axis=0))


def _group_mean_sq(x, ones_bd, width):
    hi, lo = _split_bf16(x * x)
    return (_dot(hi, ones_bd) + _dot(lo, ones_bd)) * (1.0 / width)


def _mixers_kernel(xp_ref, xq_ref, uv_ref, invcnt_ref, pool_w_ref, pool_s_ref, fcc_ref, fsc_ref,
                   fpos_ref, ones_ref, sgu_w_ref, sgu_b_ref, pool_o, four_o, sgu_o, *, seq):
    lane = lax.broadcasted_iota(jnp.int32, (1, SGU_WIDTH), 1)
    w_stack = sgu_w_ref[...]
    bias = sgu_b_ref[...]
    for b in range(xp_ref.shape[0] // seq):
        rows = slice(b * seq, (b + 1) * seq)
        pool_o[rows, :] = _pool_mixer(xp_ref[rows, :], invcnt_ref[...], pool_w_ref[...],
                                      pool_s_ref[...]).astype(BF16)
        four_o[rows, :] = _fourier_mixer(xq_ref[rows, :], fcc_ref[...], fsc_ref[...],
                                         fpos_ref[...]).astype(BF16)

        act = jax.nn.gelu(uv_ref[rows, :], approximate=True)
        u = act[:, :SGU_WIDTH]
        v = act[:, SGU_WIDTH:]
        vg = (v * lax.rsqrt(_group_mean_sq(v, ones_ref[...], SGU_CH) + EPS)).astype(BF16)
        for n in range(seq // SGU_CHUNK):
            chunk = slice(n * SGU_CHUNK, (n + 1) * SGU_CHUNK)
            r = _dot(w_stack, vg[chunk])
            per_group = [r[g * SGU_CHUNK:(g + 1) * SGU_CHUNK] for g in range(SGU_GROUPS)]
            spatial = _group_lane_select(lane, per_group, SGU_CH) + bias
            out_rows = slice(b * seq + n * SGU_CHUNK, b * seq + (n + 1) * SGU_CHUNK)
            sgu_o[out_rows, :] = (u[chunk] * spatial).astype(BF16)


def _mixers(xp, xq, uv, consts, lw, seq, n_seq, block0, per_step):
    rows = per_step * seq
    full = lambda shape: pl.BlockSpec(shape, lambda b: (0,) * len(shape))
    tok = lambda w: pl.BlockSpec((rows, w), lambda b: (block0 + b, 0))
    out = lambda: pl.BlockSpec((rows, POOL_WIDTH), lambda b: (b, 0))
    return pl.pallas_call(
        functools.partial(_mixers_kernel, seq=seq),
        grid=(n_seq // per_step,),
        in_specs=[
            tok(POOL_WIDTH), tok(FFT_WIDTH), tok(2 * SGU_WIDTH),
            full((seq, POOL_WIDTH)), full((POOL_WIDTH, POOL_WIDTH)), full((1, POOL_WIDTH)),
            full((FFT_WIDTH, FFT_WIDTH)), full((FFT_WIDTH, FFT_WIDTH)), full((seq, 2 * seq)),
            full((SGU_WIDTH, SGU_WIDTH)), full((SGU_GROUPS * SGU_CHUNK, SGU_CHUNK)),
            full((SGU_CHUNK, SGU_WIDTH)),
        ],
        out_specs=[out(), out(), out()],
        out_shape=[jax.ShapeDtypeStruct((n_seq * seq, POOL_WIDTH), BF16)] * 3,
        compiler_params=_params("arbitrary"),
        name=f"mixers_s{seq}",
    )(xp, xq, uv, consts["invcnt"], lw["pool_w_bd"], lw["pool_scale"], consts["f_cos_ch"],
      consts["f_sin_ch"], consts["f_pos"], consts["ones96"], lw["sgu_w_stack"], lw["sgu_bias"])


def _head_norm(x, ones_bd, g):
    return x * lax.rsqrt(_group_mean_sq(x, ones_bd, HEAD_DIM) + EPS) * g


def _rope(x, cos, sin_next, sin_prev):
    cols = []
    for c in range(x.shape[1] // 128):
        xc = x[:, c * 128:(c + 1) * 128]
        nxt = pltpu.roll(xc, 127, 1)
        prv = pltpu.roll(xc, 1, 1)
        cols.append(xc * cos + nxt * sin_next + prv * sin_prev)
    return cols[0] if len(cols) == 1 else jnp.concatenate(cols, axis=1)


def _attend(q, keys, vals, o_ref, stack, row0=0):
    sq = q.shape[0]
    qb = (q * (HEAD_DIM ** -0.5)).astype(BF16)
    group = N_HEADS // N_KV_HEADS
    head = lambda a, h: a[:, h * HEAD_DIM:(h + 1) * HEAD_DIM]
    for h0 in range(0, N_HEADS, stack):
        j = h0 // group
        qs = head(qb, h0) if stack == 1 else jnp.concatenate(
            [head(qb, h0 + g) for g in range(stack)], axis=0)
        s = _dot_nt(qs, head(keys, j))
        p = jnp.exp(s - jnp.max(s, axis=-1, keepdims=True))
        denom = jnp.sum(p, axis=-1, keepdims=True)
        o = (_dot(p.astype(BF16), head(vals, j)) / denom).astype(BF16)
        for g in range(stack):
            o_ref[row0:row0 + sq, (h0 + g) * HEAD_DIM:(h0 + g + 1) * HEAD_DIM] = o[g * sq:(g + 1) * sq]


N_MIXER_CONSTS = 9


def _ctx_kernel(*refs):
    xp_ref, xq_ref, uv_ref, qkv_ref = refs[:4]
    mixer_consts = refs[4:4 + N_MIXER_CONSTS]
    gq_ref, gk_ref, ones_ref = refs[4 + N_MIXER_CONSTS:7 + N_MIXER_CONSTS]
    pool_o, four_o, sgu_o, o_ref, k_ref, v_ref = refs[7 + N_MIXER_CONSTS:]
    _mixers_kernel(xp_ref, xq_ref, uv_ref, *mixer_consts, pool_o, four_o, sgu_o, seq=SEQ)
    ones = ones_ref[...]
    for b in range(CTX_SEQS_PER_STEP):
        rows = slice(b * SEQ, (b + 1) * SEQ)
        qkv = qkv_ref[rows, :]
        q = _head_norm(qkv[:, :ATTN_WIDTH], ones, gq_ref[...])
        k = _head_norm(qkv[:, ATTN_WIDTH:ATTN_WIDTH + KV_WIDTH], ones[:KV_WIDTH, :KV_WIDTH],
                       gk_ref[...])
        v = qkv[:, ATTN_WIDTH + KV_WIDTH:]
        k_ref[rows, :] = k
        v_ref[rows, :] = v
        _attend(q, k.astype(BF16), v.astype(BF16), o_ref, stack=2, row0=b * SEQ)


def _context_mixing(xp, xq, uv, qkv, consts, lw):
    rows = CTX_SEQS_PER_STEP * SEQ
    full = lambda shape: pl.BlockSpec(shape, lambda b: (0,) * len(shape))
    tok = lambda w: pl.BlockSpec((rows, w), lambda b: (b, 0))
    return pl.pallas_call(
        _ctx_kernel,
        grid=(BATCH // CTX_SEQS_PER_STEP,),
        in_specs=[
            tok(POOL_WIDTH), tok(FFT_WIDTH), tok(2 * SGU_WIDTH), tok(QKV_WIDTH),
            full((SEQ, POOL_WIDTH)), full((POOL_WIDTH, POOL_WIDTH)), full((1, POOL_WIDTH)),
            full((FFT_WIDTH, FFT_WIDTH)), full((FFT_WIDTH, FFT_WIDTH)), full((SEQ, 2 * SEQ)),
            full((SGU_WIDTH, SGU_WIDTH)), full((SGU_GROUPS * SGU_CHUNK, SGU_CHUNK)),
            full((SGU_CHUNK, SGU_WIDTH)),
            full((1, ATTN_WIDTH)), full((1, KV_WIDTH)), full((ATTN_WIDTH, ATTN_WIDTH)),
        ],
        out_specs=[tok(POOL_WIDTH), tok(FFT_WIDTH), tok(SGU_WIDTH), tok(ATTN_WIDTH), tok(KV_WIDTH),
                   tok(KV_WIDTH)],
        out_shape=[jax.ShapeDtypeStruct((T_CTX, POOL_WIDTH), BF16)] * 3
        + [jax.ShapeDtypeStruct((T_CTX, ATTN_WIDTH), BF16),
           jax.ShapeDtypeStruct((T_CTX, KV_WIDTH), F32), jax.ShapeDtypeStruct((T_CTX, KV_WIDTH), F32)],
        compiler_params=_params("arbitrary"),
        name="context_mixing",
    )(xp, xq, uv, qkv, consts["invcnt"], lw["pool_w_bd"], lw["pool_scale"], consts["f_cos_ch"],
      consts["f_sin_ch"], consts["f_pos"], consts["ones96"], lw["sgu_w_stack"], lw["sgu_bias"],
      lw["gq"], lw["gk"], consts["ones64"])


LAT_QBLK = 512


def _attn_lat_kernel(q_ref, kv_ref, ck_ref, cv_ref, gq_ref, gk_ref, ones_ref, cos_q, sn_q, sp_q,
                     cos_k, sn_k, sp_k, o_ref, keys, vals):
    ones = ones_ref[...]

    @pl.when(pl.program_id(1) == 0)
    def _():
        kv = kv_ref[...]
        k = _head_norm(kv[:, :KV_WIDTH], ones[:KV_WIDTH, :KV_WIDTH], gk_ref[...])
        keys[0:DEC_SEQ, :] = _rope(k, cos_k[...], sn_k[...], sp_k[...]).astype(BF16)
        keys[DEC_SEQ:, :] = ck_ref[...].astype(BF16)
        vals[0:DEC_SEQ, :] = kv[:, KV_WIDTH:].astype(BF16)
        vals[DEC_SEQ:, :] = cv_ref[...].astype(BF16)

    q = _head_norm(q_ref[...], ones, gq_ref[...])
    q = _rope(q, cos_q[...], sn_q[...], sp_q[...])
    _attend(q, keys[...], vals[...], o_ref, stack=1)


def _attention_lat(qkv, cache_k_l, cache_v_l, consts, lw):
    nq = DEC_SEQ // LAT_QBLK
    q0 = T_CTX // LAT_QBLK
    s0 = T_CTX // DEC_SEQ
    full = lambda shape: pl.BlockSpec(shape, lambda b, j: (0,) * len(shape))
    rope_q = lambda: pl.BlockSpec((LAT_QBLK, 128), lambda b, j: (j, 0))
    rope_k = lambda: pl.BlockSpec((DEC_SEQ, 128), lambda b, j: (0, 0))
    return pl.pallas_call(
        _attn_lat_kernel,
        grid=(DEC_BATCH, nq),
        in_specs=[
            pl.BlockSpec((LAT_QBLK, ATTN_WIDTH), lambda b, j: (q0 + b * nq + j, 0)),
            pl.BlockSpec((DEC_SEQ, 2 * KV_WIDTH), lambda b, j: (s0 + b, ATTN_WIDTH // (2 * KV_WIDTH))),
            pl.BlockSpec((None, PAST_LEN, KV_WIDTH), lambda b, j: (b, 0, 0)),
            pl.BlockSpec((None, PAST_LEN, KV_WIDTH), lambda b, j: (b, 0, 0)),
            full((1, ATTN_WIDTH)), full((1, KV_WIDTH)), full((ATTN_WIDTH, ATTN_WIDTH)),
            rope_q(), rope_q(), rope_q(), rope_k(), rope_k(), rope_k(),
        ],
        out_specs=pl.BlockSpec((LAT_QBLK, ATTN_WIDTH), lambda b, j: (b * nq + j, 0)),
        out_shape=jax.ShapeDtypeStruct((T_LAT, ATTN_WIDTH), BF16),
        scratch_shapes=[pltpu.VMEM((DEC_SEQ + PAST_LEN, KV_WIDTH), BF16),
                        pltpu.VMEM((DEC_SEQ + PAST_LEN, KV_WIDTH), BF16)],
        compiler_params=_params("arbitrary", "arbitrary"),
        name="attention_lat",
    )(qkv, qkv, cache_k_l, cache_v_l, lw["gq"], lw["gk"], consts["ones64"],
      consts["rope_cos"], consts["rope_sin_next"], consts["rope_sin_prev"],
      consts["rope_cos"], consts["rope_sin_next"], consts["rope_sin_prev"])


MERGE_TM = 512


def _merge_kernel(*refs, n_x):
    branch_refs, gate_ref, x_refs = refs[:8], refs[8], refs[9:9 + n_x]
    (mod_ref, g_ref, wp_ref, wf_ref, wa_ref, ws_ref, wo_ref, rw_ref, rb_ref,
     x_o, h_o, idx_o, wgt_o) = refs[9 + n_x:]
    merged = None
    for i, w_ref in enumerate((wp_ref, wf_ref, wa_ref, ws_ref)):
        br = _dot(_token_load(branch_refs[2 * i:2 * i + 2], MERGE_TM), w_ref[...])
        term = gate_ref[:, i * D_MODEL:(i + 1) * D_MODEL] * br
        merged = term if merged is None else merged + term
    mix = _dot(merged.astype(BF16), wo_ref[...])
    x = _token_load(x_refs, MERGE_TM) + mod_ref[2:3, :] * mix
    x_o[...] = x
    h = _ada_norm(x, g_ref[...], mod_ref[3:4, :], mod_ref[4:5, :])
    for c in range(LANE_TILES):
        h_o[:, c, :, :] = h[:, c * 128:(c + 1) * 128].reshape(MERGE_TM // ROW_GROUP, ROW_GROUP, 128)

    hh, hl = _split_bf16(h)
    rh, rl = _split_bf16(rw_ref[...])
    logits = _dot_nt(rh, hh) + _dot_nt(rl, hh) + _dot_nt(rh, hl) + rb_ref[...]
    expert = lax.broadcasted_iota(jnp.int32, logits.shape, 0).astype(F32)
    work = logits
    top = jnp.max(logits, axis=0, keepdims=True)
    idx, wgt = [], []
    denom = jnp.zeros_like(top)
    for _ in range(TOP_K):
        m = jnp.max(work, axis=0, keepdims=True)
        first = jnp.min(jnp.where(work == m, expert, float(N_EXPERTS)), axis=0, keepdims=True)
        e = jnp.exp(m - top)
        idx.append(first)
        wgt.append(e)
        denom = denom + e
        work = jnp.where(expert == first, -jnp.inf, work)
    idx_o[...] = jnp.concatenate(idx, axis=0).astype(jnp.int32)
    wgt_o[...] = jnp.concatenate(wgt, axis=0) / denom


def _merge(pool, four, attn, sgu, gates, x_parts, mod_l, g_ffn, lw):
    tm = MERGE_TM
    row = functools.partial(_cond_row, blocks_ctx=T_CTX // tm, blocks_per_latent=DEC_SEQ // tm)
    full = lambda shape: pl.BlockSpec(shape, lambda i: (0,) * len(shape))
    tok = lambda w: pl.BlockSpec((tm, w), lambda i: (i, 0))
    branch_specs = []
    for pair, width in ((pool, POOL_WIDTH), (four, FFT_WIDTH), (attn, ATTN_WIDTH), (sgu, SGU_WIDTH)):
        branch_specs += _token_specs(pair, tm, width)
    return pl.pallas_call(
        functools.partial(_merge_kernel, n_x=len(x_parts)),
        grid=(T_ALL // tm,),
        in_specs=branch_specs + [tok(GATE_WIDTH)] + _token_specs(x_parts, tm, D_MODEL) + [
            pl.BlockSpec((None, 6, D_MODEL), lambda i: (row(i), 0, 0)),
            full((1, D_MODEL)),
            full((POOL_WIDTH, D_MODEL)), full((FFT_WIDTH, D_MODEL)), full((ATTN_WIDTH, D_MODEL)),
            full((SGU_WIDTH, D_MODEL)), full((D_MODEL, D_MODEL)),
            full((N_EXPERTS, D_MODEL)), full((N_EXPERTS, 1)),
        ],
        out_specs=[tok(D_MODEL),
                   pl.BlockSpec((tm // ROW_GROUP, LANE_TILES, ROW_GROUP, 128), lambda i: (i, 0, 0, 0)),
                   pl.BlockSpec((TOP_K, tm), lambda i: (0, i)), pl.BlockSpec((TOP_K, tm), lambda i: (0, i))],
        out_shape=[jax.ShapeDtypeStruct((T_ALL, D_MODEL), F32),
                   jax.ShapeDtypeStruct((T_ALL // ROW_GROUP, LANE_TILES, ROW_GROUP, 128), F32),
                   jax.ShapeDtypeStruct((TOP_K, T_ALL), jnp.int32),
                   jax.ShapeDtypeStruct((TOP_K, T_ALL), F32)],
        compiler_params=_params("arbitrary"),
        name="merge_router",
    )(*pool, *four, *attn, *sgu, gates, *x_parts, mod_l, g_ffn.reshape(1, D_MODEL),
      lw["w_br_pool"], lw["w_br_fourier"], lw["w_br_attn"], lw["w_br_sgu"], lw["w_out"],
      lw["router_w"], lw["router_b"])


N_PAIRS = T_ALL * TOP_K
FFN_TM = 512
FFN_ROW_OPTIONS = (128, 256, 384, 512)
N_ROW_TILES = N_PAIRS // FFN_TM + N_EXPERTS
N_SLOTS = N_ROW_TILES * FFN_TM
ROUTE_TB = 512


def _route(top_idx):
    experts = jnp.arange(N_EXPERTS, dtype=jnp.int32)
    onehot = (top_idx.T[:, :, None] == experts[None, None, :]).astype(jnp.int32)
    per_token = jnp.sum(onehot, axis=1)
    csum = jnp.cumsum(per_token, axis=0)
    counts = csum[-1]
    tiles = (counts + FFN_TM - 1) // FFN_TM
    tile_end = jnp.cumsum(tiles)
    row_start = (tile_end - tiles) * FFN_TM
    first_row = csum - per_token + row_start[None, :]
    slot = jnp.sum(onehot * first_row[:, None, :], axis=2)
    tile_ids = jnp.arange(N_ROW_TILES, dtype=jnp.int32)
    tile_expert = jnp.sum((tile_end[None, :] <= tile_ids[:, None]).astype(jnp.int32), axis=1)
    tile_expert = jnp.minimum(tile_expert, N_EXPERTS - 1)
    rows_before = (tile_ids - (tile_end - tiles)[tile_expert]) * FFN_TM
    tile_rows = jnp.where(tile_ids < tile_end[-1],
                          jnp.clip(counts[tile_expert] - rows_before, 0, FFN_TM), 0)
    slot = slot.astype(jnp.int32).reshape(T_ALL // ROUTE_TB, 1, ROUTE_TB * TOP_K)
    used = (tiles > 0).astype(jnp.int32)
    rank = jnp.cumsum(used) - 1
    group_expert = jnp.sum(jnp.where((rank[None, :] == experts[:, None]) & (used[None, :] > 0),
                                     experts[None, :], 0), axis=1)
    tiles_info = dict(expert=tile_expert.astype(jnp.int32), rows=tile_rows.astype(jnp.int32),
                      group=rank[tile_expert].astype(jnp.int32),
                      group_expert=group_expert.astype(jnp.int32),
                      n_groups=jnp.sum(used).reshape(1).astype(jnp.int32),
                      n_used=tile_end[-1:].astype(jnp.int32))
    return slot, tiles_info


def _start_pair_rows(copy):
    def body(g, carry):
        for u in range(ROW_GROUP):
            for k in range(TOP_K):
                copy(g, u, k).start(priority=k % 2)
        return carry

    lax.fori_loop(0, ROUTE_TB // ROW_GROUP, body, 0)


def _pair_slot(slot_ref, g, u, k):
    return slot_ref[0, (g * ROW_GROUP + u) * TOP_K + k]


def _dispatch_kernel(tr_ref, slot_ref, h_ref, xs_out, zeros, sem, zero_sem):
    @pl.when(pl.program_id(0) == 0)
    def _():
        zeros[...] = jnp.zeros_like(zeros)

        def for_each_unfilled_tile(fn):
            group = FFN_ROW_OPTIONS[0]

            def body(i, carry):
                for full_groups in range(FFN_TM // group):
                    first = full_groups * group

                    @pl.when(tr_ref[i] // group == full_groups)
                    def _():
                        fn(pltpu.make_async_copy(zeros.at[pl.ds(0, FFN_TM - first)],
                                                 xs_out.at[pl.ds(i * FFN_TM + first, FFN_TM - first)],
                                                 zero_sem))
                return carry

            lax.fori_loop(0, N_ROW_TILES, body, 0)

        for_each_unfilled_tile(lambda c: c.start())
        for_each_unfilled_tile(lambda c: c.wait())

    def copy(g, u, k):
        return pltpu.make_async_copy(h_ref.at[g, :, u, :], xs_out.at[_pair_slot(slot_ref, g, u, k)], sem)

    _start_pair_rows(copy)
    for _ in range(TOP_K):
        pltpu.make_async_copy(xs_out.at[pl.ds(0, ROUTE_TB)], xs_out.at[pl.ds(0, ROUTE_TB)], sem).wait()


def _dispatch(tile_rows, slot, h):
    return pl.pallas_call(
        _dispatch_kernel,
        grid_spec=pltpu.PrefetchScalarGridSpec(
            num_scalar_prefetch=1,
            grid=(T_ALL // ROUTE_TB,),
            in_specs=[
                pl.BlockSpec((None, 1, ROUTE_TB * TOP_K), lambda i, tr: (i, 0, 0),
                             memory_space=pltpu.SMEM),
                pl.BlockSpec((ROUTE_TB // ROW_GROUP, LANE_TILES, ROW_GROUP, 128),
                             lambda i, tr: (i, 0, 0, 0)),
            ],
            out_specs=pl.BlockSpec(memory_space=pl.ANY),
            scratch_shapes=[pltpu.VMEM((FFN_TM, LANE_TILES, 128), F32), pltpu.SemaphoreType.DMA(()),
                            pltpu.SemaphoreType.DMA(())],
        ),
        out_shape=jax.ShapeDtypeStruct((N_SLOTS, LANE_TILES, 128), F32),
        compiler_params=_params("arbitrary"),
        name="moe_dispatch",
    )(tile_rows, slot, h)


def _expert_ffn(xb, wg, bg, wu, bu, wd, bd):
    gate = jnp.minimum(_dot(xb, wg) + bg, SWIGLU_LIMIT)
    up = jnp.clip(_dot(xb, wu) + bu, -SWIGLU_LIMIT, SWIGLU_LIMIT)
    glu = gate * jax.nn.sigmoid(SWIGLU_ALPHA * gate)
    return _dot(((up + 1.0) * glu).astype(BF16), wd) + bd


def _ffn_kernel(te_ref, tr_ref, tg_ref, ge_ref, ng_ref, nu_ref, xs_ref, wg_hbm, bg_ref, wu_hbm,
                bu_ref, wd_hbm, bd_ref, ys_ref, stage, w_bf16, xb, sems, *, layer):
    i = pl.program_id(0)
    n_rows = tr_ref[i]
    group = tg_ref[i]
    slot = group % 2

    def weight_copies(g, s):
        e = ge_ref[g]
        return [pltpu.make_async_copy(w.at[layer, e], stage.at[s, j], sems.at[s])
                for j, w in enumerate((wg_hbm, wu_hbm, wd_hbm))]

    def lane_tile(ref, c, rows):
        return ref.at[pl.ds(c, rows, stride=LANE_TILES), :]

    def ffn(rows):
        for c in range(LANE_TILES):
            xb[:rows, c * 128:(c + 1) * 128] = lane_tile(xs_ref, c, rows)[...].astype(BF16)
        y = _expert_ffn(xb[:rows, :], w_bf16[0], bg_ref[...], w_bf16[1], bu_ref[...], w_bf16[2],
                        bd_ref[...])
        for c in range(LANE_TILES):
            lane_tile(ys_ref, c, rows)[...] = y[:, c * 128:(c + 1) * 128]
        if rows < FFN_TM:
            ys_ref[rows * LANE_TILES:, :] = jnp.zeros(((FFN_TM - rows) * LANE_TILES, 128), F32)

    @pl.when(i == 0)
    def _():
        for c in weight_copies(0, 0):
            c.start()

    @pl.when((n_rows > 0) & ((i == 0) | (te_ref[i] != te_ref[jnp.maximum(i - 1, 0)])))
    def _():
        for c in weight_copies(group, slot):
            c.wait()
        def cast_rows(r, carry):
            rows = pl.ds(pl.multiple_of(r * CAST_ROWS, CAST_ROWS), CAST_ROWS)
            for j in range(3):
                w_bf16[j, rows, :] = stage[slot, j, rows, :].astype(BF16)
            return carry

        lax.fori_loop(0, D_MODEL // CAST_ROWS, cast_rows, 0)

        @pl.when(group + 1 < ng_ref[0])
        def _():
            for c in weight_copies(group + 1, 1 - slot):
                c.start()

    @pl.when(n_rows == 0)
    def _():
        ys_ref[...] = jnp.zeros_like(ys_ref)

    lower = 0
    for rows in FFN_ROW_OPTIONS:
        @pl.when((n_rows > lower) & (n_rows <= rows))
        def _(rows=rows):
            ffn(rows)
        lower = rows


def _routed_ffn(tiles, xs, l, p):
    n_prefetch = 6
    rows = pl.BlockSpec((FFN_TM * LANE_TILES, 128),
                        lambda i, te, tr, tg, ge, ng, nu: (jnp.minimum(i, nu[0] - 1), 0))
    bias_spec = lambda w: pl.BlockSpec((None, None, 1, w),
                                       lambda i, te, tr, tg, ge, ng, nu: (l, te[i], 0, 0))
    bias = lambda a: a.reshape(DEPTH, N_EXPERTS, 1, a.shape[-1])
    hbm = pl.BlockSpec(memory_space=pl.ANY)
    return pl.pallas_call(
        functools.partial(_ffn_kernel, layer=l),
        grid_spec=pltpu.PrefetchScalarGridSpec(
            num_scalar_prefetch=n_prefetch,
            grid=(N_ROW_TILES,),
            in_specs=[rows, hbm, bias_spec(D_FF), hbm, bias_spec(D_FF), hbm, bias_spec(D_MODEL)],
            out_specs=pl.BlockSpec((FFN_TM * LANE_TILES, 128),
                                   lambda i, te, tr, tg, ge, ng, nu: (i, 0)),
            scratch_shapes=[pltpu.VMEM((2, 3, D_MODEL, D_FF), F32), pltpu.VMEM((3, D_MODEL, D_FF), BF16),
                            pltpu.VMEM((FFN_TM, D_MODEL), BF16), pltpu.SemaphoreType.DMA((2,))],
        ),
        out_shape=jax.ShapeDtypeStruct((N_SLOTS * LANE_TILES, 128), F32),
        compiler_params=_params("arbitrary"),
        name="moe_ffn",
    )(tiles["expert"], tiles["rows"], tiles["group"], tiles["group_expert"], tiles["n_groups"],
      tiles["n_used"], xs.reshape(N_SLOTS * LANE_TILES, 128), p["moe_w_gate"], bias(p["moe_b_gate"]),
      p["moe_w_up"], bias(p["moe_b_up"]), p["moe_w_down"], bias(p["moe_b_down"]))


def _combine_kernel(slot_ref, slot_next_ref, w_ref, x_ref, mod_ref, fg_ref, ys_ref, *rest, final_norm):
    out_refs, (acc, buf, sems) = rest[:-3], rest[-3:]
    i = pl.program_id(0)
    cur = i % 2

    def copy(slots, b, g, u, k):
        return pltpu.make_async_copy(ys_ref.at[_pair_slot(slots, g, u, k)], buf.at[b, k, g, :, u, :],
                                     sems.at[b])

    @pl.when(i == 0)
    def _():
        _start_pair_rows(functools.partial(copy, slot_ref, 0))

    @pl.when(i + 1 < pl.num_programs(0))
    def _():
        _start_pair_rows(functools.partial(copy, slot_next_ref, 1 - cur))

    for k in range(TOP_K):
        pltpu.make_async_copy(ys_ref.at[pl.ds(0, ROUTE_TB)], ys_ref.at[pl.ds(0, ROUTE_TB)],
                              sems.at[cur]).wait()

    def rows_of(k, c):
        return buf[cur, k, :, c, :, :].reshape(ROUTE_TB, 128)

    w = w_ref[...]
    sum_sq = jnp.zeros((ROUTE_TB, 1), F32)
    for c in range(LANE_TILES):
        cols = slice(c * 128, (c + 1) * 128)
        mix = w[:, 0:1] * rows_of(0, c)
        for k in range(1, TOP_K):
            mix = mix + w[:, k:k + 1] * rows_of(k, c)
        piece = x_ref[:, cols] + mod_ref[5:6, cols] * mix
        sum_sq = sum_sq + jnp.sum(piece * piece, axis=-1, keepdims=True)
        acc[:, cols] = piece

    if not final_norm:
        out_refs[0][...] = acc[...]
        return
    normed = acc[...] * lax.rsqrt(sum_sq * (1.0 / D_MODEL) + EPS) * fg_ref[...]

    @pl.when(i < T_CTX // ROUTE_TB)
    def _():
        out_refs[0][...] = normed

    @pl.when(i >= T_CTX // ROUTE_TB)
    def _():
        out_refs[1][...] = normed


def _combine(slot, top_w, x, mod_l, final_g, ys, final_norm):
    tb = ROUTE_TB
    nb = T_ALL // tb
    nc = T_CTX // tb
    row = functools.partial(_cond_row, blocks_ctx=nc, blocks_per_latent=DEC_SEQ // tb)
    tok = lambda w: pl.BlockSpec((tb, w), lambda i: (i, 0))
    slots = lambda ahead: pl.BlockSpec((None, 1, tb * TOP_K),
                                       lambda i: (jnp.minimum(i + ahead, nb - 1), 0, 0),
                                       memory_space=pltpu.SMEM)
    if final_norm:
        out_specs = [pl.BlockSpec((tb, D_MODEL), lambda i: (jnp.minimum(i, nc - 1), 0)),
                     pl.BlockSpec((tb, D_MODEL), lambda i: (jnp.maximum(i - nc, 0), 0))]
        out_shape = [jax.ShapeDtypeStruct((T_CTX, D_MODEL), F32),
                     jax.ShapeDtypeStruct((T_LAT, D_MODEL), F32)]
    else:
        out_specs = [tok(D_MODEL)]
        out_shape = [jax.ShapeDtypeStruct((T_ALL, D_MODEL), F32)]
    return pl.pallas_call(
        functools.partial(_combine_kernel, final_norm=final_norm),
        grid=(nb,),
        in_specs=[
            slots(0), slots(1), tok(TOP_K), tok(D_MODEL),
            pl.BlockSpec((None, 6, D_MODEL), lambda i: (row(i), 0, 0)),
            pl.BlockSpec((1, D_MODEL), lambda i: (0, 0)),
            pl.BlockSpec(memory_space=pl.ANY),
        ],
        out_specs=out_specs,
        out_shape=out_shape,
        scratch_shapes=[pltpu.VMEM((tb, D_MODEL), F32),
                        pltpu.VMEM((2, TOP_K, tb // ROW_GROUP, LANE_TILES, ROW_GROUP, 128), F32),
                        pltpu.SemaphoreType.DMA((2,))],
        compiler_params=_params("arbitrary"),
        name="moe_combine",
    )(slot, slot, top_w, x, mod_l, final_g.reshape(1, D_MODEL), ys.reshape(N_SLOTS, LANE_TILES, 128))


def _block_diag(blocks):
    g, a, b = blocks.shape
    eye = jnp.eye(g, dtype=blocks.dtype)
    return (eye[:, None, :, None] * blocks[:, :, None, :]).reshape(g * a, g * b)


def _dft_tables(n):
    k = np.arange(n, dtype=np.int64)
    ang = 2.0 * np.pi * ((k[:, None] * k[None, :]) % n).astype(np.float64) / n
    return np.cos(ang) / math.sqrt(n), np.sin(ang) / math.sqrt(n)


def _seq_constants(seq):
    t = np.arange(seq)[:, None]
    win = np.array(POOL_WINDOWS)[None, :]
    lo = np.clip(t - win // 2, 0, seq)
    hi = np.clip(t - win // 2 + win, 0, seq)
    invcnt = np.repeat(1.0 / (hi - lo).astype(np.float64), POOL_CH, axis=1)
    cos_p, sin_p = _dft_tables(seq)
    cos_c, sin_c = _dft_tables(FFT_CH)
    eye = np.eye(POOL_GROUPS)
    as_bf16 = lambda a: jnp.asarray(a, F32).astype(BF16)
    return {
        "invcnt": jnp.asarray(invcnt, F32),
        "f_pos": as_bf16(np.concatenate([cos_p, -sin_p], axis=1)),
        "f_cos_ch": as_bf16(np.kron(eye, cos_c)),
        "f_sin_ch": as_bf16(np.kron(eye, sin_c)),
        "ones96": as_bf16(np.kron(eye, np.ones((SGU_CH, SGU_CH)))),
    }


def _rope_constants():
    rows = DEC_SEQ // GRID_W
    row = jnp.repeat(jnp.arange(rows, dtype=F32), GRID_W)
    col = jnp.tile(jnp.arange(GRID_W, dtype=F32), rows)
    freqs = ROPE_THETA ** (-jnp.arange(ROPE_PAIRS, dtype=F32) / ROPE_PAIRS)
    ang = jnp.stack([row[:, None] * freqs, col[:, None] * freqs], axis=1)
    ang = jnp.repeat(ang.reshape(DEC_SEQ, 2 * ROPE_PAIRS), 2, axis=1)
    ang = jnp.tile(ang, (1, 128 // HEAD_DIM))
    even = (jnp.arange(128) % 2 == 0)[None, :]
    sin = jnp.sin(ang)
    return {
        "rope_cos": jnp.cos(ang),
        "rope_sin_next": jnp.where(even, -sin, 0.0),
        "rope_sin_prev": jnp.where(even, 0.0, sin),
    }


def _layer_weights(p, l):
    group_of_lane = np.arange(SGU_WIDTH) // SGU_CH
    return {
        "pool_w_bd": _block_diag(p["pool_w"][l]).astype(BF16),
        "pool_scale": p["pool_scale"][l].reshape(1, POOL_WIDTH),
        "sgu_w_stack": p["sgu_w"][l].reshape(SGU_GROUPS * SGU_CHUNK, SGU_CHUNK).astype(BF16),
        "sgu_bias": p["sgu_b"][l].T[:, group_of_lane],
        "gq": jnp.tile(p["q_norm_g"][l], N_HEADS).reshape(1, ATTN_WIDTH),
        "gk": jnp.tile(p["k_norm_g"][l], N_KV_HEADS).reshape(1, KV_WIDTH),
        "w_br_pool": p["w_br_pool"][l].astype(BF16),
        "w_br_fourier": p["w_br_fourier"][l].astype(BF16),
        "w_br_attn": p["w_br_attn"][l].astype(BF16),
        "w_br_sgu": p["w_br_sgu"][l].astype(BF16),
        "w_out": p["w_out"][l].astype(BF16),
        "router_w": p["router_w"][l].T,
        "router_b": p["router_b"][l].reshape(N_EXPERTS, 1),
    }


def kernel(x_prompt, x_sample, cache_k, cache_v, c, c_ctx, w_mod, b_mod, norm_mix_g, norm_ffn_g, w_in, pool_w, pool_scale, q_norm_g, k_norm_g, sgu_w, sgu_b, w_br_pool, w_br_fourier, w_br_attn, w_br_sgu, w_out, router_w, router_b, moe_w_gate, moe_b_gate, moe_w_up, moe_b_up, moe_w_down, moe_b_down, final_norm_g):
    p = dict(w_in=w_in, pool_w=pool_w, pool_scale=pool_scale, q_norm_g=q_norm_g, k_norm_g=k_norm_g,
             sgu_w=sgu_w, sgu_b=sgu_b, w_br_pool=w_br_pool, w_br_fourier=w_br_fourier,
             w_br_attn=w_br_attn, w_br_sgu=w_br_sgu, w_out=w_out, router_w=router_w,
             router_b=router_b, moe_w_gate=moe_w_gate, moe_b_gate=moe_b_gate, moe_w_up=moe_w_up,
             moe_b_up=moe_b_up, moe_w_down=moe_w_down, moe_b_down=moe_b_down)

    cond = jnp.concatenate([c_ctx[None, :], c, jnp.zeros((N_COND - 1 - DEC_BATCH, D_MODEL), F32)])
    mod = _modulation(cond, w_mod, b_mod).reshape(DEPTH, N_COND, 6, D_MODEL)

    ones64 = jnp.asarray(np.kron(np.eye(N_HEADS), np.ones((HEAD_DIM, HEAD_DIM))), BF16)
    consts_ctx = dict(_seq_constants(SEQ), ones64=ones64)
    consts_lat = dict(_seq_constants(DEC_SEQ), ones64=ones64, **_rope_constants())
    ck = cache_k.reshape(DEC_BATCH, DEPTH, PAST_LEN, KV_WIDTH)
    cv = cache_v.reshape(DEC_BATCH, DEPTH, PAST_LEN, KV_WIDTH)

    x_parts = (x_prompt.reshape(T_CTX, D_MODEL), x_sample.reshape(T_LAT, D_MODEL))
    new_k, new_v = [], []
    for l in range(DEPTH):
        lw = _layer_weights(p, l)
        xp, xq, qkv, uv, gates = _in_projection(x_parts, mod[l], norm_mix_g[l], w_in, l)
        pool_c, four_c, sgu_c, attn_c, k_c, v_c = _context_mixing(xp, xq, uv, qkv, consts_ctx, lw)
        pool_l, four_l, sgu_l = _mixers(xp, xq, uv, consts_lat, lw, DEC_SEQ, DEC_BATCH,
                                        T_CTX // DEC_SEQ, 1)
        attn_l = _attention_lat(qkv, ck[:, l], cv[:, l], consts_lat, lw)
        new_k.append(k_c.reshape(BATCH, SEQ, N_KV_HEADS, HEAD_DIM))
        new_v.append(v_c.reshape(BATCH, SEQ, N_KV_HEADS, HEAD_DIM))
        x, h, top_idx, top_w = _merge((pool_c, pool_l), (four_c, four_l), (attn_c, attn_l),
                                      (sgu_c, sgu_l), gates, x_parts, mod[l], norm_ffn_g[l], lw)
        slot, tiles = _route(top_idx)
        ys = _routed_ffn(tiles, _dispatch(tiles["rows"], slot, h), l, p)
        x_parts = tuple(_combine(slot, top_w.T, x, mod[l], final_norm_g, ys,
                                 final_norm=(l == DEPTH - 1)))

    y_prompt = x_parts[0].reshape(BATCH, SEQ, D_MODEL)
    y_sample = x_parts[1].reshape(DEC_BATCH, DEC_SEQ, D_MODEL)
    return (y_prompt, y_sample, jnp.stack(new_k, axis=1), jnp.stack(new_v, axis=1))
```
